```python
import math
import jax, jax.numpy as jnp
from jax import lax
import numpy as np

D_MODEL = 1024
BATCH = 8
SEQ = 8192
DEPTH = 2

CHUNK = 64
LEFT_CHUNKS = 8
BAND = (LEFT_CHUNKS + 1) * CHUNK
S5_WIDTH = D_MODEL // 2
S5_GROUP = 16
S5_GROUPS = S5_WIDTH // S5_GROUP
S5_STATE = 64
CONV_WIDTH = D_MODEL - S5_WIDTH
CONV_KERNEL = 31
ATT_HEADS = 16
ATT_HEAD_DIM = D_MODEL // ATT_HEADS
MAX_REL = 128
MEM_LEN = 256
XA_HEADS = 4
XA_HEAD_DIM = D_MODEL // XA_HEADS
EPS = 1e-6
N_EVEN = (DEPTH + 1) // 2
N_ODD = DEPTH // 2
EVEN_IN = 2 * S5_WIDTH + 3 * CONV_WIDTH
ODD_IN = 4 * D_MODEL

kernel_name = 'streaming_hybrid_s5_conv_chunkattn'


def rms_norm(x, g):
    xf = x.astype(jnp.float32)
    y = xf * lax.rsqrt(jnp.mean(xf * xf, axis=-1, keepdims=True) + EPS)
    return (y * g.astype(jnp.float32)).astype(x.dtype)


def _complex_affine_combine(e1, e2):
    a1r, a1i, b1r, b1i = e1
    a2r, a2i, b2r, b2i = e2
    ar = a2r * a1r - a2i * a1i
    ai = a2r * a1i + a2i * a1r
    br = a2r * b1r - a2i * b1i + b2r
    bi = a2r * b1i + a2i * b1r + b2i
    return (ar, ai, br, bi)


def s5_mixer(u, lam_re, lam_im, log_dt, b_re, b_im, c_re, c_im, d_skip, glu_w, glu_b):
    bsz, L, _ = u.shape
    f32 = jnp.float32
    uf = u.astype(f32)
    ug = uf.reshape(bsz, L, S5_GROUPS, S5_GROUP)
    lr = lam_re.astype(f32)
    li = lam_im.astype(f32)
    dt = jnp.exp(log_dt.astype(f32))[:, None]
    mag = jnp.exp(lr * dt)
    ab_re = mag * jnp.cos(li * dt)
    ab_im = mag * jnp.sin(li * dt)
    den = lr * lr + li * li
    nr = ab_re - 1.0
    coef_re = (nr * lr + ab_im * li) / den
    coef_im = (ab_im * lr - nr * li) / den
    br = b_re.astype(f32)
    bi = b_im.astype(f32)
    bb_re = coef_re[..., None] * br - coef_im[..., None] * bi
    bb_im = coef_re[..., None] * bi + coef_im[..., None] * br
    bu_re = jnp.einsum('blgc,gpc->blgp', ug, bb_re)
    bu_im = jnp.einsum('blgc,gpc->blgp', ug, bb_im)
    a_re = jnp.broadcast_to(ab_re, bu_re.shape)
    a_im = jnp.broadcast_to(ab_im, bu_im.shape)
    _, _, h_re, h_im = lax.associative_scan(_complex_affine_combine, (a_re, a_im, bu_re, bu_im), axis=1)
    y = (jnp.einsum('blgp,gcp->blgc', h_re, c_re.astype(f32))
         - jnp.einsum('blgp,gcp->blgc', h_im, c_im.astype(f32)))
    y = y.reshape(bsz, L, S5_WIDTH) + d_skip.astype(f32) * uf
    z = jax.nn.gelu(y)
    out = z * jax.nn.sigmoid(z @ glu_w.astype(f32) + glu_b.astype(f32))
    return out.astype(u.dtype)


def conv_module(val, glu_gate, conv_w, conv_b, ln_g, ln_b):
    v = val * jax.nn.sigmoid(glu_gate)
    vpad = jnp.pad(v, ((0, 0), (CONV_KERNEL - 1, 0), (0, 0)))
    c = lax.conv_general_dilated(vpad, conv_w[:, None, :].astype(v.dtype), window_strides=(1,),
                                 padding='VALID', dimension_numbers=('NWC', 'WIO', 'NWC'),
                                 feature_group_count=CONV_WIDTH) + conv_b.astype(v.dtype)
    cf = c.astype(jnp.float32)
    mu = jnp.mean(cf, axis=-1, keepdims=True)
    var = jnp.mean(jnp.square(cf - mu), axis=-1, keepdims=True)
    cn = (cf - mu) * lax.rsqrt(var + EPS) * ln_g.astype(jnp.float32) + ln_b.astype(jnp.float32)
    return jax.nn.silu(cn).astype(val.dtype)


def ssm_conv_layer(h, w_in, lam_re, lam_im, log_dt, b_re, b_im, c_re, c_im, d_skip, glu_w, glu_b,
                   conv_w, conv_b, ln_g, ln_b, w_out):
    z = h @ w_in
    cuts = [S5_WIDTH, 2 * S5_WIDTH, 2 * S5_WIDTH + CONV_WIDTH, 2 * S5_WIDTH + 2 * CONV_WIDTH]
    u_a, gate_a, val_b, glu_b_in, gate_b = jnp.split(z, cuts, axis=-1)
    y_a = s5_mixer(u_a, lam_re, lam_im, log_dt, b_re, b_im, c_re, c_im, d_skip, glu_w, glu_b) * jax.nn.silu(gate_a)
    y_b = conv_module(val_b, glu_b_in, conv_w, conv_b, ln_g, ln_b) * jax.nn.silu(gate_b)
    return jnp.concatenate([y_a, y_b], axis=-1) @ w_out


def chunk_attention_layer(h, w_in, rel_bias, w_out):
    bsz, L, _ = h.shape
    q, k, v, g = jnp.split(h @ w_in, 4, axis=-1)
    q = q.reshape(bsz, L, ATT_HEADS, ATT_HEAD_DIM)
    k = k.reshape(bsz, L, ATT_HEADS, ATT_HEAD_DIM)
    v = v.reshape(bsz, L, ATT_HEADS, ATT_HEAD_DIM)
    pad = BAND - CHUNK
    kp = jnp.pad(k, ((0, 0), (pad, 0), (0, 0), (0, 0)))
    vp = jnp.pad(v, ((0, 0), (pad, 0), (0, 0), (0, 0)))
    qi = jnp.arange(CHUNK)[:, None]
    kj = jnp.arange(BAND)[None, :]
    rel = jnp.clip(qi - kj + pad, -MAX_REL, MAX_REL) + MAX_REL
    bias = rel_bias.astype(jnp.float32)[:, rel]
    scale = ATT_HEAD_DIM ** -0.5

    def one_chunk(c):
        start = c * CHUNK
        qc = lax.dynamic_slice_in_dim(q, start, CHUNK, axis=1)
        kc = lax.dynamic_slice_in_dim(kp, start, BAND, axis=1)
        vc = lax.dynamic_slice_in_dim(vp, start, BAND, axis=1)
        s = jnp.einsum('bqhd,bkhd->bhqk', qc, kc).astype(jnp.float32) * scale + bias
        valid = (start - pad + jnp.arange(BAND)) >= 0
        s = jnp.where(valid[None, None, None, :], s, -1e30)
        p = jax.nn.softmax(s, axis=-1).astype(h.dtype)
        return jnp.einsum('bhqk,bkhd->bqhd', p, vc)

    o = lax.map(one_chunk, jnp.arange(L // CHUNK))
    o = jnp.moveaxis(o, 0, 1).reshape(bsz, L, D_MODEL)
    return (o * jax.nn.silu(g)) @ w_out


def mem_cross_attention(h, mem_n, w_qg, w_kv, w_o):
    bsz, L, _ = h.shape
    q, g = jnp.split(h @ w_qg, 2, axis=-1)
    k, v = jnp.split(mem_n @ w_kv, 2, axis=-1)
    q = q.reshape(bsz, L, XA_HEADS, XA_HEAD_DIM)
    k = k.reshape(bsz, -1, XA_HEADS, XA_HEAD_DIM)
    v = v.reshape(bsz, -1, XA_HEADS, XA_HEAD_DIM)
    s = jnp.einsum('bqhd,bkhd->bhqk', q, k).astype(jnp.float32) * (XA_HEAD_DIM ** -0.5)
    p = jax.nn.softmax(s, axis=-1).astype(h.dtype)
    o = jnp.einsum('bhqk,bkhd->bqhd', p, v).reshape(bsz, L, D_MODEL)
    return (o * jax.nn.silu(g)) @ w_o


def _fwd_setup_inputs(seed: int = 0) -> dict:
    key = jax.random.key(seed)
    ks = iter(jax.random.split(key, 40))

    def nrm(shape, scale):
        return jax.random.normal(next(ks), shape, jnp.float32) * scale

    def gain(shape):
        return 1.0 + nrm(shape, 0.02)

    lam_im_base = jnp.pi * jnp.arange(S5_STATE, dtype=jnp.float32)
    return {
        'x': nrm((BATCH, SEQ, D_MODEL), 1.0),
        'mem': nrm((BATCH, MEM_LEN, D_MODEL), 1.0),
        'mem_norm_g': gain((D_MODEL,)),
        'ev_norm_g': gain((N_EVEN, D_MODEL)),
        'ev_w_in': nrm((N_EVEN, D_MODEL, EVEN_IN), D_MODEL ** -0.5),
        'ev_s5_lambda_re': -0.5 + nrm((N_EVEN, S5_GROUPS, S5_STATE), 0.01),
        'ev_s5_lambda_im': lam_im_base + nrm((N_EVEN, S5_GROUPS, S5_STATE), 0.01),
        'ev_s5_log_dt': jax.random.uniform(next(ks), (N_EVEN, S5_GROUPS), jnp.float32,
                                           math.log(1e-3), math.log(1e-1)),
        'ev_s5_b_re': nrm((N_EVEN, S5_GROUPS, S5_STATE, S5_GROUP), (2 * S5_GROUP) ** -0.5),
        'ev_s5_b_im': nrm((N_EVEN, S5_GROUPS, S5_STATE, S5_GROUP), (2 * S5_GROUP) ** -0.5),
        'ev_s5_c_re': nrm((N_EVEN, S5_GROUPS, S5_GROUP, S5_STATE), S5_STATE ** -0.5),
        'ev_s5_c_im': nrm((N_EVEN, S5_GROUPS, S5_GROUP, S5_STATE), S5_STATE ** -0.5),
        'ev_s5_d': nrm((N_EVEN, S5_WIDTH), 1.0),
        'ev_s5_glu_w': nrm((N_EVEN, S5_WIDTH, S5_WIDTH), S5_WIDTH ** -0.5),
        'ev_s5_glu_b': nrm((N_EVEN, S5_WIDTH), 0.01),
        'ev_conv_w': nrm((N_EVEN, CONV_KERNEL, CONV_WIDTH), CONV_KERNEL ** -0.5),
        'ev_conv_b': nrm((N_EVEN, CONV_WIDTH), 0.01),
        'ev_conv_ln_g': gain((N_EVEN, CONV_WIDTH)),
        'ev_conv_ln_b': nrm((N_EVEN, CONV_WIDTH), 0.01),
        'ev_w_out': nrm((N_EVEN, D_MODEL, D_MODEL), D_MODEL ** -0.5),
        'od_norm_g': gain((N_ODD, D_MODEL)),
        'od_w_in': nrm((N_ODD, D_MODEL, ODD_IN), D_MODEL ** -0.5),
        'od_rel_bias': nrm((N_ODD, ATT_HEADS, 2 * MAX_REL + 1), 0.1),
        'od_w_out': nrm((N_ODD, D_MODEL, D_MODEL), D_MODEL ** -0.5),
        'xa_norm_g': gain((DEPTH, D_MODEL)),
        'xa_w_qg': nrm((DEPTH, D_MODEL, 2 * D_MODEL), D_MODEL ** -0.5),
        'xa_w_kv': nrm((DEPTH, D_MODEL, 2 * D_MODEL), D_MODEL ** -0.5),
        'xa_w_o': nrm((DEPTH, D_MODEL, D_MODEL), D_MODEL ** -0.5),
        'final_norm_g': gain((D_MODEL,)),
    }


def _fwd_reference(x, mem, mem_norm_g, ev_norm_g, ev_w_in, ev_s5_lambda_re, ev_s5_lambda_im, ev_s5_log_dt,
              ev_s5_b_re, ev_s5_b_im, ev_s5_c_re, ev_s5_c_im, ev_s5_d, ev_s5_glu_w, ev_s5_glu_b,
              ev_conv_w, ev_conv_b, ev_conv_ln_g, ev_conv_ln_b, ev_w_out,
              od_norm_g, od_w_in, od_rel_bias, od_w_out,
              xa_norm_g, xa_w_qg, xa_w_kv, xa_w_o, final_norm_g):
    mem_n = rms_norm(mem, mem_norm_g)
    for layer in range(DEPTH):
        i = layer // 2
        if layer % 2 == 0:
            x = x + ssm_conv_layer(rms_norm(x, ev_norm_g[i]), ev_w_in[i], ev_s5_lambda_re[i], ev_s5_lambda_im[i],
                                   ev_s5_log_dt[i], ev_s5_b_re[i], ev_s5_b_im[i], ev_s5_c_re[i], ev_s5_c_im[i],
                                   ev_s5_d[i], ev_s5_glu_w[i], ev_s5_glu_b[i], ev_conv_w[i], ev_conv_b[i],
                                   ev_conv_ln_g[i], ev_conv_ln_b[i], ev_w_out[i])
        else:
            x = x + chunk_attention_layer(rms_norm(x, od_norm_g[i]), od_w_in[i], od_rel_bias[i], od_w_out[i])
        x = x + mem_cross_attention(rms_norm(x, xa_norm_g[layer]), mem_n, xa_w_qg[layer], xa_w_kv[layer], xa_w_o[layer])
    return rms_norm(x, final_norm_g)


import jax as _jax
import jax.numpy as _jnp

TWIN_FORMAT = 'train_step'
FWD_PARAMS = ['x', 'mem', 'mem_norm_g', 'ev_norm_g', 'ev_w_in', 'ev_s5_lambda_re', 'ev_s5_lambda_im', 'ev_s5_log_dt', 'ev_s5_b_re', 'ev_s5_b_im', 'ev_s5_c_re', 'ev_s5_c_im', 'ev_s5_d', 'ev_s5_glu_w', 'ev_s5_glu_b', 'ev_conv_w', 'ev_conv_b', 'ev_conv_ln_g', 'ev_conv_ln_b', 'ev_w_out', 'od_norm_g', 'od_w_in', 'od_rel_bias', 'od_w_out', 'xa_norm_g', 'xa_w_qg', 'xa_w_kv', 'xa_w_o', 'final_norm_g']
TWIN_WEIGHTS = ['mem_norm_g', 'ev_norm_g', 'ev_w_in', 'ev_s5_lambda_re', 'ev_s5_lambda_im', 'ev_s5_log_dt', 'ev_s5_b_re', 'ev_s5_b_im', 'ev_s5_c_re', 'ev_s5_c_im', 'ev_s5_d', 'ev_s5_glu_w', 'ev_s5_glu_b', 'ev_conv_w', 'ev_conv_b', 'ev_conv_ln_g', 'ev_conv_ln_b', 'ev_w_out', 'od_norm_g', 'od_w_in', 'od_rel_bias', 'od_w_out', 'xa_norm_g', 'xa_w_qg', 'xa_w_kv', 'xa_w_o', 'final_norm_g']
TWIN_DIFF_INPUT = 'x'
TWIN_INPUTS = ['x', 'mem', 'mem_norm_g', 'ev_norm_g', 'ev_w_in', 'ev_s5_lambda_re', 'ev_s5_lambda_im', 'ev_s5_log_dt', 'ev_s5_b_re', 'ev_s5_b_im', 'ev_s5_c_re', 'ev_s5_c_im', 'ev_s5_d', 'ev_s5_glu_w', 'ev_s5_glu_b', 'ev_conv_w', 'ev_conv_b', 'ev_conv_ln_g', 'ev_conv_ln_b', 'ev_w_out', 'od_norm_g', 'od_w_in', 'od_rel_bias', 'od_w_out', 'xa_norm_g', 'xa_w_qg', 'xa_w_kv', 'xa_w_o', 'final_norm_g', 'loss_target', 'm_mem_norm_g', 'm_ev_norm_g', 'm_ev_w_in', 'm_ev_s5_lambda_re', 'm_ev_s5_lambda_im', 'm_ev_s5_log_dt', 'm_ev_s5_b_re', 'm_ev_s5_b_im', 'm_ev_s5_c_re', 'm_ev_s5_c_im', 'm_ev_s5_d', 'm_ev_s5_glu_w', 'm_ev_s5_glu_b', 'm_ev_conv_w', 'm_ev_conv_b', 'm_ev_conv_ln_g', 'm_ev_conv_ln_b', 'm_ev_w_out', 'm_od_norm_g', 'm_od_w_in', 'm_od_rel_bias', 'm_od_w_out', 'm_xa_norm_g', 'm_xa_w_qg', 'm_xa_w_kv', 'm_xa_w_o', 'm_final_norm_g', 'v_mem_norm_g', 'v_ev_norm_g', 'v_ev_w_in', 'v_ev_s5_lambda_re', 'v_ev_s5_lambda_im', 'v_ev_s5_log_dt', 'v_ev_s5_b_re', 'v_ev_s5_b_im', 'v_ev_s5_c_re', 'v_ev_s5_c_im', 'v_ev_s5_d', 'v_ev_s5_glu_w', 'v_ev_s5_glu_b', 'v_ev_conv_w', 'v_ev_conv_b', 'v_ev_conv_ln_g', 'v_ev_conv_ln_b', 'v_ev_w_out', 'v_od_norm_g', 'v_od_w_in', 'v_od_rel_bias', 'v_od_w_out', 'v_xa_norm_g', 'v_xa_w_qg', 'v_xa_w_kv', 'v_xa_w_o', 'v_final_norm_g']
TWIN_OUTPUTS = ['loss', 'grad_x', 'grad_mem_norm_g', 'grad_ev_norm_g', 'grad_ev_w_in', 'grad_ev_s5_lambda_re', 'grad_ev_s5_lambda_im', 'grad_ev_s5_log_dt', 'grad_ev_s5_b_re', 'grad_ev_s5_b_im', 'grad_ev_s5_c_re', 'grad_ev_s5_c_im', 'grad_ev_s5_d', 'grad_ev_s5_glu_w', 'grad_ev_s5_glu_b', 'grad_ev_conv_w', 'grad_ev_conv_b', 'grad_ev_conv_ln_g', 'grad_ev_conv_ln_b', 'grad_ev_w_out', 'grad_od_norm_g', 'grad_od_w_in', 'grad_od_rel_bias', 'grad_od_w_out', 'grad_xa_norm_g', 'grad_xa_w_qg', 'grad_xa_w_kv', 'grad_xa_w_o', 'grad_final_norm_g', 'delta_mem_norm_g', 'delta_ev_norm_g', 'delta_ev_w_in', 'delta_ev_s5_lambda_re', 'delta_ev_s5_lambda_im', 'delta_ev_s5_log_dt', 'delta_ev_s5_b_re', 'delta_ev_s5_b_im', 'delta_ev_s5_c_re', 'delta_ev_s5_c_im', 'delta_ev_s5_d', 'delta_ev_s5_glu_w', 'delta_ev_s5_glu_b', 'delta_ev_conv_w', 'delta_ev_conv_b', 'delta_ev_conv_ln_g', 'delta_ev_conv_ln_b', 'delta_ev_w_out', 'delta_od_norm_g', 'delta_od_w_in', 'delta_od_rel_bias', 'delta_od_w_out', 'delta_xa_norm_g', 'delta_xa_w_qg', 'delta_xa_w_kv', 'delta_xa_w_o', 'delta_final_norm_g', 'new_m_mem_norm_g', 'new_m_ev_norm_g', 'new_m_ev_w_in', 'new_m_ev_s5_lambda_re', 'new_m_ev_s5_lambda_im', 'new_m_ev_s5_log_dt', 'new_m_ev_s5_b_re', 'new_m_ev_s5_b_im', 'new_m_ev_s5_c_re', 'new_m_ev_s5_c_im', 'new_m_ev_s5_d', 'new_m_ev_s5_glu_w', 'new_m_ev_s5_glu_b', 'new_m_ev_conv_w', 'new_m_ev_conv_b', 'new_m_ev_conv_ln_g', 'new_m_ev_conv_ln_b', 'new_m_ev_w_out', 'new_m_od_norm_g', 'new_m_od_w_in', 'new_m_od_rel_bias', 'new_m_od_w_out', 'new_m_xa_norm_g', 'new_m_xa_w_qg', 'new_m_xa_w_kv', 'new_m_xa_w_o', 'new_m_final_norm_g', 'new_v_mem_norm_g', 'new_v_ev_norm_g', 'new_v_ev_w_in', 'new_v_ev_s5_lambda_re', 'new_v_ev_s5_lambda_im', 'new_v_ev_s5_log_dt', 'new_v_ev_s5_b_re', 'new_v_ev_s5_b_im', 'new_v_ev_s5_c_re', 'new_v_ev_s5_c_im', 'new_v_ev_s5_d', 'new_v_ev_s5_glu_w', 'new_v_ev_s5_glu_b', 'new_v_ev_conv_w', 'new_v_ev_conv_b', 'new_v_ev_conv_ln_g', 'new_v_ev_conv_ln_b', 'new_v_ev_w_out', 'new_v_od_norm_g', 'new_v_od_w_in', 'new_v_od_rel_bias', 'new_v_od_w_out', 'new_v_xa_norm_g', 'new_v_xa_w_qg', 'new_v_xa_w_kv', 'new_v_xa_w_o', 'new_v_final_norm_g']
TWIN_LEAF_KINDS = {'loss': 'loss', 'grad_x': 'grad_x', 'grad_mem_norm_g': 'grad_w', 'grad_ev_norm_g': 'grad_w', 'grad_ev_w_in': 'grad_w', 'grad_ev_s5_lambda_re': 'grad_w', 'grad_ev_s5_lambda_im': 'grad_w', 'grad_ev_s5_log_dt': 'grad_w', 'grad_ev_s5_b_re': 'grad_w', 'grad_ev_s5_b_im': 'grad_w', 'grad_ev_s5_c_re': 'grad_w', 'grad_ev_s5_c_im': 'grad_w', 'grad_ev_s5_d': 'grad_w', 'grad_ev_s5_glu_w': 'grad_w', 'grad_ev_s5_glu_b': 'grad_w', 'grad_ev_conv_w': 'grad_w', 'grad_ev_conv_b': 'grad_w', 'grad_ev_conv_ln_g': 'grad_w', 'grad_ev_conv_ln_b': 'grad_w', 'grad_ev_w_out': 'grad_w', 'grad_od_norm_g': 'grad_w', 'grad_od_w_in': 'grad_w', 'grad_od_rel_bias': 'grad_w', 'grad_od_w_out': 'grad_w', 'grad_xa_norm_g': 'grad_w', 'grad_xa_w_qg': 'grad_w', 'grad_xa_w_kv': 'grad_w', 'grad_xa_w_o': 'grad_w', 'grad_final_norm_g': 'grad_w', 'delta_mem_norm_g': 'delta_w', 'delta_ev_norm_g': 'delta_w', 'delta_ev_w_in': 'delta_w', 'delta_ev_s5_lambda_re': 'delta_w', 'delta_ev_s5_lambda_im': 'delta_w', 'delta_ev_s5_log_dt': 'delta_w', 'delta_ev_s5_b_re': 'delta_w', 'delta_ev_s5_b_im': 'delta_w', 'delta_ev_s5_c_re': 'delta_w', 'delta_ev_s5_c_im': 'delta_w', 'delta_ev_s5_d': 'delta_w', 'delta_ev_s5_glu_w': 'delta_w', 'delta_ev_s5_glu_b': 'delta_w', 'delta_ev_conv_w': 'delta_w', 'delta_ev_conv_b': 'delta_w', 'delta_ev_conv_ln_g': 'delta_w', 'delta_ev_conv_ln_b': 'delta_w', 'delta_ev_w_out': 'delta_w', 'delta_od_norm_g': 'delta_w', 'delta_od_w_in': 'delta_w', 'delta_od_rel_bias': 'delta_w', 'delta_od_w_out': 'delta_w', 'delta_xa_norm_g': 'delta_w', 'delta_xa_w_qg': 'delta_w', 'delta_xa_w_kv': 'delta_w', 'delta_xa_w_o': 'delta_w', 'delta_final_norm_g': 'delta_w', 'new_m_mem_norm_g': 'new_m', 'new_m_ev_norm_g': 'new_m', 'new_m_ev_w_in': 'new_m', 'new_m_ev_s5_lambda_re': 'new_m', 'new_m_ev_s5_lambda_im': 'new_m', 'new_m_ev_s5_log_dt': 'new_m', 'new_m_ev_s5_b_re': 'new_m', 'new_m_ev_s5_b_im': 'new_m', 'new_m_ev_s5_c_re': 'new_m', 'new_m_ev_s5_c_im': 'new_m', 'new_m_ev_s5_d': 'new_m', 'new_m_ev_s5_glu_w': 'new_m', 'new_m_ev_s5_glu_b': 'new_m', 'new_m_ev_conv_w': 'new_m', 'new_m_ev_conv_b': 'new_m', 'new_m_ev_conv_ln_g': 'new_m', 'new_m_ev_conv_ln_b': 'new_m', 'new_m_ev_w_out': 'new_m', 'new_m_od_norm_g': 'new_m', 'new_m_od_w_in': 'new_m', 'new_m_od_rel_bias': 'new_m', 'new_m_od_w_out': 'new_m', 'new_m_xa_norm_g': 'new_m', 'new_m_xa_w_qg': 'new_m', 'new_m_xa_w_kv': 'new_m', 'new_m_xa_w_o': 'new_m', 'new_m_final_norm_g': 'new_m', 'new_v_mem_norm_g': 'new_v', 'new_v_ev_norm_g': 'new_v', 'new_v_ev_w_in': 'new_v', 'new_v_ev_s5_lambda_re': 'new_v', 'new_v_ev_s5_lambda_im': 'new_v', 'new_v_ev_s5_log_dt': 'new_v', 'new_v_ev_s5_b_re': 'new_v', 'new_v_ev_s5_b_im': 'new_v', 'new_v_ev_s5_c_re': 'new_v', 'new_v_ev_s5_c_im': 'new_v', 'new_v_ev_s5_d': 'new_v', 'new_v_ev_s5_glu_w': 'new_v', 'new_v_ev_s5_glu_b': 'new_v', 'new_v_ev_conv_w': 'new_v', 'new_v_ev_conv_b': 'new_v', 'new_v_ev_conv_ln_g': 'new_v', 'new_v_ev_conv_ln_b': 'new_v', 'new_v_ev_w_out': 'new_v', 'new_v_od_norm_g': 'new_v', 'new_v_od_w_in': 'new_v', 'new_v_od_rel_bias': 'new_v', 'new_v_od_w_out': 'new_v', 'new_v_xa_norm_g': 'new_v', 'new_v_xa_w_qg': 'new_v', 'new_v_xa_w_kv': 'new_v', 'new_v_xa_w_o': 'new_v', 'new_v_final_norm_g': 'new_v'}


def _forward(args):
    return _fwd_reference(*[args[k] for k in FWD_PARAMS])


def _output_shape():
    def fwd():
        inp = _fwd_setup_inputs(0)
        return _fwd_reference(*[inp[k] for k in FWD_PARAMS])
    out = _jax.eval_shape(fwd)
    return out.shape, out.dtype

N_MICROBATCH = 1
ADAM_LR = 0.001
ADAM_B1 = 0.9
ADAM_B2 = 0.999
ADAM_EPS = 1e-08
ADAM_WD = 0.01
ADAM_STEP = 10
PER_EXAMPLE_BATCH_AXIS = {'x': 0, 'mem': 0, 'loss_target': 0}
SHARED_INPUTS = []
_WEIGHT_DTYPES = {'mem_norm_g': _jnp.float32, 'ev_norm_g': _jnp.float32, 'ev_w_in': _jnp.float32, 'ev_s5_lambda_re': _jnp.float32, 'ev_s5_lambda_im': _jnp.float32, 'ev_s5_log_dt': _jnp.float32, 'ev_s5_b_re': _jnp.float32, 'ev_s5_b_im': _jnp.float32, 'ev_s5_c_re': _jnp.float32, 'ev_s5_c_im': _jnp.float32, 'ev_s5_d': _jnp.float32, 'ev_s5_glu_w': _jnp.float32, 'ev_s5_glu_b': _jnp.float32, 'ev_conv_w': _jnp.float32, 'ev_conv_b': _jnp.float32, 'ev_conv_ln_g': _jnp.float32, 'ev_conv_ln_b': _jnp.float32, 'ev_w_out': _jnp.float32, 'od_norm_g': _jnp.float32, 'od_w_in': _jnp.float32, 'od_rel_bias': _jnp.float32, 'od_w_out': _jnp.float32, 'xa_norm_g': _jnp.float32, 'xa_w_qg': _jnp.float32, 'xa_w_kv': _jnp.float32, 'xa_w_o': _jnp.float32, 'final_norm_g': _jnp.float32}
MOMENT_SCALE = {'mem_norm_g': 3.416315e-02, 'ev_norm_g': 1.061319e-01, 'ev_w_in': 6.786721e-02, 'ev_s5_lambda_re': 4.558937e-03, 'ev_s5_lambda_im': 4.503302e-03, 'ev_s5_log_dt': 2.660678e+00, 'ev_s5_b_re': 2.447005e-03, 'ev_s5_b_im': 2.500464e-03, 'ev_s5_c_re': 3.525304e-03, 'ev_s5_c_im': 3.589881e-03, 'ev_s5_d': 5.399586e-02, 'ev_s5_glu_w': 1.421924e-02, 'ev_s5_glu_b': 2.248136e-02, 'ev_conv_w': 9.015025e-02, 'ev_conv_b': 1.964338e-01, 'ev_conv_ln_g': 1.060205e-01, 'ev_conv_ln_b': 9.230008e-02, 'ev_w_out': 7.167856e-02, 'od_norm_g': 4.267540e-02, 'od_w_in': 2.140684e-02, 'od_rel_bias': 9.636745e-03, 'od_w_out': 2.208684e-02, 'xa_norm_g': 2.232295e-02, 'xa_w_qg': 1.541828e-02, 'xa_w_kv': 1.520858e-02, 'xa_w_o': 1.532816e-02, 'final_norm_g': 6.398363e+01}


def _to_microbatches(a, axis):
    t = _jnp.moveaxis(a, axis, 0)
    t = t.reshape((N_MICROBATCH, t.shape[0] // N_MICROBATCH) + t.shape[1:])
    return _jnp.moveaxis(t, 1, axis + 1)


def setup_inputs(seed: int = 0) -> dict:
    inp = _fwd_setup_inputs(seed)
    key = _jax.random.fold_in(_jax.random.key(seed), 7919)
    shape, _ = _output_shape()
    out = dict(inp)
    out["loss_target"] = _jax.random.normal(_jax.random.fold_in(key, 0), shape, _jnp.float32)
    for i, name in enumerate(TWIN_WEIGHTS):
        w = inp[name].astype(_jnp.float32)
        if MOMENT_SCALE is None:
            s = _jnp.sqrt(_jnp.mean(_jnp.square(w)) + 1e-30)
        else:
            s = MOMENT_SCALE[name]
        km, kv = _jax.random.split(_jax.random.fold_in(key, i + 1))
        out[name] = w
        out["m_" + name] = s * _jax.random.normal(km, w.shape, _jnp.float32)
        out["v_" + name] = (s * s) * _jax.random.uniform(kv, w.shape, _jnp.float32, 0.5, 1.5)
    if N_MICROBATCH > 1:
        for name, axis in PER_EXAMPLE_BATCH_AXIS.items():
            out[name] = _to_microbatches(out[name], axis)
    return {'x': out['x'], 'mem': out['mem'], 'mem_norm_g': out['mem_norm_g'], 'ev_norm_g': out['ev_norm_g'], 'ev_w_in': out['ev_w_in'], 'ev_s5_lambda_re': out['ev_s5_lambda_re'], 'ev_s5_lambda_im': out['ev_s5_lambda_im'], 'ev_s5_log_dt': out['ev_s5_log_dt'], 'ev_s5_b_re': out['ev_s5_b_re'], 'ev_s5_b_im': out['ev_s5_b_im'], 'ev_s5_c_re': out['ev_s5_c_re'], 'ev_s5_c_im': out['ev_s5_c_im'], 'ev_s5_d': out['ev_s5_d'], 'ev_s5_glu_w': out['ev_s5_glu_w'], 'ev_s5_glu_b': out['ev_s5_glu_b'], 'ev_conv_w': out['ev_conv_w'], 'ev_conv_b': out['ev_conv_b'], 'ev_conv_ln_g': out['ev_conv_ln_g'], 'ev_conv_ln_b': out['ev_conv_ln_b'], 'ev_w_out': out['ev_w_out'], 'od_norm_g': out['od_norm_g'], 'od_w_in': out['od_w_in'], 'od_rel_bias': out['od_rel_bias'], 'od_w_out': out['od_w_out'], 'xa_norm_g': out['xa_norm_g'], 'xa_w_qg': out['xa_w_qg'], 'xa_w_kv': out['xa_w_kv'], 'xa_w_o': out['xa_w_o'], 'final_norm_g': out['final_norm_g'], 'loss_target': out['loss_target'], 'm_mem_norm_g': out['m_mem_norm_g'], 'm_ev_norm_g': out['m_ev_norm_g'], 'm_ev_w_in': out['m_ev_w_in'], 'm_ev_s5_lambda_re': out['m_ev_s5_lambda_re'], 'm_ev_s5_lambda_im': out['m_ev_s5_lambda_im'], 'm_ev_s5_log_dt': out['m_ev_s5_log_dt'], 'm_ev_s5_b_re': out['m_ev_s5_b_re'], 'm_ev_s5_b_im': out['m_ev_s5_b_im'], 'm_ev_s5_c_re': out['m_ev_s5_c_re'], 'm_ev_s5_c_im': out['m_ev_s5_c_im'], 'm_ev_s5_d': out['m_ev_s5_d'], 'm_ev_s5_glu_w': out['m_ev_s5_glu_w'], 'm_ev_s5_glu_b': out['m_ev_s5_glu_b'], 'm_ev_conv_w': out['m_ev_conv_w'], 'm_ev_conv_b': out['m_ev_conv_b'], 'm_ev_conv_ln_g': out['m_ev_conv_ln_g'], 'm_ev_conv_ln_b': out['m_ev_conv_ln_b'], 'm_ev_w_out': out['m_ev_w_out'], 'm_od_norm_g': out['m_od_norm_g'], 'm_od_w_in': out['m_od_w_in'], 'm_od_rel_bias': out['m_od_rel_bias'], 'm_od_w_out': out['m_od_w_out'], 'm_xa_norm_g': out['m_xa_norm_g'], 'm_xa_w_qg': out['m_xa_w_qg'], 'm_xa_w_kv': out['m_xa_w_kv'], 'm_xa_w_o': out['m_xa_w_o'], 'm_final_norm_g': out['m_final_norm_g'], 'v_mem_norm_g': out['v_mem_norm_g'], 'v_ev_norm_g': out['v_ev_norm_g'], 'v_ev_w_in': out['v_ev_w_in'], 'v_ev_s5_lambda_re': out['v_ev_s5_lambda_re'], 'v_ev_s5_lambda_im': out['v_ev_s5_lambda_im'], 'v_ev_s5_log_dt': out['v_ev_s5_log_dt'], 'v_ev_s5_b_re': out['v_ev_s5_b_re'], 'v_ev_s5_b_im': out['v_ev_s5_b_im'], 'v_ev_s5_c_re': out['v_ev_s5_c_re'], 'v_ev_s5_c_im': out['v_ev_s5_c_im'], 'v_ev_s5_d': out['v_ev_s5_d'], 'v_ev_s5_glu_w': out['v_ev_s5_glu_w'], 'v_ev_s5_glu_b': out['v_ev_s5_glu_b'], 'v_ev_conv_w': out['v_ev_conv_w'], 'v_ev_conv_b': out['v_ev_conv_b'], 'v_ev_conv_ln_g': out['v_ev_conv_ln_g'], 'v_ev_conv_ln_b': out['v_ev_conv_ln_b'], 'v_ev_w_out': out['v_ev_w_out'], 'v_od_norm_g': out['v_od_norm_g'], 'v_od_w_in': out['v_od_w_in'], 'v_od_rel_bias': out['v_od_rel_bias'], 'v_od_w_out': out['v_od_w_out'], 'v_xa_norm_g': out['v_xa_norm_g'], 'v_xa_w_qg': out['v_xa_w_qg'], 'v_xa_w_kv': out['v_xa_w_kv'], 'v_xa_w_o': out['v_xa_w_o'], 'v_final_norm_g': out['v_final_norm_g']}


def _loss(weights, diff, rest, loss_target):
    with _jax.named_scope("forward"):
        args = {**rest, TWIN_DIFF_INPUT: diff, **{k: w.astype(_WEIGHT_DTYPES[k]) for k, w in weights.items()}}
        y = _forward(args)
    with _jax.named_scope("loss_head"):
        err = _jnp.square(y.astype(_jnp.float32) - loss_target)
        return 0.5 * _jnp.sum(_jnp.mean(err, axis=-1)) if err.ndim else 0.5 * err


def _adamw(w, g, m, v):
    m = ADAM_B1 * m + (1.0 - ADAM_B1) * g
    v = ADAM_B2 * v + (1.0 - ADAM_B2) * _jnp.square(g)
    m_hat = m / (1.0 - ADAM_B1 ** ADAM_STEP)
    v_hat = v / (1.0 - ADAM_B2 ** ADAM_STEP)
    delta = -ADAM_LR * (m_hat / (_jnp.sqrt(v_hat) + ADAM_EPS) + ADAM_WD * w)
    return delta, m, v


def reference(x, mem, mem_norm_g, ev_norm_g, ev_w_in, ev_s5_lambda_re, ev_s5_lambda_im, ev_s5_log_dt, ev_s5_b_re, ev_s5_b_im, ev_s5_c_re, ev_s5_c_im, ev_s5_d, ev_s5_glu_w, ev_s5_glu_b, ev_conv_w, ev_conv_b, ev_conv_ln_g, ev_conv_ln_b, ev_w_out, od_norm_g, od_w_in, od_rel_bias, od_w_out, xa_norm_g, xa_w_qg, xa_w_kv, xa_w_o, final_norm_g, loss_target, m_mem_norm_g, m_ev_norm_g, m_ev_w_in, m_ev_s5_lambda_re, m_ev_s5_lambda_im, m_ev_s5_log_dt, m_ev_s5_b_re, m_ev_s5_b_im, m_ev_s5_c_re, m_ev_s5_c_im, m_ev_s5_d, m_ev_s5_glu_w, m_ev_s5_glu_b, m_ev_conv_w, m_ev_conv_b, m_ev_conv_ln_g, m_ev_conv_ln_b, m_ev_w_out, m_od_norm_g, m_od_w_in, m_od_rel_bias, m_od_w_out, m_xa_norm_g, m_xa_w_qg, m_xa_w_kv, m_xa_w_o, m_final_norm_g, v_mem_norm_g, v_ev_norm_g, v_ev_w_in, v_ev_s5_lambda_re, v_ev_s5_lambda_im, v_ev_s5_log_dt, v_ev_s5_b_re, v_ev_s5_b_im, v_ev_s5_c_re, v_ev_s5_c_im, v_ev_s5_d, v_ev_s5_glu_w, v_ev_s5_glu_b, v_ev_conv_w, v_ev_conv_b, v_ev_conv_ln_g, v_ev_conv_ln_b, v_ev_w_out, v_od_norm_g, v_od_w_in, v_od_rel_bias, v_od_w_out, v_xa_norm_g, v_xa_w_qg, v_xa_w_kv, v_xa_w_o, v_final_norm_g):
    given = dict(x=x, mem=mem, mem_norm_g=mem_norm_g, ev_norm_g=ev_norm_g, ev_w_in=ev_w_in, ev_s5_lambda_re=ev_s5_lambda_re, ev_s5_lambda_im=ev_s5_lambda_im, ev_s5_log_dt=ev_s5_log_dt, ev_s5_b_re=ev_s5_b_re, ev_s5_b_im=ev_s5_b_im, ev_s5_c_re=ev_s5_c_re, ev_s5_c_im=ev_s5_c_im, ev_s5_d=ev_s5_d, ev_s5_glu_w=ev_s5_glu_w, ev_s5_glu_b=ev_s5_glu_b, ev_conv_w=ev_conv_w, ev_conv_b=ev_conv_b, ev_conv_ln_g=ev_conv_ln_g, ev_conv_ln_b=ev_conv_ln_b, ev_w_out=ev_w_out, od_norm_g=od_norm_g, od_w_in=od_w_in, od_rel_bias=od_rel_bias, od_w_out=od_w_out, xa_norm_g=xa_norm_g, xa_w_qg=xa_w_qg, xa_w_kv=xa_w_kv, xa_w_o=xa_w_o, final_norm_g=final_norm_g, loss_target=loss_target, m_mem_norm_g=m_mem_norm_g, m_ev_norm_g=m_ev_norm_g, m_ev_w_in=m_ev_w_in, m_ev_s5_lambda_re=m_ev_s5_lambda_re, m_ev_s5_lambda_im=m_ev_s5_lambda_im, m_ev_s5_log_dt=m_ev_s5_log_dt, m_ev_s5_b_re=m_ev_s5_b_re, m_ev_s5_b_im=m_ev_s5_b_im, m_ev_s5_c_re=m_ev_s5_c_re, m_ev_s5_c_im=m_ev_s5_c_im, m_ev_s5_d=m_ev_s5_d, m_ev_s5_glu_w=m_ev_s5_glu_w, m_ev_s5_glu_b=m_ev_s5_glu_b, m_ev_conv_w=m_ev_conv_w, m_ev_conv_b=m_ev_conv_b, m_ev_conv_ln_g=m_ev_conv_ln_g, m_ev_conv_ln_b=m_ev_conv_ln_b, m_ev_w_out=m_ev_w_out, m_od_norm_g=m_od_norm_g, m_od_w_in=m_od_w_in, m_od_rel_bias=m_od_rel_bias, m_od_w_out=m_od_w_out, m_xa_norm_g=m_xa_norm_g, m_xa_w_qg=m_xa_w_qg, m_xa_w_kv=m_xa_w_kv, m_xa_w_o=m_xa_w_o, m_final_norm_g=m_final_norm_g, v_mem_norm_g=v_mem_norm_g, v_ev_norm_g=v_ev_norm_g, v_ev_w_in=v_ev_w_in, v_ev_s5_lambda_re=v_ev_s5_lambda_re, v_ev_s5_lambda_im=v_ev_s5_lambda_im, v_ev_s5_log_dt=v_ev_s5_log_dt, v_ev_s5_b_re=v_ev_s5_b_re, v_ev_s5_b_im=v_ev_s5_b_im, v_ev_s5_c_re=v_ev_s5_c_re, v_ev_s5_c_im=v_ev_s5_c_im, v_ev_s5_d=v_ev_s5_d, v_ev_s5_glu_w=v_ev_s5_glu_w, v_ev_s5_glu_b=v_ev_s5_glu_b, v_ev_conv_w=v_ev_conv_w, v_ev_conv_b=v_ev_conv_b, v_ev_conv_ln_g=v_ev_conv_ln_g, v_ev_conv_ln_b=v_ev_conv_ln_b, v_ev_w_out=v_ev_w_out, v_od_norm_g=v_od_norm_g, v_od_w_in=v_od_w_in, v_od_rel_bias=v_od_rel_bias, v_od_w_out=v_od_w_out, v_xa_norm_g=v_xa_norm_g, v_xa_w_qg=v_xa_w_qg, v_xa_w_kv=v_xa_w_kv, v_xa_w_o=v_xa_w_o, v_final_norm_g=v_final_norm_g)
    weights = {n: given[n] for n in TWIN_WEIGHTS}
    shared = {n: given[n] for n in SHARED_INPUTS}
    per_example = {n: given[n] for n in ['x', 'mem']}
    grad_fn = _jax.value_and_grad(_loss, argnums=(0, 1))

    def one_microbatch(ex, loss_target):
        ex = dict(ex)
        diff = ex.pop(TWIN_DIFF_INPUT)
        return grad_fn(weights, diff, {**shared, **ex}, loss_target)

    if N_MICROBATCH == 1:
        loss, (grad_w, grad_x) = one_microbatch(per_example, given["loss_target"])
    else:
        def body(carry, xs):
            loss_sum, grad_sum = carry
            l_k, (gw_k, gx_k) = one_microbatch(xs[0], xs[1])
            with _jax.named_scope("update"):
                return (loss_sum + l_k, _jax.tree.map(_jnp.add, grad_sum, gw_k)), gx_k

        init = (_jnp.zeros((), _jnp.float32), _jax.tree.map(_jnp.zeros_like, weights))
        (loss, grad_w), grad_x = _jax.lax.scan(body, init, (per_example, given["loss_target"]))
    with _jax.named_scope("update"):
        delta_w, new_m, new_v = {}, {}, {}
        for n in TWIN_WEIGHTS:
            delta_w[n], new_m[n], new_v[n] = _adamw(weights[n], grad_w[n], given["m_" + n], given["v_" + n])
    return (loss, grad_x, *[grad_w[n] for n in TWIN_WEIGHTS], *[delta_w[n] for n in TWIN_WEIGHTS],
            *[new_m[n] for n in TWIN_WEIGHTS], *[new_v[n] for n in TWIN_WEIGHTS])
```

```python
import functools

import jax
import jax.numpy as jnp
import numpy as np
from jax import lax
from jax.experimental import pallas as pl
from jax.experimental.pallas import tpu as pltpu

f32, bf16 = jnp.float32, jnp.bfloat16
SDS = jax.ShapeDtypeStruct

D = 1024
SW = 512
NG, GS, NS = 32, 16, 64
NST = NG * NS
CW = 512
CK = 31
EVEN_IN = 2 * SW + 3 * CW
HEADS, DH = 16, 64
CHUNK, LEFT = 64, 8
PAD = LEFT * CHUNK
MAX_REL = 128
MEM_LEN = 256
XH, XD = 4, 256
EPS = 1e-6
NDEV = 8
AXES = ("x", "y", "c")

ADAM_LR, ADAM_B1, ADAM_B2, ADAM_EPS, ADAM_WD, ADAM_STEP = 0.001, 0.9, 0.999, 1e-08, 0.01, 10

VMEM_LIMIT = 56 << 20
S5_T = 512
S5_CH = 512
AQ = 256
AW = AQ + PAD

NN = ((1,), (0,))
NT = ((1,), (1,))
TN = ((0,), (0,))


def _dot(a, b, dims):
    return lax.dot_general(a.astype(bf16), b.astype(bf16), (dims, ((), ())), preferred_element_type=f32)


@jax.custom_vjp
def _mm(a, w):
    return _dot(a, w, NN)


def _mm_f(a, w):
    return _dot(a, w, NN), w


def _mm_b(w, g):
    return _dot(g, w, NT), None


_mm.defvjp(_mm_f, _mm_b)


@jax.custom_vjp
def _mm_nt(a, b):
    return _dot(a, b, NT)


def _mm_nt_f(a, b):
    return _dot(a, b, NT), (a, b)


def _mm_nt_b(res, g):
    a, b = res
    return _dot(g, b, NN), _dot(g, a, TN)


_mm_nt.defvjp(_mm_nt_f, _mm_nt_b)


@jax.custom_vjp
def _mm_nn(a, b):
    return _dot(a, b, NN)


def _mm_nn_f(a, b):
    return _dot(a, b, NN), (a, b)


def _mm_nn_b(res, g):
    a, b = res
    return _dot(g, b, NT), _dot(a, g, TN)


_mm_nn.defvjp(_mm_nn_f, _mm_nn_b)


def _plain_mm(a, w):
    return _dot(a, w, NN)


def _plain_nt(a, b):
    return _dot(a, b, NT)


def _rms(x, g):
    return x * lax.rsqrt(jnp.mean(x * x, axis=-1, keepdims=True) + EPS) * g


def _cp(sem=("arbitrary",)):
    return pltpu.CompilerParams(dimension_semantics=sem, vmem_limit_bytes=VMEM_LIMIT)


def _row(tm, w, col=0):
    return pl.BlockSpec((tm, w), lambda i: (i, col))


def _full(shape):
    nd = len(shape)
    return pl.BlockSpec(tuple(shape), lambda i: (0,) * nd, pipeline_mode=pl.Buffered(1))


def _tiled(name, body, L, tm, row_ins, full_ins, row_outs, acc_outs=()):
    nr, nf, no = len(row_ins), len(full_ins), len(row_outs)

    def kern(*refs):
        rin, fin = refs[:nr], refs[nr:nr + nf]
        rout, aout = refs[nr + nf:nr + nf + no], refs[nr + nf + no:]
        routs, accs = body([r[...] for r in rin], fin)
        for r, v in zip(rout, routs):
            r[...] = v.astype(r.dtype)
        if aout:
            @pl.when(pl.program_id(0) == 0)
            def _():
                for a in aout:
                    a[...] = jnp.zeros(a.shape, a.dtype)

            for a, v in zip(aout, accs):
                a[...] += v.astype(a.dtype)

    return pl.pallas_call(
        kern, name=name, grid=(L // tm,),
        in_specs=[_row(tm, w, c) for (_, w, c) in row_ins] + [_full(a.shape) for a in full_ins],
        out_specs=[_row(tm, w) for (w, _) in row_outs] + [_full(s) for (s, _) in acc_outs],
        out_shape=[SDS((L, w), dt) for (w, dt) in row_outs] + [SDS(tuple(s), dt) for (s, dt) in acc_outs],
        compiler_params=_cp(),
    )(*[a for (a, _, _) in row_ins], *full_ins)


def _norm_mm_fwd(name, x, g, w, tm):
    def body(rows, fulls):
        g_ref, w_ref = fulls
        return [_dot(_rms(rows[0], g_ref[...]), w_ref[...], NN)], []

    return _tiled(name, body, x.shape[0], tm, [(x, D, 0)], [g, w], [(w.shape[1], f32)])[0]


def _norm_mm_bwd(name, x, g, w, dz_parts, dres, tm):
    n = len(dz_parts)

    def body(rows, fulls):
        g_ref, w_ref = fulls
        dz = rows[1] if n == 1 else jnp.concatenate(rows[1:1 + n], axis=1)
        h, vjp = jax.vjp(_rms, rows[0], g_ref[...])
        dx, dg = vjp(_dot(dz, w_ref[...], NT))
        return [dx + rows[1 + n]], [dg, _dot(h, dz, TN)]

    return _tiled(name, body, x.shape[0], tm, [(x, D, 0), *dz_parts, (dres, D, 0)], [g, w],
                  [(D, f32)], [((1, D), f32), (w.shape, f32)])


def _scan_tables(tab_ref, ar, ai, reverse):
    pw = [(ar, ai)]
    for _ in range(7):
        pr, pi = pw[-1]
        pw.append((pr * ar - pi * ai, pr * ai + pi * ar))
    row = lax.broadcasted_iota(jnp.int32, (8, NST), 0)
    for si, s in enumerate((1, 2, 4)):
        keep = (row <= 7 - s) if reverse else (row >= s)
        tab_ref[2 * si] = jnp.where(keep, pw[s - 1][0], 0.0)
        tab_ref[2 * si + 1] = jnp.where(keep, pw[s - 1][1], 0.0)
    pr = jnp.zeros((8, NST), f32)
    pi = jnp.zeros((8, NST), f32)
    for t in range(8):
        k = (7 - t) if reverse else t
        pr = jnp.where(row == t, pw[k][0], pr)
        pi = jnp.where(row == t, pw[k][1], pi)
    tab_ref[6] = pr
    tab_ref[7] = pi


def _scan(re_ref, im_ref, row0, nrows, tab_ref, car_re, car_im, reverse, extra=None):
    nt = nrows // 8
    accs = []
    for cc in range(NST // S5_CH):
        cols = slice(cc * S5_CH, (cc + 1) * S5_CH)

        def step(j, carry, cols=cols):
            cr, ci, acc = carry
            jj = (nt - 1 - j) if reverse else j
            r0 = pl.multiple_of(row0 + jj * 8, 8)
            xr = re_ref[pl.ds(r0, 8), cols]
            xi = im_ref[pl.ds(r0, 8), cols]
            for si, s in enumerate((1, 2, 4)):
                sh = (8 - s) if reverse else s
                yr = pltpu.roll(xr, sh, 0)
                yi = pltpu.roll(xi, sh, 0)
                tr = tab_ref[2 * si, :, cols]
                ti = tab_ref[2 * si + 1, :, cols]
                xr, xi = xr + tr * yr - ti * yi, xi + tr * yi + ti * yr
            tr = tab_ref[6, :, cols]
            ti = tab_ref[7, :, cols]
            xr, xi = xr + tr * cr - ti * ci, xi + tr * ci + ti * cr
            re_ref[pl.ds(r0, 8), cols] = xr
            im_ref[pl.ds(r0, 8), cols] = xi
            if extra is not None:
                acc = extra(cols, jj, xr, xi, acc)
            b = 0 if reverse else 7
            return (jnp.broadcast_to(xr[b:b + 1, :], (8, S5_CH)), jnp.broadcast_to(xi[b:b + 1, :], (8, S5_CH)), acc)

        acc0 = (jnp.zeros((8, S5_CH), f32), jnp.zeros((8, S5_CH), f32)) if extra is not None else 0
        cr, ci, acc = lax.fori_loop(0, nt, step, (car_re[:, cols], car_im[:, cols], acc0))
        car_re[:, cols] = cr
        car_im[:, cols] = ci
        accs.append(acc)
    return accs


def _s5_fwd(z, bmat, cmat, a_re, a_im, d_skip):
    L = z.shape[0]
    nb = L // S5_T

    def kern(u_ref, bre, bim, cre, cim, ar_ref, ai_ref, d_ref, y_ref, hbr_ref, hbi_ref, h_re, h_im, tab, car_re, car_im):
        @pl.when(pl.program_id(0) == 0)
        def _():
            _scan_tables(tab, ar_ref[...], ai_ref[...], False)
            car_re[...] = jnp.zeros_like(car_re)
            car_im[...] = jnp.zeros_like(car_im)

        hbr_ref[...] = car_re[...]
        hbi_ref[...] = car_im[...]
        u = u_ref[...]
        for hf in range(2):
            uh = u[:, hf * 256:(hf + 1) * 256]
            h_re[:, hf * 1024:(hf + 1) * 1024] = _dot(uh, bre[hf], NN)
            h_im[:, hf * 1024:(hf + 1) * 1024] = _dot(uh, bim[hf], NN)
        _scan(h_re, h_im, 0, S5_T, tab, car_re, car_im, False)
        ys = []
        for hf in range(2):
            cs = slice(hf * 1024, (hf + 1) * 1024)
            ys.append(_dot(h_re[:, cs], cre[hf], NT) - _dot(h_im[:, cs], cim[hf], NT))
        y_ref[...] = jnp.concatenate(ys, axis=1) + d_ref[...] * u

    fulls = [*bmat, *cmat, a_re, a_im, d_skip]
    return pl.pallas_call(
        kern, name="s5_fwd", grid=(nb,),
        in_specs=[_row(S5_T, SW, 0)] + [_full(a.shape) for a in fulls],
        out_specs=[_row(S5_T, SW), _row(8, NST), _row(8, NST)],
        out_shape=[SDS((L, SW), f32), SDS((nb * 8, NST), f32), SDS((nb * 8, NST), f32)],
        scratch_shapes=[pltpu.VMEM((S5_T, NST), f32), pltpu.VMEM((S5_T, NST), f32), pltpu.VMEM((8, 8, NST), f32),
                        pltpu.VMEM((8, NST), f32), pltpu.VMEM((8, NST), f32)],
        compiler_params=_cp(),
    )(z, *fulls)


def _s5_bwd(z, dy, hb_re, hb_im, bmat, cmat, a_re, a_im, d_skip):
    L = z.shape[0]
    nb = L // S5_T
    T = S5_T

    def kern(u_ref, dy_ref, hbr_ref, hbi_ref, bre, bim, cre, cim, ar_ref, ai_ref, d_ref,
             du_ref, dbre, dbim, dcre, dcim, dar, dai, dd,
             h_re, h_im, g_re, g_im, tabf, tabr, car_re, car_im, gcar_re, gcar_im):
        @pl.when(pl.program_id(0) == 0)
        def _():
            _scan_tables(tabf, ar_ref[...], ai_ref[...], False)
            _scan_tables(tabr, ar_ref[...], -ai_ref[...], True)
            gcar_re[...] = jnp.zeros_like(gcar_re)
            gcar_im[...] = jnp.zeros_like(gcar_im)
            for r in (dbre, dbim, dcre, dcim, dar, dai, dd):
                r[...] = jnp.zeros_like(r)

        u = u_ref[...]
        dyv = dy_ref[...]
        h_re[0:8, :] = hbr_ref[...]
        h_im[0:8, :] = hbi_ref[...]
        car_re[...] = hbr_ref[...]
        car_im[...] = hbi_ref[...]
        for hf in range(2):
            cs = slice(hf * 1024, (hf + 1) * 1024)
            uh = u[:, hf * 256:(hf + 1) * 256]
            dyh = dyv[:, hf * 256:(hf + 1) * 256]
            h_re[8:, cs] = _dot(uh, bre[hf], NN)
            h_im[8:, cs] = _dot(uh, bim[hf], NN)
            g_re[:, cs] = _dot(dyh, cre[hf], NN)
            g_im[:, cs] = -_dot(dyh, cim[hf], NN)
        _scan(h_re, h_im, 8, T, tabf, car_re, car_im, False)

        row = lax.broadcasted_iota(jnp.int32, (8, S5_CH), 0)

        def fold_da(cols, jj, gr, gi, acc):
            r0 = pl.multiple_of(jj * 8, 8)
            hr = jnp.where(row == 0, jnp.broadcast_to(h_re[pl.ds(r0, 8), cols][7:8, :], (8, S5_CH)),
                           pltpu.roll(h_re[pl.ds(r0 + 8, 8), cols], 1, 0))
            hi = jnp.where(row == 0, jnp.broadcast_to(h_im[pl.ds(r0, 8), cols][7:8, :], (8, S5_CH)),
                           pltpu.roll(h_im[pl.ds(r0 + 8, 8), cols], 1, 0))
            return acc[0] + gr * hr + gi * hi, acc[1] + gi * hr - gr * hi

        accs = _scan(g_re, g_im, 0, T, tabr, gcar_re, gcar_im, True, extra=fold_da)
        for cc, (ar_, ai_) in enumerate(accs):
            cols = slice(cc * S5_CH, (cc + 1) * S5_CH)
            dar[:, cols] += jnp.sum(ar_, axis=0, keepdims=True)
            dai[:, cols] += jnp.sum(ai_, axis=0, keepdims=True)

        dus = []
        for hf in range(2):
            cs = slice(hf * 1024, (hf + 1) * 1024)
            uh = u[:, hf * 256:(hf + 1) * 256]
            dyh = dyv[:, hf * 256:(hf + 1) * 256]
            gr = g_re[:, cs]
            gi = g_im[:, cs]
            dus.append(_dot(gr, bre[hf], NT) + _dot(gi, bim[hf], NT))
            dbre[hf] += _dot(uh, gr, TN)
            dbim[hf] += _dot(uh, gi, TN)
            dcre[hf] += _dot(dyh, h_re[8:, cs], TN)
            dcim[hf] -= _dot(dyh, h_im[8:, cs], TN)
        du_ref[...] = jnp.concatenate(dus, axis=1) + d_ref[...] * dyv
        dd[...] += jnp.sum(dyv * u, axis=0, keepdims=True)

    rev = lambda w: pl.BlockSpec((T, w), lambda k: (nb - 1 - k, 0))
    rev8 = pl.BlockSpec((8, NST), lambda k: (nb - 1 - k, 0))
    fulls = [*bmat, *cmat, a_re, a_im, d_skip]
    mat = ((2, 256, 1024), f32)
    accs = [mat, mat, mat, mat, ((1, NST), f32), ((1, NST), f32), ((1, SW), f32)]
    return pl.pallas_call(
        kern, name="s5_bwd", grid=(nb,),
        in_specs=[rev(SW), rev(SW), rev8, rev8] + [_full(a.shape) for a in fulls],
        out_specs=[rev(SW)] + [_full(s) for (s, _) in accs],
        out_shape=[SDS((L, SW), f32)] + [SDS(s, dt) for (s, dt) in accs],
        scratch_shapes=[pltpu.VMEM((T + 8, NST), f32), pltpu.VMEM((T + 8, NST), f32),
                        pltpu.VMEM((T, NST), f32), pltpu.VMEM((T, NST), f32),
                        pltpu.VMEM((8, 8, NST), f32), pltpu.VMEM((8, 8, NST), f32),
                        pltpu.VMEM((8, NST), f32), pltpu.VMEM((8, NST), f32),
                        pltpu.VMEM((8, NST), f32), pltpu.VMEM((8, NST), f32)],
        compiler_params=_cp(),
    )(z, dy, hb_re, hb_im, *fulls)


def _s5_disc(lam_re, lam_im, log_dt, bt_re, bt_im):
    dt = jnp.exp(log_dt)
    mag = jnp.exp(lam_re * dt)
    ab_re = mag * jnp.cos(lam_im * dt)
    ab_im = mag * jnp.sin(lam_im * dt)
    den = lam_re * lam_re + lam_im * lam_im
    nr = ab_re - 1.0
    coef_re = (nr * lam_re + ab_im * lam_im) / den
    coef_im = (ab_im * lam_re - nr * lam_im) / den
    rep = lambda c: jnp.broadcast_to(c[:, None, :], (NG, GS, NS)).reshape(NG * GS, NS)
    cr, ci = rep(coef_re), rep(coef_im)
    return ab_re, ab_im, cr * bt_re - ci * bt_im, cr * bt_im + ci * bt_re


def _s5_disc_fwd(*params):
    def kern(*refs):
        outs = _s5_disc(*[r[...] for r in refs[:5]])
        for r, v in zip(refs[5:], outs):
            r[...] = v

    shapes = [SDS((NG, NS), f32), SDS((NG, NS), f32), SDS((NG * GS, NS), f32), SDS((NG * GS, NS), f32)]
    return pl.pallas_call(kern, name="s5_disc_fwd", out_shape=shapes)(*params)


def _s5_disc_bwd(params, cts):
    def kern(*refs):
        _, vjp = jax.vjp(_s5_disc, *[r[...] for r in refs[:5]])
        for r, v in zip(refs[9:], vjp(tuple(r[...] for r in refs[5:9]))):
            r[...] = v

    return pl.pallas_call(kern, name="s5_disc_bwd", out_shape=[SDS(p.shape, f32) for p in params])(*params, *cts)


def _diag_mask():
    r = np.arange(256)[:, None] // GS
    c = np.arange(1024)[None, :] // NS
    return jnp.asarray(r == c, f32)


def _to_blockdiag(m):
    return jnp.tile(m.reshape(2, 256, NS), (1, 1, 16)) * _diag_mask()


def _from_blockdiag(dm):
    return (dm * _diag_mask()).reshape(2, 256, 16, NS).sum(axis=2).reshape(NG * GS, NS)


def _conv_fwd(z, w32, b, tm):
    L = z.shape[0]
    hb = tm // 32

    def kern(val, glu, hval, hglu, w_ref, b_ref, c_ref, win):
        i = pl.program_id(0)
        win[0:32, :] = hval[...] * jax.nn.sigmoid(hglu[...]) * (i > 0).astype(f32)
        win[32:, :] = val[...] * jax.nn.sigmoid(glu[...])
        acc = jnp.zeros((tm, CW), f32) + b_ref[...]
        for k in range(CK):
            acc = acc + w_ref[k:k + 1, :] * win[k + 2:k + 2 + tm, :]
        c_ref[...] = acc

    halo = lambda col: pl.BlockSpec((32, CW), lambda i: (jnp.maximum(i * hb - 1, 0), col))
    return pl.pallas_call(
        kern, name="conv_fwd", grid=(L // tm,),
        in_specs=[_row(tm, CW, 2), _row(tm, CW, 3), halo(2), halo(3), _full(w32.shape), _full(b.shape)],
        out_specs=_row(tm, CW), out_shape=SDS((L, CW), f32),
        scratch_shapes=[pltpu.VMEM((tm + 32, CW), f32)], compiler_params=_cp(),
    )(z, z, z, z, w32, b)


def _conv_bwd(z, dc, w32, tm):
    L = z.shape[0]
    hb = tm // 32
    nt = L // tm

    def kern(val, glu, hval, hglu, dc_ref, hdc, w_ref, dval_ref, dglu_ref, dw_ref, db_ref, win, dwin):
        i = pl.program_id(0)

        @pl.when(i == 0)
        def _():
            dw_ref[...] = jnp.zeros_like(dw_ref)
            db_ref[...] = jnp.zeros_like(db_ref)

        sg = jax.nn.sigmoid(glu[...])
        win[0:32, :] = hval[...] * jax.nn.sigmoid(hglu[...]) * (i > 0).astype(f32)
        win[32:, :] = val[...] * sg
        dcv = dc_ref[...]
        dwin[0:tm, :] = dcv
        dwin[tm:, :] = hdc[...] * (i < nt - 1).astype(f32)
        dv = jnp.zeros((tm, CW), f32)
        for k in range(CK):
            dv = dv + w_ref[k:k + 1, :] * dwin[30 - k:30 - k + tm, :]
            dw_ref[k:k + 1, :] += jnp.sum(dcv * win[k + 2:k + 2 + tm, :], axis=0, keepdims=True)
        db_ref[...] += jnp.sum(dcv, axis=0, keepdims=True)
        dval_ref[...] = dv * sg
        dglu_ref[...] = dv * val[...] * sg * (1.0 - sg)

    halo = lambda col: pl.BlockSpec((32, CW), lambda i: (jnp.maximum(i * hb - 1, 0), col))
    nxt = pl.BlockSpec((32, CW), lambda i: (jnp.minimum((i + 1) * hb, L // 32 - 1), 0))
    return pl.pallas_call(
        kern, name="conv_bwd", grid=(nt,),
        in_specs=[_row(tm, CW, 2), _row(tm, CW, 3), halo(2), halo(3), _row(tm, CW), nxt, _full(w32.shape)],
        out_specs=[_row(tm, CW), _row(tm, CW), _full((32, CW)), _full((1, CW))],
        out_shape=[SDS((L, CW), f32), SDS((L, CW), f32), SDS((32, CW), f32), SDS((1, CW), f32)],
        scratch_shapes=[pltpu.VMEM((tm + 32, CW), f32), pltpu.VMEM((tm + 32, CW), f32)], compiler_params=_cp(),
    )(z, z, z, z, dc, dc, w32)


def _ev_out_f(y, c, ga, gb, tap, glu_b, ln_g, ln_b, glu_w, w_out, mm):
    z1 = jax.nn.gelu(y)
    ya = z1 * jax.nn.sigmoid(mm(z1, glu_w) + glu_b + tap) * jax.nn.silu(ga)
    mu = jnp.mean(c, axis=-1, keepdims=True)
    var = jnp.mean(jnp.square(c - mu), axis=-1, keepdims=True)
    cn = (c - mu) * lax.rsqrt(var + EPS) * ln_g + ln_b
    cat = jnp.concatenate([ya, jax.nn.silu(cn) * jax.nn.silu(gb)], axis=1)
    return mm(cat, w_out), (z1, cat)


def _ev_out_fwd(x, y, c, z, glu_b, ln_g, ln_b, glu_w, w_out, tm):
    def body(rows, fulls):
        xv, yv, cv, ga, gb = rows
        gb_ref, lg_ref, lb_ref, gw_ref, wo_ref = fulls
        out, _ = _ev_out_f(yv, cv, ga, gb, 0.0, gb_ref[...], lg_ref[...], lb_ref[...], gw_ref[...], wo_ref[...], _plain_mm)
        return [xv + out], []

    return _tiled("ev_out_fwd", body, x.shape[0], tm, [(x, D, 0), (y, SW, 0), (c, CW, 0), (z, SW, 1), (z, CW, 4)],
                  [glu_b, ln_g, ln_b, glu_w, w_out], [(D, f32)])[0]


def _ev_out_bwd(dx1, y, c, z, glu_b, ln_g, ln_b, glu_w, w_out, tm):
    def body(rows, fulls):
        dxv, yv, cv, ga, gb = rows
        gb_ref, lg_ref, lb_ref, gw_ref, wo_ref = fulls
        gw, wo = gw_ref[...], wo_ref[...]
        f = lambda yv, cv, ga, gb, tap, b, lg, lb: _ev_out_f(yv, cv, ga, gb, tap, b, lg, lb, gw, wo, _mm)
        _, vjp, (z1, cat) = jax.vjp(f, yv, cv, ga, gb, jnp.zeros((tm, SW), f32), gb_ref[...], lg_ref[...], lb_ref[...],
                                    has_aux=True)
        dy, dc, dga, dgb, dtap, db, dlg, dlb = vjp(dxv)
        return [dy, dc, dga, dgb], [db, dlg, dlb, _dot(z1, dtap, TN), _dot(cat, dxv, TN)]

    return _tiled("ev_out_bwd", body, dx1.shape[0], tm, [(dx1, D, 0), (y, SW, 0), (c, CW, 0), (z, SW, 1), (z, CW, 4)],
                  [glu_b, ln_g, ln_b, glu_w, w_out], [(SW, f32), (CW, f32), (SW, f32), (CW, f32)],
                  [((1, SW), f32), ((1, CW), f32), ((1, CW), f32), ((SW, SW), f32), ((D, D), f32)])


def _kv_f(mem, g, tap0, tap1, w0, w1, mm):
    mn = _rms(mem, g)
    return (mm(mn, w0) + tap0, mm(mn, w1) + tap1), mn


def _kv_fwd(mem, g, w0, w1):
    def body(rows, fulls):
        g_ref, w0_ref, w1_ref = fulls
        (kv0, kv1), _ = _kv_f(rows[0], g_ref[...], 0.0, 0.0, w0_ref[...], w1_ref[...], _plain_mm)
        return [kv0, kv1], []

    return _tiled("kv_fwd", body, MEM_LEN, MEM_LEN, [(mem, D, 0)], [g, w0, w1], [(2 * D, f32), (2 * D, f32)])


def _kv_bwd(mem, g, w0, w1, dkv0, dkv1):
    def body(rows, fulls):
        g_ref, w0_ref, w1_ref = fulls
        w0v, w1v = w0_ref[...], w1_ref[...]
        tap = jnp.zeros((MEM_LEN, 2 * D), f32)
        _, vjp, mn = jax.vjp(lambda g_, t0, t1: _kv_f(rows[0], g_, t0, t1, w0v, w1v, _mm), g_ref[...], tap, tap, has_aux=True)
        dg, d0, d1 = vjp((rows[1], rows[2]))
        return [], [dg, _dot(mn, d0, TN), _dot(mn, d1, TN)]

    return _tiled("kv_bwd", body, MEM_LEN, MEM_LEN, [(mem, D, 0), (dkv0, 2 * D, 0), (dkv1, 2 * D, 0)], [g, w0, w1], [],
                  [((1, D), f32), ((D, 2 * D), f32), ((D, 2 * D), f32)])


def _xa_f(x, g, kv, tap, w_qg, w_o, mm, nt, nn):
    h = _rms(x, g)
    qg = mm(h, w_qg) + tap
    outs = []
    for hd in range(XH):
        q = qg[:, hd * XD:(hd + 1) * XD]
        k = kv[:, hd * XD:(hd + 1) * XD]
        v = kv[:, D + hd * XD:D + (hd + 1) * XD]
        s = nt(q, k) * (XD ** -0.5)
        e = jnp.exp(s - jnp.max(s, axis=-1, keepdims=True))
        outs.append(nn(e / jnp.sum(e, axis=-1, keepdims=True), v))
    u = jnp.concatenate(outs, axis=1) * jax.nn.silu(qg[:, D:])
    return mm(u, w_o), (h, u)


def _xa_fwd(name, x, g, kv, w_qg, w_o, tm):
    def body(rows, fulls):
        g_ref, kv_ref, wq_ref, wo_ref = fulls
        out, _ = _xa_f(rows[0], g_ref[...], kv_ref[...], 0.0, wq_ref[...], wo_ref[...], _plain_mm, _plain_nt, _plain_mm)
        return [rows[0] + out], []

    return _tiled(name, body, x.shape[0], tm, [(x, D, 0)], [g, kv, w_qg, w_o], [(D, f32)])[0]


def _xa_bwd(name, x, dxo, g, kv, w_qg, w_o, tm):
    def body(rows, fulls):
        xv, dxv = rows
        g_ref, kv_ref, wq_ref, wo_ref = fulls
        wq, wo = wq_ref[...], wo_ref[...]
        f = lambda xv, gv, kvv, tap: _xa_f(xv, gv, kvv, tap, wq, wo, _mm, _mm_nt, _mm_nn)
        _, vjp, (h, u) = jax.vjp(f, xv, g_ref[...], kv_ref[...], jnp.zeros((tm, 2 * D), f32), has_aux=True)
        dx, dg, dkv, dtap = vjp(dxv)
        return [dx + dxv], [dg, dkv, _dot(h, dtap, TN), _dot(u, dxv, TN)]

    return _tiled(name, body, x.shape[0], tm, [(x, D, 0), (dxo, D, 0)], [g, kv, w_qg, w_o], [(D, f32)],
                  [((1, D), f32), ((MEM_LEN, 2 * D), f32), ((D, 2 * D), f32), ((D, D), f32)])


def _attn_scores(q, kwin, bm, hh, r0, lane_hi, colv):
    qm = jnp.where(lane_hi == hh, q, 0.0).astype(bf16)
    s = _dot(qm, kwin, NT) * (DH ** -0.5) + bm
    s = jnp.where(colv >= PAD - r0, s, -1e30)
    e = jnp.exp(s - jnp.max(s, axis=-1, keepdims=True))
    return qm, e / jnp.sum(e, axis=-1, keepdims=True)


def _attn_fwd(qkvg, bias):
    L = qkvg.shape[0]
    nq = L // AQ

    def kern(q_ref, k_ref, v_ref, b_ref, o_ref, kpad, vpad):
        i = pl.program_id(1)

        @pl.when(i == 0)
        def _():
            kpad[0:PAD, :] = jnp.zeros((PAD, 128), bf16)
            vpad[0:PAD, :] = jnp.zeros((PAD, 128), bf16)
            kpad[PAD:, :] = k_ref[...].astype(bf16)
            vpad[PAD:, :] = v_ref[...].astype(bf16)

        r0 = pl.multiple_of(i * AQ, AQ)
        q = q_ref[...]
        kwin = kpad[pl.ds(r0, AW), :]
        vwin = vpad[pl.ds(r0, AW), :]
        lane_hi = lax.broadcasted_iota(jnp.int32, (AQ, 128), 1) // DH
        colv = lax.broadcasted_iota(jnp.int32, (1, AW), 1)
        outs = []
        for hh in range(2):
            _, p = _attn_scores(q, kwin, b_ref[hh], hh, r0, lane_hi, colv)
            outs.append(_dot(p, vwin, NN))
        o_ref[...] = jnp.where(lane_hi == 0, outs[0], outs[1])

    return pl.pallas_call(
        kern, name="attn_fwd", grid=(HEADS // 2, nq),
        in_specs=[pl.BlockSpec((AQ, 128), lambda j, i: (i, j)),
                  pl.BlockSpec((L, 128), lambda j, i: (0, 8 + j)),
                  pl.BlockSpec((L, 128), lambda j, i: (0, 16 + j)),
                  pl.BlockSpec((2, AQ, AW), lambda j, i: (j, 0, 0))],
        out_specs=pl.BlockSpec((AQ, 128), lambda j, i: (i, j)),
        out_shape=SDS((L, D), f32),
        scratch_shapes=[pltpu.VMEM((L + PAD, 128), bf16), pltpu.VMEM((L + PAD, 128), bf16)],
        compiler_params=_cp(("arbitrary", "arbitrary")),
    )(qkvg, qkvg, qkvg, bias)


def _attn_bwd(qkvg, do, bias):
    L = qkvg.shape[0]
    nq = L // AQ

    def kern(q_ref, k_ref, v_ref, b_ref, do_ref, dq_ref, dk_ref, dv_ref, db_ref, kpad, vpad, dkp, dvp):
        i = pl.program_id(1)

        @pl.when(i == 0)
        def _():
            kpad[0:PAD, :] = jnp.zeros((PAD, 128), bf16)
            vpad[0:PAD, :] = jnp.zeros((PAD, 128), bf16)
            kpad[PAD:, :] = k_ref[...].astype(bf16)
            vpad[PAD:, :] = v_ref[...].astype(bf16)
            dkp[...] = jnp.zeros_like(dkp)
            dvp[...] = jnp.zeros_like(dvp)
            db_ref[...] = jnp.zeros_like(db_ref)

        r0 = pl.multiple_of(i * AQ, AQ)
        q = q_ref[...]
        dov = do_ref[...]
        kwin = kpad[pl.ds(r0, AW), :]
        vwin = vpad[pl.ds(r0, AW), :]
        lane_hi = lax.broadcasted_iota(jnp.int32, (AQ, 128), 1) // DH
        colv = lax.broadcasted_iota(jnp.int32, (1, AW), 1)
        dqs = []
        dk = jnp.zeros((AW, 128), f32)
        dv = jnp.zeros((AW, 128), f32)
        for hh in range(2):
            qm, p = _attn_scores(q, kwin, b_ref[hh], hh, r0, lane_hi, colv)
            dom = jnp.where(lane_hi == hh, dov, 0.0).astype(bf16)
            dp = _dot(dom, vwin, NT)
            ds = p * (dp - jnp.sum(p * dp, axis=-1, keepdims=True))
            db_ref[hh] += ds
            dsb = (ds * (DH ** -0.5)).astype(bf16)
            dqs.append(_dot(dsb, kwin, NN))
            dk = dk + _dot(dsb, qm, TN)
            dv = dv + _dot(p, dom, TN)
        dq_ref[...] = jnp.where(lane_hi == 0, dqs[0], dqs[1])
        dkp[pl.ds(r0, AW), :] += dk
        dvp[pl.ds(r0, AW), :] += dv

        @pl.when(i == nq - 1)
        def _():
            dk_ref[...] = dkp[PAD:, :]
            dv_ref[...] = dvp[PAD:, :]

    return pl.pallas_call(
        kern, name="attn_bwd", grid=(HEADS // 2, nq),
        in_specs=[pl.BlockSpec((AQ, 128), lambda j, i: (i, j)),
                  pl.BlockSpec((L, 128), lambda j, i: (0, 8 + j)),
                  pl.BlockSpec((L, 128), lambda j, i: (0, 16 + j)),
                  pl.BlockSpec((2, AQ, AW), lambda j, i: (j, 0, 0)),
                  pl.BlockSpec((AQ, 128), lambda j, i: (i, j))],
        out_specs=[pl.BlockSpec((AQ, 128), lambda j, i: (i, j)),
                   pl.BlockSpec((L, 128), lambda j, i: (0, j)),
                   pl.BlockSpec((L, 128), lambda j, i: (0, j)),
                   pl.BlockSpec((2, AQ, AW), lambda j, i: (j, 0, 0))],
        out_shape=[SDS((L, D), f32), SDS((L, D), f32), SDS((L, D), f32), SDS((HEADS, AQ, AW), f32)],
        scratch_shapes=[pltpu.VMEM((L + PAD, 128), bf16), pltpu.VMEM((L + PAD, 128), bf16),
                        pltpu.VMEM((L + PAD, 128), f32), pltpu.VMEM((L + PAD, 128), f32)],
        compiler_params=_cp(("arbitrary", "arbitrary")),
    )(qkvg, qkvg, qkvg, bias, do)


_SKEW = AQ + AW - 1


def _rel_index():
    d = (AW - 1) - np.arange(_SKEW)
    return np.clip(d, -MAX_REL, MAX_REL) + MAX_REL


def _band_mask():
    qc = np.arange(AQ)[:, None] // CHUNK + LEFT
    kc = np.arange(AW)[None, :] // CHUNK
    return jnp.asarray(np.where((kc <= qc) & (kc >= qc - LEFT), 0.0, -1e30), f32)


def _bias_matrix(rel_bias):
    tv = jnp.take(rel_bias, jnp.asarray(_rel_index()), axis=1)
    flat = jnp.tile(jnp.pad(tv, ((0, 0), (0, 1))), (1, AQ))
    top = flat[:, AQ - 1:AQ - 1 + AQ * _SKEW].reshape(HEADS, AQ, _SKEW)[:, :, :AW]
    return top + _band_mask()


def _bias_grad(dbias, onehot):
    flat = jnp.pad(dbias, ((0, 0), (0, 0), (0, _SKEW - AW))).reshape(HEADS, AQ * _SKEW)
    z = jnp.pad(flat, ((0, 0), (AQ - 1, AQ * 1024 - AQ * _SKEW - (AQ - 1)))).reshape(HEADS, AQ, 1024)

    def kern(z_ref, oh_ref, out_ref):
        diag = jnp.sum(z_ref[...], axis=1)
        out_ref[...] = jnp.dot(diag, oh_ref[...], preferred_element_type=f32, precision=lax.Precision.HIGHEST)

    return pl.pallas_call(kern, name="bias_grad", out_shape=SDS((HEADS, 2 * MAX_REL + 1), f32),
                          compiler_params=_cp(()))(z, onehot)


def _rel_onehot():
    oh = np.zeros((1024, 2 * MAX_REL + 1), np.float32)
    oh[np.arange(_SKEW), _rel_index()] = 1.0
    return jnp.asarray(oh)


def _gated_out_fwd(name, x, o, gate_src, gate_col, w, tm):
    def body(rows, fulls):
        xv, ov, gv = rows
        return [xv + _dot(ov * jax.nn.silu(gv), fulls[0][...], NN)], []

    return _tiled(name, body, x.shape[0], tm, [(x, D, 0), (o, D, 0), (gate_src, D, gate_col)], [w], [(D, f32)])[0]


def _gated_out_bwd(name, dxo, o, gate_src, gate_col, w, tm):
    def body(rows, fulls):
        dxv, ov, gv = rows
        u, vjp = jax.vjp(lambda ov, gv: ov * jax.nn.silu(gv), ov, gv)
        do, dg = vjp(_dot(dxv, fulls[0][...], NT))
        return [do, dg], [_dot(u, dxv, TN)]

    return _tiled(name, body, dxo.shape[0], tm, [(dxo, D, 0), (o, D, 0), (gate_src, D, gate_col)], [w],
                  [(D, f32), (D, f32)], [((D, D), f32)])


def _loss_head(x, g, target, tm):
    def body(rows, fulls):
        xv, tv = rows

        def f(xv, gv):
            e = jnp.square(_rms(xv, gv) - tv)
            return 0.5 * jnp.sum(jnp.mean(e, axis=-1, keepdims=True), axis=0, keepdims=True)

        loss, vjp = jax.vjp(f, xv, fulls[0][...])
        dx, dg = vjp(jnp.ones((1, 1), f32))
        return [dx], [jnp.broadcast_to(loss, (1, 128)), dg]

    return _tiled("loss_head", body, x.shape[0], tm, [(x, D, 0), (target, D, 0)], [g], [(D, f32)],
                  [((1, 128), f32), ((1, D), f32)])


def _local_step(x, mem, target, p):
    L = x.shape[0]
    TM = 256
    row = lambda v: v.reshape(1, -1)

    bt = lambda b: jnp.transpose(b, (0, 2, 1)).reshape(NG * GS, NS)
    disc_in = (p["ev_s5_lambda_re"], p["ev_s5_lambda_im"], p["ev_s5_log_dt"].reshape(NG, 1),
               bt(p["ev_s5_b_re"]), bt(p["ev_s5_b_im"]))
    ab_re, ab_im, bbt_re, bbt_im = _s5_disc_fwd(*disc_in)
    bmat = (_to_blockdiag(bbt_re).astype(bf16), _to_blockdiag(bbt_im).astype(bf16))
    cmat = (_to_blockdiag(p["ev_s5_c_re"].reshape(NG * GS, NS)).astype(bf16),
            _to_blockdiag(p["ev_s5_c_im"].reshape(NG * GS, NS)).astype(bf16))
    a_re, a_im = ab_re.reshape(1, NST), ab_im.reshape(1, NST)
    d_skip = row(p["ev_s5_d"])
    w32 = jnp.pad(p["ev_conv_w"], ((0, 1), (0, 0)))
    conv_b, ln_g, ln_b, glu_b = row(p["ev_conv_b"]), row(p["ev_conv_ln_g"]), row(p["ev_conv_ln_b"]), row(p["ev_s5_glu_b"])
    g_mem, g_ev, g_od, g_fin = row(p["mem_norm_g"]), row(p["ev_norm_g"]), row(p["od_norm_g"]), row(p["final_norm_g"])
    g_xa = [row(p["xa_norm_g"][l]) for l in range(2)]
    bias = _bias_matrix(p["od_rel_bias"])

    kv0, kv1 = _kv_fwd(mem, g_mem, p["xa_w_kv"][0], p["xa_w_kv"][1])
    z = _norm_mm_fwd("ev_in_fwd", x, g_ev, p["ev_w_in"], TM)
    y_s5, hb_re, hb_im = _s5_fwd(z, bmat, cmat, a_re, a_im, d_skip)
    c = _conv_fwd(z, w32, conv_b, 512)
    x1 = _ev_out_fwd(x, y_s5, c, z, glu_b, ln_g, ln_b, p["ev_s5_glu_w"], p["ev_w_out"], TM)
    x2 = _xa_fwd("xa0_fwd", x1, g_xa[0], kv0, p["xa_w_qg"][0], p["xa_w_o"][0], TM)
    qkvg = _norm_mm_fwd("od_in_fwd", x2, g_od, p["od_w_in"], TM)
    o = _attn_fwd(qkvg, bias)
    x3 = _gated_out_fwd("od_out_fwd", x2, o, qkvg, 3, p["od_w_out"], TM)
    x4 = _xa_fwd("xa1_fwd", x3, g_xa[1], kv1, p["xa_w_qg"][1], p["xa_w_o"][1], TM)

    G = {}
    dx4, loss_row, G["final_norm_g"] = _loss_head(x4, g_fin, target, TM)
    dx3, dg_xa1, dkv1, dwqg1, dwo1 = _xa_bwd("xa1_bwd", x3, dx4, g_xa[1], kv1, p["xa_w_qg"][1], p["xa_w_o"][1], TM)
    do, dgate, G["od_w_out"] = _gated_out_bwd("od_out_bwd", dx3, o, qkvg, 3, p["od_w_out"], TM)
    dq, dk, dv, dbias = _attn_bwd(qkvg, do, bias)
    G["od_rel_bias"] = _bias_grad(dbias, _rel_onehot())
    dx2, G["od_norm_g"], G["od_w_in"] = _norm_mm_bwd(
        "od_in_bwd", x2, g_od, p["od_w_in"], [(dq, D, 0), (dk, D, 0), (dv, D, 0), (dgate, D, 0)], dx3, TM)
    dx1, dg_xa0, dkv0, dwqg0, dwo0 = _xa_bwd("xa0_bwd", x1, dx2, g_xa[0], kv0, p["xa_w_qg"][0], p["xa_w_o"][0], TM)
    G["xa_norm_g"] = jnp.concatenate([dg_xa0, dg_xa1], axis=0)
    G["xa_w_qg"] = jnp.stack([dwqg0, dwqg1])
    G["xa_w_o"] = jnp.stack([dwo0, dwo1])
    G["mem_norm_g"], dwkv0, dwkv1 = _kv_bwd(mem, g_mem, p["xa_w_kv"][0], p["xa_w_kv"][1], dkv0, dkv1)
    G["xa_w_kv"] = jnp.stack([dwkv0, dwkv1])
    (dy_s5, dc, dga, dgb, G["ev_s5_glu_b"], G["ev_conv_ln_g"], G["ev_conv_ln_b"], G["ev_s5_glu_w"],
     G["ev_w_out"]) = _ev_out_bwd(dx1, y_s5, c, z, glu_b, ln_g, ln_b, p["ev_s5_glu_w"], p["ev_w_out"], TM)
    dval, dglu, dw32, G["ev_conv_b"] = _conv_bwd(z, dc, w32, 512)
    G["ev_conv_w"] = dw32[:CK]
    du, dbre, dbim, dcre, dcim, da_re, da_im, G["ev_s5_d"] = _s5_bwd(z, dy_s5, hb_re, hb_im, bmat, cmat, a_re, a_im, d_skip)
    G["ev_s5_c_re"] = _from_blockdiag(dcre).reshape(NG, GS, NS)
    G["ev_s5_c_im"] = _from_blockdiag(dcim).reshape(NG, GS, NS)
    dlr, dli, dldt, dbt_re, dbt_im = _s5_disc_bwd(
        disc_in, (da_re.reshape(NG, NS), da_im.reshape(NG, NS), _from_blockdiag(dbre), _from_blockdiag(dbim)))
    unbt = lambda b: jnp.transpose(b.reshape(NG, GS, NS), (0, 2, 1))
    G["ev_s5_lambda_re"], G["ev_s5_lambda_im"], G["ev_s5_log_dt"] = dlr, dli, dldt.reshape(NG)
    G["ev_s5_b_re"], G["ev_s5_b_im"] = unbt(dbt_re), unbt(dbt_im)
    dx, G["ev_norm_g"], G["ev_w_in"] = _norm_mm_bwd(
        "ev_in_bwd", x, g_ev, p["ev_w_in"], [(du, SW, 0), (dga, SW, 0), (dval, CW, 0), (dglu, CW, 0), (dgb, CW, 0)], dx1, TM)
    return loss_row[0, 0], dx, G


_ANY = pl.BlockSpec(memory_space=pl.ANY)


def _mesh_pos():
    return tuple(lax.axis_index(a) for a in AXES)


def _slot(ref, dev):
    return ref.at[4 * dev[0] + 2 * dev[1] + dev[2]]


def _all_gather(payloads):
    n = len(payloads)

    def body(*refs):
        ins, outs = refs[:n], refs[n:2 * n]
        send_sems, recv_sems, local_sems = refs[2 * n:]
        x, y, c = _mesh_pos()
        me, sibling = (x, y, c), (x, y, 1 - c)
        chips = [(1 - x, y), (x, 1 - y), (1 - x, 1 - y)]

        def copy(p, k, block, to, src=None):
            return pltpu.make_async_remote_copy(
                src_ref=_slot(outs[p], block) if src is None else src, dst_ref=_slot(outs[p], block),
                send_sem=send_sems.at[7 * p + k], recv_sem=recv_sems.at[7 * p + k],
                device_id=to, device_id_type=pl.DeviceIdType.MESH)

        mine = [pltpu.make_async_copy(ins[p], _slot(outs[p], me), local_sems.at[p]) for p in range(n)]
        for cp in mine:
            cp.start()
        first = []
        for p in range(n):
            first.append(copy(p, 0, me, sibling, src=ins[p]))
            first += [copy(p, 1 + j, me, (*chip, c), src=ins[p]) for j, chip in enumerate(chips)]
        for cp in first:
            cp.start()
        passed = []
        for j, chip in enumerate(chips):
            for p in range(n):
                copy(p, 1 + j, (*chip, c), me).wait_recv()
                passed.append(copy(p, 4 + j, (*chip, c), sibling))
                passed[-1].start()
        for p in range(n):
            copy(p, 0, sibling, me).wait_recv()
            for j, chip in enumerate(chips):
                copy(p, 4 + j, (*chip, 1 - c), me).wait_recv()
        for cp in first + passed:
            cp.wait_send()
        for cp in mine:
            cp.wait()

    return pl.pallas_call(
        body, name="all_gather", in_specs=[_ANY] * n, out_specs=[_ANY] * n,
        out_shape=[SDS((NDEV, *a.shape), a.dtype) for a in payloads],
        scratch_shapes=[pltpu.SemaphoreType.DMA((7 * n,)), pltpu.SemaphoreType.DMA((7 * n,)), pltpu.SemaphoreType.DMA((n,))],
    )(*payloads)


def _exchange(big, small):
    def body(big_ref, small_ref, obig, osmall, send_sems, recv_sems, local_sems):
        x, y, c = _mesh_pos()
        me = (x, y, c)
        mine = [pltpu.make_async_copy(_slot(big_ref, me), _slot(obig, me), local_sems.at[0]),
                pltpu.make_async_copy(small_ref, _slot(osmall, me), local_sems.at[1])]
        for cp in mine:
            cp.start()
        copies = []
        for k in range(1, NDEV):
            flip = lambda v, bit: 1 - v if (k >> bit) & 1 else v
            peer = (flip(x, 2), flip(y, 1), flip(c, 0))
            copies.append(pltpu.make_async_remote_copy(
                src_ref=_slot(big_ref, peer), dst_ref=_slot(obig, me), send_sem=send_sems.at[k - 1],
                recv_sem=recv_sems.at[k - 1], device_id=peer, device_id_type=pl.DeviceIdType.MESH))
            copies.append(pltpu.make_async_remote_copy(
                src_ref=small_ref, dst_ref=_slot(osmall, me), send_sem=send_sems.at[6 + k],
                recv_sem=recv_sems.at[6 + k], device_id=peer, device_id_type=pl.DeviceIdType.MESH))
        for cp in copies:
            cp.start()
        for cp in copies:
            cp.wait()
        for cp in mine:
            cp.wait()

    return pl.pallas_call(
        body, name="grad_exchange", in_specs=[_ANY, _ANY], out_specs=[_ANY, _ANY],
        out_shape=[SDS(big.shape, big.dtype), SDS((NDEV, *small.shape), small.dtype)],
        scratch_shapes=[pltpu.SemaphoreType.DMA((14,)), pltpu.SemaphoreType.DMA((14,)), pltpu.SemaphoreType.DMA((2,))],
    )(big, small)


def _sum_slots(name, recv, tr):
    _, R, C = recv.shape

    def kern(r_ref, o_ref):
        acc = r_ref[0].astype(f32)
        for k in range(1, NDEV):
            acc = acc + r_ref[k].astype(f32)
        o_ref[...] = acc

    return pl.pallas_call(
        kern, name=name, grid=(R // tr,), in_specs=[pl.BlockSpec((NDEV, tr, C), lambda i: (0, i, 0))],
        out_specs=pl.BlockSpec((tr, C), lambda i: (i, 0)), out_shape=SDS((R, C), f32), compiler_params=_cp(),
    )(recv)


def _adamw(name, g, w, m, v, tr):
    R, C = g.shape

    def body(rows, _):
        gv, wv, mv, vv = rows
        m2 = ADAM_B1 * mv + (1.0 - ADAM_B1) * gv
        v2 = ADAM_B2 * vv + (1.0 - ADAM_B2) * jnp.square(gv)
        m_hat = m2 / (1.0 - ADAM_B1 ** ADAM_STEP)
        v_hat = v2 / (1.0 - ADAM_B2 ** ADAM_STEP)
        return [-ADAM_LR * (m_hat / (jnp.sqrt(v_hat) + ADAM_EPS) + ADAM_WD * wv), m2, v2], []

    return _tiled(name, body, R, tr, [(g, C, 0), (w, C, 0), (m, C, 0), (v, C, 0)], [], [(C, f32)] * 3)


def _pack(arrs, width, row_mult, dtype):
    flat = jnp.concatenate([a.reshape(-1).astype(dtype) for a in arrs])
    rows = -(-flat.shape[0] // (width * row_mult)) * row_mult
    return jnp.pad(flat, (0, rows * width - flat.shape[0])).reshape(rows, width)


def _pack_slots(arrs, width, row_mult, dtype):
    flat = jnp.concatenate([a.reshape(NDEV, -1).astype(dtype) for a in arrs], axis=1)
    rows = -(-flat.shape[1] // (width * row_mult)) * row_mult
    return jnp.pad(flat, ((0, 0), (0, rows * width - flat.shape[1]))).reshape(NDEV, rows, width)


def _unpack(buf, shapes, lead=()):
    flat = buf.reshape(*lead, -1)
    out, off = [], 0
    for s in shapes:
        n = int(np.prod(s))
        out.append(flat[..., off:off + n].reshape(*lead, *s))
        off += n
    return out


def _to_slots(full, ax):
    s = full.shape
    return jnp.moveaxis(full.reshape(*s[:ax], NDEV, s[ax] // NDEV, *s[ax + 1:]), ax, 0)


def _from_slots(slots, ax):
    t = jnp.moveaxis(slots, 0, ax)
    s = t.shape
    return t.reshape(*s[:ax], s[ax] * s[ax + 1], *s[ax + 2:])


WEIGHTS = ["mem_norm_g", "ev_norm_g", "ev_w_in", "ev_s5_lambda_re", "ev_s5_lambda_im", "ev_s5_log_dt", "ev_s5_b_re",
           "ev_s5_b_im", "ev_s5_c_re", "ev_s5_c_im", "ev_s5_d", "ev_s5_glu_w", "ev_s5_glu_b", "ev_conv_w", "ev_conv_b",
           "ev_conv_ln_g", "ev_conv_ln_b", "ev_w_out", "od_norm_g", "od_w_in", "od_rel_bias", "od_w_out", "xa_norm_g",
           "xa_w_qg", "xa_w_kv", "xa_w_o", "final_norm_g"]
MATMUL_SHARDED = {"ev_w_in": 2, "ev_s5_glu_w": 1, "ev_w_out": 1, "od_w_in": 2, "od_w_out": 1, "xa_w_qg": 2, "xa_w_kv": 2,
                  "xa_w_o": 1}
OTHER_SHARDED = {"ev_conv_w": 2, "od_norm_g": 1}
PER_LAYER = ("xa_norm_g", "xa_w_qg", "xa_w_kv", "xa_w_o")
REPLICATED = [n for n in WEIGHTS if n not in MATMUL_SHARDED and n not in OTHER_SHARDED]


def kernel(*args):
    names = ["x", "mem", *WEIGHTS, "loss_target", *["m_" + n for n in WEIGHTS], *["v_" + n for n in WEIGHTS]]
    a = dict(zip(names, args, strict=True))
    me = 4 * lax.axis_index("x") + 2 * lax.axis_index("y") + lax.axis_index("c")
    big_names, oth_names = list(MATMUL_SHARDED), list(OTHER_SHARDED)

    gb, go = _all_gather([_pack([a[n] for n in big_names], 1024, 16, bf16), _pack([a[n] for n in oth_names], 128, 8, f32)])
    full = {}
    for names_, buf, table in ((big_names, gb, MATMUL_SHARDED), (oth_names, go, OTHER_SHARDED)):
        for n, slots in zip(names_, _unpack(buf, [a[n].shape for n in names_], lead=(NDEV,))):
            full[n] = _from_slots(slots, table[n])
    p = {n: full.get(n, a[n]) for n in WEIGHTS}
    p = {n: (w if n in PER_LAYER else w[0] if n.startswith(("ev_", "od_")) else w) for n, w in p.items()}

    loss_part, dx, G = _local_step(a["x"][0], a["mem"][0], a["loss_target"][0], p)
    G = {n: g.reshape(p[n].shape) for n, g in G.items()}
    G = {n: (g if n in PER_LAYER or not n.startswith(("ev_", "od_")) else g[None]) for n, g in G.items()}

    small_names = REPLICATED + oth_names
    gbig = _pack_slots([_to_slots(G[n], MATMUL_SHARDED[n]) for n in big_names], 1024, 256, bf16)
    gsmall = _pack([G[n] for n in small_names], 128, 64, f32)
    rbig, rsmall = _exchange(gbig, gsmall)
    g_big = _sum_slots("sum_big", rbig, 256)
    g_small = _sum_slots("sum_small", rsmall, 64)
    grads = dict(zip(big_names, _unpack(g_big, [a[n].shape for n in big_names])))
    for n, g in zip(small_names, _unpack(g_small, [G[n].shape for n in small_names])):
        grads[n] = lax.dynamic_index_in_dim(_to_slots(g, OTHER_SHARDED[n]), me, 0, keepdims=False) if n in OTHER_SHARDED else g

    outs = {}
    for grp, width, mult in ((big_names, 1024, 256), (small_names, 128, 64)):
        packed = [_pack([src(n) for n in grp], width, mult, f32)
                  for src in (lambda n: grads[n], lambda n: a[n], lambda n: a["m_" + n], lambda n: a["v_" + n])]
        res = _adamw("adamw_%d" % width, *packed, mult)
        for kind, buf in zip(("delta", "new_m", "new_v"), res):
            for n, t in zip(grp, _unpack(buf, [a[n].shape for n in grp])):
                outs[kind, n] = t
    loss = lax.psum(loss_part, AXES)
    return (loss, dx[None], *[grads[n] for n in WEIGHTS], *[outs["delta", n] for n in WEIGHTS],
            *[outs["new_m", n] for n in WEIGHTS], *[outs["new_v", n] for n in WEIGHTS])
```

```python
import functools
from typing import NamedTuple

import jax
import jax.numpy as jnp
import numpy as np
from jax import lax
from jax.experimental import pallas as pl
from jax.experimental.pallas import tpu as pltpu

f32, bf16 = jnp.float32, jnp.bfloat16
SDS = jax.ShapeDtypeStruct

D = 1024
SW = 512
NG, GS, NS = 32, 16, 64
NST = NG * NS
CW = 512
CK = 31
EVEN_IN = 2 * SW + 3 * CW
HEADS, DH = 16, 64
CHUNK, LEFT = 64, 8
PAD = LEFT * CHUNK
MAX_REL = 128
MEM_LEN = 256
XH, XD = 4, 256
EPS = 1e-6
NDEV = 8
AXES = ("x", "y", "c")

ADAM_LR, ADAM_B1, ADAM_B2, ADAM_EPS, ADAM_WD, ADAM_STEP = 0.001, 0.9, 0.999, 1e-08, 0.01, 10

VMEM_LIMIT = 56 << 20
S5_T = 512
S5_CH = 512
AQ = 256
AW = AQ + PAD

NN = ((1,), (0,))
NT = ((1,), (1,))
TN = ((0,), (0,))


def _dot(a, b, dims):
    return lax.dot_general(a.astype(bf16), b.astype(bf16), (dims, ((), ())), preferred_element_type=f32)


def _dotw(a, w):
    if w.ndim == 2:
        return _dot(a, w, NN)
    return jnp.concatenate([_dot(a, w[j], NN) for j in range(w.shape[0])], axis=1)


def _dotw_t(g, w):
    if w.ndim == 2:
        return _dot(g, w, NT)
    n = w.shape[2]
    out = _dot(g[:, :n], w[0], NT)
    for j in range(1, w.shape[0]):
        out = out + _dot(g[:, j * n:(j + 1) * n], w[j], NT)
    return out


def _wgrad(a, g, w):
    if w.ndim == 2:
        return _dot(a, g, TN)
    n = w.shape[2]
    return jnp.stack([_dot(a, g[:, j * n:(j + 1) * n], TN) for j in range(w.shape[0])])


@jax.custom_vjp
def _mm(a, w):
    return _dotw(a, w)


def _mm_f(a, w):
    return _dotw(a, w), w


def _mm_b(w, g):
    return _dotw_t(g, w), None


_mm.defvjp(_mm_f, _mm_b)


@jax.custom_vjp
def _mm_nt(a, b):
    return _dot(a, b, NT)


def _mm_nt_f(a, b):
    return _dot(a, b, NT), (a, b)


def _mm_nt_b(res, g):
    a, b = res
    return _dot(g, b, NN), _dot(g, a, TN)


_mm_nt.defvjp(_mm_nt_f, _mm_nt_b)


@jax.custom_vjp
def _mm_nn(a, b):
    return _dot(a, b, NN)


def _mm_nn_f(a, b):
    return _dot(a, b, NN), (a, b)


def _mm_nn_b(res, g):
    a, b = res
    return _dot(g, b, NT), _dot(a, g, TN)


_mm_nn.defvjp(_mm_nn_f, _mm_nn_b)


def _plain_mm(a, w):
    return _dotw(a, w)


def _plain_nt(a, b):
    return _dot(a, b, NT)


def _rms(x, g):
    return x * lax.rsqrt(jnp.mean(x * x, axis=-1, keepdims=True) + EPS) * g


def _cp(sem=("arbitrary",)):
    return pltpu.CompilerParams(dimension_semantics=sem, vmem_limit_bytes=VMEM_LIMIT)


def _row(tm, w, col=0):
    return pl.BlockSpec((tm, w), lambda i: (i, col))


def _full(shape):
    nd = len(shape)
    return pl.BlockSpec(tuple(shape), lambda i: (0,) * nd, pipeline_mode=pl.Buffered(1))


class _W(NamedTuple):
    arr: jax.Array
    rows: int
    idx: int
    kind: str

    @property
    def shape(self):
        c = self.arr.shape[2]
        return (NDEV * self.rows, c) if self.kind == "rows" else (NDEV, self.rows, c)


class _WRef:
    def __init__(self, ref, kind):
        self.ref, self.kind = ref, kind

    def __getitem__(self, _):
        v = self.ref[...]
        return v.reshape(v.shape[0] * v.shape[1], v.shape[2]) if self.kind == "rows" else v


def _wspec(w):
    if isinstance(w, _W):
        return pl.BlockSpec((NDEV, w.rows, w.arr.shape[2]), lambda i: (0, w.idx, 0), pipeline_mode=pl.Buffered(1))
    return _full(w.shape)


def _tiled(name, body, L, tm, row_ins, full_ins, row_outs, acc_outs=()):
    nr, nf, no = len(row_ins), len(full_ins), len(row_outs)

    def kern(*refs):
        rin = refs[:nr]
        fin = [_WRef(r, w.kind) if isinstance(w, _W) else r for r, w in zip(refs[nr:nr + nf], full_ins)]
        rout, aout = refs[nr + nf:nr + nf + no], refs[nr + nf + no:]
        routs, accs = body([r[...] for r in rin], fin)
        for r, v in zip(rout, routs):
            r[...] = v.astype(r.dtype)
        if aout:
            @pl.when(pl.program_id(0) == 0)
            def _():
                for a in aout:
                    a[...] = jnp.zeros(a.shape, a.dtype)

            for a, v in zip(aout, accs):
                a[...] += v.astype(a.dtype)

    return pl.pallas_call(
        kern, name=name, grid=(L // tm,),
        in_specs=[_row(tm, w, c) for (_, w, c) in row_ins] + [_wspec(a) for a in full_ins],
        out_specs=[_row(tm, w) for (w, _) in row_outs] + [_full(s) for (s, _) in acc_outs],
        out_shape=[SDS((L, w), dt) for (w, dt) in row_outs] + [SDS(tuple(s), dt) for (s, dt) in acc_outs],
        compiler_params=_cp(),
    )(*[a for (a, _, _) in row_ins], *[a.arr if isinstance(a, _W) else a for a in full_ins])


def _norm_mm_fwd(name, x, g, w, tm):
    def body(rows, fulls):
        g_ref, w_ref = fulls
        return [_dotw(_rms(rows[0], g_ref[...]), w_ref[...])], []

    n_out = w.shape[1] if len(w.shape) == 2 else w.shape[0] * w.shape[2]
    return _tiled(name, body, x.shape[0], tm, [(x, D, 0)], [g, w], [(n_out, f32)])[0]


def _norm_mm_bwd(name, x, g, w, dz_parts, dres, tm):
    n = len(dz_parts)

    def body(rows, fulls):
        g_ref, w_ref = fulls
        dz = rows[1] if n == 1 else jnp.concatenate(rows[1:1 + n], axis=1)
        wv = w_ref[...]
        h, vjp = jax.vjp(_rms, rows[0], g_ref[...])
        dx, dg = vjp(_dotw_t(dz, wv))
        return [dx + rows[1 + n]], [dg, _wgrad(h, dz, wv)]

    return _tiled(name, body, x.shape[0], tm, [(x, D, 0), *dz_parts, (dres, D, 0)], [g, w],
                  [(D, f32)], [((1, D), f32), (tuple(w.shape), f32)])


def _scan_tables(tab_ref, ar, ai, reverse):
    pw = [(ar, ai)]
    for _ in range(7):
        pr, pi = pw[-1]
        pw.append((pr * ar - pi * ai, pr * ai + pi * ar))
    row = lax.broadcasted_iota(jnp.int32, (8, NST), 0)
    for si, s in enumerate((1, 2, 4)):
        keep = (row <= 7 - s) if reverse else (row >= s)
        tab_ref[2 * si] = jnp.where(keep, pw[s - 1][0], 0.0)
        tab_ref[2 * si + 1] = jnp.where(keep, pw[s - 1][1], 0.0)
    pr = jnp.zeros((8, NST), f32)
    pi = jnp.zeros((8, NST), f32)
    for t in range(8):
        k = (7 - t) if reverse else t
        pr = jnp.where(row == t, pw[k][0], pr)
        pi = jnp.where(row == t, pw[k][1], pi)
    tab_ref[6] = pr
    tab_ref[7] = pi


def _scan(re_ref, im_ref, row0, nrows, tab_ref, car_re, car_im, reverse, extra=None):
    nt = nrows // 8
    accs = []
    for cc in range(NST // S5_CH):
        cols = slice(cc * S5_CH, (cc + 1) * S5_CH)

        def step(j, carry, cols=cols):
            cr, ci, acc = carry
            jj = (nt - 1 - j) if reverse else j
            r0 = pl.multiple_of(row0 + jj * 8, 8)
            xr = re_ref[pl.ds(r0, 8), cols]
            xi = im_ref[pl.ds(r0, 8), cols]
            for si, s in enumerate((1, 2, 4)):
                sh = (8 - s) if reverse else s
                yr = pltpu.roll(xr, sh, 0)
                yi = pltpu.roll(xi, sh, 0)
                tr = tab_ref[2 * si, :, cols]
                ti = tab_ref[2 * si + 1, :, cols]
                xr, xi = xr + tr * yr - ti * yi, xi + tr * yi + ti * yr
            tr = tab_ref[6, :, cols]
            ti = tab_ref[7, :, cols]
            xr, xi = xr + tr * cr - ti * ci, xi + tr * ci + ti * cr
            re_ref[pl.ds(r0, 8), cols] = xr
            im_ref[pl.ds(r0, 8), cols] = xi
            if extra is not None:
                acc = extra(cols, jj, xr, xi, acc)
            b = 0 if reverse else 7
            return (jnp.broadcast_to(xr[b:b + 1, :], (8, S5_CH)), jnp.broadcast_to(xi[b:b + 1, :], (8, S5_CH)), acc)

        acc0 = (jnp.zeros((8, S5_CH), f32), jnp.zeros((8, S5_CH), f32)) if extra is not None else 0
        cr, ci, acc = lax.fori_loop(0, nt, step, (car_re[:, cols], car_im[:, cols], acc0), unroll=2 if extra else 4)
        car_re[:, cols] = cr
        car_im[:, cols] = ci
        accs.append(acc)
    return accs


def _s5_fwd(z, bmat, cmat, a_re, a_im, d_skip):
    L = z.shape[0]
    nb = L // S5_T

    def kern(u_ref, bre, bim, cre, cim, ar_ref, ai_ref, d_ref, y_ref, hbr_ref, hbi_ref, h_re, h_im, tab, car_re, car_im):
        @pl.when(pl.program_id(0) == 0)
        def _():
            _scan_tables(tab, ar_ref[...], ai_ref[...], False)
            car_re[...] = jnp.zeros_like(car_re)
            car_im[...] = jnp.zeros_like(car_im)

        hbr_ref[...] = car_re[...]
        hbi_ref[...] = car_im[...]
        u = u_ref[...]
        for hf in range(2):
            uh = u[:, hf * 256:(hf + 1) * 256]
            h_re[:, hf * 1024:(hf + 1) * 1024] = _dot(uh, bre[hf], NN)
            h_im[:, hf * 1024:(hf + 1) * 1024] = _dot(uh, bim[hf], NN)
        _scan(h_re, h_im, 0, S5_T, tab, car_re, car_im, False)
        ys = []
        for hf in range(2):
            cs = slice(hf * 1024, (hf + 1) * 1024)
            ys.append(_dot(h_re[:, cs], cre[hf], NT) - _dot(h_im[:, cs], cim[hf], NT))
        y_ref[...] = jnp.concatenate(ys, axis=1) + d_ref[...] * u

    fulls = [*bmat, *cmat, a_re, a_im, d_skip]
    return pl.pallas_call(
        kern, name="s5_fwd", grid=(nb,),
        in_specs=[_row(S5_T, SW, 0)] + [_full(a.shape) for a in fulls],
        out_specs=[_row(S5_T, SW), _row(8, NST), _row(8, NST)],
        out_shape=[SDS((L, SW), f32), SDS((nb * 8, NST), f32), SDS((nb * 8, NST), f32)],
        scratch_shapes=[pltpu.VMEM((S5_T, NST), f32), pltpu.VMEM((S5_T, NST), f32), pltpu.VMEM((8, 8, NST), f32),
                        pltpu.VMEM((8, NST), f32), pltpu.VMEM((8, NST), f32)],
        compiler_params=_cp(),
    )(z, *fulls)


def _s5_bwd(z, dy, hb_re, hb_im, bmat, cmat, a_re, a_im, d_skip):
    L = z.shape[0]
    nb = L // S5_T
    T = S5_T

    def kern(u_ref, dy_ref, hbr_ref, hbi_ref, bre, bim, cre, cim, ar_ref, ai_ref, d_ref,
             du_ref, dbre, dbim, dcre, dcim, dar, dai, dd,
             h_re, h_im, g_re, g_im, tabf, tabr, car_re, car_im, gcar_re, gcar_im):
        @pl.when(pl.program_id(0) == 0)
        def _():
            _scan_tables(tabf, ar_ref[...], ai_ref[...], False)
            _scan_tables(tabr, ar_ref[...], -ai_ref[...], True)
            gcar_re[...] = jnp.zeros_like(gcar_re)
            gcar_im[...] = jnp.zeros_like(gcar_im)
            for r in (dbre, dbim, dcre, dcim, dar, dai, dd):
                r[...] = jnp.zeros_like(r)

        u = u_ref[...]
        dyv = dy_ref[...]
        h_re[0:8, :] = hbr_ref[...]
        h_im[0:8, :] = hbi_ref[...]
        car_re[...] = hbr_ref[...]
        car_im[...] = hbi_ref[...]
        for hf in range(2):
            cs = slice(hf * 1024, (hf + 1) * 1024)
            uh = u[:, hf * 256:(hf + 1) * 256]
            dyh = dyv[:, hf * 256:(hf + 1) * 256]
            h_re[8:, cs] = _dot(uh, bre[hf], NN)
            h_im[8:, cs] = _dot(uh, bim[hf], NN)
            g_re[:, cs] = _dot(dyh, cre[hf], NN)
            g_im[:, cs] = -_dot(dyh, cim[hf], NN)
        _scan(h_re, h_im, 8, T, tabf, car_re, car_im, False)

        row = lax.broadcasted_iota(jnp.int32, (8, S5_CH), 0)

        def fold_da(cols, jj, gr, gi, acc):
            r0 = pl.multiple_of(jj * 8, 8)
            hr = jnp.where(row == 0, jnp.broadcast_to(h_re[pl.ds(r0, 8), cols][7:8, :], (8, S5_CH)),
                           pltpu.roll(h_re[pl.ds(r0 + 8, 8), cols], 1, 0))
            hi = jnp.where(row == 0, jnp.broadcast_to(h_im[pl.ds(r0, 8), cols][7:8, :], (8, S5_CH)),
                           pltpu.roll(h_im[pl.ds(r0 + 8, 8), cols], 1, 0))
            return acc[0] + gr * hr + gi * hi, acc[1] + gi * hr - gr * hi

        accs = _scan(g_re, g_im, 0, T, tabr, gcar_re, gcar_im, True, extra=fold_da)
        for cc, (ar_, ai_) in enumerate(accs):
            cols = slice(cc * S5_CH, (cc + 1) * S5_CH)
            dar[:, cols] += jnp.sum(ar_, axis=0, keepdims=True)
            dai[:, cols] += jnp.sum(ai_, axis=0, keepdims=True)

        dus = []
        for hf in range(2):
            cs = slice(hf * 1024, (hf + 1) * 1024)
            uh = u[:, hf * 256:(hf + 1) * 256]
            dyh = dyv[:, hf * 256:(hf + 1) * 256]
            gr = g_re[:, cs]
            gi = g_im[:, cs]
            dus.append(_dot(gr, bre[hf], NT) + _dot(gi, bim[hf], NT))
            dbre[hf] += _dot(uh, gr, TN)
            dbim[hf] += _dot(uh, gi, TN)
            dcre[hf] += _dot(dyh, h_re[8:, cs], TN)
            dcim[hf] -= _dot(dyh, h_im[8:, cs], TN)
        du_ref[...] = jnp.concatenate(dus, axis=1) + d_ref[...] * dyv
        dd[...] += jnp.sum(dyv * u, axis=0, keepdims=True)

    rev = lambda w: pl.BlockSpec((T, w), lambda k: (nb - 1 - k, 0))
    rev8 = pl.BlockSpec((8, NST), lambda k: (nb - 1 - k, 0))
    fulls = [*bmat, *cmat, a_re, a_im, d_skip]
    mat = ((2, 256, 1024), f32)
    accs = [mat, mat, mat, mat, ((1, NST), f32), ((1, NST), f32), ((1, SW), f32)]
    return pl.pallas_call(
        kern, name="s5_bwd", grid=(nb,),
        in_specs=[rev(SW), rev(SW), rev8, rev8] + [_full(a.shape) for a in fulls],
        out_specs=[rev(SW)] + [_full(s) for (s, _) in accs],
        out_shape=[SDS((L, SW), f32)] + [SDS(s, dt) for (s, dt) in accs],
        scratch_shapes=[pltpu.VMEM((T + 8, NST), f32), pltpu.VMEM((T + 8, NST), f32),
                        pltpu.VMEM((T, NST), f32), pltpu.VMEM((T, NST), f32),
                        pltpu.VMEM((8, 8, NST), f32), pltpu.VMEM((8, 8, NST), f32),
                        pltpu.VMEM((8, NST), f32), pltpu.VMEM((8, NST), f32),
                        pltpu.VMEM((8, NST), f32), pltpu.VMEM((8, NST), f32)],
        compiler_params=_cp(),
    )(z, dy, hb_re, hb_im, *fulls)


def _s5_disc(lam_re, lam_im, log_dt, bt_re, bt_im):
    dt = jnp.exp(log_dt)
    mag = jnp.exp(lam_re * dt)
    ab_re = mag * jnp.cos(lam_im * dt)
    ab_im = mag * jnp.sin(lam_im * dt)
    den = lam_re * lam_re + lam_im * lam_im
    nr = ab_re - 1.0
    coef_re = (nr * lam_re + ab_im * lam_im) / den
    coef_im = (ab_im * lam_re - nr * lam_im) / den
    rep = lambda c: jnp.broadcast_to(c[:, None, :], (NG, GS, NS)).reshape(NG * GS, NS)
    cr, ci = rep(coef_re), rep(coef_im)
    return ab_re, ab_im, cr * bt_re - ci * bt_im, cr * bt_im + ci * bt_re


def _s5_disc_fwd(*params):
    def kern(*refs):
        outs = _s5_disc(*[r[...] for r in refs[:5]])
        for r, v in zip(refs[5:], outs):
            r[...] = v

    shapes = [SDS((NG, NS), f32), SDS((NG, NS), f32), SDS((NG * GS, NS), f32), SDS((NG * GS, NS), f32)]
    return pl.pallas_call(kern, name="s5_disc_fwd", out_shape=shapes)(*params)


def _s5_disc_bwd(params, cts):
    def kern(*refs):
        _, vjp = jax.vjp(_s5_disc, *[r[...] for r in refs[:5]])
        for r, v in zip(refs[9:], vjp(tuple(r[...] for r in refs[5:9]))):
            r[...] = v

    return pl.pallas_call(kern, name="s5_disc_bwd", out_shape=[SDS(p.shape, f32) for p in params])(*params, *cts)


def _diag_mask():
    r = np.arange(256)[:, None] // GS
    c = np.arange(1024)[None, :] // NS
    return jnp.asarray(r == c, f32)


def _to_blockdiag(m):
    return jnp.tile(m.reshape(2, 256, NS), (1, 1, 16)) * _diag_mask()


def _from_blockdiag(dm):
    return (dm * _diag_mask()).reshape(2, 256, 16, NS).sum(axis=2).reshape(NG * GS, NS)


def _conv_fwd(z, w32, b, tm):
    L = z.shape[0]
    hb = tm // 32

    def kern(val, glu, hval, hglu, w_ref, b_ref, c_ref, win):
        i = pl.program_id(0)
        win[0:32, :] = hval[...] * jax.nn.sigmoid(hglu[...]) * (i > 0).astype(f32)
        win[32:, :] = val[...] * jax.nn.sigmoid(glu[...])
        acc = jnp.zeros((tm, CW), f32) + b_ref[...]
        for k in range(CK):
            acc = acc + w_ref[k:k + 1, :] * win[k + 2:k + 2 + tm, :]
        c_ref[...] = acc

    halo = lambda col: pl.BlockSpec((32, CW), lambda i: (jnp.maximum(i * hb - 1, 0), col))
    return pl.pallas_call(
        kern, name="conv_fwd", grid=(L // tm,),
        in_specs=[_row(tm, CW, 2), _row(tm, CW, 3), halo(2), halo(3), _full(w32.shape), _full(b.shape)],
        out_specs=_row(tm, CW), out_shape=SDS((L, CW), f32),
        scratch_shapes=[pltpu.VMEM((tm + 32, CW), f32)], compiler_params=_cp(),
    )(z, z, z, z, w32, b)


def _conv_bwd(z, dc, w32, tm):
    L = z.shape[0]
    hb = tm // 32
    nt = L // tm

    def kern(val, glu, hval, hglu, dc_ref, hdc, w_ref, dval_ref, dglu_ref, dw_ref, db_ref, win, dwin):
        i = pl.program_id(0)

        @pl.when(i == 0)
        def _():
            dw_ref[...] = jnp.zeros_like(dw_ref)
            db_ref[...] = jnp.zeros_like(db_ref)

        sg = jax.nn.sigmoid(glu[...])
        win[0:32, :] = hval[...] * jax.nn.sigmoid(hglu[...]) * (i > 0).astype(f32)
        win[32:, :] = val[...] * sg
        dcv = dc_ref[...]
        dwin[0:tm, :] = dcv
        dwin[tm:, :] = hdc[...] * (i < nt - 1).astype(f32)
        dv = jnp.zeros((tm, CW), f32)
        for k in range(CK):
            dv = dv + w_ref[k:k + 1, :] * dwin[30 - k:30 - k + tm, :]
            dw_ref[k:k + 1, :] += jnp.sum(dcv * win[k + 2:k + 2 + tm, :], axis=0, keepdims=True)
        db_ref[...] += jnp.sum(dcv, axis=0, keepdims=True)
        dval_ref[...] = dv * sg
        dglu_ref[...] = dv * val[...] * sg * (1.0 - sg)

    halo = lambda col: pl.BlockSpec((32, CW), lambda i: (jnp.maximum(i * hb - 1, 0), col))
    nxt = pl.BlockSpec((32, CW), lambda i: (jnp.minimum((i + 1) * hb, L // 32 - 1), 0))
    return pl.pallas_call(
        kern, name="conv_bwd", grid=(nt,),
        in_specs=[_row(tm, CW, 2), _row(tm, CW, 3), halo(2), halo(3), _row(tm, CW), nxt, _full(w32.shape)],
        out_specs=[_row(tm, CW), _row(tm, CW), _full((32, CW)), _full((1, CW))],
        out_shape=[SDS((L, CW), f32), SDS((L, CW), f32), SDS((32, CW), f32), SDS((1, CW), f32)],
        scratch_shapes=[pltpu.VMEM((tm + 32, CW), f32), pltpu.VMEM((tm + 32, CW), f32)], compiler_params=_cp(),
    )(z, z, z, z, dc, dc, w32)


def _ev_out_f(y, c, ga, gb, tap, glu_b, ln_g, ln_b, glu_w, w_out, mm):
    z1 = jax.nn.gelu(y)
    ya = z1 * jax.nn.sigmoid(mm(z1, glu_w) + glu_b + tap) * jax.nn.silu(ga)
    mu = jnp.mean(c, axis=-1, keepdims=True)
    var = jnp.mean(jnp.square(c - mu), axis=-1, keepdims=True)
    cn = (c - mu) * lax.rsqrt(var + EPS) * ln_g + ln_b
    cat = jnp.concatenate([ya, jax.nn.silu(cn) * jax.nn.silu(gb)], axis=1)
    return mm(cat, w_out), (z1, cat)


def _ev_out_fwd(x, y, c, z, glu_b, ln_g, ln_b, glu_w, w_out, tm):
    def body(rows, fulls):
        xv, yv, cv, ga, gb = rows
        gb_ref, lg_ref, lb_ref, gw_ref, wo_ref = fulls
        out, _ = _ev_out_f(yv, cv, ga, gb, 0.0, gb_ref[...], lg_ref[...], lb_ref[...], gw_ref[...], wo_ref[...], _plain_mm)
        return [xv + out], []

    return _tiled("ev_out_fwd", body, x.shape[0], tm, [(x, D, 0), (y, SW, 0), (c, CW, 0), (z, SW, 1), (z, CW, 4)],
                  [glu_b, ln_g, ln_b, glu_w, w_out], [(D, f32)])[0]


def _ev_out_bwd(dx1, y, c, z, glu_b, ln_g, ln_b, glu_w, w_out, tm):
    def body(rows, fulls):
        dxv, yv, cv, ga, gb = rows
        gb_ref, lg_ref, lb_ref, gw_ref, wo_ref = fulls
        gw, wo = gw_ref[...], wo_ref[...]
        f = lambda yv, cv, ga, gb, tap, b, lg, lb: _ev_out_f(yv, cv, ga, gb, tap, b, lg, lb, gw, wo, _mm)
        _, vjp, (z1, cat) = jax.vjp(f, yv, cv, ga, gb, jnp.zeros((tm, SW), f32), gb_ref[...], lg_ref[...], lb_ref[...],
                                    has_aux=True)
        dy, dc, dga, dgb, dtap, db, dlg, dlb = vjp(dxv)
        return [dy, dc, dga, dgb], [db, dlg, dlb, _wgrad(z1, dtap, gw), _wgrad(cat, dxv, wo)]

    return _tiled("ev_out_bwd", body, dx1.shape[0], tm, [(dx1, D, 0), (y, SW, 0), (c, CW, 0), (z, SW, 1), (z, CW, 4)],
                  [glu_b, ln_g, ln_b, glu_w, w_out], [(SW, f32), (CW, f32), (SW, f32), (CW, f32)],
                  [((1, SW), f32), ((1, CW), f32), ((1, CW), f32), (glu_w.shape, f32), (w_out.shape, f32)])


def _kv_f(mem, g, tap0, tap1, w0, w1, mm):
    mn = _rms(mem, g)
    return (mm(mn, w0) + tap0, mm(mn, w1) + tap1), mn


def _kv_fwd(mem, g, w0, w1):
    def body(rows, fulls):
        g_ref, w0_ref, w1_ref = fulls
        (kv0, kv1), _ = _kv_f(rows[0], g_ref[...], 0.0, 0.0, w0_ref[...], w1_ref[...], _plain_mm)
        return [kv0, kv1], []

    return _tiled("kv_fwd", body, MEM_LEN, MEM_LEN, [(mem, D, 0)], [g, w0, w1], [(2 * D, f32), (2 * D, f32)])


def _kv_bwd(mem, g, w0, w1, dkv0, dkv1):
    def body(rows, fulls):
        g_ref, w0_ref, w1_ref = fulls
        w0v, w1v = w0_ref[...], w1_ref[...]
        tap = jnp.zeros((MEM_LEN, 2 * D), f32)
        _, vjp, mn = jax.vjp(lambda g_, t0, t1: _kv_f(rows[0], g_, t0, t1, w0v, w1v, _mm), g_ref[...], tap, tap, has_aux=True)
        dg, d0, d1 = vjp((rows[1], rows[2]))
        return [], [dg, _wgrad(mn, d0, w0v), _wgrad(mn, d1, w1v)]

    return _tiled("kv_bwd", body, MEM_LEN, MEM_LEN, [(mem, D, 0), (dkv0, 2 * D, 0), (dkv1, 2 * D, 0)], [g, w0, w1], [],
                  [((1, D), f32), (w0.shape, f32), (w1.shape, f32)])


def _xa_f(x, g, kv, tap, w_qg, w_o, mm, nt, nn):
    h = _rms(x, g)
    qg = mm(h, w_qg) + tap
    outs = []
    for hd in range(XH):
        q = qg[:, hd * XD:(hd + 1) * XD]
        k = kv[:, hd * XD:(hd + 1) * XD]
        v = kv[:, D + hd * XD:D + (hd + 1) * XD]
        s = nt(q, k) * (XD ** -0.5)
        e = jnp.exp(s - jnp.max(s, axis=-1, keepdims=True))
        outs.append(nn(e / jnp.sum(e, axis=-1, keepdims=True), v))
    u = jnp.concatenate(outs, axis=1) * jax.nn.silu(qg[:, D:])
    return mm(u, w_o), (h, u)


def _xa_fwd(name, x, g, kv, w_qg, w_o, tm):
    def body(rows, fulls):
        g_ref, kv_ref, wq_ref, wo_ref = fulls
        out, _ = _xa_f(rows[0], g_ref[...], kv_ref[...], 0.0, wq_ref[...], wo_ref[...], _plain_mm, _plain_nt, _plain_mm)
        return [rows[0] + out], []

    return _tiled(name, body, x.shape[0], tm, [(x, D, 0)], [g, kv, w_qg, w_o], [(D, f32)])[0]


def _xa_bwd(name, x, dxo, g, kv, w_qg, w_o, tm):
    def body(rows, fulls):
        xv, dxv = rows
        g_ref, kv_ref, wq_ref, wo_ref = fulls
        wq, wo = wq_ref[...], wo_ref[...]
        f = lambda xv, gv, kvv, tap: _xa_f(xv, gv, kvv, tap, wq, wo, _mm, _mm_nt, _mm_nn)
        _, vjp, (h, u) = jax.vjp(f, xv, g_ref[...], kv_ref[...], jnp.zeros((tm, 2 * D), f32), has_aux=True)
        dx, dg, dkv, dtap = vjp(dxv)
        return [dx + dxv], [dg, dkv, _wgrad(h, dtap, wq), _wgrad(u, dxv, wo)]

    return _tiled(name, body, x.shape[0], tm, [(x, D, 0), (dxo, D, 0)], [g, kv, w_qg, w_o], [(D, f32)],
                  [((1, D), f32), ((MEM_LEN, 2 * D), f32), (w_qg.shape, f32), (w_o.shape, f32)])


ASTRIP = 16


def _attn_pad(k_ref, v_ref, kpad, vpad):
    kpad[0:PAD, :] = jnp.zeros((PAD, 128), bf16)
    vpad[0:PAD, :] = jnp.zeros((PAD, 128), bf16)
    kpad[PAD:, :] = k_ref[...].astype(bf16)
    vpad[PAD:, :] = v_ref[...].astype(bf16)


def _attn_probs(s_scr, b_ref, hh, rows, mask_row):
    x = s_scr[rows, :] + b_ref[hh, rows, :] + mask_row
    e = jnp.exp(x - jnp.max(x, axis=-1, keepdims=True))
    return e * (1.0 / jnp.sum(e, axis=-1, keepdims=True))


def _attn_fwd(qkvg, bias):
    L = qkvg.shape[0]
    nq = L // AQ

    def kern(q_ref, k_ref, v_ref, b_ref, o_ref, kpad, vpad, s_scr, p_scr):
        i = pl.program_id(1)

        @pl.when(i == 0)
        def _():
            _attn_pad(k_ref, v_ref, kpad, vpad)

        r0 = pl.multiple_of(i * AQ, AQ)
        q = q_ref[...] * (DH ** -0.5)
        kwin = kpad[pl.ds(r0, AW), :]
        vwin = vpad[pl.ds(r0, AW), :]
        lane_hi = lax.broadcasted_iota(jnp.int32, (AQ, 128), 1) // DH
        mask_row = jnp.where(lax.broadcasted_iota(jnp.int32, (1, AW), 1) >= PAD - r0, 0.0, -1e30)
        outs = []
        for hh in range(2):
            s_scr[...] = _dot(jnp.where(lane_hi == hh, q, 0.0), kwin, NT)

            def strip(r, carry, hh=hh):
                rows = pl.ds(pl.multiple_of(r * ASTRIP, ASTRIP), ASTRIP)
                p_scr[rows, :] = _attn_probs(s_scr, b_ref, hh, rows, mask_row).astype(bf16)
                return carry

            lax.fori_loop(0, AQ // ASTRIP, strip, 0, unroll=2)
            outs.append(_dot(p_scr[...], vwin, NN))
        o_ref[...] = jnp.where(lane_hi == 0, outs[0], outs[1])

    return pl.pallas_call(
        kern, name="attn_fwd", grid=(HEADS // 2, nq),
        in_specs=[pl.BlockSpec((AQ, 128), lambda j, i: (i, j)),
                  pl.BlockSpec((L, 128), lambda j, i: (0, 8 + j)),
                  pl.BlockSpec((L, 128), lambda j, i: (0, 16 + j)),
                  pl.BlockSpec((2, AQ, AW), lambda j, i: (j, 0, 0))],
        out_specs=pl.BlockSpec((AQ, 128), lambda j, i: (i, j)),
        out_shape=SDS((L, D), f32),
        scratch_shapes=[pltpu.VMEM((L + PAD, 128), bf16), pltpu.VMEM((L + PAD, 128), bf16),
                        pltpu.VMEM((AQ, AW), f32), pltpu.VMEM((AQ, AW), bf16)],
        compiler_params=_cp(("arbitrary", "arbitrary")),
    )(qkvg, qkvg, qkvg, bias)


def _attn_bwd(qkvg, do, bias):
    L = qkvg.shape[0]
    nq = L // AQ

    def kern(q_ref, k_ref, v_ref, b_ref, do_ref, dq_ref, dk_ref, dv_ref, db_ref, kpad, vpad, dkp, dvp,
             s_scr, dp_scr, p_scr, ds_scr):
        i = pl.program_id(1)

        @pl.when(i == 0)
        def _():
            _attn_pad(k_ref, v_ref, kpad, vpad)
            dkp[...] = jnp.zeros_like(dkp)
            dvp[...] = jnp.zeros_like(dvp)
            db_ref[...] = jnp.zeros_like(db_ref)

        r0 = pl.multiple_of(i * AQ, AQ)
        q = q_ref[...] * (DH ** -0.5)
        dov = do_ref[...]
        kwin = kpad[pl.ds(r0, AW), :]
        vwin = vpad[pl.ds(r0, AW), :]
        lane_hi = lax.broadcasted_iota(jnp.int32, (AQ, 128), 1) // DH
        mask_row = jnp.where(lax.broadcasted_iota(jnp.int32, (1, AW), 1) >= PAD - r0, 0.0, -1e30)
        dqs = []
        dk = jnp.zeros((AW, 128), f32)
        dv = jnp.zeros((AW, 128), f32)
        for hh in range(2):
            qm = jnp.where(lane_hi == hh, q, 0.0).astype(bf16)
            dom = jnp.where(lane_hi == hh, dov, 0.0).astype(bf16)
            s_scr[...] = _dot(qm, kwin, NT)
            dp_scr[...] = _dot(dom, vwin, NT)

            def strip(r, carry, hh=hh):
                rows = pl.ds(pl.multiple_of(r * ASTRIP, ASTRIP), ASTRIP)
                p = _attn_probs(s_scr, b_ref, hh, rows, mask_row)
                dp = dp_scr[rows, :]
                ds = p * (dp - jnp.sum(p * dp, axis=-1, keepdims=True))
                db_ref[hh, rows, :] += ds
                p_scr[rows, :] = p.astype(bf16)
                ds_scr[rows, :] = ds.astype(bf16)
                return carry

            lax.fori_loop(0, AQ // ASTRIP, strip, 0, unroll=2)
            dsb = ds_scr[...]
            dqs.append(_dot(dsb, kwin, NN) * (DH ** -0.5))
            dk = dk + _dot(dsb, qm, TN)
            dv = dv + _dot(p_scr[...], dom, TN)
        dq_ref[...] = jnp.where(lane_hi == 0, dqs[0], dqs[1])
        dkp[pl.ds(r0, AW), :] += dk
        dvp[pl.ds(r0, AW), :] += dv

        @pl.when(i == nq - 1)
        def _():
            dk_ref[...] = dkp[PAD:, :]
            dv_ref[...] = dvp[PAD:, :]

    return pl.pallas_call(
        kern, name="attn_bwd", grid=(HEADS // 2, nq),
        in_specs=[pl.BlockSpec((AQ, 128), lambda j, i: (i, j)),
                  pl.BlockSpec((L, 128), lambda j, i: (0, 8 + j)),
                  pl.BlockSpec((L, 128), lambda j, i: (0, 16 + j)),
                  pl.BlockSpec((2, AQ, AW), lambda j, i: (j, 0, 0)),
                  pl.BlockSpec((AQ, 128), lambda j, i: (i, j))],
        out_specs=[pl.BlockSpec((AQ, 128), lambda j, i: (i, j)),
                   pl.BlockSpec((L, 128), lambda j, i: (0, j)),
                   pl.BlockSpec((L, 128), lambda j, i: (0, j)),
                   pl.BlockSpec((2, AQ, AW), lambda j, i: (j, 0, 0))],
        out_shape=[SDS((L, D), f32), SDS((L, D), f32), SDS((L, D), f32), SDS((HEADS, AQ, AW), f32)],
        scratch_shapes=[pltpu.VMEM((L + PAD, 128), bf16), pltpu.VMEM((L + PAD, 128), bf16),
                        pltpu.VMEM((L + PAD, 128), f32), pltpu.VMEM((L + PAD, 128), f32),
                        pltpu.VMEM((AQ, AW), f32), pltpu.VMEM((AQ, AW), f32),
                        pltpu.VMEM((AQ, AW), bf16), pltpu.VMEM((AQ, AW), bf16)],
        compiler_params=_cp(("arbitrary", "arbitrary")),
    )(qkvg, qkvg, qkvg, bias, do)


_SKEW = AQ + AW - 1


def _rel_index():
    d = (AW - 1) - np.arange(_SKEW)
    return np.clip(d, -MAX_REL, MAX_REL) + MAX_REL


def _band_mask():
    qc = np.arange(AQ)[:, None] // CHUNK + LEFT
    kc = np.arange(AW)[None, :] // CHUNK
    return jnp.asarray(np.where((kc <= qc) & (kc >= qc - LEFT), 0.0, -1e30), f32)


def _bias_matrix(rel_bias):
    tv = jnp.take(rel_bias, jnp.asarray(_rel_index()), axis=1)
    flat = jnp.tile(jnp.pad(tv, ((0, 0), (0, 1))), (1, AQ))
    top = flat[:, AQ - 1:AQ - 1 + AQ * _SKEW].reshape(HEADS, AQ, _SKEW)[:, :, :AW]
    return top + _band_mask()


def _bias_grad(dbias, onehot):
    flat = jnp.pad(dbias, ((0, 0), (0, 0), (0, _SKEW - AW))).reshape(HEADS, AQ * _SKEW)
    z = jnp.pad(flat, ((0, 0), (AQ - 1, AQ * 1024 - AQ * _SKEW - (AQ - 1)))).reshape(HEADS, AQ, 1024)

    def kern(z_ref, oh_ref, out_ref):
        diag = jnp.sum(z_ref[...], axis=1)
        out_ref[...] = jnp.dot(diag, oh_ref[...], preferred_element_type=f32, precision=lax.Precision.HIGHEST)

    return pl.pallas_call(kern, name="bias_grad", out_shape=SDS((HEADS, 2 * MAX_REL + 1), f32),
                          compiler_params=_cp(()))(z, onehot)


def _rel_onehot():
    oh = np.zeros((1024, 2 * MAX_REL + 1), np.float32)
    oh[np.arange(_SKEW), _rel_index()] = 1.0
    return jnp.asarray(oh)


def _gated_out_fwd(name, x, o, gate_src, gate_col, w, tm):
    def body(rows, fulls):
        xv, ov, gv = rows
        return [xv + _dotw(ov * jax.nn.silu(gv), fulls[0][...])], []

    return _tiled(name, body, x.shape[0], tm, [(x, D, 0), (o, D, 0), (gate_src, D, gate_col)], [w], [(D, f32)])[0]


def _gated_out_bwd(name, dxo, o, gate_src, gate_col, w, tm):
    def body(rows, fulls):
        dxv, ov, gv = rows
        wv = fulls[0][...]
        u, vjp = jax.vjp(lambda ov, gv: ov * jax.nn.silu(gv), ov, gv)
        do, dg = vjp(_dotw_t(dxv, wv))
        return [do, dg], [_wgrad(u, dxv, wv)]

    return _tiled(name, body, dxo.shape[0], tm, [(dxo, D, 0), (o, D, 0), (gate_src, D, gate_col)], [w],
                  [(D, f32), (D, f32)], [(w.shape, f32)])


def _loss_head(x, g, target, tm):
    def body(rows, fulls):
        xv, tv = rows

        def f(xv, gv):
            e = jnp.square(_rms(xv, gv) - tv)
            return 0.5 * jnp.sum(jnp.mean(e, axis=-1, keepdims=True), axis=0, keepdims=True)

        loss, vjp = jax.vjp(f, xv, fulls[0][...])
        dx, dg = vjp(jnp.ones((1, 1), f32))
        return [dx], [jnp.broadcast_to(loss, (1, 128)), dg]

    return _tiled("loss_head", body, x.shape[0], tm, [(x, D, 0), (target, D, 0)], [g], [(D, f32)],
                  [((1, 128), f32), ((1, D), f32)])


def _local_step(x, mem, target, p):
    L = x.shape[0]
    TM = 256
    row = lambda v: v.reshape(1, -1)

    bt = lambda b: jnp.transpose(b, (0, 2, 1)).reshape(NG * GS, NS)
    disc_in = (p["ev_s5_lambda_re"], p["ev_s5_lambda_im"], p["ev_s5_log_dt"].reshape(NG, 1),
               bt(p["ev_s5_b_re"]), bt(p["ev_s5_b_im"]))
    ab_re, ab_im, bbt_re, bbt_im = _s5_disc_fwd(*disc_in)
    bmat = (_to_blockdiag(bbt_re).astype(bf16), _to_blockdiag(bbt_im).astype(bf16))
    cmat = (_to_blockdiag(p["ev_s5_c_re"].reshape(NG * GS, NS)).astype(bf16),
            _to_blockdiag(p["ev_s5_c_im"].reshape(NG * GS, NS)).astype(bf16))
    a_re, a_im = ab_re.reshape(1, NST), ab_im.reshape(1, NST)
    d_skip = row(p["ev_s5_d"])
    w32 = jnp.pad(p["ev_conv_w"], ((0, 1), (0, 0)))
    conv_b, ln_g, ln_b, glu_b = row(p["ev_conv_b"]), row(p["ev_conv_ln_g"]), row(p["ev_conv_ln_b"]), row(p["ev_s5_glu_b"])
    g_mem, g_ev, g_od, g_fin = row(p["mem_norm_g"]), row(p["ev_norm_g"]), row(p["od_norm_g"]), row(p["final_norm_g"])
    g_xa = [row(p["xa_norm_g"][l]) for l in range(2)]
    bias = _bias_matrix(p["od_rel_bias"])
    wqg, wkv, wo = p["xa_w_qg"], p["xa_w_kv"], p["xa_w_o"]

    kv0, kv1 = _kv_fwd(mem, g_mem, wkv[0], wkv[1])
    z = _norm_mm_fwd("ev_in_fwd", x, g_ev, p["ev_w_in"], TM)
    y_s5, hb_re, hb_im = _s5_fwd(z, bmat, cmat, a_re, a_im, d_skip)
    c = _conv_fwd(z, w32, conv_b, 512)
    x1 = _ev_out_fwd(x, y_s5, c, z, glu_b, ln_g, ln_b, p["ev_s5_glu_w"], p["ev_w_out"], TM)
    x2 = _xa_fwd("xa0_fwd", x1, g_xa[0], kv0, wqg[0], wo[0], TM)
    qkvg = _norm_mm_fwd("od_in_fwd", x2, g_od, p["od_w_in"], TM)
    o = _attn_fwd(qkvg, bias)
    x3 = _gated_out_fwd("od_out_fwd", x2, o, qkvg, 3, p["od_w_out"], TM)
    x4 = _xa_fwd("xa1_fwd", x3, g_xa[1], kv1, wqg[1], wo[1], TM)

    G = {}
    dx4, loss_row, G["final_norm_g"] = _loss_head(x4, g_fin, target, TM)
    dx3, dg_xa1, dkv1, dwqg1, dwo1 = _xa_bwd("xa1_bwd", x3, dx4, g_xa[1], kv1, wqg[1], wo[1], TM)
    do, dgate, G["od_w_out"] = _gated_out_bwd("od_out_bwd", dx3, o, qkvg, 3, p["od_w_out"], TM)
    dq, dk, dv, dbias = _attn_bwd(qkvg, do, bias)
    G["od_rel_bias"] = _bias_grad(dbias, _rel_onehot())
    dx2, G["od_norm_g"], G["od_w_in"] = _norm_mm_bwd(
        "od_in_bwd", x2, g_od, p["od_w_in"], [(dq, D, 0), (dk, D, 0), (dv, D, 0), (dgate, D, 0)], dx3, TM)
    dx1, dg_xa0, dkv0, dwqg0, dwo0 = _xa_bwd("xa0_bwd", x1, dx2, g_xa[0], kv0, wqg[0], wo[0], TM)
    G["xa_norm_g"] = jnp.concatenate([dg_xa0, dg_xa1], axis=0)
    G["xa_w_qg"], G["xa_w_o"] = [dwqg0, dwqg1], [dwo0, dwo1]
    G["mem_norm_g"], dwkv0, dwkv1 = _kv_bwd(mem, g_mem, wkv[0], wkv[1], dkv0, dkv1)
    G["xa_w_kv"] = [dwkv0, dwkv1]
    (dy_s5, dc, dga, dgb, G["ev_s5_glu_b"], G["ev_conv_ln_g"], G["ev_conv_ln_b"], G["ev_s5_glu_w"],
     G["ev_w_out"]) = _ev_out_bwd(dx1, y_s5, c, z, glu_b, ln_g, ln_b, p["ev_s5_glu_w"], p["ev_w_out"], TM)
    dval, dglu, dw32, G["ev_conv_b"] = _conv_bwd(z, dc, w32, 512)
    G["ev_conv_w"] = dw32[:CK]
    du, dbre, dbim, dcre, dcim, da_re, da_im, G["ev_s5_d"] = _s5_bwd(z, dy_s5, hb_re, hb_im, bmat, cmat, a_re, a_im, d_skip)
    G["ev_s5_c_re"], G["ev_s5_c_im"] = _from_blockdiag(dcre), _from_blockdiag(dcim)
    G["ev_s5_lambda_re"], G["ev_s5_lambda_im"], G["ev_s5_log_dt"], G["ev_s5_b_re"], G["ev_s5_b_im"] = _s5_disc_bwd(
        disc_in, (da_re.reshape(NG, NS), da_im.reshape(NG, NS), _from_blockdiag(dbre), _from_blockdiag(dbim)))
    dx, G["ev_norm_g"], G["ev_w_in"] = _norm_mm_bwd(
        "ev_in_bwd", x, g_ev, p["ev_w_in"], [(du, SW, 0), (dga, SW, 0), (dval, CW, 0), (dglu, CW, 0), (dgb, CW, 0)], dx1, TM)
    return loss_row[0, 0], dx, G


_ANY = pl.BlockSpec(memory_space=pl.ANY)


def _mesh_pos():
    return tuple(lax.axis_index(a) for a in AXES)


def _slot(ref, dev):
    return ref.at[4 * dev[0] + 2 * dev[1] + dev[2]]


def _all_gather(payloads):
    n = len(payloads)

    def body(*refs):
        ins, outs = refs[:n], refs[n:2 * n]
        send_sems, recv_sems, local_sems = refs[2 * n:]
        x, y, c = _mesh_pos()
        me, sibling = (x, y, c), (x, y, 1 - c)
        chips = [(1 - x, y), (x, 1 - y), (1 - x, 1 - y)]

        def copy(p, k, block, to, src=None):
            return pltpu.make_async_remote_copy(
                src_ref=_slot(outs[p], block) if src is None else src, dst_ref=_slot(outs[p], block),
                send_sem=send_sems.at[7 * p + k], recv_sem=recv_sems.at[7 * p + k],
                device_id=to, device_id_type=pl.DeviceIdType.MESH)

        mine = [pltpu.make_async_copy(ins[p], _slot(outs[p], me), local_sems.at[p]) for p in range(n)]
        for cp in mine:
            cp.start()
        first = []
        for p in range(n):
            first.append(copy(p, 0, me, sibling, src=ins[p]))
            first += [copy(p, 1 + j, me, (*chip, c), src=ins[p]) for j, chip in enumerate(chips)]
        for cp in first:
            cp.start()
        passed = []
        for j, chip in enumerate(chips):
            for p in range(n):
                copy(p, 1 + j, (*chip, c), me).wait_recv()
                passed.append(copy(p, 4 + j, (*chip, c), sibling))
                passed[-1].start()
        for p in range(n):
            copy(p, 0, sibling, me).wait_recv()
            for j, chip in enumerate(chips):
                copy(p, 4 + j, (*chip, 1 - c), me).wait_recv()
        for cp in first + passed:
            cp.wait_send()
        for cp in mine:
            cp.wait()

    return pl.pallas_call(
        body, name="all_gather", in_specs=[_ANY] * n, out_specs=[_ANY] * n,
        out_shape=[SDS((NDEV, *a.shape), a.dtype) for a in payloads],
        scratch_shapes=[pltpu.SemaphoreType.DMA((7 * n,)), pltpu.SemaphoreType.DMA((7 * n,)), pltpu.SemaphoreType.DMA((n,))],
    )(*payloads)


def _exchange(slotted, whole):
    ns, n = len(slotted), len(slotted) + len(whole)

    def body(*refs):
        ins, outs = refs[:n], refs[n:2 * n]
        send_sems, recv_sems, local_sems = refs[2 * n:]
        x, y, c = _mesh_pos()
        me = (x, y, c)
        src = lambda p, dev: _slot(ins[p], dev) if p < ns else ins[p]
        mine = [pltpu.make_async_copy(src(p, me), _slot(outs[p], me), local_sems.at[p]) for p in range(n)]
        for cp in mine:
            cp.start()
        copies = []
        for k in range(1, NDEV):
            flip = lambda v, bit: 1 - v if (k >> bit) & 1 else v
            peer = (flip(x, 2), flip(y, 1), flip(c, 0))
            for p in range(n):
                copies.append(pltpu.make_async_remote_copy(
                    src_ref=src(p, peer), dst_ref=_slot(outs[p], me), send_sem=send_sems.at[7 * p + k - 1],
                    recv_sem=recv_sems.at[7 * p + k - 1], device_id=peer, device_id_type=pl.DeviceIdType.MESH))
        for cp in copies:
            cp.start()
        for cp in copies:
            cp.wait()
        for cp in mine:
            cp.wait()

    return pl.pallas_call(
        body, name="grad_exchange", in_specs=[_ANY] * n, out_specs=[_ANY] * n,
        out_shape=[SDS(a.shape, a.dtype) for a in slotted] + [SDS((NDEV, *a.shape), a.dtype) for a in whole],
        scratch_shapes=[pltpu.SemaphoreType.DMA((7 * n,)), pltpu.SemaphoreType.DMA((7 * n,)), pltpu.SemaphoreType.DMA((n,))],
    )(*slotted, *whole)


def _adamw_math(g, w, m, v):
    m2 = ADAM_B1 * m + (1.0 - ADAM_B1) * g
    v2 = ADAM_B2 * v + (1.0 - ADAM_B2) * jnp.square(g)
    m_hat = m2 / (1.0 - ADAM_B1 ** ADAM_STEP)
    v_hat = v2 / (1.0 - ADAM_B2 ** ADAM_STEP)
    return -ADAM_LR * (m_hat / (jnp.sqrt(v_hat) + ADAM_EPS) + ADAM_WD * w), m2, v2


def _sum_slots(r_ref, rows=None):
    acc = r_ref[0].astype(f32) if rows is None else r_ref[0, rows[0]:rows[1], :].astype(f32)
    for k in range(1, NDEV):
        acc = acc + (r_ref[k] if rows is None else r_ref[k, rows[0]:rows[1], :]).astype(f32)
    return acc


def _sum_adamw(name, recv, off, w, m, v, tr):
    rows, C = w.shape

    def kern(r_ref, w_ref, m_ref, v_ref, g_ref, d_ref, m2_ref, v2_ref):
        g = _sum_slots(r_ref)
        g_ref[...] = g
        d_ref[...], m2_ref[...], v2_ref[...] = _adamw_math(g, w_ref[...], m_ref[...], v_ref[...])

    blk = pl.BlockSpec((tr, C), lambda i: (i, 0))
    return pl.pallas_call(
        kern, name=name, grid=(rows // tr,),
        in_specs=[pl.BlockSpec((NDEV, tr, C), lambda i: (0, off // tr + i, 0)), blk, blk, blk],
        out_specs=[blk] * 4, out_shape=[SDS((rows, C), f32)] * 4, compiler_params=_cp(),
    )(recv, w, m, v)


def _small_adamw(recvs, table, wmv):
    rnames = list(recvs)
    nr, nw = len(rnames), len(table)

    def kern(*refs):
        rr = dict(zip(rnames, refs[:nr]))
        ins, outs = refs[nr:nr + 3 * nw], refs[nr + 3 * nw:]
        for i, (_, shape, src, r0) in enumerate(table):
            g = _sum_slots(rr[src], (r0, r0 + shape[0]))[:, :shape[1]]
            outs[4 * i][...] = g
            res = _adamw_math(g, ins[3 * i][...], ins[3 * i + 1][...], ins[3 * i + 2][...])
            for o, val in zip(outs[4 * i + 1:4 * i + 4], res):
                o[...] = val

    flat = [t for (n, _, _, _) in table for t in wmv[n]]
    res = pl.pallas_call(
        kern, name="small_adamw", out_shape=[SDS(shape, f32) for (_, shape, _, _) in table for _ in range(4)],
        compiler_params=_cp(()),
    )(*[recvs[n] for n in rnames], *flat)
    return {n: tuple(res[4 * i:4 * i + 4]) for i, (n, _, _, _) in enumerate(table)}


WEIGHTS = ["mem_norm_g", "ev_norm_g", "ev_w_in", "ev_s5_lambda_re", "ev_s5_lambda_im", "ev_s5_log_dt", "ev_s5_b_re",
           "ev_s5_b_im", "ev_s5_c_re", "ev_s5_c_im", "ev_s5_d", "ev_s5_glu_w", "ev_s5_glu_b", "ev_conv_w", "ev_conv_b",
           "ev_conv_ln_g", "ev_conv_ln_b", "ev_w_out", "od_norm_g", "od_w_in", "od_rel_bias", "od_w_out", "xa_norm_g",
           "xa_w_qg", "xa_w_kv", "xa_w_o", "final_norm_g"]
PER_LAYER = ("xa_w_qg", "xa_w_kv", "xa_w_o")
GROUPS = [(1024, [("ev_w_out", "rows", 128), ("od_w_out", "rows", 128), ("xa_w_o", "rows", 128)]),
          (512, [("od_w_in", "cols", 1024), ("ev_s5_glu_w", "rows", 64)]),
          (256, [("xa_w_qg", "cols", 1024), ("xa_w_kv", "cols", 1024)]),
          (320, [("ev_w_in", "cols", 1024)])]
ADAM_ROWS = {1024: 128, 512: 64, 256: 512, 320: 256}
SMALL = [("mem_norm_g", (1, D), "b1024", 0), ("ev_norm_g", (1, D), "b1024", 1), ("xa_norm_g", (2, D), "b1024", 2),
         ("final_norm_g", (1, D), "b1024", 4),
         ("ev_s5_d", (1, SW), "b512", 0), ("ev_s5_glu_b", (1, SW), "b512", 1), ("ev_conv_b", (1, CW), "b512", 2),
         ("ev_conv_ln_g", (1, CW), "b512", 3), ("ev_conv_ln_b", (1, CW), "b512", 4),
         ("ev_s5_c_re", (NG * GS, NS), "b64", 0), ("ev_s5_c_im", (NG * GS, NS), "b64", 512),
         ("ev_s5_b_re", (NG * GS, NS), "b64", 1024), ("ev_s5_b_im", (NG * GS, NS), "b64", 1536),
         ("ev_s5_lambda_re", (NG, NS), "b64", 2048), ("ev_s5_lambda_im", (NG, NS), "b64", 2080),
         ("ev_s5_log_dt", (1, NG), "b64", 2112),
         ("od_rel_bias", (HEADS, 2 * MAX_REL + 1), "b257", 0),
         ("ev_conv_w", (CK, CW // NDEV), "cw", 0), ("od_norm_g", (1, D // NDEV), "on", 0)]
S5_B = ("ev_s5_b_re", "ev_s5_b_im")


def _pad_rows(a, rows):
    return jnp.pad(a, ((0, rows - a.shape[0]), (0, 0)))


def _weight_views(gathered):
    views = {}
    for minor, members in GROUPS:
        off = 0
        for name, kind, rows in members:
            layers = 2 if name in PER_LAYER else 1
            vs = [_W(gathered[minor], rows, (off + l * rows) // rows, kind) for l in range(layers)]
            views[name] = vs if layers == 2 else vs[0]
            off += layers * rows
    return views


def _bt(b):
    return jnp.transpose(b.reshape(NG, NS, GS), (0, 2, 1)).reshape(NG * GS, NS)


def _unbt(b, shape):
    return jnp.transpose(b.reshape(NG, GS, NS), (0, 2, 1)).reshape(shape)


def kernel(*args):
    names = ["x", "mem", *WEIGHTS, "loss_target", *["m_" + n for n in WEIGHTS], *["v_" + n for n in WEIGHTS]]
    a = dict(zip(names, args, strict=True))

    payloads = [jnp.concatenate([a[n].reshape(-1, minor).astype(bf16) for n, _, _ in members], axis=0)
                for minor, members in GROUPS]
    other = jnp.concatenate([_pad_rows(a["ev_conv_w"][0], 32).reshape(16, 128), _pad_rows(a["od_norm_g"], 8)], axis=0)
    *gathered, gother = _all_gather([*payloads, other])
    p = {n: (a[n] if n in ("xa_norm_g", "mem_norm_g", "final_norm_g") else a[n][0]) for n in WEIGHTS
         if not any(n == m[0] for _, ms in GROUPS for m in ms)}
    p.update(_weight_views({minor: g for (minor, _), g in zip(GROUPS, gathered)}))
    p["ev_w_in"] = jnp.transpose(gathered[3], (1, 0, 2)).reshape(D, EVEN_IN)
    p["ev_conv_w"] = jnp.transpose(gother[:, :16].reshape(NDEV, 32, CW // NDEV)[:, :CK], (1, 0, 2)).reshape(CK, CW)
    p["od_norm_g"] = gother[:, 16].reshape(D)

    loss_part, dx, G = _local_step(a["x"][0], a["mem"][0], a["loss_target"][0], p)

    slot3 = lambda g, rows: g.reshape(NDEV, rows, g.shape[-1]) if g.ndim == 2 else g
    G["ev_w_in"] = jnp.transpose(G["ev_w_in"].reshape(D, NDEV, EVEN_IN // NDEV), (1, 0, 2))
    slotted = []
    for minor, members in GROUPS:
        parts = [slot3(g, rows) for n, _, rows in members for g in (G[n] if n in PER_LAYER else [G[n]])]
        slotted.append(jnp.concatenate([t.astype(bf16) for t in parts], axis=1))
    cw = jnp.transpose(G["ev_conv_w"].reshape(CK, NDEV, CW // NDEV), (1, 0, 2))
    slotted += [jnp.pad(cw, ((0, 0), (0, 1), (0, 0))), G["od_norm_g"].reshape(NDEV, 1, D // NDEV)]
    whole = {"b1024": _pad_rows(jnp.concatenate([G[n] for n in ("mem_norm_g", "ev_norm_g", "xa_norm_g", "final_norm_g")]), 8),
             "b512": _pad_rows(jnp.concatenate([G[n] for n in ("ev_s5_d", "ev_s5_glu_b", "ev_conv_b", "ev_conv_ln_g",
                                                                 "ev_conv_ln_b")]), 8),
             "b64": _pad_rows(jnp.concatenate([G[n] for n in ("ev_s5_c_re", "ev_s5_c_im", "ev_s5_b_re", "ev_s5_b_im",
                                                                "ev_s5_lambda_re", "ev_s5_lambda_im")]
                                              + [jnp.pad(G["ev_s5_log_dt"].reshape(1, NG), ((0, 0), (0, NS - NG)))]), 2120),
             "b257": G["od_rel_bias"]}
    recv = _exchange(slotted, list(whole.values()))
    recvs = dict(zip([m for m, _ in GROUPS] + ["cw", "on"] + list(whole), recv))

    res = {}
    for minor, members in GROUPS:
        off = 0
        for n, _, rows in members:
            rows *= 2 if n in PER_LAYER else 1
            w, m, v = (a[k + n].reshape(rows, minor) for k in ("", "m_", "v_"))
            res[n] = _sum_adamw("adamw_" + n, recvs[minor], off, w, m, v, ADAM_ROWS[minor])
            off += rows
    two_d = lambda n, t: _bt(t[0]) if n in S5_B else t.reshape(dict((s[0], s[1]) for s in SMALL)[n])
    res.update(_small_adamw({k: recvs[k] for k in ("b1024", "b512", "b64", "b257", "cw", "on")}, SMALL,
                            {n: tuple(two_d(n, a[k + n]) for k in ("", "m_", "v_")) for n, _, _, _ in SMALL}))
    shaped = lambda n, t: _unbt(t, a[n].shape) if n in S5_B else t.reshape(a[n].shape)
    loss = lax.psum(loss_part, AXES)
    return (loss, dx[None], *[shaped(n, res[n][k]) for k in range(4) for n in WEIGHTS])
```

```python
import functools
from typing import NamedTuple

import jax
import jax.numpy as jnp
import numpy as np
from jax import lax
from jax.experimental import pallas as pl
from jax.experimental.pallas import tpu as pltpu

f32, bf16 = jnp.float32, jnp.bfloat16
SDS = jax.ShapeDtypeStruct

D = 1024
SW = 512
NG, GS, NS = 32, 16, 64
NST = NG * NS
CW = 512
CK = 31
EVEN_IN = 2 * SW + 3 * CW
HEADS, DH = 16, 64
CHUNK, LEFT = 64, 8
PAD = LEFT * CHUNK
MAX_REL = 128
MEM_LEN = 256
XH, XD = 4, 256
EPS = 1e-6
NDEV = 8
AXES = ("x", "y", "c")

ADAM_LR, ADAM_B1, ADAM_B2, ADAM_EPS, ADAM_WD, ADAM_STEP = 0.001, 0.9, 0.999, 1e-08, 0.01, 10

VMEM_LIMIT = 56 << 20
S5_T = 512
S5_CH = 512
AQ = 256
AW = AQ + PAD

NN = ((1,), (0,))
NT = ((1,), (1,))
TN = ((0,), (0,))


def _dot(a, b, dims):
    return lax.dot_general(a.astype(bf16), b.astype(bf16), (dims, ((), ())), preferred_element_type=f32)


def _dotw(a, w):
    if w.ndim == 2:
        return _dot(a, w, NN)
    return jnp.concatenate([_dot(a, w[j], NN) for j in range(w.shape[0])], axis=1)


def _dotw_t(g, w):
    if w.ndim == 2:
        return _dot(g, w, NT)
    n = w.shape[2]
    out = _dot(g[:, :n], w[0], NT)
    for j in range(1, w.shape[0]):
        out = out + _dot(g[:, j * n:(j + 1) * n], w[j], NT)
    return out


def _wgrad(a, g, w):
    if w.ndim == 2:
        return _dot(a, g, TN)
    n = w.shape[2]
    return jnp.stack([_dot(a, g[:, j * n:(j + 1) * n], TN) for j in range(w.shape[0])])


@jax.custom_vjp
def _mm(a, w):
    return _dotw(a, w)


def _mm_f(a, w):
    return _dotw(a, w), w


def _mm_b(w, g):
    return _dotw_t(g, w), None


_mm.defvjp(_mm_f, _mm_b)


@jax.custom_vjp
def _mm_nt(a, b):
    return _dot(a, b, NT)


def _mm_nt_f(a, b):
    return _dot(a, b, NT), (a, b)


def _mm_nt_b(res, g):
    a, b = res
    return _dot(g, b, NN), _dot(g, a, TN)


_mm_nt.defvjp(_mm_nt_f, _mm_nt_b)


@jax.custom_vjp
def _mm_nn(a, b):
    return _dot(a, b, NN)


def _mm_nn_f(a, b):
    return _dot(a, b, NN), (a, b)


def _mm_nn_b(res, g):
    a, b = res
    return _dot(g, b, NT), _dot(a, g, TN)


_mm_nn.defvjp(_mm_nn_f, _mm_nn_b)


def _plain_mm(a, w):
    return _dotw(a, w)


def _plain_nt(a, b):
    return _dot(a, b, NT)


def _rms(x, g):
    return x * lax.rsqrt(jnp.mean(x * x, axis=-1, keepdims=True) + EPS) * g


def _cp(sem=("arbitrary",)):
    return pltpu.CompilerParams(dimension_semantics=sem, vmem_limit_bytes=VMEM_LIMIT)


def _row(tm, w, col=0):
    return pl.BlockSpec((tm, w), lambda i: (i, col))


def _full(shape):
    nd = len(shape)
    return pl.BlockSpec(tuple(shape), lambda i: (0,) * nd, pipeline_mode=pl.Buffered(1))


class _W(NamedTuple):
    arr: jax.Array
    rows: int
    idx: int
    kind: str

    @property
    def shape(self):
        c = self.arr.shape[2]
        return (NDEV * self.rows, c) if self.kind == "rows" else (NDEV, self.rows, c)


class _WRef:
    def __init__(self, ref, kind):
        self.ref, self.kind = ref, kind

    def __getitem__(self, _):
        v = self.ref[...]
        return v.reshape(v.shape[0] * v.shape[1], v.shape[2]) if self.kind == "rows" else v


def _wspec(w):
    if isinstance(w, _W):
        return pl.BlockSpec((NDEV, w.rows, w.arr.shape[2]), lambda i: (0, w.idx, 0), pipeline_mode=pl.Buffered(1))
    return _full(w.shape)


def _tiled(name, body, L, tm, row_ins, full_ins, row_outs, acc_outs=()):
    nr, nf, no = len(row_ins), len(full_ins), len(row_outs)

    def kern(*refs):
        rin = refs[:nr]
        fin = [_WRef(r, w.kind) if isinstance(w, _W) else r for r, w in zip(refs[nr:nr + nf], full_ins)]
        rout, aout = refs[nr + nf:nr + nf + no], refs[nr + nf + no:]
        routs, accs = body([r[...] for r in rin], fin)
        for r, v in zip(rout, routs):
            r[...] = v.astype(r.dtype)
        if aout:
            @pl.when(pl.program_id(0) == 0)
            def _():
                for a in aout:
                    a[...] = jnp.zeros(a.shape, a.dtype)

            for a, v in zip(aout, accs):
                a[...] += v.astype(a.dtype)

    return pl.pallas_call(
        kern, name=name, grid=(L // tm,),
        in_specs=[_row(tm, w, c) for (_, w, c) in row_ins] + [_wspec(a) for a in full_ins],
        out_specs=[_row(tm, w) for (w, _) in row_outs] + [_full(s) for (s, _) in acc_outs],
        out_shape=[SDS((L, w), dt) for (w, dt) in row_outs] + [SDS(tuple(s), dt) for (s, dt) in acc_outs],
        compiler_params=_cp(),
    )(*[a for (a, _, _) in row_ins], *[a.arr if isinstance(a, _W) else a for a in full_ins])


def _norm_mm_fwd(name, x, g, w, tm):
    def body(rows, fulls):
        g_ref, w_ref = fulls
        return [_dotw(_rms(rows[0], g_ref[...]), w_ref[...])], []

    n_out = w.shape[1] if len(w.shape) == 2 else w.shape[0] * w.shape[2]
    return _tiled(name, body, x.shape[0], tm, [(x, D, 0)], [g, w], [(n_out, f32)])[0]


def _norm_mm_bwd(name, x, g, w, dz_parts, dres, tm):
    n = len(dz_parts)

    def body(rows, fulls):
        g_ref, w_ref = fulls
        dz = rows[1] if n == 1 else jnp.concatenate(rows[1:1 + n], axis=1)
        wv = w_ref[...]
        h, vjp = jax.vjp(_rms, rows[0], g_ref[...])
        dx, dg = vjp(_dotw_t(dz, wv))
        return [dx + rows[1 + n]], [dg, _wgrad(h, dz, wv)]

    return _tiled(name, body, x.shape[0], tm, [(x, D, 0), *dz_parts, (dres, D, 0)], [g, w],
                  [(D, f32)], [((1, D), f32), (tuple(w.shape), f32)])


def _scan_tables(tab_ref, ar, ai, reverse):
    pw = [(ar, ai)]
    for _ in range(7):
        pr, pi = pw[-1]
        pw.append((pr * ar - pi * ai, pr * ai + pi * ar))
    row = lax.broadcasted_iota(jnp.int32, (8, NST), 0)
    for si, s in enumerate((1, 2, 4)):
        keep = (row <= 7 - s) if reverse else (row >= s)
        tab_ref[2 * si] = jnp.where(keep, pw[s - 1][0], 0.0)
        tab_ref[2 * si + 1] = jnp.where(keep, pw[s - 1][1], 0.0)
    pr = jnp.zeros((8, NST), f32)
    pi = jnp.zeros((8, NST), f32)
    for t in range(8):
        k = (7 - t) if reverse else t
        pr = jnp.where(row == t, pw[k][0], pr)
        pi = jnp.where(row == t, pw[k][1], pi)
    tab_ref[6] = pr
    tab_ref[7] = pi


def _scan(re_ref, im_ref, row0, nrows, tab_ref, car_re, car_im, reverse, extra=None):
    nt = nrows // 8
    accs = []
    for cc in range(NST // S5_CH):
        cols = slice(cc * S5_CH, (cc + 1) * S5_CH)

        def step(j, carry, cols=cols):
            cr, ci, acc = carry
            jj = (nt - 1 - j) if reverse else j
            r0 = pl.multiple_of(row0 + jj * 8, 8)
            xr = re_ref[pl.ds(r0, 8), cols]
            xi = im_ref[pl.ds(r0, 8), cols]
            for si, s in enumerate((1, 2, 4)):
                sh = (8 - s) if reverse else s
                yr = pltpu.roll(xr, sh, 0)
                yi = pltpu.roll(xi, sh, 0)
                tr = tab_ref[2 * si, :, cols]
                ti = tab_ref[2 * si + 1, :, cols]
                xr, xi = xr + tr * yr - ti * yi, xi + tr * yi + ti * yr
            tr = tab_ref[6, :, cols]
            ti = tab_ref[7, :, cols]
            xr, xi = xr + tr * cr - ti * ci, xi + tr * ci + ti * cr
            re_ref[pl.ds(r0, 8), cols] = xr
            im_ref[pl.ds(r0, 8), cols] = xi
            if extra is not None:
                acc = extra(cols, jj, xr, xi, acc)
            b = 0 if reverse else 7
            return (jnp.broadcast_to(xr[b:b + 1, :], (8, S5_CH)), jnp.broadcast_to(xi[b:b + 1, :], (8, S5_CH)), acc)

        acc0 = (jnp.zeros((8, S5_CH), f32), jnp.zeros((8, S5_CH), f32)) if extra is not None else 0
        cr, ci, acc = lax.fori_loop(0, nt, step, (car_re[:, cols], car_im[:, cols], acc0))
        car_re[:, cols] = cr
        car_im[:, cols] = ci
        accs.append(acc)
    return accs


def _s5_fwd(z, bmat, cmat, a_re, a_im, d_skip):
    L = z.shape[0]
    nb = L // S5_T

    def kern(u_ref, bre, bim, cre, cim, ar_ref, ai_ref, d_ref, y_ref, hbr_ref, hbi_ref, h_re, h_im, tab, car_re, car_im):
        @pl.when(pl.program_id(0) == 0)
        def _():
            _scan_tables(tab, ar_ref[...], ai_ref[...], False)
            car_re[...] = jnp.zeros_like(car_re)
            car_im[...] = jnp.zeros_like(car_im)

        hbr_ref[...] = car_re[...]
        hbi_ref[...] = car_im[...]
        u = u_ref[...]
        for hf in range(2):
            uh = u[:, hf * 256:(hf + 1) * 256]
            h_re[:, hf * 1024:(hf + 1) * 1024] = _dot(uh, bre[hf], NN)
            h_im[:, hf * 1024:(hf + 1) * 1024] = _dot(uh, bim[hf], NN)
        _scan(h_re, h_im, 0, S5_T, tab, car_re, car_im, False)
        ys = []
        for hf in range(2):
            cs = slice(hf * 1024, (hf + 1) * 1024)
            ys.append(_dot(h_re[:, cs], cre[hf], NT) - _dot(h_im[:, cs], cim[hf], NT))
        y_ref[...] = jnp.concatenate(ys, axis=1) + d_ref[...] * u

    fulls = [*bmat, *cmat, a_re, a_im, d_skip]
    return pl.pallas_call(
        kern, name="s5_fwd", grid=(nb,),
        in_specs=[_row(S5_T, SW, 0)] + [_full(a.shape) for a in fulls],
        out_specs=[_row(S5_T, SW), _row(8, NST), _row(8, NST)],
        out_shape=[SDS((L, SW), f32), SDS((nb * 8, NST), f32), SDS((nb * 8, NST), f32)],
        scratch_shapes=[pltpu.VMEM((S5_T, NST), f32), pltpu.VMEM((S5_T, NST), f32), pltpu.VMEM((8, 8, NST), f32),
                        pltpu.VMEM((8, NST), f32), pltpu.VMEM((8, NST), f32)],
        compiler_params=_cp(),
    )(z, *fulls)


def _s5_bwd(z, dy, hb_re, hb_im, bmat, cmat, a_re, a_im, d_skip):
    L = z.shape[0]
    nb = L // S5_T
    T = S5_T

    def kern(u_ref, dy_ref, hbr_ref, hbi_ref, bre, bim, cre, cim, ar_ref, ai_ref, d_ref,
             du_ref, dbre, dbim, dcre, dcim, dar, dai, dd,
             h_re, h_im, g_re, g_im, tabf, tabr, car_re, car_im, gcar_re, gcar_im):
        @pl.when(pl.program_id(0) == 0)
        def _():
            _scan_tables(tabf, ar_ref[...], ai_ref[...], False)
            _scan_tables(tabr, ar_ref[...], -ai_ref[...], True)
            gcar_re[...] = jnp.zeros_like(gcar_re)
            gcar_im[...] = jnp.zeros_like(gcar_im)
            for r in (dbre, dbim, dcre, dcim, dar, dai, dd):
                r[...] = jnp.zeros_like(r)

        u = u_ref[...]
        dyv = dy_ref[...]
        h_re[0:8, :] = hbr_ref[...]
        h_im[0:8, :] = hbi_ref[...]
        car_re[...] = hbr_ref[...]
        car_im[...] = hbi_ref[...]
        for hf in range(2):
            cs = slice(hf * 1024, (hf + 1) * 1024)
            uh = u[:, hf * 256:(hf + 1) * 256]
            dyh = dyv[:, hf * 256:(hf + 1) * 256]
            h_re[8:, cs] = _dot(uh, bre[hf], NN)
            h_im[8:, cs] = _dot(uh, bim[hf], NN)
            g_re[:, cs] = _dot(dyh, cre[hf], NN)
            g_im[:, cs] = -_dot(dyh, cim[hf], NN)
        _scan(h_re, h_im, 8, T, tabf, car_re, car_im, False)

        row = lax.broadcasted_iota(jnp.int32, (8, S5_CH), 0)

        def fold_da(cols, jj, gr, gi, acc):
            r0 = pl.multiple_of(jj * 8, 8)
            hr = jnp.where(row == 0, jnp.broadcast_to(h_re[pl.ds(r0, 8), cols][7:8, :], (8, S5_CH)),
                           pltpu.roll(h_re[pl.ds(r0 + 8, 8), cols], 1, 0))
            hi = jnp.where(row == 0, jnp.broadcast_to(h_im[pl.ds(r0, 8), cols][7:8, :], (8, S5_CH)),
                           pltpu.roll(h_im[pl.ds(r0 + 8, 8), cols], 1, 0))
            return acc[0] + gr * hr + gi * hi, acc[1] + gi * hr - gr * hi

        accs = _scan(g_re, g_im, 0, T, tabr, gcar_re, gcar_im, True, extra=fold_da)
        for cc, (ar_, ai_) in enumerate(accs):
            cols = slice(cc * S5_CH, (cc + 1) * S5_CH)
            dar[:, cols] += jnp.sum(ar_, axis=0, keepdims=True)
            dai[:, cols] += jnp.sum(ai_, axis=0, keepdims=True)

        dus = []
        for hf in range(2):
            cs = slice(hf * 1024, (hf + 1) * 1024)
            uh = u[:, hf * 256:(hf + 1) * 256]
            dyh = dyv[:, hf * 256:(hf + 1) * 256]
            gr = g_re[:, cs]
            gi = g_im[:, cs]
            dus.append(_dot(gr, bre[hf], NT) + _dot(gi, bim[hf], NT))
            dbre[hf] += _dot(uh, gr, TN)
            dbim[hf] += _dot(uh, gi, TN)
            dcre[hf] += _dot(dyh, h_re[8:, cs], TN)
            dcim[hf] -= _dot(dyh, h_im[8:, cs], TN)
        du_ref[...] = jnp.concatenate(dus, axis=1) + d_ref[...] * dyv
        dd[...] += jnp.sum(dyv * u, axis=0, keepdims=True)

    rev = lambda w: pl.BlockSpec((T, w), lambda k: (nb - 1 - k, 0))
    rev8 = pl.BlockSpec((8, NST), lambda k: (nb - 1 - k, 0))
    fulls = [*bmat, *cmat, a_re, a_im, d_skip]
    mat = ((2, 256, 1024), f32)
    accs = [mat, mat, mat, mat, ((1, NST), f32), ((1, NST), f32), ((1, SW), f32)]
    return pl.pallas_call(
        kern, name="s5_bwd", grid=(nb,),
        in_specs=[rev(SW), rev(SW), rev8, rev8] + [_full(a.shape) for a in fulls],
        out_specs=[rev(SW)] + [_full(s) for (s, _) in accs],
        out_shape=[SDS((L, SW), f32)] + [SDS(s, dt) for (s, dt) in accs],
        scratch_shapes=[pltpu.VMEM((T + 8, NST), f32), pltpu.VMEM((T + 8, NST), f32),
                        pltpu.VMEM((T, NST), f32), pltpu.VMEM((T, NST), f32),
                        pltpu.VMEM((8, 8, NST), f32), pltpu.VMEM((8, 8, NST), f32),
                        pltpu.VMEM((8, NST), f32), pltpu.VMEM((8, NST), f32),
                        pltpu.VMEM((8, NST), f32), pltpu.VMEM((8, NST), f32)],
        compiler_params=_cp(),
    )(z, dy, hb_re, hb_im, *fulls)


def _s5_disc(lam_re, lam_im, log_dt, bt_re, bt_im):
    dt = jnp.exp(log_dt)
    mag = jnp.exp(lam_re * dt)
    ab_re = mag * jnp.cos(lam_im * dt)
    ab_im = mag * jnp.sin(lam_im * dt)
    den = lam_re * lam_re + lam_im * lam_im
    nr = ab_re - 1.0
    coef_re = (nr * lam_re + ab_im * lam_im) / den
    coef_im = (ab_im * lam_re - nr * lam_im) / den
    rep = lambda c: jnp.broadcast_to(c[:, None, :], (NG, GS, NS)).reshape(NG * GS, NS)
    cr, ci = rep(coef_re), rep(coef_im)
    return ab_re, ab_im, cr * bt_re - ci * bt_im, cr * bt_im + ci * bt_re


def _s5_disc_fwd(*params):
    def kern(*refs):
        outs = _s5_disc(*[r[...] for r in refs[:5]])
        for r, v in zip(refs[5:], outs):
            r[...] = v

    shapes = [SDS((NG, NS), f32), SDS((NG, NS), f32), SDS((NG * GS, NS), f32), SDS((NG * GS, NS), f32)]
    return pl.pallas_call(kern, name="s5_disc_fwd", out_shape=shapes)(*params)


def _s5_disc_bwd(params, cts):
    def kern(*refs):
        _, vjp = jax.vjp(_s5_disc, *[r[...] for r in refs[:5]])
        for r, v in zip(refs[9:], vjp(tuple(r[...] for r in refs[5:9]))):
            r[...] = v

    return pl.pallas_call(kern, name="s5_disc_bwd", out_shape=[SDS(p.shape, f32) for p in params])(*params, *cts)


def _diag_mask():
    r = np.arange(256)[:, None] // GS
    c = np.arange(1024)[None, :] // NS
    return jnp.asarray(r == c, f32)


def _to_blockdiag(m):
    return jnp.tile(m.reshape(2, 256, NS), (1, 1, 16)) * _diag_mask()


def _from_blockdiag(dm):
    return (dm * _diag_mask()).reshape(2, 256, 16, NS).sum(axis=2).reshape(NG * GS, NS)


def _conv_fwd(z, w32, b, tm):
    L = z.shape[0]
    hb = tm // 32

    def kern(val, glu, hval, hglu, w_ref, b_ref, c_ref, win):
        i = pl.program_id(0)
        win[0:32, :] = hval[...] * jax.nn.sigmoid(hglu[...]) * (i > 0).astype(f32)
        win[32:, :] = val[...] * jax.nn.sigmoid(glu[...])
        acc = jnp.zeros((tm, CW), f32) + b_ref[...]
        for k in range(CK):
            acc = acc + w_ref[k:k + 1, :] * win[k + 2:k + 2 + tm, :]
        c_ref[...] = acc

    halo = lambda col: pl.BlockSpec((32, CW), lambda i: (jnp.maximum(i * hb - 1, 0), col))
    return pl.pallas_call(
        kern, name="conv_fwd", grid=(L // tm,),
        in_specs=[_row(tm, CW, 2), _row(tm, CW, 3), halo(2), halo(3), _full(w32.shape), _full(b.shape)],
        out_specs=_row(tm, CW), out_shape=SDS((L, CW), f32),
        scratch_shapes=[pltpu.VMEM((tm + 32, CW), f32)], compiler_params=_cp(),
    )(z, z, z, z, w32, b)


def _conv_bwd(z, dc, w32, tm):
    L = z.shape[0]
    hb = tm // 32
    nt = L // tm

    def kern(val, glu, hval, hglu, dc_ref, hdc, w_ref, dval_ref, dglu_ref, dw_ref, db_ref, win, dwin):
        i = pl.program_id(0)

        @pl.when(i == 0)
        def _():
            dw_ref[...] = jnp.zeros_like(dw_ref)
            db_ref[...] = jnp.zeros_like(db_ref)

        sg = jax.nn.sigmoid(glu[...])
        win[0:32, :] = hval[...] * jax.nn.sigmoid(hglu[...]) * (i > 0).astype(f32)
        win[32:, :] = val[...] * sg
        dcv = dc_ref[...]
        dwin[0:tm, :] = dcv
        dwin[tm:, :] = hdc[...] * (i < nt - 1).astype(f32)
        dv = jnp.zeros((tm, CW), f32)
        for k in range(CK):
            dv = dv + w_ref[k:k + 1, :] * dwin[30 - k:30 - k + tm, :]
            dw_ref[k:k + 1, :] += jnp.sum(dcv * win[k + 2:k + 2 + tm, :], axis=0, keepdims=True)
        db_ref[...] += jnp.sum(dcv, axis=0, keepdims=True)
        dval_ref[...] = dv * sg
        dglu_ref[...] = dv * val[...] * sg * (1.0 - sg)

    halo = lambda col: pl.BlockSpec((32, CW), lambda i: (jnp.maximum(i * hb - 1, 0), col))
    nxt = pl.BlockSpec((32, CW), lambda i: (jnp.minimum((i + 1) * hb, L // 32 - 1), 0))
    return pl.pallas_call(
        kern, name="conv_bwd", grid=(nt,),
        in_specs=[_row(tm, CW, 2), _row(tm, CW, 3), halo(2), halo(3), _row(tm, CW), nxt, _full(w32.shape)],
        out_specs=[_row(tm, CW), _row(tm, CW), _full((32, CW)), _full((1, CW))],
        out_shape=[SDS((L, CW), f32), SDS((L, CW), f32), SDS((32, CW), f32), SDS((1, CW), f32)],
        scratch_shapes=[pltpu.VMEM((tm + 32, CW), f32), pltpu.VMEM((tm + 32, CW), f32)], compiler_params=_cp(),
    )(z, z, z, z, dc, dc, w32)


def _ev_out_f(y, c, ga, gb, tap, glu_b, ln_g, ln_b, glu_w, w_out, mm):
    z1 = jax.nn.gelu(y)
    ya = z1 * jax.nn.sigmoid(mm(z1, glu_w) + glu_b + tap) * jax.nn.silu(ga)
    mu = jnp.mean(c, axis=-1, keepdims=True)
    var = jnp.mean(jnp.square(c - mu), axis=-1, keepdims=True)
    cn = (c - mu) * lax.rsqrt(var + EPS) * ln_g + ln_b
    cat = jnp.concatenate([ya, jax.nn.silu(cn) * jax.nn.silu(gb)], axis=1)
    return mm(cat, w_out), (z1, cat)


def _ev_out_fwd(x, y, c, z, glu_b, ln_g, ln_b, glu_w, w_out, tm):
    def body(rows, fulls):
        xv, yv, cv, ga, gb = rows
        gb_ref, lg_ref, lb_ref, gw_ref, wo_ref = fulls
        out, _ = _ev_out_f(yv, cv, ga, gb, 0.0, gb_ref[...], lg_ref[...], lb_ref[...], gw_ref[...], wo_ref[...], _plain_mm)
        return [xv + out], []

    return _tiled("ev_out_fwd", body, x.shape[0], tm, [(x, D, 0), (y, SW, 0), (c, CW, 0), (z, SW, 1), (z, CW, 4)],
                  [glu_b, ln_g, ln_b, glu_w, w_out], [(D, f32)])[0]


def _ev_out_bwd(dx1, y, c, z, glu_b, ln_g, ln_b, glu_w, w_out, tm):
    def body(rows, fulls):
        dxv, yv, cv, ga, gb = rows
        gb_ref, lg_ref, lb_ref, gw_ref, wo_ref = fulls
        gw, wo = gw_ref[...], wo_ref[...]
        f = lambda yv, cv, ga, gb, tap, b, lg, lb: _ev_out_f(yv, cv, ga, gb, tap, b, lg, lb, gw, wo, _mm)
        _, vjp, (z1, cat) = jax.vjp(f, yv, cv, ga, gb, jnp.zeros((tm, SW), f32), gb_ref[...], lg_ref[...], lb_ref[...],
                                    has_aux=True)
        dy, dc, dga, dgb, dtap, db, dlg, dlb = vjp(dxv)
        return [dy, dc, dga, dgb], [db, dlg, dlb, _wgrad(z1, dtap, gw), _wgrad(cat, dxv, wo)]

    return _tiled("ev_out_bwd", body, dx1.shape[0], tm, [(dx1, D, 0), (y, SW, 0), (c, CW, 0), (z, SW, 1), (z, CW, 4)],
                  [glu_b, ln_g, ln_b, glu_w, w_out], [(SW, f32), (CW, f32), (SW, f32), (CW, f32)],
                  [((1, SW), f32), ((1, CW), f32), ((1, CW), f32), (glu_w.shape, f32), (w_out.shape, f32)])


def _kv_f(mem, g, tap0, tap1, w0, w1, mm):
    mn = _rms(mem, g)
    return (mm(mn, w0) + tap0, mm(mn, w1) + tap1), mn


def _kv_fwd(mem, g, w0, w1):
    def body(rows, fulls):
        g_ref, w0_ref, w1_ref = fulls
        (kv0, kv1), _ = _kv_f(rows[0], g_ref[...], 0.0, 0.0, w0_ref[...], w1_ref[...], _plain_mm)
        return [kv0, kv1], []

    return _tiled("kv_fwd", body, MEM_LEN, MEM_LEN, [(mem, D, 0)], [g, w0, w1], [(2 * D, f32), (2 * D, f32)])


def _kv_bwd(mem, g, w0, w1, dkv0, dkv1):
    def body(rows, fulls):
        g_ref, w0_ref, w1_ref = fulls
        w0v, w1v = w0_ref[...], w1_ref[...]
        tap = jnp.zeros((MEM_LEN, 2 * D), f32)
        _, vjp, mn = jax.vjp(lambda g_, t0, t1: _kv_f(rows[0], g_, t0, t1, w0v, w1v, _mm), g_ref[...], tap, tap, has_aux=True)
        dg, d0, d1 = vjp((rows[1], rows[2]))
        return [], [dg, _wgrad(mn, d0, w0v), _wgrad(mn, d1, w1v)]

    return _tiled("kv_bwd", body, MEM_LEN, MEM_LEN, [(mem, D, 0), (dkv0, 2 * D, 0), (dkv1, 2 * D, 0)], [g, w0, w1], [],
                  [((1, D), f32), (w0.shape, f32), (w1.shape, f32)])


def _xa_f(x, g, kv, tap, w_qg, w_o, mm, nt, nn):
    h = _rms(x, g)
    qg = mm(h, w_qg) + tap
    outs = []
    for hd in range(XH):
        q = qg[:, hd * XD:(hd + 1) * XD]
        k = kv[:, hd * XD:(hd + 1) * XD]
        v = kv[:, D + hd * XD:D + (hd + 1) * XD]
        s = nt(q, k) * (XD ** -0.5)
        e = jnp.exp(s - jnp.max(s, axis=-1, keepdims=True))
        outs.append(nn(e / jnp.sum(e, axis=-1, keepdims=True), v))
    u = jnp.concatenate(outs, axis=1) * jax.nn.silu(qg[:, D:])
    return mm(u, w_o), (h, u)


def _xa_fwd(name, x, g, kv, w_qg, w_o, tm):
    def body(rows, fulls):
        g_ref, kv_ref, wq_ref, wo_ref = fulls
        out, _ = _xa_f(rows[0], g_ref[...], kv_ref[...], 0.0, wq_ref[...], wo_ref[...], _plain_mm, _plain_nt, _plain_mm)
        return [rows[0] + out], []

    return _tiled(name, body, x.shape[0], tm, [(x, D, 0)], [g, kv, w_qg, w_o], [(D, f32)])[0]


def _xa_bwd(name, x, dxo, g, kv, w_qg, w_o, tm):
    def body(rows, fulls):
        xv, dxv = rows
        g_ref, kv_ref, wq_ref, wo_ref = fulls
        wq, wo = wq_ref[...], wo_ref[...]
        f = lambda xv, gv, kvv, tap: _xa_f(xv, gv, kvv, tap, wq, wo, _mm, _mm_nt, _mm_nn)
        _, vjp, (h, u) = jax.vjp(f, xv, g_ref[...], kv_ref[...], jnp.zeros((tm, 2 * D), f32), has_aux=True)
        dx, dg, dkv, dtap = vjp(dxv)
        return [dx + dxv], [dg, dkv, _wgrad(h, dtap, wq), _wgrad(u, dxv, wo)]

    return _tiled(name, body, x.shape[0], tm, [(x, D, 0), (dxo, D, 0)], [g, kv, w_qg, w_o], [(D, f32)],
                  [((1, D), f32), ((MEM_LEN, 2 * D), f32), (w_qg.shape, f32), (w_o.shape, f32)])


def _attn_pad(k_ref, v_ref, kpad, vpad):
    kpad[0:PAD, :] = jnp.zeros((PAD, 128), bf16)
    vpad[0:PAD, :] = jnp.zeros((PAD, 128), bf16)
    kpad[PAD:, :] = k_ref[...].astype(bf16)
    vpad[PAD:, :] = v_ref[...].astype(bf16)


def _attn_exp(qm, kwin, bm, mask_row):
    x = _dot(qm, kwin, NT) + bm + mask_row
    e = jnp.exp(x - jnp.max(x, axis=-1, keepdims=True))
    return e, 1.0 / jnp.sum(e, axis=-1, keepdims=True)


def _attn_fwd(qkvg, bias):
    L = qkvg.shape[0]
    nq = L // AQ

    def kern(q_ref, k_ref, v_ref, b_ref, o_ref, kpad, vpad):
        i = pl.program_id(1)

        @pl.when(i == 0)
        def _():
            _attn_pad(k_ref, v_ref, kpad, vpad)

        r0 = pl.multiple_of(i * AQ, AQ)
        q = q_ref[...] * (DH ** -0.5)
        kwin = kpad[pl.ds(r0, AW), :]
        vwin = vpad[pl.ds(r0, AW), :]
        lane_hi = lax.broadcasted_iota(jnp.int32, (AQ, 128), 1) // DH
        mask_row = jnp.where(lax.broadcasted_iota(jnp.int32, (1, AW), 1) >= PAD - r0, 0.0, -1e30)
        outs = []
        for hh in range(2):
            e, inv = _attn_exp(jnp.where(lane_hi == hh, q, 0.0), kwin, b_ref[hh], mask_row)
            outs.append(_dot(e, vwin, NN) * inv)
        o_ref[...] = jnp.where(lane_hi == 0, outs[0], outs[1])

    return pl.pallas_call(
        kern, name="attn_fwd", grid=(HEADS // 2, nq),
        in_specs=[pl.BlockSpec((AQ, 128), lambda j, i: (i, j)),
                  pl.BlockSpec((L, 128), lambda j, i: (0, 8 + j)),
                  pl.BlockSpec((L, 128), lambda j, i: (0, 16 + j)),
                  pl.BlockSpec((2, AQ, AW), lambda j, i: (j, 0, 0))],
        out_specs=pl.BlockSpec((AQ, 128), lambda j, i: (i, j)),
        out_shape=SDS((L, D), f32),
        scratch_shapes=[pltpu.VMEM((L + PAD, 128), bf16), pltpu.VMEM((L + PAD, 128), bf16)],
        compiler_params=_cp(("arbitrary", "arbitrary")),
    )(qkvg, qkvg, qkvg, bias)


def _attn_bwd(qkvg, do, bias):
    L = qkvg.shape[0]
    nq = L // AQ

    def kern(q_ref, k_ref, v_ref, b_ref, do_ref, dq_ref, dk_ref, dv_ref, db_ref, kpad, vpad, dkp, dvp):
        i = pl.program_id(1)

        @pl.when(i == 0)
        def _():
            _attn_pad(k_ref, v_ref, kpad, vpad)
            dkp[...] = jnp.zeros_like(dkp)
            dvp[...] = jnp.zeros_like(dvp)
            db_ref[...] = jnp.zeros_like(db_ref)

        r0 = pl.multiple_of(i * AQ, AQ)
        q = q_ref[...] * (DH ** -0.5)
        dov = do_ref[...]
        kwin = kpad[pl.ds(r0, AW), :]
        vwin = vpad[pl.ds(r0, AW), :]
        lane_hi = lax.broadcasted_iota(jnp.int32, (AQ, 128), 1) // DH
        mask_row = jnp.where(lax.broadcasted_iota(jnp.int32, (1, AW), 1) >= PAD - r0, 0.0, -1e30)
        dqs = []
        dk = jnp.zeros((AW, 128), f32)
        dv = jnp.zeros((AW, 128), f32)
        for hh in range(2):
            qm = jnp.where(lane_hi == hh, q, 0.0).astype(bf16)
            e, inv = _attn_exp(qm, kwin, b_ref[hh], mask_row)
            dom = (jnp.where(lane_hi == hh, dov, 0.0) * inv).astype(bf16)
            dpl = _dot(dom, vwin, NT)
            ds = e * (dpl - inv * jnp.sum(e * dpl, axis=-1, keepdims=True))
            db_ref[hh] += ds
            dsb = ds.astype(bf16)
            dqs.append(_dot(dsb, kwin, NN) * (DH ** -0.5))
            dk = dk + _dot(dsb, qm, TN)
            dv = dv + _dot(e, dom, TN)
        dq_ref[...] = jnp.where(lane_hi == 0, dqs[0], dqs[1])
        dkp[pl.ds(r0, AW), :] += dk
        dvp[pl.ds(r0, AW), :] += dv

        @pl.when(i == nq - 1)
        def _():
            dk_ref[...] = dkp[PAD:, :]
            dv_ref[...] = dvp[PAD:, :]

    return pl.pallas_call(
        kern, name="attn_bwd", grid=(HEADS // 2, nq),
        in_specs=[pl.BlockSpec((AQ, 128), lambda j, i: (i, j)),
                  pl.BlockSpec((L, 128), lambda j, i: (0, 8 + j)),
                  pl.BlockSpec((L, 128), lambda j, i: (0, 16 + j)),
                  pl.BlockSpec((2, AQ, AW), lambda j, i: (j, 0, 0)),
                  pl.BlockSpec((AQ, 128), lambda j, i: (i, j))],
        out_specs=[pl.BlockSpec((AQ, 128), lambda j, i: (i, j)),
                   pl.BlockSpec((L, 128), lambda j, i: (0, j)),
                   pl.BlockSpec((L, 128), lambda j, i: (0, j)),
                   pl.BlockSpec((2, AQ, AW), lambda j, i: (j, 0, 0))],
        out_shape=[SDS((L, D), f32), SDS((L, D), f32), SDS((L, D), f32), SDS((HEADS, AQ, AW), f32)],
        scratch_shapes=[pltpu.VMEM((L + PAD, 128), bf16), pltpu.VMEM((L + PAD, 128), bf16),
                        pltpu.VMEM((L + PAD, 128), f32), pltpu.VMEM((L + PAD, 128), f32)],
        compiler_params=_cp(("arbitrary", "arbitrary")),
    )(qkvg, qkvg, qkvg, bias, do)


_SKEW = AQ + AW - 1


def _rel_index():
    d = (AW - 1) - np.arange(_SKEW)
    return np.clip(d, -MAX_REL, MAX_REL) + MAX_REL


def _band_mask():
    qc = np.arange(AQ)[:, None] // CHUNK + LEFT
    kc = np.arange(AW)[None, :] // CHUNK
    return jnp.asarray(np.where((kc <= qc) & (kc >= qc - LEFT), 0.0, -1e30), f32)


def _bias_matrix(rel_bias):
    tv = jnp.take(rel_bias, jnp.asarray(_rel_index()), axis=1)
    flat = jnp.tile(jnp.pad(tv, ((0, 0), (0, 1))), (1, AQ))
    top = flat[:, AQ - 1:AQ - 1 + AQ * _SKEW].reshape(HEADS, AQ, _SKEW)[:, :, :AW]
    return top + _band_mask()


def _bias_grad(dbias, onehot):
    flat = jnp.pad(dbias, ((0, 0), (0, 0), (0, _SKEW - AW))).reshape(HEADS, AQ * _SKEW)
    z = jnp.pad(flat, ((0, 0), (AQ - 1, AQ * 1024 - AQ * _SKEW - (AQ - 1)))).reshape(HEADS, AQ, 1024)

    def kern(z_ref, oh_ref, out_ref):
        diag = jnp.sum(z_ref[...], axis=1)
        out_ref[...] = jnp.dot(diag, oh_ref[...], preferred_element_type=f32, precision=lax.Precision.HIGHEST)

    return pl.pallas_call(kern, name="bias_grad", out_shape=SDS((HEADS, 2 * MAX_REL + 1), f32),
                          compiler_params=_cp(()))(z, onehot)


def _rel_onehot():
    oh = np.zeros((1024, 2 * MAX_REL + 1), np.float32)
    oh[np.arange(_SKEW), _rel_index()] = 1.0
    return jnp.asarray(oh)


def _gated_out_fwd(name, x, o, gate_src, gate_col, w, tm):
    def body(rows, fulls):
        xv, ov, gv = rows
        return [xv + _dotw(ov * jax.nn.silu(gv), fulls[0][...])], []

    return _tiled(name, body, x.shape[0], tm, [(x, D, 0), (o, D, 0), (gate_src, D, gate_col)], [w], [(D, f32)])[0]


def _gated_out_bwd(name, dxo, o, gate_src, gate_col, w, tm):
    def body(rows, fulls):
        dxv, ov, gv = rows
        wv = fulls[0][...]
        u, vjp = jax.vjp(lambda ov, gv: ov * jax.nn.silu(gv), ov, gv)
        do, dg = vjp(_dotw_t(dxv, wv))
        return [do, dg], [_wgrad(u, dxv, wv)]

    return _tiled(name, body, dxo.shape[0], tm, [(dxo, D, 0), (o, D, 0), (gate_src, D, gate_col)], [w],
                  [(D, f32), (D, f32)], [(w.shape, f32)])


def _loss_head(x, g, target, tm):
    def body(rows, fulls):
        xv, tv = rows

        def f(xv, gv):
            e = jnp.square(_rms(xv, gv) - tv)
            return 0.5 * jnp.sum(jnp.mean(e, axis=-1, keepdims=True), axis=0, keepdims=True)

        loss, vjp = jax.vjp(f, xv, fulls[0][...])
        dx, dg = vjp(jnp.ones((1, 1), f32))
        return [dx], [jnp.broadcast_to(loss, (1, 128)), dg]

    return _tiled("loss_head", body, x.shape[0], tm, [(x, D, 0), (target, D, 0)], [g], [(D, f32)],
                  [((1, 128), f32), ((1, D), f32)])


def _local_step(x, mem, target, p):
    L = x.shape[0]
    TM = 256
    row = lambda v: v.reshape(1, -1)

    bt = lambda b: jnp.transpose(b, (0, 2, 1)).reshape(NG * GS, NS)
    disc_in = (p["ev_s5_lambda_re"], p["ev_s5_lambda_im"], p["ev_s5_log_dt"].reshape(NG, 1),
               bt(p["ev_s5_b_re"]), bt(p["ev_s5_b_im"]))
    ab_re, ab_im, bbt_re, bbt_im = _s5_disc_fwd(*disc_in)
    bmat = (_to_blockdiag(bbt_re).astype(bf16), _to_blockdiag(bbt_im).astype(bf16))
    cmat = (_to_blockdiag(p["ev_s5_c_re"].reshape(NG * GS, NS)).astype(bf16),
            _to_blockdiag(p["ev_s5_c_im"].reshape(NG * GS, NS)).astype(bf16))
    a_re, a_im = ab_re.reshape(1, NST), ab_im.reshape(1, NST)
    d_skip = row(p["ev_s5_d"])
    w32 = jnp.pad(p["ev_conv_w"], ((0, 1), (0, 0)))
    conv_b, ln_g, ln_b, glu_b = row(p["ev_conv_b"]), row(p["ev_conv_ln_g"]), row(p["ev_conv_ln_b"]), row(p["ev_s5_glu_b"])
    g_mem, g_ev, g_od, g_fin = row(p["mem_norm_g"]), row(p["ev_norm_g"]), row(p["od_norm_g"]), row(p["final_norm_g"])
    g_xa = [row(p["xa_norm_g"][l]) for l in range(2)]
    bias = _bias_matrix(p["od_rel_bias"])
    wqg, wkv, wo = p["xa_w_qg"], p["xa_w_kv"], p["xa_w_o"]

    kv0, kv1 = _kv_fwd(mem, g_mem, wkv[0], wkv[1])
    z = _norm_mm_fwd("ev_in_fwd", x, g_ev, p["ev_w_in"], TM)
    y_s5, hb_re, hb_im = _s5_fwd(z, bmat, cmat, a_re, a_im, d_skip)
    c = _conv_fwd(z, w32, conv_b, 512)
    x1 = _ev_out_fwd(x, y_s5, c, z, glu_b, ln_g, ln_b, p["ev_s5_glu_w"], p["ev_w_out"], TM)
    x2 = _xa_fwd("xa0_fwd", x1, g_xa[0], kv0, wqg[0], wo[0], TM)
    qkvg = _norm_mm_fwd("od_in_fwd", x2, g_od, p["od_w_in"], TM)
    o = _attn_fwd(qkvg, bias)
    x3 = _gated_out_fwd("od_out_fwd", x2, o, qkvg, 3, p["od_w_out"], TM)
    x4 = _xa_fwd("xa1_fwd", x3, g_xa[1], kv1, wqg[1], wo[1], TM)

    G = {}
    dx4, loss_row, G["final_norm_g"] = _loss_head(x4, g_fin, target, TM)
    dx3, dg_xa1, dkv1, dwqg1, dwo1 = _xa_bwd("xa1_bwd", x3, dx4, g_xa[1], kv1, wqg[1], wo[1], TM)
    do, dgate, G["od_w_out"] = _gated_out_bwd("od_out_bwd", dx3, o, qkvg, 3, p["od_w_out"], TM)
    dq, dk, dv, dbias = _attn_bwd(qkvg, do, bias)
    G["od_rel_bias"] = _bias_grad(dbias, _rel_onehot())
    dx2, G["od_norm_g"], G["od_w_in"] = _norm_mm_bwd(
        "od_in_bwd", x2, g_od, p["od_w_in"], [(dq, D, 0), (dk, D, 0), (dv, D, 0), (dgate, D, 0)], dx3, TM)
    dx1, dg_xa0, dkv0, dwqg0, dwo0 = _xa_bwd("xa0_bwd", x1, dx2, g_xa[0], kv0, wqg[0], wo[0], TM)
    G["xa_norm_g"] = jnp.concatenate([dg_xa0, dg_xa1], axis=0)
    G["xa_w_qg"], G["xa_w_o"] = [dwqg0, dwqg1], [dwo0, dwo1]
    G["mem_norm_g"], dwkv0, dwkv1 = _kv_bwd(mem, g_mem, wkv[0], wkv[1], dkv0, dkv1)
    G["xa_w_kv"] = [dwkv0, dwkv1]
    (dy_s5, dc, dga, dgb, G["ev_s5_glu_b"], G["ev_conv_ln_g"], G["ev_conv_ln_b"], G["ev_s5_glu_w"],
     G["ev_w_out"]) = _ev_out_bwd(dx1, y_s5, c, z, glu_b, ln_g, ln_b, p["ev_s5_glu_w"], p["ev_w_out"], TM)
    dval, dglu, dw32, G["ev_conv_b"] = _conv_bwd(z, dc, w32, 512)
    G["ev_conv_w"] = dw32[:CK]
    du, dbre, dbim, dcre, dcim, da_re, da_im, G["ev_s5_d"] = _s5_bwd(z, dy_s5, hb_re, hb_im, bmat, cmat, a_re, a_im, d_skip)
    G["ev_s5_c_re"], G["ev_s5_c_im"] = _from_blockdiag(dcre), _from_blockdiag(dcim)
    G["ev_s5_lambda_re"], G["ev_s5_lambda_im"], G["ev_s5_log_dt"], G["ev_s5_b_re"], G["ev_s5_b_im"] = _s5_disc_bwd(
        disc_in, (da_re.reshape(NG, NS), da_im.reshape(NG, NS), _from_blockdiag(dbre), _from_blockdiag(dbim)))
    dx, G["ev_norm_g"], G["ev_w_in"] = _norm_mm_bwd(
        "ev_in_bwd", x, g_ev, p["ev_w_in"], [(du, SW, 0), (dga, SW, 0), (dval, CW, 0), (dglu, CW, 0), (dgb, CW, 0)], dx1, TM)
    return loss_row[0, 0], dx, G


_ANY = pl.BlockSpec(memory_space=pl.ANY)


def _mesh_pos():
    return tuple(lax.axis_index(a) for a in AXES)


def _slot(ref, dev):
    return ref.at[4 * dev[0] + 2 * dev[1] + dev[2]]


def _all_gather(payloads):
    n = len(payloads)

    def body(*refs):
        ins, outs = refs[:n], refs[n:2 * n]
        send_sems, recv_sems, local_sems = refs[2 * n:]
        x, y, c = _mesh_pos()
        me, sibling = (x, y, c), (x, y, 1 - c)
        chips = [(1 - x, y), (x, 1 - y), (1 - x, 1 - y)]

        def copy(p, k, block, to, src=None):
            return pltpu.make_async_remote_copy(
                src_ref=_slot(outs[p], block) if src is None else src, dst_ref=_slot(outs[p], block),
                send_sem=send_sems.at[7 * p + k], recv_sem=recv_sems.at[7 * p + k],
                device_id=to, device_id_type=pl.DeviceIdType.MESH)

        mine = [pltpu.make_async_copy(ins[p], _slot(outs[p], me), local_sems.at[p]) for p in range(n)]
        for cp in mine:
            cp.start()
        first = []
        for p in range(n):
            first.append(copy(p, 0, me, sibling, src=ins[p]))
            first += [copy(p, 1 + j, me, (*chip, c), src=ins[p]) for j, chip in enumerate(chips)]
        for cp in first:
            cp.start()
        passed = []
        for j, chip in enumerate(chips):
            for p in range(n):
                copy(p, 1 + j, (*chip, c), me).wait_recv()
                passed.append(copy(p, 4 + j, (*chip, c), sibling))
                passed[-1].start()
        for p in range(n):
            copy(p, 0, sibling, me).wait_recv()
            for j, chip in enumerate(chips):
                copy(p, 4 + j, (*chip, 1 - c), me).wait_recv()
        for cp in first + passed:
            cp.wait_send()
        for cp in mine:
            cp.wait()

    return pl.pallas_call(
        body, name="all_gather", in_specs=[_ANY] * n, out_specs=[_ANY] * n,
        out_shape=[SDS((NDEV, *a.shape), a.dtype) for a in payloads],
        scratch_shapes=[pltpu.SemaphoreType.DMA((7 * n,)), pltpu.SemaphoreType.DMA((7 * n,)), pltpu.SemaphoreType.DMA((n,))],
    )(*payloads)


def _exchange(slotted, whole):
    ns, n = len(slotted), len(slotted) + len(whole)

    def body(*refs):
        ins, outs = refs[:n], refs[n:2 * n]
        send_sems, recv_sems, local_sems = refs[2 * n:]
        x, y, c = _mesh_pos()
        me = (x, y, c)
        src = lambda p, dev: _slot(ins[p], dev) if p < ns else ins[p]
        mine = [pltpu.make_async_copy(src(p, me), _slot(outs[p], me), local_sems.at[p]) for p in range(n)]
        for cp in mine:
            cp.start()
        copies = []
        for k in range(1, NDEV):
            flip = lambda v, bit: 1 - v if (k >> bit) & 1 else v
            peer = (flip(x, 2), flip(y, 1), flip(c, 0))
            for p in range(n):
                copies.append(pltpu.make_async_remote_copy(
                    src_ref=src(p, peer), dst_ref=_slot(outs[p], me), send_sem=send_sems.at[7 * p + k - 1],
                    recv_sem=recv_sems.at[7 * p + k - 1], device_id=peer, device_id_type=pl.DeviceIdType.MESH))
        for cp in copies:
            cp.start()
        for cp in copies:
            cp.wait()
        for cp in mine:
            cp.wait()

    return pl.pallas_call(
        body, name="grad_exchange", in_specs=[_ANY] * n, out_specs=[_ANY] * n,
        out_shape=[SDS(a.shape, a.dtype) for a in slotted] + [SDS((NDEV, *a.shape), a.dtype) for a in whole],
        scratch_shapes=[pltpu.SemaphoreType.DMA((7 * n,)), pltpu.SemaphoreType.DMA((7 * n,)), pltpu.SemaphoreType.DMA((n,))],
    )(*slotted, *whole)


def _adamw_math(g, w, m, v):
    m2 = ADAM_B1 * m + (1.0 - ADAM_B1) * g
    v2 = ADAM_B2 * v + (1.0 - ADAM_B2) * jnp.square(g)
    m_hat = m2 / (1.0 - ADAM_B1 ** ADAM_STEP)
    v_hat = v2 / (1.0 - ADAM_B2 ** ADAM_STEP)
    return -ADAM_LR * (m_hat / (jnp.sqrt(v_hat) + ADAM_EPS) + ADAM_WD * w), m2, v2


def _sum_slots(r_ref, rows=None):
    acc = r_ref[0].astype(f32) if rows is None else r_ref[0, rows[0]:rows[1], :].astype(f32)
    for k in range(1, NDEV):
        acc = acc + (r_ref[k] if rows is None else r_ref[k, rows[0]:rows[1], :]).astype(f32)
    return acc


def _sum_adamw(name, recv, off, w, m, v, tr):
    rows, C = w.shape

    def kern(r_ref, w_ref, m_ref, v_ref, g_ref, d_ref, m2_ref, v2_ref):
        g = _sum_slots(r_ref)
        g_ref[...] = g
        d_ref[...], m2_ref[...], v2_ref[...] = _adamw_math(g, w_ref[...], m_ref[...], v_ref[...])

    blk = pl.BlockSpec((tr, C), lambda i: (i, 0))
    return pl.pallas_call(
        kern, name=name, grid=(rows // tr,),
        in_specs=[pl.BlockSpec((NDEV, tr, C), lambda i: (0, off // tr + i, 0)), blk, blk, blk],
        out_specs=[blk] * 4, out_shape=[SDS((rows, C), f32)] * 4, compiler_params=_cp(),
    )(recv, w, m, v)


def _small_adamw(recvs, table, wmv):
    rnames = list(recvs)
    nr, nw = len(rnames), len(table)

    def kern(*refs):
        rr = dict(zip(rnames, refs[:nr]))
        ins, outs = refs[nr:nr + 3 * nw], refs[nr + 3 * nw:]
        for i, (_, shape, src, r0) in enumerate(table):
            g = _sum_slots(rr[src], (r0, r0 + shape[0]))[:, :shape[1]]
            outs[4 * i][...] = g
            res = _adamw_math(g, ins[3 * i][...], ins[3 * i + 1][...], ins[3 * i + 2][...])
            for o, val in zip(outs[4 * i + 1:4 * i + 4], res):
                o[...] = val

    flat = [t for (n, _, _, _) in table for t in wmv[n]]
    res = pl.pallas_call(
        kern, name="small_adamw", out_shape=[SDS(shape, f32) for (_, shape, _, _) in table for _ in range(4)],
        compiler_params=_cp(()),
    )(*[recvs[n] for n in rnames], *flat)
    return {n: tuple(res[4 * i:4 * i + 4]) for i, (n, _, _, _) in enumerate(table)}


WEIGHTS = ["mem_norm_g", "ev_norm_g", "ev_w_in", "ev_s5_lambda_re", "ev_s5_lambda_im", "ev_s5_log_dt", "ev_s5_b_re",
           "ev_s5_b_im", "ev_s5_c_re", "ev_s5_c_im", "ev_s5_d", "ev_s5_glu_w", "ev_s5_glu_b", "ev_conv_w", "ev_conv_b",
           "ev_conv_ln_g", "ev_conv_ln_b", "ev_w_out", "od_norm_g", "od_w_in", "od_rel_bias", "od_w_out", "xa_norm_g",
           "xa_w_qg", "xa_w_kv", "xa_w_o", "final_norm_g"]
PER_LAYER = ("xa_w_qg", "xa_w_kv", "xa_w_o")
GROUPS = [(1024, [("ev_w_out", "rows", 128), ("od_w_out", "rows", 128), ("xa_w_o", "rows", 128)]),
          (512, [("od_w_in", "cols", 1024), ("ev_s5_glu_w", "rows", 64)]),
          (256, [("xa_w_qg", "cols", 1024), ("xa_w_kv", "cols", 1024)]),
          (320, [("ev_w_in", "cols", 1024)])]
ADAM_ROWS = {1024: 128, 512: 64, 256: 512, 320: 256}
SMALL = [("mem_norm_g", (1, D), "b1024", 0), ("ev_norm_g", (1, D), "b1024", 1), ("xa_norm_g", (2, D), "b1024", 2),
         ("final_norm_g", (1, D), "b1024", 4),
         ("ev_s5_d", (1, SW), "b512", 0), ("ev_s5_glu_b", (1, SW), "b512", 1), ("ev_conv_b", (1, CW), "b512", 2),
         ("ev_conv_ln_g", (1, CW), "b512", 3), ("ev_conv_ln_b", (1, CW), "b512", 4),
         ("ev_s5_c_re", (NG * GS, NS), "b64", 0), ("ev_s5_c_im", (NG * GS, NS), "b64", 512),
         ("ev_s5_b_re", (NG * GS, NS), "b64", 1024), ("ev_s5_b_im", (NG * GS, NS), "b64", 1536),
         ("ev_s5_lambda_re", (NG, NS), "b64", 2048), ("ev_s5_lambda_im", (NG, NS), "b64", 2080),
         ("ev_s5_log_dt", (1, NG), "b64", 2112),
         ("od_rel_bias", (HEADS, 2 * MAX_REL + 1), "b257", 0),
         ("ev_conv_w", (CK, CW // NDEV), "cw", 0), ("od_norm_g", (1, D // NDEV), "on", 0)]
S5_B = ("ev_s5_b_re", "ev_s5_b_im")


def _pad_rows(a, rows):
    return jnp.pad(a, ((0, rows - a.shape[0]), (0, 0)))


def _weight_views(gathered):
    views = {}
    for minor, members in GROUPS:
        off = 0
        for name, kind, rows in members:
            layers = 2 if name in PER_LAYER else 1
            vs = [_W(gathered[minor], rows, (off + l * rows) // rows, kind) for l in range(layers)]
            views[name] = vs if layers == 2 else vs[0]
            off += layers * rows
    return views


def _bt(b):
    return jnp.transpose(b.reshape(NG, NS, GS), (0, 2, 1)).reshape(NG * GS, NS)


def _unbt(b, shape):
    return jnp.transpose(b.reshape(NG, GS, NS), (0, 2, 1)).reshape(shape)


def kernel(*args):
    names = ["x", "mem", *WEIGHTS, "loss_target", *["m_" + n for n in WEIGHTS], *["v_" + n for n in WEIGHTS]]
    a = dict(zip(names, args, strict=True))

    payloads = [jnp.concatenate([a[n].reshape(-1, minor).astype(bf16) for n, _, _ in members], axis=0)
                for minor, members in GROUPS]
    other = jnp.concatenate([_pad_rows(a["ev_conv_w"][0], 32).reshape(16, 128), _pad_rows(a["od_norm_g"], 8)], axis=0)
    *gathered, gother = _all_gather([*payloads, other])
    p = {n: (a[n] if n in ("xa_norm_g", "mem_norm_g", "final_norm_g") else a[n][0]) for n in WEIGHTS
         if not any(n == m[0] for _, ms in GROUPS for m in ms)}
    p.update(_weight_views({minor: g for (minor, _), g in zip(GROUPS, gathered)}))
    p["ev_w_in"] = jnp.transpose(gathered[3], (1, 0, 2)).reshape(D, EVEN_IN)
    p["ev_conv_w"] = jnp.transpose(gother[:, :16].reshape(NDEV, 32, CW // NDEV)[:, :CK], (1, 0, 2)).reshape(CK, CW)
    p["od_norm_g"] = gother[:, 16].reshape(D)

    loss_part, dx, G = _local_step(a["x"][0], a["mem"][0], a["loss_target"][0], p)

    slot3 = lambda g, rows: g.reshape(NDEV, rows, g.shape[-1]) if g.ndim == 2 else g
    G["ev_w_in"] = jnp.transpose(G["ev_w_in"].reshape(D, NDEV, EVEN_IN // NDEV), (1, 0, 2))
    slotted = []
    for minor, members in GROUPS:
        parts = [slot3(g, rows) for n, _, rows in members for g in (G[n] if n in PER_LAYER else [G[n]])]
        slotted.append(jnp.concatenate([t.astype(bf16) for t in parts], axis=1))
    cw = jnp.transpose(G["ev_conv_w"].reshape(CK, NDEV, CW // NDEV), (1, 0, 2))
    slotted += [jnp.pad(cw, ((0, 0), (0, 1), (0, 0))), G["od_norm_g"].reshape(NDEV, 1, D // NDEV)]
    whole = {"b1024": _pad_rows(jnp.concatenate([G[n] for n in ("mem_norm_g", "ev_norm_g", "xa_norm_g", "final_norm_g")]), 8),
             "b512": _pad_rows(jnp.concatenate([G[n] for n in ("ev_s5_d", "ev_s5_glu_b", "ev_conv_b", "ev_conv_ln_g",
                                                                 "ev_conv_ln_b")]), 8),
             "b64": _pad_rows(jnp.concatenate([G[n] for n in ("ev_s5_c_re", "ev_s5_c_im", "ev_s5_b_re", "ev_s5_b_im",
                                                                "ev_s5_lambda_re", "ev_s5_lambda_im")]
                                              + [jnp.pad(G["ev_s5_log_dt"].reshape(1, NG), ((0, 0), (0, NS - NG)))]), 2120),
             "b257": G["od_rel_bias"]}
    recv = _exchange(slotted, list(whole.values()))
    recvs = dict(zip([m for m, _ in GROUPS] + ["cw", "on"] + list(whole), recv))

    res = {}
    for minor, members in GROUPS:
        off = 0
        for n, _, rows in members:
            rows *= 2 if n in PER_LAYER else 1
            w, m, v = (a[k + n].reshape(rows, minor) for k in ("", "m_", "v_"))
            res[n] = _sum_adamw("adamw_" + n, recvs[minor], off, w, m, v, ADAM_ROWS[minor])
            off += rows
    two_d = lambda n, t: _bt(t[0]) if n in S5_B else t.reshape(dict((s[0], s[1]) for s in SMALL)[n])
    res.update(_small_adamw({k: recvs[k] for k in ("b1024", "b512", "b64", "b257", "cw", "on")}, SMALL,
                            {n: tuple(two_d(n, a[k + n]) for k in ("", "m_", "v_")) for n, _, _, _ in SMALL}))
    shaped = lambda n, t: _unbt(t, a[n].shape) if n in S5_B else t.reshape(a[n].shape)
    loss = lax.psum(loss_part, AXES)
    return (loss, dx[None], *[shaped(n, res[n][k]) for k in range(4) for n in WEIGHTS])
```

```python
import functools
from typing import NamedTuple

import jax
import jax.numpy as jnp
import numpy as np
from jax import lax
from jax.experimental import pallas as pl
from jax.experimental.pallas import tpu as pltpu

f32, bf16 = jnp.float32, jnp.bfloat16
SDS = jax.ShapeDtypeStruct

D = 1024
SW = 512
NG, GS, NS = 32, 16, 64
NST = NG * NS
CW = 512
CK = 31
EVEN_IN = 2 * SW + 3 * CW
HEADS, DH = 16, 64
CHUNK, LEFT = 64, 8
PAD = LEFT * CHUNK
MAX_REL = 128
MEM_LEN = 256
XH, XD = 4, 256
EPS = 1e-6
NDEV = 8
AXES = ("x", "y", "c")

ADAM_LR, ADAM_B1, ADAM_B2, ADAM_EPS, ADAM_WD, ADAM_STEP = 0.001, 0.9, 0.999, 1e-08, 0.01, 10

VMEM_LIMIT = 56 << 20
S5_T = 512
S5_CH = 512
AQ = 256
AW = AQ + PAD

NN = ((1,), (0,))
NT = ((1,), (1,))
TN = ((0,), (0,))


def _dot(a, b, dims):
    return lax.dot_general(a.astype(bf16), b.astype(bf16), (dims, ((), ())), preferred_element_type=f32)


def _dotw(a, w):
    if w.ndim == 2:
        return _dot(a, w, NN)
    return jnp.concatenate([_dot(a, w[j], NN) for j in range(w.shape[0])], axis=1)


def _dotw_t(g, w):
    if w.ndim == 2:
        return _dot(g, w, NT)
    n = w.shape[2]
    out = _dot(g[:, :n], w[0], NT)
    for j in range(1, w.shape[0]):
        out = out + _dot(g[:, j * n:(j + 1) * n], w[j], NT)
    return out


def _wgrad(a, g, w):
    if w.ndim == 2:
        return _dot(a, g, TN)
    n = w.shape[2]
    return jnp.stack([_dot(a, g[:, j * n:(j + 1) * n], TN) for j in range(w.shape[0])])


@jax.custom_vjp
def _mm(a, w):
    return _dotw(a, w)


def _mm_f(a, w):
    return _dotw(a, w), w


def _mm_b(w, g):
    return _dotw_t(g, w), None


_mm.defvjp(_mm_f, _mm_b)


@jax.custom_vjp
def _mm_nt(a, b):
    return _dot(a, b, NT)


def _mm_nt_f(a, b):
    return _dot(a, b, NT), (a, b)


def _mm_nt_b(res, g):
    a, b = res
    return _dot(g, b, NN), _dot(g, a, TN)


_mm_nt.defvjp(_mm_nt_f, _mm_nt_b)


@jax.custom_vjp
def _mm_nn(a, b):
    return _dot(a, b, NN)


def _mm_nn_f(a, b):
    return _dot(a, b, NN), (a, b)


def _mm_nn_b(res, g):
    a, b = res
    return _dot(g, b, NT), _dot(a, g, TN)


_mm_nn.defvjp(_mm_nn_f, _mm_nn_b)


def _plain_mm(a, w):
    return _dotw(a, w)


def _plain_nt(a, b):
    return _dot(a, b, NT)


def _rms(x, g):
    return x * lax.rsqrt(jnp.mean(x * x, axis=-1, keepdims=True) + EPS) * g


def _cp(sem=("arbitrary",)):
    return pltpu.CompilerParams(dimension_semantics=sem, vmem_limit_bytes=VMEM_LIMIT)


def _row(tm, w, col=0):
    return pl.BlockSpec((tm, w), lambda i: (i, col))


def _full(shape):
    nd = len(shape)
    return pl.BlockSpec(tuple(shape), lambda i: (0,) * nd, pipeline_mode=pl.Buffered(1))


class _W(NamedTuple):
    arr: jax.Array
    rows: int
    idx: int
    kind: str

    @property
    def shape(self):
        c = self.arr.shape[2]
        return (NDEV * self.rows, c) if self.kind == "rows" else (NDEV, self.rows, c)


class _WRef:
    def __init__(self, ref, kind):
        self.ref, self.kind = ref, kind

    def __getitem__(self, _):
        v = self.ref[...]
        return v.reshape(v.shape[0] * v.shape[1], v.shape[2]) if self.kind == "rows" else v


def _wspec(w):
    if isinstance(w, _W):
        return pl.BlockSpec((NDEV, w.rows, w.arr.shape[2]), lambda i: (0, w.idx, 0), pipeline_mode=pl.Buffered(1))
    return _full(w.shape)


_ANY = pl.BlockSpec(memory_space=pl.ANY)
_NO_COMM = ((), ())


def _mesh_pos():
    return tuple(lax.axis_index(a) for a in AXES)


def _slot(ref, dev):
    return ref.at[4 * dev[0] + 2 * dev[1] + dev[2]]


def _comm_shapes(comm):
    slotted, whole = comm
    n = len(slotted) + len(whole)
    shapes = [SDS(a.shape, a.dtype) for a in slotted] + [SDS((NDEV, *a.shape), a.dtype) for a in whole]
    sems = [pltpu.SemaphoreType.DMA((7 * n,)), pltpu.SemaphoreType.DMA((7 * n,)), pltpu.SemaphoreType.DMA((n,))] if n else []
    return shapes, sems


def _comm_copies(ns, ins, outs, send_sems, recv_sems, local_sems):
    n = len(ins)
    x, y, c = _mesh_pos()
    me = (x, y, c)
    src = lambda p, dev: _slot(ins[p], dev) if p < ns else ins[p]
    copies = [pltpu.make_async_copy(src(p, me), _slot(outs[p], me), local_sems.at[p]) for p in range(n)]
    for k in range(1, NDEV):
        flip = lambda v, bit: 1 - v if (k >> bit) & 1 else v
        peer = (flip(x, 2), flip(y, 1), flip(c, 0))
        for p in range(n):
            copies.append(pltpu.make_async_remote_copy(
                src_ref=src(p, peer), dst_ref=_slot(outs[p], me), send_sem=send_sems.at[7 * p + k - 1],
                recv_sem=recv_sems.at[7 * p + k - 1], device_id=peer, device_id_type=pl.DeviceIdType.MESH))
    return copies


def _comm_hook(comm, cin, cout, sems, step, last):
    if not cin:
        return lambda: None

    @pl.when(step == 0)
    def _():
        for cp in _comm_copies(len(comm[0]), cin, cout, *sems):
            cp.start()

    def finish():
        @pl.when(step == last)
        def _():
            for cp in _comm_copies(len(comm[0]), cin, cout, *sems):
                cp.wait()

    return finish


def _tiled(name, body, L, tm, row_ins, full_ins, row_outs, acc_outs=(), comm=_NO_COMM):
    nr, nf, no, na = len(row_ins), len(full_ins), len(row_outs), len(acc_outs)
    cins = [*comm[0], *comm[1]]
    nc = len(cins)
    cshapes, csems = _comm_shapes(comm)

    def kern(*refs):
        rin = refs[:nr]
        fin = [_WRef(r, w.kind) if isinstance(w, _W) else r for r, w in zip(refs[nr:nr + nf], full_ins)]
        cin = refs[nr + nf:nr + nf + nc]
        outs = refs[nr + nf + nc:]
        rout, aout, cout, sems = outs[:no], outs[no:no + na], outs[no + na:no + na + nc], outs[no + na + nc:]
        finish = _comm_hook(comm, cin, cout, sems, pl.program_id(0), L // tm - 1)
        routs, accs = body([r[...] for r in rin], fin)
        for r, v in zip(rout, routs):
            r[...] = v.astype(r.dtype)
        if aout:
            @pl.when(pl.program_id(0) == 0)
            def _():
                for a in aout:
                    a[...] = jnp.zeros(a.shape, a.dtype)

            for a, v in zip(aout, accs):
                a[...] += v.astype(a.dtype)
        finish()

    return pl.pallas_call(
        kern, name=name, grid=(L // tm,),
        in_specs=[_row(tm, w, c) for (_, w, c) in row_ins] + [_wspec(a) for a in full_ins] + [_ANY] * nc,
        out_specs=[_row(tm, w) for (w, _) in row_outs] + [_full(s) for (s, _) in acc_outs] + [_ANY] * nc,
        out_shape=[SDS((L, w), dt) for (w, dt) in row_outs] + [SDS(tuple(s), dt) for (s, dt) in acc_outs] + cshapes,
        scratch_shapes=csems, compiler_params=_cp(),
    )(*[a for (a, _, _) in row_ins], *[a.arr if isinstance(a, _W) else a for a in full_ins], *cins)


def _norm_mm_fwd(name, x, g, w, tm):
    def body(rows, fulls):
        g_ref, w_ref = fulls
        return [_dotw(_rms(rows[0], g_ref[...]), w_ref[...])], []

    n_out = w.shape[1] if len(w.shape) == 2 else w.shape[0] * w.shape[2]
    return _tiled(name, body, x.shape[0], tm, [(x, D, 0)], [g, w], [(n_out, f32)])[0]


def _norm_mm_bwd(name, x, g, w, dz_parts, dres, tm):
    n = len(dz_parts)

    def body(rows, fulls):
        g_ref, w_ref = fulls
        dz = rows[1] if n == 1 else jnp.concatenate(rows[1:1 + n], axis=1)
        wv = w_ref[...]
        h, vjp = jax.vjp(_rms, rows[0], g_ref[...])
        dx, dg = vjp(_dotw_t(dz, wv))
        return [dx + rows[1 + n]], [dg, _wgrad(h, dz, wv)]

    return _tiled(name, body, x.shape[0], tm, [(x, D, 0), *dz_parts, (dres, D, 0)], [g, w],
                  [(D, f32)], [((1, D), f32), (tuple(w.shape), f32)])


def _scan_tables(tab_ref, ar, ai, reverse):
    pw = [(ar, ai)]
    for _ in range(7):
        pr, pi = pw[-1]
        pw.append((pr * ar - pi * ai, pr * ai + pi * ar))
    row = lax.broadcasted_iota(jnp.int32, (8, NST), 0)
    for si, s in enumerate((1, 2, 4)):
        keep = (row <= 7 - s) if reverse else (row >= s)
        tab_ref[2 * si] = jnp.where(keep, pw[s - 1][0], 0.0)
        tab_ref[2 * si + 1] = jnp.where(keep, pw[s - 1][1], 0.0)
    pr = jnp.zeros((8, NST), f32)
    pi = jnp.zeros((8, NST), f32)
    for t in range(8):
        k = (7 - t) if reverse else t
        pr = jnp.where(row == t, pw[k][0], pr)
        pi = jnp.where(row == t, pw[k][1], pi)
    tab_ref[6] = pr
    tab_ref[7] = pi


def _scan(re_ref, im_ref, row0, nrows, tab_ref, car_re, car_im, reverse, extra=None):
    nt = nrows // 8
    accs = []
    for cc in range(NST // S5_CH):
        cols = slice(cc * S5_CH, (cc + 1) * S5_CH)

        def step(j, carry, cols=cols):
            cr, ci, acc = carry
            jj = (nt - 1 - j) if reverse else j
            r0 = pl.multiple_of(row0 + jj * 8, 8)
            xr = re_ref[pl.ds(r0, 8), cols]
            xi = im_ref[pl.ds(r0, 8), cols]
            for si, s in enumerate((1, 2, 4)):
                sh = (8 - s) if reverse else s
                yr = pltpu.roll(xr, sh, 0)
                yi = pltpu.roll(xi, sh, 0)
                tr = tab_ref[2 * si, :, cols]
                ti = tab_ref[2 * si + 1, :, cols]
                xr, xi = xr + tr * yr - ti * yi, xi + tr * yi + ti * yr
            tr = tab_ref[6, :, cols]
            ti = tab_ref[7, :, cols]
            xr, xi = xr + tr * cr - ti * ci, xi + tr * ci + ti * cr
            re_ref[pl.ds(r0, 8), cols] = xr
            im_ref[pl.ds(r0, 8), cols] = xi
            if extra is not None:
                acc = extra(cols, jj, xr, xi, acc)
            b = 0 if reverse else 7
            return (jnp.broadcast_to(xr[b:b + 1, :], (8, S5_CH)), jnp.broadcast_to(xi[b:b + 1, :], (8, S5_CH)), acc)

        acc0 = (jnp.zeros((8, S5_CH), f32), jnp.zeros((8, S5_CH), f32)) if extra is not None else 0
        cr, ci, acc = lax.fori_loop(0, nt, step, (car_re[:, cols], car_im[:, cols], acc0))
        car_re[:, cols] = cr
        car_im[:, cols] = ci
        accs.append(acc)
    return accs


def _s5_fwd(z, bmat, cmat, a_re, a_im, d_skip, comm=_NO_COMM):
    L = z.shape[0]
    nb = L // S5_T
    cins = [*comm[0], *comm[1]]
    nc = len(cins)
    cshapes, csems = _comm_shapes(comm)

    def kern(u_ref, bre, bim, cre, cim, ar_ref, ai_ref, d_ref, *rest):
        cin, (y_ref, hbr_ref, hbi_ref), cout = rest[:nc], rest[nc:nc + 3], rest[nc + 3:2 * nc + 3]
        h_re, h_im, tab, car_re, car_im, *sems = rest[2 * nc + 3:]
        finish = _comm_hook(comm, cin, cout, sems, pl.program_id(0), nb - 1)

        @pl.when(pl.program_id(0) == 0)
        def _():
            _scan_tables(tab, ar_ref[...], ai_ref[...], False)
            car_re[...] = jnp.zeros_like(car_re)
            car_im[...] = jnp.zeros_like(car_im)

        hbr_ref[...] = car_re[...]
        hbi_ref[...] = car_im[...]
        u = u_ref[...]
        for hf in range(2):
            uh = u[:, hf * 256:(hf + 1) * 256]
            h_re[:, hf * 1024:(hf + 1) * 1024] = _dot(uh, bre[hf], NN)
            h_im[:, hf * 1024:(hf + 1) * 1024] = _dot(uh, bim[hf], NN)
        _scan(h_re, h_im, 0, S5_T, tab, car_re, car_im, False)
        ys = []
        for hf in range(2):
            cs = slice(hf * 1024, (hf + 1) * 1024)
            ys.append(_dot(h_re[:, cs], cre[hf], NT) - _dot(h_im[:, cs], cim[hf], NT))
        y_ref[...] = jnp.concatenate(ys, axis=1) + d_ref[...] * u
        finish()

    fulls = [*bmat, *cmat, a_re, a_im, d_skip]
    return pl.pallas_call(
        kern, name="s5_fwd", grid=(nb,),
        in_specs=[_row(S5_T, SW, 0)] + [_full(a.shape) for a in fulls] + [_ANY] * nc,
        out_specs=[_row(S5_T, SW), _row(8, NST), _row(8, NST)] + [_ANY] * nc,
        out_shape=[SDS((L, SW), f32), SDS((nb * 8, NST), f32), SDS((nb * 8, NST), f32)] + cshapes,
        scratch_shapes=[pltpu.VMEM((S5_T, NST), f32), pltpu.VMEM((S5_T, NST), f32), pltpu.VMEM((8, 8, NST), f32),
                        pltpu.VMEM((8, NST), f32), pltpu.VMEM((8, NST), f32)] + csems,
        compiler_params=_cp(),
    )(z, *fulls, *cins)


def _s5_bwd(z, dy, hb_re, hb_im, bmat, cmat, a_re, a_im, d_skip, comm=_NO_COMM):
    L = z.shape[0]
    nb = L // S5_T
    T = S5_T
    cins = [*comm[0], *comm[1]]
    nc = len(cins)
    cshapes, csems = _comm_shapes(comm)

    def kern(u_ref, dy_ref, hbr_ref, hbi_ref, bre, bim, cre, cim, ar_ref, ai_ref, d_ref, *rest):
        cin, (du_ref, dbre, dbim, dcre, dcim, dar, dai, dd), cout = rest[:nc], rest[nc:nc + 8], rest[nc + 8:2 * nc + 8]
        h_re, h_im, g_re, g_im, tabf, tabr, car_re, car_im, gcar_re, gcar_im, *sems = rest[2 * nc + 8:]
        finish = _comm_hook(comm, cin, cout, sems, pl.program_id(0), nb - 1)

        @pl.when(pl.program_id(0) == 0)
        def _():
            _scan_tables(tabf, ar_ref[...], ai_ref[...], False)
            _scan_tables(tabr, ar_ref[...], -ai_ref[...], True)
            gcar_re[...] = jnp.zeros_like(gcar_re)
            gcar_im[...] = jnp.zeros_like(gcar_im)
            for r in (dbre, dbim, dcre, dcim, dar, dai, dd):
                r[...] = jnp.zeros_like(r)

        u = u_ref[...]
        dyv = dy_ref[...]
        h_re[0:8, :] = hbr_ref[...]
        h_im[0:8, :] = hbi_ref[...]
        car_re[...] = hbr_ref[...]
        car_im[...] = hbi_ref[...]
        for hf in range(2):
            cs = slice(hf * 1024, (hf + 1) * 1024)
            uh = u[:, hf * 256:(hf + 1) * 256]
            dyh = dyv[:, hf * 256:(hf + 1) * 256]
            h_re[8:, cs] = _dot(uh, bre[hf], NN)
            h_im[8:, cs] = _dot(uh, bim[hf], NN)
            g_re[:, cs] = _dot(dyh, cre[hf], NN)
            g_im[:, cs] = -_dot(dyh, cim[hf], NN)
        _scan(h_re, h_im, 8, T, tabf, car_re, car_im, False)

        row = lax.broadcasted_iota(jnp.int32, (8, S5_CH), 0)

        def fold_da(cols, jj, gr, gi, acc):
            r0 = pl.multiple_of(jj * 8, 8)
            hr = jnp.where(row == 0, jnp.broadcast_to(h_re[pl.ds(r0, 8), cols][7:8, :], (8, S5_CH)),
                           pltpu.roll(h_re[pl.ds(r0 + 8, 8), cols], 1, 0))
            hi = jnp.where(row == 0, jnp.broadcast_to(h_im[pl.ds(r0, 8), cols][7:8, :], (8, S5_CH)),
                           pltpu.roll(h_im[pl.ds(r0 + 8, 8), cols], 1, 0))
            return acc[0] + gr * hr + gi * hi, acc[1] + gi * hr - gr * hi

        accs = _scan(g_re, g_im, 0, T, tabr, gcar_re, gcar_im, True, extra=fold_da)
        for cc, (ar_, ai_) in enumerate(accs):
            cols = slice(cc * S5_CH, (cc + 1) * S5_CH)
            dar[:, cols] += jnp.sum(ar_, axis=0, keepdims=True)
            dai[:, cols] += jnp.sum(ai_, axis=0, keepdims=True)

        dus = []
        for hf in range(2):
            cs = slice(hf * 1024, (hf + 1) * 1024)
            uh = u[:, hf * 256:(hf + 1) * 256]
            dyh = dyv[:, hf * 256:(hf + 1) * 256]
            gr = g_re[:, cs]
            gi = g_im[:, cs]
            dus.append(_dot(gr, bre[hf], NT) + _dot(gi, bim[hf], NT))
            dbre[hf] += _dot(uh, gr, TN)
            dbim[hf] += _dot(uh, gi, TN)
            dcre[hf] += _dot(dyh, h_re[8:, cs], TN)
            dcim[hf] -= _dot(dyh, h_im[8:, cs], TN)
        du_ref[...] = jnp.concatenate(dus, axis=1) + d_ref[...] * dyv
        dd[...] += jnp.sum(dyv * u, axis=0, keepdims=True)
        finish()

    rev = lambda w: pl.BlockSpec((T, w), lambda k: (nb - 1 - k, 0))
    rev8 = pl.BlockSpec((8, NST), lambda k: (nb - 1 - k, 0))
    fulls = [*bmat, *cmat, a_re, a_im, d_skip]
    mat = ((2, 256, 1024), f32)
    accs = [mat, mat, mat, mat, ((1, NST), f32), ((1, NST), f32), ((1, SW), f32)]
    return pl.pallas_call(
        kern, name="s5_bwd", grid=(nb,),
        in_specs=[rev(SW), rev(SW), rev8, rev8] + [_full(a.shape) for a in fulls] + [_ANY] * nc,
        out_specs=[rev(SW)] + [_full(s) for (s, _) in accs] + [_ANY] * nc,
        out_shape=[SDS((L, SW), f32)] + [SDS(s, dt) for (s, dt) in accs] + cshapes,
        scratch_shapes=[pltpu.VMEM((T + 8, NST), f32), pltpu.VMEM((T + 8, NST), f32),
                        pltpu.VMEM((T, NST), f32), pltpu.VMEM((T, NST), f32),
                        pltpu.VMEM((8, 8, NST), f32), pltpu.VMEM((8, 8, NST), f32),
                        pltpu.VMEM((8, NST), f32), pltpu.VMEM((8, NST), f32),
                        pltpu.VMEM((8, NST), f32), pltpu.VMEM((8, NST), f32)] + csems,
        compiler_params=_cp(),
    )(z, dy, hb_re, hb_im, *fulls, *cins)


def _s5_disc(lam_re, lam_im, log_dt, bt_re, bt_im):
    dt = jnp.exp(log_dt)
    mag = jnp.exp(lam_re * dt)
    ab_re = mag * jnp.cos(lam_im * dt)
    ab_im = mag * jnp.sin(lam_im * dt)
    den = lam_re * lam_re + lam_im * lam_im
    nr = ab_re - 1.0
    coef_re = (nr * lam_re + ab_im * lam_im) / den
    coef_im = (ab_im * lam_re - nr * lam_im) / den
    rep = lambda c: jnp.broadcast_to(c[:, None, :], (NG, GS, NS)).reshape(NG * GS, NS)
    cr, ci = rep(coef_re), rep(coef_im)
    return ab_re, ab_im, cr * bt_re - ci * bt_im, cr * bt_im + ci * bt_re


def _s5_disc_fwd(*params):
    def kern(*refs):
        outs = _s5_disc(*[r[...] for r in refs[:5]])
        for r, v in zip(refs[5:], outs):
            r[...] = v

    shapes = [SDS((NG, NS), f32), SDS((NG, NS), f32), SDS((NG * GS, NS), f32), SDS((NG * GS, NS), f32)]
    return pl.pallas_call(kern, name="s5_disc_fwd", out_shape=shapes)(*params)


def _s5_disc_bwd(params, cts):
    def kern(*refs):
        _, vjp = jax.vjp(_s5_disc, *[r[...] for r in refs[:5]])
        for r, v in zip(refs[9:], vjp(tuple(r[...] for r in refs[5:9]))):
            r[...] = v

    return pl.pallas_call(kern, name="s5_disc_bwd", out_shape=[SDS(p.shape, f32) for p in params])(*params, *cts)


def _diag_mask():
    r = np.arange(256)[:, None] // GS
    c = np.arange(1024)[None, :] // NS
    return jnp.asarray(r == c, f32)


def _to_blockdiag(m):
    return jnp.tile(m.reshape(2, 256, NS), (1, 1, 16)) * _diag_mask()


def _from_blockdiag(dm):
    return (dm * _diag_mask()).reshape(2, 256, 16, NS).sum(axis=2).reshape(NG * GS, NS)


def _conv_fwd(z, w32, b, tm):
    L = z.shape[0]
    hb = tm // 32

    def kern(val, glu, hval, hglu, w_ref, b_ref, c_ref, win):
        i = pl.program_id(0)
        win[0:32, :] = hval[...] * jax.nn.sigmoid(hglu[...]) * (i > 0).astype(f32)
        win[32:, :] = val[...] * jax.nn.sigmoid(glu[...])
        acc = jnp.zeros((tm, CW), f32) + b_ref[...]
        for k in range(CK):
            acc = acc + w_ref[k:k + 1, :] * win[k + 2:k + 2 + tm, :]
        c_ref[...] = acc

    halo = lambda col: pl.BlockSpec((32, CW), lambda i: (jnp.maximum(i * hb - 1, 0), col))
    return pl.pallas_call(
        kern, name="conv_fwd", grid=(L // tm,),
        in_specs=[_row(tm, CW, 2), _row(tm, CW, 3), halo(2), halo(3), _full(w32.shape), _full(b.shape)],
        out_specs=_row(tm, CW), out_shape=SDS((L, CW), f32),
        scratch_shapes=[pltpu.VMEM((tm + 32, CW), f32)], compiler_params=_cp(),
    )(z, z, z, z, w32, b)


def _conv_bwd(z, dc, w32, tm):
    L = z.shape[0]
    hb = tm // 32
    nt = L // tm

    def kern(val, glu, hval, hglu, dc_ref, hdc, w_ref, dval_ref, dglu_ref, dw_ref, db_ref, win, dwin):
        i = pl.program_id(0)

        @pl.when(i == 0)
        def _():
            dw_ref[...] = jnp.zeros_like(dw_ref)
            db_ref[...] = jnp.zeros_like(db_ref)

        sg = jax.nn.sigmoid(glu[...])
        win[0:32, :] = hval[...] * jax.nn.sigmoid(hglu[...]) * (i > 0).astype(f32)
        win[32:, :] = val[...] * sg
        dcv = dc_ref[...]
        dwin[0:tm, :] = dcv
        dwin[tm:, :] = hdc[...] * (i < nt - 1).astype(f32)
        dv = jnp.zeros((tm, CW), f32)
        for k in range(CK):
            dv = dv + w_ref[k:k + 1, :] * dwin[30 - k:30 - k + tm, :]
            dw_ref[k:k + 1, :] += jnp.sum(dcv * win[k + 2:k + 2 + tm, :], axis=0, keepdims=True)
        db_ref[...] += jnp.sum(dcv, axis=0, keepdims=True)
        dval_ref[...] = dv * sg
        dglu_ref[...] = dv * val[...] * sg * (1.0 - sg)

    halo = lambda col: pl.BlockSpec((32, CW), lambda i: (jnp.maximum(i * hb - 1, 0), col))
    nxt = pl.BlockSpec((32, CW), lambda i: (jnp.minimum((i + 1) * hb, L // 32 - 1), 0))
    return pl.pallas_call(
        kern, name="conv_bwd", grid=(nt,),
        in_specs=[_row(tm, CW, 2), _row(tm, CW, 3), halo(2), halo(3), _row(tm, CW), nxt, _full(w32.shape)],
        out_specs=[_row(tm, CW), _row(tm, CW), _full((32, CW)), _full((1, CW))],
        out_shape=[SDS((L, CW), f32), SDS((L, CW), f32), SDS((32, CW), f32), SDS((1, CW), f32)],
        scratch_shapes=[pltpu.VMEM((tm + 32, CW), f32), pltpu.VMEM((tm + 32, CW), f32)], compiler_params=_cp(),
    )(z, z, z, z, dc, dc, w32)


def _ev_out_f(y, c, ga, gb, tap, glu_b, ln_g, ln_b, glu_w, w_out, mm):
    z1 = jax.nn.gelu(y)
    ya = z1 * jax.nn.sigmoid(mm(z1, glu_w) + glu_b + tap) * jax.nn.silu(ga)
    mu = jnp.mean(c, axis=-1, keepdims=True)
    var = jnp.mean(jnp.square(c - mu), axis=-1, keepdims=True)
    cn = (c - mu) * lax.rsqrt(var + EPS) * ln_g + ln_b
    cat = jnp.concatenate([ya, jax.nn.silu(cn) * jax.nn.silu(gb)], axis=1)
    return mm(cat, w_out), (z1, cat)


def _ev_out_fwd(x, y, c, z, glu_b, ln_g, ln_b, glu_w, w_out, tm):
    def body(rows, fulls):
        xv, yv, cv, ga, gb = rows
        gb_ref, lg_ref, lb_ref, gw_ref, wo_ref = fulls
        out, _ = _ev_out_f(yv, cv, ga, gb, 0.0, gb_ref[...], lg_ref[...], lb_ref[...], gw_ref[...], wo_ref[...], _plain_mm)
        return [xv + out], []

    return _tiled("ev_out_fwd", body, x.shape[0], tm, [(x, D, 0), (y, SW, 0), (c, CW, 0), (z, SW, 1), (z, CW, 4)],
                  [glu_b, ln_g, ln_b, glu_w, w_out], [(D, f32)])[0]


def _ev_out_bwd(dx1, y, c, z, glu_b, ln_g, ln_b, glu_w, w_out, tm):
    def body(rows, fulls):
        dxv, yv, cv, ga, gb = rows
        gb_ref, lg_ref, lb_ref, gw_ref, wo_ref = fulls
        gw, wo = gw_ref[...], wo_ref[...]
        f = lambda yv, cv, ga, gb, tap, b, lg, lb: _ev_out_f(yv, cv, ga, gb, tap, b, lg, lb, gw, wo, _mm)
        _, vjp, (z1, cat) = jax.vjp(f, yv, cv, ga, gb, jnp.zeros((tm, SW), f32), gb_ref[...], lg_ref[...], lb_ref[...],
                                    has_aux=True)
        dy, dc, dga, dgb, dtap, db, dlg, dlb = vjp(dxv)
        return [dy, dc, dga, dgb], [db, dlg, dlb, _wgrad(z1, dtap, gw), _wgrad(cat, dxv, wo)]

    return _tiled("ev_out_bwd", body, dx1.shape[0], tm, [(dx1, D, 0), (y, SW, 0), (c, CW, 0), (z, SW, 1), (z, CW, 4)],
                  [glu_b, ln_g, ln_b, glu_w, w_out], [(SW, f32), (CW, f32), (SW, f32), (CW, f32)],
                  [((1, SW), f32), ((1, CW), f32), ((1, CW), f32), (glu_w.shape, f32), (w_out.shape, f32)])


def _kv_f(mem, g, tap0, tap1, w0, w1, mm):
    mn = _rms(mem, g)
    return (mm(mn, w0) + tap0, mm(mn, w1) + tap1), mn


def _kv_fwd(mem, g, w0, w1):
    def body(rows, fulls):
        g_ref, w0_ref, w1_ref = fulls
        (kv0, kv1), _ = _kv_f(rows[0], g_ref[...], 0.0, 0.0, w0_ref[...], w1_ref[...], _plain_mm)
        return [kv0, kv1], []

    return _tiled("kv_fwd", body, MEM_LEN, MEM_LEN, [(mem, D, 0)], [g, w0, w1], [(2 * D, f32), (2 * D, f32)])


def _kv_bwd(mem, g, w0, w1, dkv0, dkv1):
    def body(rows, fulls):
        g_ref, w0_ref, w1_ref = fulls
        w0v, w1v = w0_ref[...], w1_ref[...]
        tap = jnp.zeros((MEM_LEN, 2 * D), f32)
        _, vjp, mn = jax.vjp(lambda g_, t0, t1: _kv_f(rows[0], g_, t0, t1, w0v, w1v, _mm), g_ref[...], tap, tap, has_aux=True)
        dg, d0, d1 = vjp((rows[1], rows[2]))
        return [], [dg, _wgrad(mn, d0, w0v), _wgrad(mn, d1, w1v)]

    return _tiled("kv_bwd", body, MEM_LEN, MEM_LEN, [(mem, D, 0), (dkv0, 2 * D, 0), (dkv1, 2 * D, 0)], [g, w0, w1], [],
                  [((1, D), f32), (w0.shape, f32), (w1.shape, f32)])


def _xa_f(x, g, kv, tap, w_qg, w_o, mm, nt, nn):
    h = _rms(x, g)
    qg = mm(h, w_qg) + tap
    outs = []
    for hd in range(XH):
        q = qg[:, hd * XD:(hd + 1) * XD]
        k = kv[:, hd * XD:(hd + 1) * XD]
        v = kv[:, D + hd * XD:D + (hd + 1) * XD]
        s = nt(q, k) * (XD ** -0.5)
        e = jnp.exp(s - jnp.max(s, axis=-1, keepdims=True))
        outs.append(nn(e / jnp.sum(e, axis=-1, keepdims=True), v))
    u = jnp.concatenate(outs, axis=1) * jax.nn.silu(qg[:, D:])
    return mm(u, w_o), (h, u)


def _xa_fwd(name, x, g, kv, w_qg, w_o, tm):
    def body(rows, fulls):
        g_ref, kv_ref, wq_ref, wo_ref = fulls
        out, _ = _xa_f(rows[0], g_ref[...], kv_ref[...], 0.0, wq_ref[...], wo_ref[...], _plain_mm, _plain_nt, _plain_mm)
        return [rows[0] + out], []

    return _tiled(name, body, x.shape[0], tm, [(x, D, 0)], [g, kv, w_qg, w_o], [(D, f32)])[0]


def _xa_bwd(name, x, dxo, g, kv, w_qg, w_o, tm, comm=_NO_COMM):
    def body(rows, fulls):
        xv, dxv = rows
        g_ref, kv_ref, wq_ref, wo_ref = fulls
        wq, wo = wq_ref[...], wo_ref[...]
        f = lambda xv, gv, kvv, tap: _xa_f(xv, gv, kvv, tap, wq, wo, _mm, _mm_nt, _mm_nn)
        _, vjp, (h, u) = jax.vjp(f, xv, g_ref[...], kv_ref[...], jnp.zeros((tm, 2 * D), f32), has_aux=True)
        dx, dg, dkv, dtap = vjp(dxv)
        return [dx + dxv], [dg, dkv, _wgrad(h, dtap, wq), _wgrad(u, dxv, wo)]

    return _tiled(name, body, x.shape[0], tm, [(x, D, 0), (dxo, D, 0)], [g, kv, w_qg, w_o], [(D, f32)],
                  [((1, D), f32), ((MEM_LEN, 2 * D), f32), (w_qg.shape, f32), (w_o.shape, f32)], comm=comm)


def _attn_pad(k_ref, v_ref, kpad, vpad):
    kpad[0:PAD, :] = jnp.zeros((PAD, 128), bf16)
    vpad[0:PAD, :] = jnp.zeros((PAD, 128), bf16)
    kpad[PAD:, :] = k_ref[...].astype(bf16)
    vpad[PAD:, :] = v_ref[...].astype(bf16)


def _attn_exp(qm, kwin, bm, mask_row):
    x = _dot(qm, kwin, NT) + bm + mask_row
    e = jnp.exp(x - jnp.max(x, axis=-1, keepdims=True))
    return e, 1.0 / jnp.sum(e, axis=-1, keepdims=True)


def _attn_fwd(qkvg, bias):
    L = qkvg.shape[0]
    nq = L // AQ

    def kern(q_ref, k_ref, v_ref, b_ref, o_ref, kpad, vpad):
        i = pl.program_id(1)

        @pl.when(i == 0)
        def _():
            _attn_pad(k_ref, v_ref, kpad, vpad)

        r0 = pl.multiple_of(i * AQ, AQ)
        q = q_ref[...] * (DH ** -0.5)
        kwin = kpad[pl.ds(r0, AW), :]
        vwin = vpad[pl.ds(r0, AW), :]
        lane_hi = lax.broadcasted_iota(jnp.int32, (AQ, 128), 1) // DH
        mask_row = jnp.where(lax.broadcasted_iota(jnp.int32, (1, AW), 1) >= PAD - r0, 0.0, -1e30)
        outs = []
        for hh in range(2):
            e, inv = _attn_exp(jnp.where(lane_hi == hh, q, 0.0), kwin, b_ref[hh], mask_row)
            outs.append(_dot(e, vwin, NN) * inv)
        o_ref[...] = jnp.where(lane_hi == 0, outs[0], outs[1])

    return pl.pallas_call(
        kern, name="attn_fwd", grid=(HEADS // 2, nq),
        in_specs=[pl.BlockSpec((AQ, 128), lambda j, i: (i, j)),
                  pl.BlockSpec((L, 128), lambda j, i: (0, 8 + j)),
                  pl.BlockSpec((L, 128), lambda j, i: (0, 16 + j)),
                  pl.BlockSpec((2, AQ, AW), lambda j, i: (j, 0, 0))],
        out_specs=pl.BlockSpec((AQ, 128), lambda j, i: (i, j)),
        out_shape=SDS((L, D), f32),
        scratch_shapes=[pltpu.VMEM((L + PAD, 128), bf16), pltpu.VMEM((L + PAD, 128), bf16)],
        compiler_params=_cp(("arbitrary", "arbitrary")),
    )(qkvg, qkvg, qkvg, bias)


def _attn_bwd(qkvg, do, bias):
    L = qkvg.shape[0]
    nq = L // AQ

    def kern(q_ref, k_ref, v_ref, b_ref, do_ref, dq_ref, dk_ref, dv_ref, db_ref, kpad, vpad, dkp, dvp):
        i = pl.program_id(1)

        @pl.when(i == 0)
        def _():
            _attn_pad(k_ref, v_ref, kpad, vpad)
            dkp[...] = jnp.zeros_like(dkp)
            dvp[...] = jnp.zeros_like(dvp)
            db_ref[...] = jnp.zeros_like(db_ref)

        r0 = pl.multiple_of(i * AQ, AQ)
        q = q_ref[...] * (DH ** -0.5)
        dov = do_ref[...]
        kwin = kpad[pl.ds(r0, AW), :]
        vwin = vpad[pl.ds(r0, AW), :]
        lane_hi = lax.broadcasted_iota(jnp.int32, (AQ, 128), 1) // DH
        mask_row = jnp.where(lax.broadcasted_iota(jnp.int32, (1, AW), 1) >= PAD - r0, 0.0, -1e30)
        dqs = []
        dk = jnp.zeros((AW, 128), f32)
        dv = jnp.zeros((AW, 128), f32)
        for hh in range(2):
            qm = jnp.where(lane_hi == hh, q, 0.0).astype(bf16)
            dom = jnp.where(lane_hi == hh, dov, 0.0).astype(bf16)
            dp = _dot(dom, vwin, NT)
            e, inv = _attn_exp(qm, kwin, b_ref[hh], mask_row)
            p = e * inv
            ds = p * (dp - jnp.sum(p * dp, axis=-1, keepdims=True))
            db_ref[hh] += ds
            dsb = ds.astype(bf16)
            dqs.append(_dot(dsb, kwin, NN) * (DH ** -0.5))
            dk = dk + _dot(dsb, qm, TN)
            dv = dv + _dot(p, dom, TN)
        dq_ref[...] = jnp.where(lane_hi == 0, dqs[0], dqs[1])
        dkp[pl.ds(r0, AW), :] += dk
        dvp[pl.ds(r0, AW), :] += dv

        @pl.when(i == nq - 1)
        def _():
            dk_ref[...] = dkp[PAD:, :]
            dv_ref[...] = dvp[PAD:, :]

    return pl.pallas_call(
        kern, name="attn_bwd", grid=(HEADS // 2, nq),
        in_specs=[pl.BlockSpec((AQ, 128), lambda j, i: (i, j)),
                  pl.BlockSpec((L, 128), lambda j, i: (0, 8 + j)),
                  pl.BlockSpec((L, 128), lambda j, i: (0, 16 + j)),
                  pl.BlockSpec((2, AQ, AW), lambda j, i: (j, 0, 0)),
                  pl.BlockSpec((AQ, 128), lambda j, i: (i, j))],
        out_specs=[pl.BlockSpec((AQ, 128), lambda j, i: (i, j)),
                   pl.BlockSpec((L, 128), lambda j, i: (0, j)),
                   pl.BlockSpec((L, 128), lambda j, i: (0, j)),
                   pl.BlockSpec((2, AQ, AW), lambda j, i: (j, 0, 0))],
        out_shape=[SDS((L, D), f32), SDS((L, D), f32), SDS((L, D), f32), SDS((HEADS, AQ, AW), f32)],
        scratch_shapes=[pltpu.VMEM((L + PAD, 128), bf16), pltpu.VMEM((L + PAD, 128), bf16),
                        pltpu.VMEM((L + PAD, 128), f32), pltpu.VMEM((L + PAD, 128), f32)],
        compiler_params=_cp(("arbitrary", "arbitrary")),
    )(qkvg, qkvg, qkvg, bias, do)


_SKEW = AQ + AW - 1


def _rel_index():
    d = (AW - 1) - np.arange(_SKEW)
    return np.clip(d, -MAX_REL, MAX_REL) + MAX_REL


def _band_mask():
    qc = np.arange(AQ)[:, None] // CHUNK + LEFT
    kc = np.arange(AW)[None, :] // CHUNK
    return jnp.asarray(np.where((kc <= qc) & (kc >= qc - LEFT), 0.0, -1e30), f32)


def _bias_matrix(rel_bias):
    tv = jnp.take(rel_bias, jnp.asarray(_rel_index()), axis=1)
    flat = jnp.tile(jnp.pad(tv, ((0, 0), (0, 1))), (1, AQ))
    top = flat[:, AQ - 1:AQ - 1 + AQ * _SKEW].reshape(HEADS, AQ, _SKEW)[:, :, :AW]
    return top + _band_mask()


def _bias_grad(dbias, onehot):
    flat = jnp.pad(dbias, ((0, 0), (0, 0), (0, _SKEW - AW))).reshape(HEADS, AQ * _SKEW)
    z = jnp.pad(flat, ((0, 0), (AQ - 1, AQ * 1024 - AQ * _SKEW - (AQ - 1)))).reshape(HEADS, AQ, 1024)

    def kern(z_ref, oh_ref, out_ref):
        diag = jnp.sum(z_ref[...], axis=1)
        out_ref[...] = jnp.dot(diag, oh_ref[...], preferred_element_type=f32, precision=lax.Precision.HIGHEST)

    return pl.pallas_call(kern, name="bias_grad", out_shape=SDS((HEADS, 2 * MAX_REL + 1), f32),
                          compiler_params=_cp(()))(z, onehot)


def _rel_onehot():
    oh = np.zeros((1024, 2 * MAX_REL + 1), np.float32)
    oh[np.arange(_SKEW), _rel_index()] = 1.0
    return jnp.asarray(oh)


def _gated_out_fwd(name, x, o, gate_src, gate_col, w, tm):
    def body(rows, fulls):
        xv, ov, gv = rows
        return [xv + _dotw(ov * jax.nn.silu(gv), fulls[0][...])], []

    return _tiled(name, body, x.shape[0], tm, [(x, D, 0), (o, D, 0), (gate_src, D, gate_col)], [w], [(D, f32)])[0]


def _gated_out_bwd(name, dxo, o, gate_src, gate_col, w, tm):
    def body(rows, fulls):
        dxv, ov, gv = rows
        wv = fulls[0][...]
        u, vjp = jax.vjp(lambda ov, gv: ov * jax.nn.silu(gv), ov, gv)
        do, dg = vjp(_dotw_t(dxv, wv))
        return [do, dg], [_wgrad(u, dxv, wv)]

    return _tiled(name, body, dxo.shape[0], tm, [(dxo, D, 0), (o, D, 0), (gate_src, D, gate_col)], [w],
                  [(D, f32), (D, f32)], [(w.shape, f32)])


def _loss_head(x, g, target, tm):
    def body(rows, fulls):
        xv, tv = rows

        def f(xv, gv):
            e = jnp.square(_rms(xv, gv) - tv)
            return 0.5 * jnp.sum(jnp.mean(e, axis=-1, keepdims=True), axis=0, keepdims=True)

        loss, vjp = jax.vjp(f, xv, fulls[0][...])
        dx, dg = vjp(jnp.ones((1, 1), f32))
        return [dx], [jnp.broadcast_to(loss, (1, 128)), dg]

    return _tiled("loss_head", body, x.shape[0], tm, [(x, D, 0), (target, D, 0)], [g], [(D, f32)],
                  [((1, 128), f32), ((1, D), f32)])


def _local_step(x, mem, target, p, late_shards, late_views, pack_early, pack_mid):
    TM = 256
    row = lambda v: v.reshape(1, -1)

    bt = lambda b: jnp.transpose(b, (0, 2, 1)).reshape(NG * GS, NS)
    disc_in = (p["ev_s5_lambda_re"], p["ev_s5_lambda_im"], p["ev_s5_log_dt"].reshape(NG, 1),
               bt(p["ev_s5_b_re"]), bt(p["ev_s5_b_im"]))
    ab_re, ab_im, bbt_re, bbt_im = _s5_disc_fwd(*disc_in)
    bmat = (_to_blockdiag(bbt_re).astype(bf16), _to_blockdiag(bbt_im).astype(bf16))
    cmat = (_to_blockdiag(p["ev_s5_c_re"].reshape(NG * GS, NS)).astype(bf16),
            _to_blockdiag(p["ev_s5_c_im"].reshape(NG * GS, NS)).astype(bf16))
    a_re, a_im = ab_re.reshape(1, NST), ab_im.reshape(1, NST)
    d_skip = row(p["ev_s5_d"])
    w32 = jnp.pad(p["ev_conv_w"], ((0, 1), (0, 0)))
    conv_b, ln_g, ln_b, glu_b = row(p["ev_conv_b"]), row(p["ev_conv_ln_g"]), row(p["ev_conv_ln_b"]), row(p["ev_s5_glu_b"])
    g_mem, g_ev, g_od, g_fin = row(p["mem_norm_g"]), row(p["ev_norm_g"]), row(p["od_norm_g"]), row(p["final_norm_g"])
    g_xa = [row(p["xa_norm_g"][l]) for l in range(2)]
    bias = _bias_matrix(p["od_rel_bias"])

    wkv = p["xa_w_kv"]
    kv0, kv1 = _kv_fwd(mem, g_mem, wkv[0], wkv[1])
    z = _norm_mm_fwd("ev_in_fwd", x, g_ev, p["ev_w_in"], TM)
    y_s5, hb_re, hb_im, *late = _s5_fwd(z, bmat, cmat, a_re, a_im, d_skip, comm=((), late_shards))
    p = late_views(p, late)
    wqg, wo = p["xa_w_qg"], p["xa_w_o"]
    c = _conv_fwd(z, w32, conv_b, 512)
    x1 = _ev_out_fwd(x, y_s5, c, z, glu_b, ln_g, ln_b, p["ev_s5_glu_w"], p["ev_w_out"], TM)
    x2 = _xa_fwd("xa0_fwd", x1, g_xa[0], kv0, wqg[0], wo[0], TM)
    qkvg = _norm_mm_fwd("od_in_fwd", x2, g_od, p["od_w_in"], TM)
    o = _attn_fwd(qkvg, bias)
    x3 = _gated_out_fwd("od_out_fwd", x2, o, qkvg, 3, p["od_w_out"], TM)
    x4 = _xa_fwd("xa1_fwd", x3, g_xa[1], kv1, wqg[1], wo[1], TM)

    G = {}
    dx4, loss_row, G["final_norm_g"] = _loss_head(x4, g_fin, target, TM)
    dx3, dg_xa1, dkv1, dwqg1, dwo1 = _xa_bwd("xa1_bwd", x3, dx4, g_xa[1], kv1, wqg[1], wo[1], TM)
    do, dgate, G["od_w_out"] = _gated_out_bwd("od_out_bwd", dx3, o, qkvg, 3, p["od_w_out"], TM)
    dq, dk, dv, dbias = _attn_bwd(qkvg, do, bias)
    G["od_rel_bias"] = _bias_grad(dbias, _rel_onehot())
    dx2, G["od_norm_g"], G["od_w_in"] = _norm_mm_bwd(
        "od_in_bwd", x2, g_od, p["od_w_in"], [(dq, D, 0), (dk, D, 0), (dv, D, 0), (dgate, D, 0)], dx3, TM)
    G["xa_w_qg"], G["xa_w_o"] = [None, dwqg1], [None, dwo1]
    dx1, dg_xa0, dkv0, G["xa_w_qg"][0], G["xa_w_o"][0], *recv_early = _xa_bwd(
        "xa0_bwd", x1, dx2, g_xa[0], kv0, wqg[0], wo[0], TM, comm=(pack_early(G), ()))
    G["xa_norm_g"] = jnp.concatenate([dg_xa0, dg_xa1], axis=0)
    G["mem_norm_g"], dwkv0, dwkv1 = _kv_bwd(mem, g_mem, wkv[0], wkv[1], dkv0, dkv1)
    G["xa_w_kv"] = [dwkv0, dwkv1]
    (dy_s5, dc, dga, dgb, G["ev_s5_glu_b"], G["ev_conv_ln_g"], G["ev_conv_ln_b"], G["ev_s5_glu_w"],
     G["ev_w_out"]) = _ev_out_bwd(dx1, y_s5, c, z, glu_b, ln_g, ln_b, p["ev_s5_glu_w"], p["ev_w_out"], TM)
    dval, dglu, dw32, G["ev_conv_b"] = _conv_bwd(z, dc, w32, 512)
    G["ev_conv_w"] = dw32[:CK]
    du, dbre, dbim, dcre, dcim, da_re, da_im, G["ev_s5_d"], *recv_mid = _s5_bwd(
        z, dy_s5, hb_re, hb_im, bmat, cmat, a_re, a_im, d_skip, comm=(pack_mid(G), ()))
    G["ev_s5_c_re"], G["ev_s5_c_im"] = _from_blockdiag(dcre), _from_blockdiag(dcim)
    G["ev_s5_lambda_re"], G["ev_s5_lambda_im"], G["ev_s5_log_dt"], G["ev_s5_b_re"], G["ev_s5_b_im"] = _s5_disc_bwd(
        disc_in, (da_re.reshape(NG, NS), da_im.reshape(NG, NS), _from_blockdiag(dbre), _from_blockdiag(dbim)))
    dx, G["ev_norm_g"], G["ev_w_in"] = _norm_mm_bwd(
        "ev_in_bwd", x, g_ev, p["ev_w_in"], [(du, SW, 0), (dga, SW, 0), (dval, CW, 0), (dglu, CW, 0), (dgb, CW, 0)], dx1, TM)
    return loss_row[0, 0], dx, G, recv_early, recv_mid


def _all_gather(payloads):
    n = len(payloads)

    def body(*refs):
        ins, outs = refs[:n], refs[n:2 * n]
        send_sems, recv_sems, local_sems = refs[2 * n:]
        x, y, c = _mesh_pos()
        me, sibling = (x, y, c), (x, y, 1 - c)
        chips = [(1 - x, y), (x, 1 - y), (1 - x, 1 - y)]

        def copy(p, k, block, to, src=None):
            return pltpu.make_async_remote_copy(
                src_ref=_slot(outs[p], block) if src is None else src, dst_ref=_slot(outs[p], block),
                send_sem=send_sems.at[7 * p + k], recv_sem=recv_sems.at[7 * p + k],
                device_id=to, device_id_type=pl.DeviceIdType.MESH)

        mine = [pltpu.make_async_copy(ins[p], _slot(outs[p], me), local_sems.at[p]) for p in range(n)]
        for cp in mine:
            cp.start()
        first = []
        for p in range(n):
            first.append(copy(p, 0, me, sibling, src=ins[p]))
            first += [copy(p, 1 + j, me, (*chip, c), src=ins[p]) for j, chip in enumerate(chips)]
        for cp in first:
            cp.start()
        passed = []
        for j, chip in enumerate(chips):
            for p in range(n):
                copy(p, 1 + j, (*chip, c), me).wait_recv()
                passed.append(copy(p, 4 + j, (*chip, c), sibling))
                passed[-1].start()
        for p in range(n):
            copy(p, 0, sibling, me).wait_recv()
            for j, chip in enumerate(chips):
                copy(p, 4 + j, (*chip, 1 - c), me).wait_recv()
        for cp in first + passed:
            cp.wait_send()
        for cp in mine:
            cp.wait()

    return pl.pallas_call(
        body, name="all_gather", in_specs=[_ANY] * n, out_specs=[_ANY] * n,
        out_shape=[SDS((NDEV, *a.shape), a.dtype) for a in payloads],
        scratch_shapes=[pltpu.SemaphoreType.DMA((7 * n,)), pltpu.SemaphoreType.DMA((7 * n,)), pltpu.SemaphoreType.DMA((n,))],
    )(*payloads)


def _exchange(slotted, whole):
    n = len(slotted) + len(whole)
    shapes, sems = _comm_shapes((slotted, whole))

    def body(*refs):
        copies = _comm_copies(len(slotted), refs[:n], refs[n:2 * n], *refs[2 * n:])
        for cp in copies:
            cp.start()
        for cp in copies:
            cp.wait()

    return pl.pallas_call(body, name="grad_exchange", in_specs=[_ANY] * n, out_specs=[_ANY] * n, out_shape=shapes,
                          scratch_shapes=sems)(*slotted, *whole)


def _adamw_math(g, w, m, v):
    m2 = ADAM_B1 * m + (1.0 - ADAM_B1) * g
    v2 = ADAM_B2 * v + (1.0 - ADAM_B2) * jnp.square(g)
    m_hat = m2 / (1.0 - ADAM_B1 ** ADAM_STEP)
    v_hat = v2 / (1.0 - ADAM_B2 ** ADAM_STEP)
    return -ADAM_LR * (m_hat / (jnp.sqrt(v_hat) + ADAM_EPS) + ADAM_WD * w), m2, v2


def _sum_slots(r_ref, rows=None):
    acc = r_ref[0].astype(f32) if rows is None else r_ref[0, rows[0]:rows[1], :].astype(f32)
    for k in range(1, NDEV):
        acc = acc + (r_ref[k] if rows is None else r_ref[k, rows[0]:rows[1], :]).astype(f32)
    return acc


def _sum_adamw(name, sources, w, m, v, tr):
    rows, C = w.shape
    starts = [int(t) for t in np.cumsum([0] + [r // tr for (_, _, r) in sources])]

    def kern(*refs):
        r_refs = refs[:len(sources)]
        w_ref, m_ref, v_ref, g_ref, d_ref, m2_ref, v2_ref = refs[len(sources):]
        g = _sum_slots(r_refs[0])
        for s in range(1, len(sources)):
            g = jnp.where(pl.program_id(0) >= starts[s], _sum_slots(r_refs[s]), g)
        g_ref[...] = g
        d_ref[...], m2_ref[...], v2_ref[...] = _adamw_math(g, w_ref[...], m_ref[...], v_ref[...])

    def src_spec(s):
        _, off, r = sources[s]
        return pl.BlockSpec((NDEV, tr, C), lambda i: (0, off // tr + jnp.clip(i - starts[s], 0, r // tr - 1), 0))

    blk = pl.BlockSpec((tr, C), lambda i: (i, 0))
    return pl.pallas_call(
        kern, name=name, grid=(rows // tr,),
        in_specs=[src_spec(s) for s in range(len(sources))] + [blk, blk, blk],
        out_specs=[blk] * 4, out_shape=[SDS((rows, C), f32)] * 4, compiler_params=_cp(),
    )(*[r for (r, _, _) in sources], w, m, v)


def _small_adamw(recvs, table, wmv):
    rnames = list(recvs)
    nr, nw = len(rnames), len(table)

    def kern(*refs):
        rr = dict(zip(rnames, refs[:nr]))
        ins, outs = refs[nr:nr + 3 * nw], refs[nr + 3 * nw:]
        for i, (_, shape, src, r0) in enumerate(table):
            g = _sum_slots(rr[src], (r0, r0 + shape[0]))[:, :shape[1]]
            outs[4 * i][...] = g
            res = _adamw_math(g, ins[3 * i][...], ins[3 * i + 1][...], ins[3 * i + 2][...])
            for o, val in zip(outs[4 * i + 1:4 * i + 4], res):
                o[...] = val

    flat = [t for (n, _, _, _) in table for t in wmv[n]]
    res = pl.pallas_call(
        kern, name="small_adamw", out_shape=[SDS(shape, f32) for (_, shape, _, _) in table for _ in range(4)],
        compiler_params=_cp(()),
    )(*[recvs[n] for n in rnames], *flat)
    return {n: tuple(res[4 * i:4 * i + 4]) for i, (n, _, _, _) in enumerate(table)}


WEIGHTS = ["mem_norm_g", "ev_norm_g", "ev_w_in", "ev_s5_lambda_re", "ev_s5_lambda_im", "ev_s5_log_dt", "ev_s5_b_re",
           "ev_s5_b_im", "ev_s5_c_re", "ev_s5_c_im", "ev_s5_d", "ev_s5_glu_w", "ev_s5_glu_b", "ev_conv_w", "ev_conv_b",
           "ev_conv_ln_g", "ev_conv_ln_b", "ev_w_out", "od_norm_g", "od_w_in", "od_rel_bias", "od_w_out", "xa_norm_g",
           "xa_w_qg", "xa_w_kv", "xa_w_o", "final_norm_g"]
FIRST = [("a1024", 1024, [("ev_w_out", None, "rows", 128), ("xa_w_o", 0, "rows", 128)]),
         ("a512", 512, [("ev_s5_glu_w", None, "rows", 64)]),
         ("a256", 256, [("xa_w_qg", 0, "cols", 1024), ("xa_w_kv", 0, "cols", 1024), ("xa_w_kv", 1, "cols", 1024)]),
         ("a320", 320, [("ev_w_in", None, "cols", 1024)])]
LATE = [("b1024", 1024, [("od_w_out", None, "rows", 128), ("xa_w_o", 1, "rows", 128)]),
        ("b512", 512, [("od_w_in", None, "cols", 1024)]),
        ("b256", 256, [("xa_w_qg", 1, "cols", 1024)])]
MATMUL_WEIGHTS = ["ev_w_out", "xa_w_o", "ev_s5_glu_w", "xa_w_qg", "xa_w_kv", "ev_w_in", "od_w_out", "od_w_in"]
ADAM_ROWS = {1024: 128, 512: 64, 256: 256, 320: 256}
SMALL = [("mem_norm_g", (1, D), "s1024", 0), ("ev_norm_g", (1, D), "s1024", 1), ("xa_norm_g", (2, D), "s1024", 2),
         ("final_norm_g", (1, D), "s1024", 4),
         ("ev_s5_d", (1, SW), "s512", 0), ("ev_s5_glu_b", (1, SW), "s512", 1), ("ev_conv_b", (1, CW), "s512", 2),
         ("ev_conv_ln_g", (1, CW), "s512", 3), ("ev_conv_ln_b", (1, CW), "s512", 4),
         ("ev_s5_c_re", (NG * GS, NS), "s64", 0), ("ev_s5_c_im", (NG * GS, NS), "s64", 512),
         ("ev_s5_b_re", (NG * GS, NS), "s64", 1024), ("ev_s5_b_im", (NG * GS, NS), "s64", 1536),
         ("ev_s5_lambda_re", (NG, NS), "s64", 2048), ("ev_s5_lambda_im", (NG, NS), "s64", 2080),
         ("ev_s5_log_dt", (1, NG), "s64", 2112),
         ("od_rel_bias", (HEADS, 2 * MAX_REL + 1), "s257", 0),
         ("ev_conv_w", (CK, CW // NDEV), "cw", 0), ("od_norm_g", (1, D // NDEV), "on", 0)]
S5_B = ("ev_s5_b_re", "ev_s5_b_im")


def _pad_rows(a, rows):
    return jnp.pad(a, ((0, rows - a.shape[0]), (0, 0)))


def _shard_payloads(a, groups):
    pick = lambda n, l, minor: (a[n] if l is None else a[n][l]).reshape(-1, minor)
    return [jnp.concatenate([pick(n, l, minor).astype(bf16) for n, l, _, _ in members], axis=0) for _, minor, members in groups]


def _weight_views(p, groups, gathered):
    p = dict(p)
    for (_, _, members), buf in zip(groups, gathered):
        off = 0
        for n, l, kind, rows in members:
            view = _W(buf, rows, off // rows, kind)
            if l is None:
                p[n] = view
            else:
                p[n] = list(p.get(n, [None, None]))
                p[n][l] = view
            off += rows
    return p


def _slotted_grads(G, groups):
    slot3 = lambda g, rows: g.reshape(NDEV, rows, g.shape[-1]) if g.ndim == 2 else g
    return [jnp.concatenate([slot3(G[n] if l is None else G[n][l], rows).astype(bf16) for n, l, _, rows in members], axis=1)
            for _, _, members in groups]


def _bt(b):
    return jnp.transpose(b.reshape(NG, NS, GS), (0, 2, 1)).reshape(NG * GS, NS)


def _unbt(b, shape):
    return jnp.transpose(b.reshape(NG, GS, NS), (0, 2, 1)).reshape(shape)


def kernel(*args):
    names = ["x", "mem", *WEIGHTS, "loss_target", *["m_" + n for n in WEIGHTS], *["v_" + n for n in WEIGHTS]]
    a = dict(zip(names, args, strict=True))

    other = jnp.concatenate([_pad_rows(a["ev_conv_w"][0], 32).reshape(16, 128), _pad_rows(a["od_norm_g"], 8)], axis=0)
    *gathered, gother = _all_gather([*_shard_payloads(a, FIRST), other])
    p = {n: (a[n] if n in ("xa_norm_g", "mem_norm_g", "final_norm_g") else a[n][0]) for n in WEIGHTS if n not in MATMUL_WEIGHTS}
    p = _weight_views(p, FIRST, gathered)
    p["ev_w_in"] = jnp.transpose(gathered[3], (1, 0, 2)).reshape(D, EVEN_IN)
    p["ev_conv_w"] = jnp.transpose(gother[:, :16].reshape(NDEV, 32, CW // NDEV)[:, :CK], (1, 0, 2)).reshape(CK, CW)
    p["od_norm_g"] = gother[:, 16].reshape(D)

    loss_part, dx, G, recv_early, recv_mid = _local_step(
        a["x"][0], a["mem"][0], a["loss_target"][0], p, _shard_payloads(a, LATE),
        lambda p, gathered: _weight_views(p, LATE, gathered),
        lambda G: _slotted_grads(G, LATE), lambda G: _slotted_grads(G, FIRST[:3]))

    G["ev_w_in"] = jnp.transpose(G["ev_w_in"].reshape(D, NDEV, EVEN_IN // NDEV), (1, 0, 2))
    cw = jnp.transpose(G["ev_conv_w"].reshape(CK, NDEV, CW // NDEV), (1, 0, 2))
    slotted = [*_slotted_grads(G, FIRST[3:]), jnp.pad(cw, ((0, 0), (0, 1), (0, 0))), G["od_norm_g"].reshape(NDEV, 1, D // NDEV)]
    whole = {"s1024": _pad_rows(jnp.concatenate([G[n] for n in ("mem_norm_g", "ev_norm_g", "xa_norm_g", "final_norm_g")]), 8),
             "s512": _pad_rows(jnp.concatenate([G[n] for n in ("ev_s5_d", "ev_s5_glu_b", "ev_conv_b", "ev_conv_ln_g",
                                                                 "ev_conv_ln_b")]), 8),
             "s64": _pad_rows(jnp.concatenate([G[n] for n in ("ev_s5_c_re", "ev_s5_c_im", "ev_s5_b_re", "ev_s5_b_im",
                                                                "ev_s5_lambda_re", "ev_s5_lambda_im")]
                                              + [jnp.pad(G["ev_s5_log_dt"].reshape(1, NG), ((0, 0), (0, NS - NG)))]), 2120),
             "s257": G["od_rel_bias"]}
    recv_last = _exchange(slotted, list(whole.values()))
    recvs = dict(zip([g[0] for g in LATE], recv_early))
    recvs.update(zip([g[0] for g in FIRST[:3]], recv_mid))
    recvs.update(zip(["a320", "cw", "on", *whole], recv_last))

    sources = {}
    for buf, minor, members in FIRST + LATE:
        off = 0
        for n, l, _, rows in members:
            sources.setdefault(n, []).append((l or 0, recvs[buf], off, rows, minor))
            off += rows
    res = {}
    for n, src in sources.items():
        src = sorted(src, key=lambda t: t[0])
        minor = src[0][4]
        w, m, v = (a[k + n].reshape(-1, minor) for k in ("", "m_", "v_"))
        res[n] = _sum_adamw("adamw_" + n, [(r, off, rows) for _, r, off, rows, _ in src], w, m, v, ADAM_ROWS[minor])
    two_d = lambda n, t: _bt(t[0]) if n in S5_B else t.reshape(dict((s[0], s[1]) for s in SMALL)[n])
    res.update(_small_adamw({k: recvs[k] for k in ("s1024", "s512", "s64", "s257", "cw", "on")}, SMALL,
                            {n: tuple(two_d(n, a[k + n]) for k in ("", "m_", "v_")) for n, _, _, _ in SMALL}))
    shaped = lambda n, t: _unbt(t, a[n].shape) if n in S5_B else t.reshape(a[n].shape)
    loss = lax.psum(loss_part, AXES)
    return (loss, dx[None], *[shaped(n, res[n][k]) for k in range(4) for n in WEIGHTS])
```

```python
import functools
from typing import NamedTuple

import jax
import jax.numpy as jnp
import numpy as np
from jax import lax
from jax.experimental import pallas as pl
from jax.experimental.pallas import tpu as pltpu

f32, bf16 = jnp.float32, jnp.bfloat16
SDS = jax.ShapeDtypeStruct

D = 1024
SW = 512
NG, GS, NS = 32, 16, 64
NST = NG * NS
CW = 512
CK = 31
EVEN_IN = 2 * SW + 3 * CW
HEADS, DH = 16, 64
CHUNK, LEFT = 64, 8
PAD = LEFT * CHUNK
MAX_REL = 128
MEM_LEN = 256
XH, XD = 4, 256
EPS = 1e-6
NDEV = 8
AXES = ("x", "y", "c")

ADAM_LR, ADAM_B1, ADAM_B2, ADAM_EPS, ADAM_WD, ADAM_STEP = 0.001, 0.9, 0.999, 1e-08, 0.01, 10

VMEM_LIMIT = 56 << 20
S5_T = 512
S5_CH = 512
AQ = 256
AW = AQ + PAD

NN = ((1,), (0,))
NT = ((1,), (1,))
TN = ((0,), (0,))


def _dot(a, b, dims):
    return lax.dot_general(a.astype(bf16), b.astype(bf16), (dims, ((), ())), preferred_element_type=f32)


def _dotw(a, w):
    if w.ndim == 2:
        return _dot(a, w, NN)
    return jnp.concatenate([_dot(a, w[j], NN) for j in range(w.shape[0])], axis=1)


def _dotw_t(g, w):
    if w.ndim == 2:
        return _dot(g, w, NT)
    n = w.shape[2]
    out = _dot(g[:, :n], w[0], NT)
    for j in range(1, w.shape[0]):
        out = out + _dot(g[:, j * n:(j + 1) * n], w[j], NT)
    return out


def _wgrad(a, g, w):
    if w.ndim == 2:
        return _dot(a, g, TN)
    n = w.shape[2]
    return jnp.stack([_dot(a, g[:, j * n:(j + 1) * n], TN) for j in range(w.shape[0])])


@jax.custom_vjp
def _mm(a, w):
    return _dotw(a, w)


def _mm_f(a, w):
    return _dotw(a, w), w


def _mm_b(w, g):
    return _dotw_t(g, w), None


_mm.defvjp(_mm_f, _mm_b)


@jax.custom_vjp
def _mm_nt(a, b):
    return _dot(a, b, NT)


def _mm_nt_f(a, b):
    return _dot(a, b, NT), (a, b)


def _mm_nt_b(res, g):
    a, b = res
    return _dot(g, b, NN), _dot(g, a, TN)


_mm_nt.defvjp(_mm_nt_f, _mm_nt_b)


@jax.custom_vjp
def _mm_nn(a, b):
    return _dot(a, b, NN)


def _mm_nn_f(a, b):
    return _dot(a, b, NN), (a, b)


def _mm_nn_b(res, g):
    a, b = res
    return _dot(g, b, NT), _dot(a, g, TN)


_mm_nn.defvjp(_mm_nn_f, _mm_nn_b)


def _plain_mm(a, w):
    return _dotw(a, w)


def _plain_nt(a, b):
    return _dot(a, b, NT)


def _rms(x, g):
    return x * lax.rsqrt(jnp.mean(x * x, axis=-1, keepdims=True) + EPS) * g


def _cp(sem=("arbitrary",)):
    return pltpu.CompilerParams(dimension_semantics=sem, vmem_limit_bytes=VMEM_LIMIT)


def _row(tm, w, col=0):
    return pl.BlockSpec((tm, w), lambda i: (i, col))


def _full(shape):
    nd = len(shape)
    return pl.BlockSpec(tuple(shape), lambda i: (0,) * nd, pipeline_mode=pl.Buffered(1))


class _W(NamedTuple):
    arr: jax.Array
    rows: int
    idx: int
    kind: str

    @property
    def shape(self):
        c = self.arr.shape[2]
        return (NDEV * self.rows, c) if self.kind == "rows" else (NDEV, self.rows, c)


class _WRef:
    def __init__(self, ref, kind):
        self.ref, self.kind = ref, kind

    def __getitem__(self, _):
        v = self.ref[...]
        return v.reshape(v.shape[0] * v.shape[1], v.shape[2]) if self.kind == "rows" else v


def _wspec(w):
    if isinstance(w, _W):
        return pl.BlockSpec((NDEV, w.rows, w.arr.shape[2]), lambda i: (0, w.idx, 0), pipeline_mode=pl.Buffered(1))
    return _full(w.shape)


_ANY = pl.BlockSpec(memory_space=pl.ANY)
_NO_COMM = ((), ())


def _mesh_pos():
    return tuple(lax.axis_index(a) for a in AXES)


def _slot(ref, dev):
    return ref.at[4 * dev[0] + 2 * dev[1] + dev[2]]


def _comm_shapes(comm):
    slotted, whole = comm
    n = len(slotted) + len(whole)
    shapes = [SDS(a.shape, a.dtype) for a in slotted] + [SDS((NDEV, *a.shape), a.dtype) for a in whole]
    sems = [pltpu.SemaphoreType.DMA((7 * n,)), pltpu.SemaphoreType.DMA((7 * n,)), pltpu.SemaphoreType.DMA((n,))] if n else []
    return shapes, sems


def _comm_copies(ns, ins, outs, send_sems, recv_sems, local_sems):
    n = len(ins)
    x, y, c = _mesh_pos()
    me = (x, y, c)
    src = lambda p, dev: _slot(ins[p], dev) if p < ns else ins[p]
    copies = [pltpu.make_async_copy(src(p, me), _slot(outs[p], me), local_sems.at[p]) for p in range(n)]
    for k in range(1, NDEV):
        flip = lambda v, bit: 1 - v if (k >> bit) & 1 else v
        peer = (flip(x, 2), flip(y, 1), flip(c, 0))
        for p in range(n):
            copies.append(pltpu.make_async_remote_copy(
                src_ref=src(p, peer), dst_ref=_slot(outs[p], me), send_sem=send_sems.at[7 * p + k - 1],
                recv_sem=recv_sems.at[7 * p + k - 1], device_id=peer, device_id_type=pl.DeviceIdType.MESH))
    return copies


def _comm_hook(comm, cin, cout, sems, step, last):
    if not cin:
        return lambda: None

    @pl.when(step == 0)
    def _():
        for cp in _comm_copies(len(comm[0]), cin, cout, *sems):
            cp.start()

    def finish():
        @pl.when(step == last)
        def _():
            for cp in _comm_copies(len(comm[0]), cin, cout, *sems):
                cp.wait()

    return finish


def _tiled(name, body, L, tm, row_ins, full_ins, row_outs, acc_outs=(), comm=_NO_COMM):
    nr, nf, no, na = len(row_ins), len(full_ins), len(row_outs), len(acc_outs)
    cins = [*comm[0], *comm[1]]
    nc = len(cins)
    cshapes, csems = _comm_shapes(comm)

    def kern(*refs):
        rin = refs[:nr]
        fin = [_WRef(r, w.kind) if isinstance(w, _W) else r for r, w in zip(refs[nr:nr + nf], full_ins)]
        cin = refs[nr + nf:nr + nf + nc]
        outs = refs[nr + nf + nc:]
        rout, aout, cout, sems = outs[:no], outs[no:no + na], outs[no + na:no + na + nc], outs[no + na + nc:]
        finish = _comm_hook(comm, cin, cout, sems, pl.program_id(0), L // tm - 1)
        routs, accs = body([r[...] for r in rin], fin)
        for r, v in zip(rout, routs):
            r[...] = v.astype(r.dtype)
        if aout:
            @pl.when(pl.program_id(0) == 0)
            def _():
                for a in aout:
                    a[...] = jnp.zeros(a.shape, a.dtype)

            for a, v in zip(aout, accs):
                a[...] += v.astype(a.dtype)
        finish()

    return pl.pallas_call(
        kern, name=name, grid=(L // tm,),
        in_specs=[_row(tm, w, c) for (_, w, c) in row_ins] + [_wspec(a) for a in full_ins] + [_ANY] * nc,
        out_specs=[_row(tm, w) for (w, _) in row_outs] + [_full(s) for (s, _) in acc_outs] + [_ANY] * nc,
        out_shape=[SDS((L, w), dt) for (w, dt) in row_outs] + [SDS(tuple(s), dt) for (s, dt) in acc_outs] + cshapes,
        scratch_shapes=csems, compiler_params=_cp(),
    )(*[a for (a, _, _) in row_ins], *[a.arr if isinstance(a, _W) else a for a in full_ins], *cins)


def _norm_mm_fwd(name, x, g, w, tm):
    def body(rows, fulls):
        g_ref, w_ref = fulls
        return [_dotw(_rms(rows[0], g_ref[...]), w_ref[...])], []

    n_out = w.shape[1] if len(w.shape) == 2 else w.shape[0] * w.shape[2]
    return _tiled(name, body, x.shape[0], tm, [(x, D, 0)], [g, w], [(n_out, f32)])[0]


def _norm_mm_bwd(name, x, g, w, dz_parts, dres, tm):
    n = len(dz_parts)

    def body(rows, fulls):
        g_ref, w_ref = fulls
        dz = rows[1] if n == 1 else jnp.concatenate(rows[1:1 + n], axis=1)
        wv = w_ref[...]
        h, vjp = jax.vjp(_rms, rows[0], g_ref[...])
        dx, dg = vjp(_dotw_t(dz, wv))
        return [dx + rows[1 + n]], [dg, _wgrad(h, dz, wv)]

    return _tiled(name, body, x.shape[0], tm, [(x, D, 0), *dz_parts, (dres, D, 0)], [g, w],
                  [(D, f32)], [((1, D), f32), (tuple(w.shape), f32)])


def _scan_tables(tab_ref, ar, ai, reverse):
    pw = [(ar, ai)]
    for _ in range(7):
        pr, pi = pw[-1]
        pw.append((pr * ar - pi * ai, pr * ai + pi * ar))
    row = lax.broadcasted_iota(jnp.int32, (8, NST), 0)
    for si, s in enumerate((1, 2, 4)):
        keep = (row <= 7 - s) if reverse else (row >= s)
        tab_ref[2 * si] = jnp.where(keep, pw[s - 1][0], 0.0)
        tab_ref[2 * si + 1] = jnp.where(keep, pw[s - 1][1], 0.0)
    pr = jnp.zeros((8, NST), f32)
    pi = jnp.zeros((8, NST), f32)
    for t in range(8):
        k = (7 - t) if reverse else t
        pr = jnp.where(row == t, pw[k][0], pr)
        pi = jnp.where(row == t, pw[k][1], pi)
    tab_ref[6] = pr
    tab_ref[7] = pi


def _scan(re_ref, im_ref, row0, nrows, tab_ref, car_re, car_im, reverse, extra=None):
    nt = nrows // 8
    accs = []
    for cc in range(NST // S5_CH):
        cols = slice(cc * S5_CH, (cc + 1) * S5_CH)

        def step(j, carry, cols=cols):
            cr, ci, acc = carry
            jj = (nt - 1 - j) if reverse else j
            r0 = pl.multiple_of(row0 + jj * 8, 8)
            xr = re_ref[pl.ds(r0, 8), cols]
            xi = im_ref[pl.ds(r0, 8), cols]
            for si, s in enumerate((1, 2, 4)):
                sh = (8 - s) if reverse else s
                yr = pltpu.roll(xr, sh, 0)
                yi = pltpu.roll(xi, sh, 0)
                tr = tab_ref[2 * si, :, cols]
                ti = tab_ref[2 * si + 1, :, cols]
                xr, xi = xr + tr * yr - ti * yi, xi + tr * yi + ti * yr
            tr = tab_ref[6, :, cols]
            ti = tab_ref[7, :, cols]
            xr, xi = xr + tr * cr - ti * ci, xi + tr * ci + ti * cr
            re_ref[pl.ds(r0, 8), cols] = xr
            im_ref[pl.ds(r0, 8), cols] = xi
            if extra is not None:
                acc = extra(cols, jj, xr, xi, acc)
            b = 0 if reverse else 7
            return (jnp.broadcast_to(xr[b:b + 1, :], (8, S5_CH)), jnp.broadcast_to(xi[b:b + 1, :], (8, S5_CH)), acc)

        acc0 = (jnp.zeros((8, S5_CH), f32), jnp.zeros((8, S5_CH), f32)) if extra is not None else 0
        cr, ci, acc = lax.fori_loop(0, nt, step, (car_re[:, cols], car_im[:, cols], acc0))
        car_re[:, cols] = cr
        car_im[:, cols] = ci
        accs.append(acc)
    return accs


def _s5_fwd(z, bmat, cmat, a_re, a_im, d_skip, comm=_NO_COMM):
    L = z.shape[0]
    nb = L // S5_T
    cins = [*comm[0], *comm[1]]
    nc = len(cins)
    cshapes, csems = _comm_shapes(comm)

    def kern(u_ref, bre, bim, cre, cim, ar_ref, ai_ref, d_ref, *rest):
        cin, (y_ref, hbr_ref, hbi_ref), cout = rest[:nc], rest[nc:nc + 3], rest[nc + 3:2 * nc + 3]
        h_re, h_im, tab, car_re, car_im, *sems = rest[2 * nc + 3:]
        finish = _comm_hook(comm, cin, cout, sems, pl.program_id(0), nb - 1)

        @pl.when(pl.program_id(0) == 0)
        def _():
            _scan_tables(tab, ar_ref[...], ai_ref[...], False)
            car_re[...] = jnp.zeros_like(car_re)
            car_im[...] = jnp.zeros_like(car_im)

        hbr_ref[...] = car_re[...]
        hbi_ref[...] = car_im[...]
        u = u_ref[...]
        for hf in range(2):
            uh = u[:, hf * 256:(hf + 1) * 256]
            h_re[:, hf * 1024:(hf + 1) * 1024] = _dot(uh, bre[hf], NN)
            h_im[:, hf * 1024:(hf + 1) * 1024] = _dot(uh, bim[hf], NN)
        _scan(h_re, h_im, 0, S5_T, tab, car_re, car_im, False)
        ys = []
        for hf in range(2):
            cs = slice(hf * 1024, (hf + 1) * 1024)
            ys.append(_dot(h_re[:, cs], cre[hf], NT) - _dot(h_im[:, cs], cim[hf], NT))
        y_ref[...] = jnp.concatenate(ys, axis=1) + d_ref[...] * u
        finish()

    fulls = [*bmat, *cmat, a_re, a_im, d_skip]
    return pl.pallas_call(
        kern, name="s5_fwd", grid=(nb,),
        in_specs=[_row(S5_T, SW, 0)] + [_full(a.shape) for a in fulls] + [_ANY] * nc,
        out_specs=[_row(S5_T, SW), _row(8, NST), _row(8, NST)] + [_ANY] * nc,
        out_shape=[SDS((L, SW), f32), SDS((nb * 8, NST), f32), SDS((nb * 8, NST), f32)] + cshapes,
        scratch_shapes=[pltpu.VMEM((S5_T, NST), f32), pltpu.VMEM((S5_T, NST), f32), pltpu.VMEM((8, 8, NST), f32),
                        pltpu.VMEM((8, NST), f32), pltpu.VMEM((8, NST), f32)] + csems,
        compiler_params=_cp(),
    )(z, *fulls, *cins)


def _s5_bwd(z, dy, hb_re, hb_im, bmat, cmat, a_re, a_im, d_skip, comm=_NO_COMM):
    L = z.shape[0]
    nb = L // S5_T
    T = S5_T
    cins = [*comm[0], *comm[1]]
    nc = len(cins)
    cshapes, csems = _comm_shapes(comm)

    def kern(u_ref, dy_ref, hbr_ref, hbi_ref, bre, bim, cre, cim, ar_ref, ai_ref, d_ref, *rest):
        cin, (du_ref, dbre, dbim, dcre, dcim, dar, dai, dd), cout = rest[:nc], rest[nc:nc + 8], rest[nc + 8:2 * nc + 8]
        h_re, h_im, g_re, g_im, tabf, tabr, car_re, car_im, gcar_re, gcar_im, *sems = rest[2 * nc + 8:]
        finish = _comm_hook(comm, cin, cout, sems, pl.program_id(0), nb - 1)

        @pl.when(pl.program_id(0) == 0)
        def _():
            _scan_tables(tabf, ar_ref[...], ai_ref[...], False)
            _scan_tables(tabr, ar_ref[...], -ai_ref[...], True)
            gcar_re[...] = jnp.zeros_like(gcar_re)
            gcar_im[...] = jnp.zeros_like(gcar_im)
            for r in (dbre, dbim, dcre, dcim, dar, dai, dd):
                r[...] = jnp.zeros_like(r)

        u = u_ref[...]
        dyv = dy_ref[...]
        h_re[0:8, :] = hbr_ref[...]
        h_im[0:8, :] = hbi_ref[...]
        car_re[...] = hbr_ref[...]
        car_im[...] = hbi_ref[...]
        for hf in range(2):
            cs = slice(hf * 1024, (hf + 1) * 1024)
            uh = u[:, hf * 256:(hf + 1) * 256]
            dyh = dyv[:, hf * 256:(hf + 1) * 256]
            h_re[8:, cs] = _dot(uh, bre[hf], NN)
            h_im[8:, cs] = _dot(uh, bim[hf], NN)
            g_re[:, cs] = _dot(dyh, cre[hf], NN)
            g_im[:, cs] = -_dot(dyh, cim[hf], NN)
        _scan(h_re, h_im, 8, T, tabf, car_re, car_im, False)

        row = lax.broadcasted_iota(jnp.int32, (8, S5_CH), 0)

        def fold_da(cols, jj, gr, gi, acc):
            r0 = pl.multiple_of(jj * 8, 8)
            hr = jnp.where(row == 0, jnp.broadcast_to(h_re[pl.ds(r0, 8), cols][7:8, :], (8, S5_CH)),
                           pltpu.roll(h_re[pl.ds(r0 + 8, 8), cols], 1, 0))
            hi = jnp.where(row == 0, jnp.broadcast_to(h_im[pl.ds(r0, 8), cols][7:8, :], (8, S5_CH)),
                           pltpu.roll(h_im[pl.ds(r0 + 8, 8), cols], 1, 0))
            return acc[0] + gr * hr + gi * hi, acc[1] + gi * hr - gr * hi

        accs = _scan(g_re, g_im, 0, T, tabr, gcar_re, gcar_im, True, extra=fold_da)
        for cc, (ar_, ai_) in enumerate(accs):
            cols = slice(cc * S5_CH, (cc + 1) * S5_CH)
            dar[:, cols] += jnp.sum(ar_, axis=0, keepdims=True)
            dai[:, cols] += jnp.sum(ai_, axis=0, keepdims=True)

        dus = []
        for hf in range(2):
            cs = slice(hf * 1024, (hf + 1) * 1024)
            uh = u[:, hf * 256:(hf + 1) * 256]
            dyh = dyv[:, hf * 256:(hf + 1) * 256]
            gr = g_re[:, cs]
            gi = g_im[:, cs]
            dus.append(_dot(gr, bre[hf], NT) + _dot(gi, bim[hf], NT))
            dbre[hf] += _dot(uh, gr, TN)
            dbim[hf] += _dot(uh, gi, TN)
            dcre[hf] += _dot(dyh, h_re[8:, cs], TN)
            dcim[hf] -= _dot(dyh, h_im[8:, cs], TN)
        du_ref[...] = jnp.concatenate(dus, axis=1) + d_ref[...] * dyv
        dd[...] += jnp.sum(dyv * u, axis=0, keepdims=True)
        finish()

    rev = lambda w: pl.BlockSpec((T, w), lambda k: (nb - 1 - k, 0))
    rev8 = pl.BlockSpec((8, NST), lambda k: (nb - 1 - k, 0))
    fulls = [*bmat, *cmat, a_re, a_im, d_skip]
    mat = ((2, 256, 1024), f32)
    accs = [mat, mat, mat, mat, ((1, NST), f32), ((1, NST), f32), ((1, SW), f32)]
    return pl.pallas_call(
        kern, name="s5_bwd", grid=(nb,),
        in_specs=[rev(SW), rev(SW), rev8, rev8] + [_full(a.shape) for a in fulls] + [_ANY] * nc,
        out_specs=[rev(SW)] + [_full(s) for (s, _) in accs] + [_ANY] * nc,
        out_shape=[SDS((L, SW), f32)] + [SDS(s, dt) for (s, dt) in accs] + cshapes,
        scratch_shapes=[pltpu.VMEM((T + 8, NST), f32), pltpu.VMEM((T + 8, NST), f32),
                        pltpu.VMEM((T, NST), f32), pltpu.VMEM((T, NST), f32),
                        pltpu.VMEM((8, 8, NST), f32), pltpu.VMEM((8, 8, NST), f32),
                        pltpu.VMEM((8, NST), f32), pltpu.VMEM((8, NST), f32),
                        pltpu.VMEM((8, NST), f32), pltpu.VMEM((8, NST), f32)] + csems,
        compiler_params=_cp(),
    )(z, dy, hb_re, hb_im, *fulls, *cins)


def _s5_disc(lam_re, lam_im, log_dt, bt_re, bt_im):
    dt = jnp.exp(log_dt)
    mag = jnp.exp(lam_re * dt)
    ab_re = mag * jnp.cos(lam_im * dt)
    ab_im = mag * jnp.sin(lam_im * dt)
    den = lam_re * lam_re + lam_im * lam_im
    nr = ab_re - 1.0
    coef_re = (nr * lam_re + ab_im * lam_im) / den
    coef_im = (ab_im * lam_re - nr * lam_im) / den
    rep = lambda c: jnp.broadcast_to(c[:, None, :], (NG, GS, NS)).reshape(NG * GS, NS)
    cr, ci = rep(coef_re), rep(coef_im)
    return ab_re, ab_im, cr * bt_re - ci * bt_im, cr * bt_im + ci * bt_re


def _s5_disc_fwd(*params):
    def kern(*refs):
        outs = _s5_disc(*[r[...] for r in refs[:5]])
        for r, v in zip(refs[5:], outs):
            r[...] = v

    shapes = [SDS((NG, NS), f32), SDS((NG, NS), f32), SDS((NG * GS, NS), f32), SDS((NG * GS, NS), f32)]
    return pl.pallas_call(kern, name="s5_disc_fwd", out_shape=shapes)(*params)


def _s5_disc_bwd(params, cts):
    def kern(*refs):
        _, vjp = jax.vjp(_s5_disc, *[r[...] for r in refs[:5]])
        for r, v in zip(refs[9:], vjp(tuple(r[...] for r in refs[5:9]))):
            r[...] = v

    return pl.pallas_call(kern, name="s5_disc_bwd", out_shape=[SDS(p.shape, f32) for p in params])(*params, *cts)


def _diag_mask():
    r = np.arange(256)[:, None] // GS
    c = np.arange(1024)[None, :] // NS
    return jnp.asarray(r == c, f32)


def _to_blockdiag(m):
    return jnp.tile(m.reshape(2, 256, NS), (1, 1, 16)) * _diag_mask()


def _from_blockdiag(dm):
    return (dm * _diag_mask()).reshape(2, 256, 16, NS).sum(axis=2).reshape(NG * GS, NS)


def _conv_shifts(win, sh, tm):
    win[tm + 32:, :] = jnp.zeros((8, CW), f32)
    for s in range(8):
        sh[s] = win[s:s + tm + 32, :]


def _conv_taps(offset, w_ref, sh, r, accs):
    for k in range(CK):
        s, q = offset(k) % 8, offset(k) // 8
        wk = w_ref[k:k + 1, :]
        accs = [acc + wk * sh[s, pl.ds(pl.multiple_of(r + 8 * (a + q), 8), 8), :] for a, acc in enumerate(accs)]
    return accs


def _conv_fwd(z, w32, b, tm):
    L = z.shape[0]
    hb = tm // 32

    def kern(val, glu, hval, hglu, w_ref, b_ref, c_ref, win, sh):
        i = pl.program_id(0)
        win[0:32, :] = hval[...] * jax.nn.sigmoid(hglu[...]) * (i > 0).astype(f32)
        win[32:tm + 32, :] = val[...] * jax.nn.sigmoid(glu[...])
        _conv_shifts(win, sh, tm)

        def strip(j, carry):
            r = j * 16
            accs = _conv_taps(lambda k: k + 2, w_ref, sh, r, [jnp.zeros((8, CW), f32) + b_ref[...]] * 2)
            c_ref[pl.ds(pl.multiple_of(r, 16), 16), :] = jnp.concatenate(accs, axis=0)
            return carry

        lax.fori_loop(0, tm // 16, strip, 0)

    halo = lambda col: pl.BlockSpec((32, CW), lambda i: (jnp.maximum(i * hb - 1, 0), col))
    return pl.pallas_call(
        kern, name="conv_fwd", grid=(L // tm,),
        in_specs=[_row(tm, CW, 2), _row(tm, CW, 3), halo(2), halo(3), _full(w32.shape), _full(b.shape)],
        out_specs=_row(tm, CW), out_shape=SDS((L, CW), f32),
        scratch_shapes=[pltpu.VMEM((tm + 40, CW), f32), pltpu.VMEM((8, tm + 32, CW), f32)], compiler_params=_cp(),
    )(z, z, z, z, w32, b)


def _conv_bwd(z, dc, w32, tm):
    L = z.shape[0]
    hb = tm // 32
    nt = L // tm

    def kern(val, glu, hval, hglu, dc_ref, hdc, w_ref, dval_ref, dglu_ref, dw_ref, db_ref, win, sh, dwin, dsh):
        i = pl.program_id(0)

        @pl.when(i == 0)
        def _():
            dw_ref[...] = jnp.zeros_like(dw_ref)
            db_ref[...] = jnp.zeros_like(db_ref)

        win[0:32, :] = hval[...] * jax.nn.sigmoid(hglu[...]) * (i > 0).astype(f32)
        win[32:tm + 32, :] = val[...] * jax.nn.sigmoid(glu[...])
        _conv_shifts(win, sh, tm)
        dwin[0:tm, :] = dc_ref[...]
        dwin[tm:tm + 32, :] = hdc[...] * (i < nt - 1).astype(f32)
        _conv_shifts(dwin, dsh, tm)

        def strip(j, carry):
            r = j * 16
            rows = pl.ds(pl.multiple_of(r, 16), 16)
            dv = jnp.concatenate(_conv_taps(lambda k: 30 - k, w_ref, dsh, r, [jnp.zeros((8, CW), f32)] * 2), axis=0)
            sg = jax.nn.sigmoid(glu[rows, :])
            dval_ref[rows, :] = dv * sg
            dglu_ref[rows, :] = dv * val[rows, :] * sg * (1.0 - sg)
            return carry

        lax.fori_loop(0, tm // 16, strip, 0)

        for k0 in range(0, CK, 2):
            taps = [k for k in (k0, k0 + 1) if k < CK]

            def rows64(j, accs, taps=taps):
                for u in range(8):
                    r = pl.multiple_of(j * 64 + u * 8, 8)
                    dcv = dc_ref[pl.ds(r, 8), :]
                    accs = [acc + dcv * sh[(k + 2) % 8, pl.ds(pl.multiple_of(r + 8 * ((k + 2) // 8), 8), 8), :]
                            for k, acc in zip(taps, accs)]
                return accs

            accs = lax.fori_loop(0, tm // 64, rows64, [jnp.zeros((8, CW), f32)] * len(taps))
            for k, acc in zip(taps, accs):
                dw_ref[k:k + 1, :] += jnp.sum(acc, axis=0, keepdims=True)
        db_ref[...] += jnp.sum(dc_ref[...], axis=0, keepdims=True)

    halo = lambda col: pl.BlockSpec((32, CW), lambda i: (jnp.maximum(i * hb - 1, 0), col))
    nxt = pl.BlockSpec((32, CW), lambda i: (jnp.minimum((i + 1) * hb, L // 32 - 1), 0))
    return pl.pallas_call(
        kern, name="conv_bwd", grid=(nt,),
        in_specs=[_row(tm, CW, 2), _row(tm, CW, 3), halo(2), halo(3), _row(tm, CW), nxt, _full(w32.shape)],
        out_specs=[_row(tm, CW), _row(tm, CW), _full((32, CW)), _full((1, CW))],
        out_shape=[SDS((L, CW), f32), SDS((L, CW), f32), SDS((32, CW), f32), SDS((1, CW), f32)],
        scratch_shapes=[pltpu.VMEM((tm + 40, CW), f32), pltpu.VMEM((8, tm + 32, CW), f32),
                        pltpu.VMEM((tm + 40, CW), f32), pltpu.VMEM((8, tm + 32, CW), f32)], compiler_params=_cp(),
    )(z, z, z, z, dc, dc, w32)


def _ev_out_f(y, c, ga, gb, tap, glu_b, ln_g, ln_b, glu_w, w_out, mm):
    z1 = jax.nn.gelu(y)
    ya = z1 * jax.nn.sigmoid(mm(z1, glu_w) + glu_b + tap) * jax.nn.silu(ga)
    mu = jnp.mean(c, axis=-1, keepdims=True)
    var = jnp.mean(jnp.square(c - mu), axis=-1, keepdims=True)
    cn = (c - mu) * lax.rsqrt(var + EPS) * ln_g + ln_b
    cat = jnp.concatenate([ya, jax.nn.silu(cn) * jax.nn.silu(gb)], axis=1)
    return mm(cat, w_out), (z1, cat)


def _ev_out_fwd(x, y, c, z, glu_b, ln_g, ln_b, glu_w, w_out, tm):
    def body(rows, fulls):
        xv, yv, cv, ga, gb = rows
        gb_ref, lg_ref, lb_ref, gw_ref, wo_ref = fulls
        out, _ = _ev_out_f(yv, cv, ga, gb, 0.0, gb_ref[...], lg_ref[...], lb_ref[...], gw_ref[...], wo_ref[...], _plain_mm)
        return [xv + out], []

    return _tiled("ev_out_fwd", body, x.shape[0], tm, [(x, D, 0), (y, SW, 0), (c, CW, 0), (z, SW, 1), (z, CW, 4)],
                  [glu_b, ln_g, ln_b, glu_w, w_out], [(D, f32)])[0]


def _ev_out_bwd(dx1, y, c, z, glu_b, ln_g, ln_b, glu_w, w_out, tm):
    def body(rows, fulls):
        dxv, yv, cv, ga, gb = rows
        gb_ref, lg_ref, lb_ref, gw_ref, wo_ref = fulls
        gw, wo = gw_ref[...], wo_ref[...]
        f = lambda yv, cv, ga, gb, tap, b, lg, lb: _ev_out_f(yv, cv, ga, gb, tap, b, lg, lb, gw, wo, _mm)
        _, vjp, (z1, cat) = jax.vjp(f, yv, cv, ga, gb, jnp.zeros((tm, SW), f32), gb_ref[...], lg_ref[...], lb_ref[...],
                                    has_aux=True)
        dy, dc, dga, dgb, dtap, db, dlg, dlb = vjp(dxv)
        return [dy, dc, dga, dgb], [db, dlg, dlb, _wgrad(z1, dtap, gw), _wgrad(cat, dxv, wo)]

    return _tiled("ev_out_bwd", body, dx1.shape[0], tm, [(dx1, D, 0), (y, SW, 0), (c, CW, 0), (z, SW, 1), (z, CW, 4)],
                  [glu_b, ln_g, ln_b, glu_w, w_out], [(SW, f32), (CW, f32), (SW, f32), (CW, f32)],
                  [((1, SW), f32), ((1, CW), f32), ((1, CW), f32), (glu_w.shape, f32), (w_out.shape, f32)])


def _kv_f(mem, g, tap0, tap1, w0, w1, mm):
    mn = _rms(mem, g)
    return (mm(mn, w0) + tap0, mm(mn, w1) + tap1), mn


def _kv_fwd(mem, g, w0, w1):
    def body(rows, fulls):
        g_ref, w0_ref, w1_ref = fulls
        (kv0, kv1), _ = _kv_f(rows[0], g_ref[...], 0.0, 0.0, w0_ref[...], w1_ref[...], _plain_mm)
        return [kv0, kv1], []

    return _tiled("kv_fwd", body, MEM_LEN, MEM_LEN, [(mem, D, 0)], [g, w0, w1], [(2 * D, f32), (2 * D, f32)])


def _kv_bwd(mem, g, w0, w1, dkv0, dkv1):
    def body(rows, fulls):
        g_ref, w0_ref, w1_ref = fulls
        w0v, w1v = w0_ref[...], w1_ref[...]
        tap = jnp.zeros((MEM_LEN, 2 * D), f32)
        _, vjp, mn = jax.vjp(lambda g_, t0, t1: _kv_f(rows[0], g_, t0, t1, w0v, w1v, _mm), g_ref[...], tap, tap, has_aux=True)
        dg, d0, d1 = vjp((rows[1], rows[2]))
        return [], [dg, _wgrad(mn, d0, w0v), _wgrad(mn, d1, w1v)]

    return _tiled("kv_bwd", body, MEM_LEN, MEM_LEN, [(mem, D, 0), (dkv0, 2 * D, 0), (dkv1, 2 * D, 0)], [g, w0, w1], [],
                  [((1, D), f32), (w0.shape, f32), (w1.shape, f32)])


def _xa_f(x, g, kv, tap, w_qg, w_o, mm, nt, nn):
    h = _rms(x, g)
    qg = mm(h, w_qg) + tap
    outs = []
    for hd in range(XH):
        q = qg[:, hd * XD:(hd + 1) * XD]
        k = kv[:, hd * XD:(hd + 1) * XD]
        v = kv[:, D + hd * XD:D + (hd + 1) * XD]
        s = nt(q, k) * (XD ** -0.5)
        e = jnp.exp(s - jnp.max(s, axis=-1, keepdims=True))
        outs.append(nn(e / jnp.sum(e, axis=-1, keepdims=True), v))
    u = jnp.concatenate(outs, axis=1) * jax.nn.silu(qg[:, D:])
    return mm(u, w_o), (h, u)


def _xa_fwd(name, x, g, kv, w_qg, w_o, tm):
    def body(rows, fulls):
        g_ref, kv_ref, wq_ref, wo_ref = fulls
        out, _ = _xa_f(rows[0], g_ref[...], kv_ref[...], 0.0, wq_ref[...], wo_ref[...], _plain_mm, _plain_nt, _plain_mm)
        return [rows[0] + out], []

    return _tiled(name, body, x.shape[0], tm, [(x, D, 0)], [g, kv, w_qg, w_o], [(D, f32)])[0]


def _xa_bwd(name, x, dxo, g, kv, w_qg, w_o, tm, comm=_NO_COMM):
    def body(rows, fulls):
        xv, dxv = rows
        g_ref, kv_ref, wq_ref, wo_ref = fulls
        wq, wo = wq_ref[...], wo_ref[...]
        f = lambda xv, gv, kvv, tap: _xa_f(xv, gv, kvv, tap, wq, wo, _mm, _mm_nt, _mm_nn)
        _, vjp, (h, u) = jax.vjp(f, xv, g_ref[...], kv_ref[...], jnp.zeros((tm, 2 * D), f32), has_aux=True)
        dx, dg, dkv, dtap = vjp(dxv)
        return [dx + dxv], [dg, dkv, _wgrad(h, dtap, wq), _wgrad(u, dxv, wo)]

    return _tiled(name, body, x.shape[0], tm, [(x, D, 0), (dxo, D, 0)], [g, kv, w_qg, w_o], [(D, f32)],
                  [((1, D), f32), ((MEM_LEN, 2 * D), f32), (w_qg.shape, f32), (w_o.shape, f32)], comm=comm)


def _attn_pad(k_ref, v_ref, kpad, vpad):
    kpad[0:PAD, :] = jnp.zeros((PAD, 128), bf16)
    vpad[0:PAD, :] = jnp.zeros((PAD, 128), bf16)
    kpad[PAD:, :] = k_ref[...].astype(bf16)
    vpad[PAD:, :] = v_ref[...].astype(bf16)


def _attn_exp(qm, kwin, bm, mask_row):
    x = _dot(qm, kwin, NT) + bm + mask_row
    e = jnp.exp(x - jnp.max(x, axis=-1, keepdims=True))
    return e, 1.0 / jnp.sum(e, axis=-1, keepdims=True)


def _attn_fwd(qkvg, bias):
    L = qkvg.shape[0]
    nq = L // AQ

    def kern(q_ref, k_ref, v_ref, b_ref, o_ref, kpad, vpad):
        i = pl.program_id(1)

        @pl.when(i == 0)
        def _():
            _attn_pad(k_ref, v_ref, kpad, vpad)

        r0 = pl.multiple_of(i * AQ, AQ)
        q = q_ref[...] * (DH ** -0.5)
        kwin = kpad[pl.ds(r0, AW), :]
        vwin = vpad[pl.ds(r0, AW), :]
        lane_hi = lax.broadcasted_iota(jnp.int32, (AQ, 128), 1) // DH
        mask_row = jnp.where(lax.broadcasted_iota(jnp.int32, (1, AW), 1) >= PAD - r0, 0.0, -1e30)
        outs = []
        for hh in range(2):
            e, inv = _attn_exp(jnp.where(lane_hi == hh, q, 0.0), kwin, b_ref[hh], mask_row)
            outs.append(_dot(e, vwin, NN) * inv)
        o_ref[...] = jnp.where(lane_hi == 0, outs[0], outs[1])

    return pl.pallas_call(
        kern, name="attn_fwd", grid=(HEADS // 2, nq),
        in_specs=[pl.BlockSpec((AQ, 128), lambda j, i: (i, j)),
                  pl.BlockSpec((L, 128), lambda j, i: (0, 8 + j)),
                  pl.BlockSpec((L, 128), lambda j, i: (0, 16 + j)),
                  pl.BlockSpec((2, AQ, AW), lambda j, i: (j, 0, 0))],
        out_specs=pl.BlockSpec((AQ, 128), lambda j, i: (i, j)),
        out_shape=SDS((L, D), f32),
        scratch_shapes=[pltpu.VMEM((L + PAD, 128), bf16), pltpu.VMEM((L + PAD, 128), bf16)],
        compiler_params=_cp(("arbitrary", "arbitrary")),
    )(qkvg, qkvg, qkvg, bias)


def _attn_bwd(qkvg, do, bias):
    L = qkvg.shape[0]
    nq = L // AQ

    def kern(q_ref, k_ref, v_ref, b_ref, do_ref, dq_ref, dk_ref, dv_ref, db_ref, kpad, vpad, dkp, dvp):
        i = pl.program_id(1)

        @pl.when(i == 0)
        def _():
            _attn_pad(k_ref, v_ref, kpad, vpad)
            dkp[...] = jnp.zeros_like(dkp)
            dvp[...] = jnp.zeros_like(dvp)
            db_ref[...] = jnp.zeros_like(db_ref)

        r0 = pl.multiple_of(i * AQ, AQ)
        q = q_ref[...] * (DH ** -0.5)
        dov = do_ref[...]
        kwin = kpad[pl.ds(r0, AW), :]
        vwin = vpad[pl.ds(r0, AW), :]
        lane_hi = lax.broadcasted_iota(jnp.int32, (AQ, 128), 1) // DH
        mask_row = jnp.where(lax.broadcasted_iota(jnp.int32, (1, AW), 1) >= PAD - r0, 0.0, -1e30)
        dqs = []
        dk = jnp.zeros((AW, 128), f32)
        dv = jnp.zeros((AW, 128), f32)
        for hh in range(2):
            qm = jnp.where(lane_hi == hh, q, 0.0).astype(bf16)
            dom = jnp.where(lane_hi == hh, dov, 0.0).astype(bf16)
            dp = _dot(dom, vwin, NT)
            e, inv = _attn_exp(qm, kwin, b_ref[hh], mask_row)
            p = e * inv
            ds = p * (dp - jnp.sum(p * dp, axis=-1, keepdims=True))
            db_ref[hh] += ds
            dsb = ds.astype(bf16)
            dqs.append(_dot(dsb, kwin, NN) * (DH ** -0.5))
            dk = dk + _dot(dsb, qm, TN)
            dv = dv + _dot(p, dom, TN)
        dq_ref[...] = jnp.where(lane_hi == 0, dqs[0], dqs[1])
        dkp[pl.ds(r0, AW), :] += dk
        dvp[pl.ds(r0, AW), :] += dv

        @pl.when(i == nq - 1)
        def _():
            dk_ref[...] = dkp[PAD:, :]
            dv_ref[...] = dvp[PAD:, :]

    return pl.pallas_call(
        kern, name="attn_bwd", grid=(HEADS // 2, nq),
        in_specs=[pl.BlockSpec((AQ, 128), lambda j, i: (i, j)),
                  pl.BlockSpec((L, 128), lambda j, i: (0, 8 + j)),
                  pl.BlockSpec((L, 128), lambda j, i: (0, 16 + j)),
                  pl.BlockSpec((2, AQ, AW), lambda j, i: (j, 0, 0)),
                  pl.BlockSpec((AQ, 128), lambda j, i: (i, j))],
        out_specs=[pl.BlockSpec((AQ, 128), lambda j, i: (i, j)),
                   pl.BlockSpec((L, 128), lambda j, i: (0, j)),
                   pl.BlockSpec((L, 128), lambda j, i: (0, j)),
                   pl.BlockSpec((2, AQ, AW), lambda j, i: (j, 0, 0))],
        out_shape=[SDS((L, D), f32), SDS((L, D), f32), SDS((L, D), f32), SDS((HEADS, AQ, AW), f32)],
        scratch_shapes=[pltpu.VMEM((L + PAD, 128), bf16), pltpu.VMEM((L + PAD, 128), bf16),
                        pltpu.VMEM((L + PAD, 128), f32), pltpu.VMEM((L + PAD, 128), f32)],
        compiler_params=_cp(("arbitrary", "arbitrary")),
    )(qkvg, qkvg, qkvg, bias, do)


_SKEW = AQ + AW - 1


def _rel_index():
    d = (AW - 1) - np.arange(_SKEW)
    return np.clip(d, -MAX_REL, MAX_REL) + MAX_REL


def _band_mask():
    qc = np.arange(AQ)[:, None] // CHUNK + LEFT
    kc = np.arange(AW)[None, :] // CHUNK
    return jnp.asarray(np.where((kc <= qc) & (kc >= qc - LEFT), 0.0, -1e30), f32)


def _bias_matrix(rel_bias):
    tv = jnp.take(rel_bias, jnp.asarray(_rel_index()), axis=1)
    flat = jnp.tile(jnp.pad(tv, ((0, 0), (0, 1))), (1, AQ))
    top = flat[:, AQ - 1:AQ - 1 + AQ * _SKEW].reshape(HEADS, AQ, _SKEW)[:, :, :AW]
    return top + _band_mask()


def _bias_grad(dbias, onehot):
    flat = jnp.pad(dbias, ((0, 0), (0, 0), (0, _SKEW - AW))).reshape(HEADS, AQ * _SKEW)
    z = jnp.pad(flat, ((0, 0), (AQ - 1, AQ * 1024 - AQ * _SKEW - (AQ - 1)))).reshape(HEADS, AQ, 1024)

    def kern(z_ref, oh_ref, out_ref):
        diag = jnp.sum(z_ref[...], axis=1)
        out_ref[...] = jnp.dot(diag, oh_ref[...], preferred_element_type=f32, precision=lax.Precision.HIGHEST)

    return pl.pallas_call(kern, name="bias_grad", out_shape=SDS((HEADS, 2 * MAX_REL + 1), f32),
                          compiler_params=_cp(()))(z, onehot)


def _rel_onehot():
    oh = np.zeros((1024, 2 * MAX_REL + 1), np.float32)
    oh[np.arange(_SKEW), _rel_index()] = 1.0
    return jnp.asarray(oh)


def _gated_out_fwd(name, x, o, gate_src, gate_col, w, tm):
    def body(rows, fulls):
        xv, ov, gv = rows
        return [xv + _dotw(ov * jax.nn.silu(gv), fulls[0][...])], []

    return _tiled(name, body, x.shape[0], tm, [(x, D, 0), (o, D, 0), (gate_src, D, gate_col)], [w], [(D, f32)])[0]


def _gated_out_bwd(name, dxo, o, gate_src, gate_col, w, tm):
    def body(rows, fulls):
        dxv, ov, gv = rows
        wv = fulls[0][...]
        u, vjp = jax.vjp(lambda ov, gv: ov * jax.nn.silu(gv), ov, gv)
        do, dg = vjp(_dotw_t(dxv, wv))
        return [do, dg], [_wgrad(u, dxv, wv)]

    return _tiled(name, body, dxo.shape[0], tm, [(dxo, D, 0), (o, D, 0), (gate_src, D, gate_col)], [w],
                  [(D, f32), (D, f32)], [(w.shape, f32)])


def _loss_head(x, g, target, tm):
    def body(rows, fulls):
        xv, tv = rows

        def f(xv, gv):
            e = jnp.square(_rms(xv, gv) - tv)
            return 0.5 * jnp.sum(jnp.mean(e, axis=-1, keepdims=True), axis=0, keepdims=True)

        loss, vjp = jax.vjp(f, xv, fulls[0][...])
        dx, dg = vjp(jnp.ones((1, 1), f32))
        return [dx], [jnp.broadcast_to(loss, (1, 128)), dg]

    return _tiled("loss_head", body, x.shape[0], tm, [(x, D, 0), (target, D, 0)], [g], [(D, f32)],
                  [((1, 128), f32), ((1, D), f32)])


def _local_step(x, mem, target, p, late_shards, late_views, pack_early, pack_mid):
    TM, TMF = 256, 512
    row = lambda v: v.reshape(1, -1)

    bt = lambda b: jnp.transpose(b, (0, 2, 1)).reshape(NG * GS, NS)
    disc_in = (p["ev_s5_lambda_re"], p["ev_s5_lambda_im"], p["ev_s5_log_dt"].reshape(NG, 1),
               bt(p["ev_s5_b_re"]), bt(p["ev_s5_b_im"]))
    ab_re, ab_im, bbt_re, bbt_im = _s5_disc_fwd(*disc_in)
    bmat = (_to_blockdiag(bbt_re).astype(bf16), _to_blockdiag(bbt_im).astype(bf16))
    cmat = (_to_blockdiag(p["ev_s5_c_re"].reshape(NG * GS, NS)).astype(bf16),
            _to_blockdiag(p["ev_s5_c_im"].reshape(NG * GS, NS)).astype(bf16))
    a_re, a_im = ab_re.reshape(1, NST), ab_im.reshape(1, NST)
    d_skip = row(p["ev_s5_d"])
    w32 = jnp.pad(p["ev_conv_w"], ((0, 1), (0, 0)))
    conv_b, ln_g, ln_b, glu_b = row(p["ev_conv_b"]), row(p["ev_conv_ln_g"]), row(p["ev_conv_ln_b"]), row(p["ev_s5_glu_b"])
    g_mem, g_ev, g_od, g_fin = row(p["mem_norm_g"]), row(p["ev_norm_g"]), row(p["od_norm_g"]), row(p["final_norm_g"])
    g_xa = [row(p["xa_norm_g"][l]) for l in range(2)]
    bias = _bias_matrix(p["od_rel_bias"])

    wkv = p["xa_w_kv"]
    kv0, kv1 = _kv_fwd(mem, g_mem, wkv[0], wkv[1])
    z = _norm_mm_fwd("ev_in_fwd", x, g_ev, p["ev_w_in"], TMF)
    y_s5, hb_re, hb_im, *late = _s5_fwd(z, bmat, cmat, a_re, a_im, d_skip, comm=((), late_shards))
    p = late_views(p, late)
    wqg, wo = p["xa_w_qg"], p["xa_w_o"]
    c = _conv_fwd(z, w32, conv_b, 512)
    x1 = _ev_out_fwd(x, y_s5, c, z, glu_b, ln_g, ln_b, p["ev_s5_glu_w"], p["ev_w_out"], TMF)
    x2 = _xa_fwd("xa0_fwd", x1, g_xa[0], kv0, wqg[0], wo[0], TMF)
    qkvg = _norm_mm_fwd("od_in_fwd", x2, g_od, p["od_w_in"], TMF)
    o = _attn_fwd(qkvg, bias)
    x3 = _gated_out_fwd("od_out_fwd", x2, o, qkvg, 3, p["od_w_out"], TMF)
    x4 = _xa_fwd("xa1_fwd", x3, g_xa[1], kv1, wqg[1], wo[1], TMF)

    G = {}
    dx4, loss_row, G["final_norm_g"] = _loss_head(x4, g_fin, target, TM)
    dx3, dg_xa1, dkv1, dwqg1, dwo1 = _xa_bwd("xa1_bwd", x3, dx4, g_xa[1], kv1, wqg[1], wo[1], TM)
    do, dgate, G["od_w_out"] = _gated_out_bwd("od_out_bwd", dx3, o, qkvg, 3, p["od_w_out"], TM)
    dq, dk, dv, dbias = _attn_bwd(qkvg, do, bias)
    G["od_rel_bias"] = _bias_grad(dbias, _rel_onehot())
    dx2, G["od_norm_g"], G["od_w_in"] = _norm_mm_bwd(
        "od_in_bwd", x2, g_od, p["od_w_in"], [(dq, D, 0), (dk, D, 0), (dv, D, 0), (dgate, D, 0)], dx3, TM)
    G["xa_w_qg"], G["xa_w_o"] = [None, dwqg1], [None, dwo1]
    dx1, dg_xa0, dkv0, G["xa_w_qg"][0], G["xa_w_o"][0], *recv_early = _xa_bwd(
        "xa0_bwd", x1, dx2, g_xa[0], kv0, wqg[0], wo[0], TM, comm=(pack_early(G), ()))
    G["xa_norm_g"] = jnp.concatenate([dg_xa0, dg_xa1], axis=0)
    G["mem_norm_g"], dwkv0, dwkv1 = _kv_bwd(mem, g_mem, wkv[0], wkv[1], dkv0, dkv1)
    G["xa_w_kv"] = [dwkv0, dwkv1]
    (dy_s5, dc, dga, dgb, G["ev_s5_glu_b"], G["ev_conv_ln_g"], G["ev_conv_ln_b"], G["ev_s5_glu_w"],
     G["ev_w_out"]) = _ev_out_bwd(dx1, y_s5, c, z, glu_b, ln_g, ln_b, p["ev_s5_glu_w"], p["ev_w_out"], TM)
    dval, dglu, dw32, G["ev_conv_b"] = _conv_bwd(z, dc, w32, 512)
    G["ev_conv_w"] = dw32[:CK]
    du, dbre, dbim, dcre, dcim, da_re, da_im, G["ev_s5_d"], *recv_mid = _s5_bwd(
        z, dy_s5, hb_re, hb_im, bmat, cmat, a_re, a_im, d_skip, comm=(pack_mid(G), ()))
    G["ev_s5_c_re"], G["ev_s5_c_im"] = _from_blockdiag(dcre), _from_blockdiag(dcim)
    G["ev_s5_lambda_re"], G["ev_s5_lambda_im"], G["ev_s5_log_dt"], G["ev_s5_b_re"], G["ev_s5_b_im"] = _s5_disc_bwd(
        disc_in, (da_re.reshape(NG, NS), da_im.reshape(NG, NS), _from_blockdiag(dbre), _from_blockdiag(dbim)))
    dx, G["ev_norm_g"], G["ev_w_in"] = _norm_mm_bwd(
        "ev_in_bwd", x, g_ev, p["ev_w_in"], [(du, SW, 0), (dga, SW, 0), (dval, CW, 0), (dglu, CW, 0), (dgb, CW, 0)], dx1, TM)
    return loss_row[0, 0], dx, G, recv_early, recv_mid


def _all_gather(payloads):
    n = len(payloads)

    def body(*refs):
        ins, outs = refs[:n], refs[n:2 * n]
        send_sems, recv_sems, local_sems = refs[2 * n:]
        x, y, c = _mesh_pos()
        me, sibling = (x, y, c), (x, y, 1 - c)
        chips = [(1 - x, y), (x, 1 - y), (1 - x, 1 - y)]

        def copy(p, k, block, to, src=None):
            return pltpu.make_async_remote_copy(
                src_ref=_slot(outs[p], block) if src is None else src, dst_ref=_slot(outs[p], block),
                send_sem=send_sems.at[7 * p + k], recv_sem=recv_sems.at[7 * p + k],
                device_id=to, device_id_type=pl.DeviceIdType.MESH)

        mine = [pltpu.make_async_copy(ins[p], _slot(outs[p], me), local_sems.at[p]) for p in range(n)]
        for cp in mine:
            cp.start()
        first = []
        for p in range(n):
            first.append(copy(p, 0, me, sibling, src=ins[p]))
            first += [copy(p, 1 + j, me, (*chip, c), src=ins[p]) for j, chip in enumerate(chips)]
        for cp in first:
            cp.start()
        passed = []
        for j, chip in enumerate(chips):
            for p in range(n):
                copy(p, 1 + j, (*chip, c), me).wait_recv()
                passed.append(copy(p, 4 + j, (*chip, c), sibling))
                passed[-1].start()
        for p in range(n):
            copy(p, 0, sibling, me).wait_recv()
            for j, chip in enumerate(chips):
                copy(p, 4 + j, (*chip, 1 - c), me).wait_recv()
        for cp in first + passed:
            cp.wait_send()
        for cp in mine:
            cp.wait()

    return pl.pallas_call(
        body, name="all_gather", in_specs=[_ANY] * n, out_specs=[_ANY] * n,
        out_shape=[SDS((NDEV, *a.shape), a.dtype) for a in payloads],
        scratch_shapes=[pltpu.SemaphoreType.DMA((7 * n,)), pltpu.SemaphoreType.DMA((7 * n,)), pltpu.SemaphoreType.DMA((n,))],
    )(*payloads)


def _exchange(slotted, whole):
    n = len(slotted) + len(whole)
    shapes, sems = _comm_shapes((slotted, whole))

    def body(*refs):
        copies = _comm_copies(len(slotted), refs[:n], refs[n:2 * n], *refs[2 * n:])
        for cp in copies:
            cp.start()
        for cp in copies:
            cp.wait()

    return pl.pallas_call(body, name="grad_exchange", in_specs=[_ANY] * n, out_specs=[_ANY] * n, out_shape=shapes,
                          scratch_shapes=sems)(*slotted, *whole)


def _adamw_math(g, w, m, v):
    m2 = ADAM_B1 * m + (1.0 - ADAM_B1) * g
    v2 = ADAM_B2 * v + (1.0 - ADAM_B2) * jnp.square(g)
    m_hat = m2 / (1.0 - ADAM_B1 ** ADAM_STEP)
    v_hat = v2 / (1.0 - ADAM_B2 ** ADAM_STEP)
    return -ADAM_LR * (m_hat / (jnp.sqrt(v_hat) + ADAM_EPS) + ADAM_WD * w), m2, v2


def _sum_slots(r_ref, rows=None):
    acc = r_ref[0].astype(f32) if rows is None else r_ref[0, rows[0]:rows[1], :].astype(f32)
    for k in range(1, NDEV):
        acc = acc + (r_ref[k] if rows is None else r_ref[k, rows[0]:rows[1], :]).astype(f32)
    return acc


def _sum_adamw(name, sources, w, m, v, tr):
    rows, C = w.shape
    starts = [int(t) for t in np.cumsum([0] + [r // tr for (_, _, r) in sources])]

    def kern(*refs):
        r_refs = refs[:len(sources)]
        w_ref, m_ref, v_ref, g_ref, d_ref, m2_ref, v2_ref = refs[len(sources):]
        g = _sum_slots(r_refs[0])
        for s in range(1, len(sources)):
            g = jnp.where(pl.program_id(0) >= starts[s], _sum_slots(r_refs[s]), g)
        g_ref[...] = g
        d_ref[...], m2_ref[...], v2_ref[...] = _adamw_math(g, w_ref[...], m_ref[...], v_ref[...])

    def src_spec(s):
        _, off, r = sources[s]
        return pl.BlockSpec((NDEV, tr, C), lambda i: (0, off // tr + jnp.clip(i - starts[s], 0, r // tr - 1), 0))

    blk = pl.BlockSpec((tr, C), lambda i: (i, 0))
    return pl.pallas_call(
        kern, name=name, grid=(rows // tr,),
        in_specs=[src_spec(s) for s in range(len(sources))] + [blk, blk, blk],
        out_specs=[blk] * 4, out_shape=[SDS((rows, C), f32)] * 4, compiler_params=_cp(),
    )(*[r for (r, _, _) in sources], w, m, v)


def _small_adamw(recvs, table, wmv):
    rnames = list(recvs)
    nr, nw = len(rnames), len(table)

    def kern(*refs):
        rr = dict(zip(rnames, refs[:nr]))
        ins, outs = refs[nr:nr + 3 * nw], refs[nr + 3 * nw:]
        for i, (_, shape, src, r0) in enumerate(table):
            g = _sum_slots(rr[src], (r0, r0 + shape[0]))[:, :shape[1]]
            outs[4 * i][...] = g
            res = _adamw_math(g, ins[3 * i][...], ins[3 * i + 1][...], ins[3 * i + 2][...])
            for o, val in zip(outs[4 * i + 1:4 * i + 4], res):
                o[...] = val

    flat = [t for (n, _, _, _) in table for t in wmv[n]]
    res = pl.pallas_call(
        kern, name="small_adamw", out_shape=[SDS(shape, f32) for (_, shape, _, _) in table for _ in range(4)],
        compiler_params=_cp(()),
    )(*[recvs[n] for n in rnames], *flat)
    return {n: tuple(res[4 * i:4 * i + 4]) for i, (n, _, _, _) in enumerate(table)}


WEIGHTS = ["mem_norm_g", "ev_norm_g", "ev_w_in", "ev_s5_lambda_re", "ev_s5_lambda_im", "ev_s5_log_dt", "ev_s5_b_re",
           "ev_s5_b_im", "ev_s5_c_re", "ev_s5_c_im", "ev_s5_d", "ev_s5_glu_w", "ev_s5_glu_b", "ev_conv_w", "ev_conv_b",
           "ev_conv_ln_g", "ev_conv_ln_b", "ev_w_out", "od_norm_g", "od_w_in", "od_rel_bias", "od_w_out", "xa_norm_g",
           "xa_w_qg", "xa_w_kv", "xa_w_o", "final_norm_g"]
FIRST = [("a1024", 1024, [("ev_w_out", None, "rows", 128), ("xa_w_o", 0, "rows", 128)]),
         ("a512", 512, [("ev_s5_glu_w", None, "rows", 64)]),
         ("a256", 256, [("xa_w_qg", 0, "cols", 1024), ("xa_w_kv", 0, "cols", 1024), ("xa_w_kv", 1, "cols", 1024)]),
         ("a320", 320, [("ev_w_in", None, "cols", 1024)])]
LATE = [("b1024", 1024, [("od_w_out", None, "rows", 128), ("xa_w_o", 1, "rows", 128)]),
        ("b512", 512, [("od_w_in", None, "cols", 1024)]),
        ("b256", 256, [("xa_w_qg", 1, "cols", 1024)])]
MATMUL_WEIGHTS = ["ev_w_out", "xa_w_o", "ev_s5_glu_w", "xa_w_qg", "xa_w_kv", "ev_w_in", "od_w_out", "od_w_in"]
ADAM_ROWS = {1024: 128, 512: 64, 256: 256, 320: 256}
SMALL = [("mem_norm_g", (1, D), "s1024", 0), ("ev_norm_g", (1, D), "s1024", 1), ("xa_norm_g", (2, D), "s1024", 2),
         ("final_norm_g", (1, D), "s1024", 4),
         ("ev_s5_d", (1, SW), "s512", 0), ("ev_s5_glu_b", (1, SW), "s512", 1), ("ev_conv_b", (1, CW), "s512", 2),
         ("ev_conv_ln_g", (1, CW), "s512", 3), ("ev_conv_ln_b", (1, CW), "s512", 4),
         ("ev_s5_c_re", (NG * GS, NS), "s64", 0), ("ev_s5_c_im", (NG * GS, NS), "s64", 512),
         ("ev_s5_b_re", (NG * GS, NS), "s64", 1024), ("ev_s5_b_im", (NG * GS, NS), "s64", 1536),
         ("ev_s5_lambda_re", (NG, NS), "s64", 2048), ("ev_s5_lambda_im", (NG, NS), "s64", 2080),
         ("ev_s5_log_dt", (1, NG), "s64", 2112),
         ("od_rel_bias", (HEADS, 2 * MAX_REL + 1), "s257", 0),
         ("ev_conv_w", (CK, CW // NDEV), "cw", 0), ("od_norm_g", (1, D // NDEV), "on", 0)]
S5_B = ("ev_s5_b_re", "ev_s5_b_im")


def _pad_rows(a, rows):
    return jnp.pad(a, ((0, rows - a.shape[0]), (0, 0)))


def _shard_payloads(a, groups):
    pick = lambda n, l, minor: (a[n] if l is None else a[n][l]).reshape(-1, minor)
    return [jnp.concatenate([pick(n, l, minor).astype(bf16) for n, l, _, _ in members], axis=0) for _, minor, members in groups]


def _weight_views(p, groups, gathered):
    p = dict(p)
    for (_, _, members), buf in zip(groups, gathered):
        off = 0
        for n, l, kind, rows in members:
            view = _W(buf, rows, off // rows, kind)
            if l is None:
                p[n] = view
            else:
                p[n] = list(p.get(n, [None, None]))
                p[n][l] = view
            off += rows
    return p


def _slotted_grads(G, groups):
    slot3 = lambda g, rows: g.reshape(NDEV, rows, g.shape[-1]) if g.ndim == 2 else g
    return [jnp.concatenate([slot3(G[n] if l is None else G[n][l], rows).astype(bf16) for n, l, _, rows in members], axis=1)
            for _, _, members in groups]


def _bt(b):
    return jnp.transpose(b.reshape(NG, NS, GS), (0, 2, 1)).reshape(NG * GS, NS)


def _unbt(b, shape):
    return jnp.transpose(b.reshape(NG, GS, NS), (0, 2, 1)).reshape(shape)


def kernel(*args):
    names = ["x", "mem", *WEIGHTS, "loss_target", *["m_" + n for n in WEIGHTS], *["v_" + n for n in WEIGHTS]]
    a = dict(zip(names, args, strict=True))

    other = jnp.concatenate([_pad_rows(a["ev_conv_w"][0], 32).reshape(16, 128), _pad_rows(a["od_norm_g"], 8)], axis=0)
    *gathered, gother = _all_gather([*_shard_payloads(a, FIRST), other])
    p = {n: (a[n] if n in ("xa_norm_g", "mem_norm_g", "final_norm_g") else a[n][0]) for n in WEIGHTS if n not in MATMUL_WEIGHTS}
    p = _weight_views(p, FIRST, gathered)
    p["ev_w_in"] = jnp.transpose(gathered[3], (1, 0, 2)).reshape(D, EVEN_IN)
    p["ev_conv_w"] = jnp.transpose(gother[:, :16].reshape(NDEV, 32, CW // NDEV)[:, :CK], (1, 0, 2)).reshape(CK, CW)
    p["od_norm_g"] = gother[:, 16].reshape(D)

    loss_part, dx, G, recv_early, recv_mid = _local_step(
        a["x"][0], a["mem"][0], a["loss_target"][0], p, _shard_payloads(a, LATE),
        lambda p, gathered: _weight_views(p, LATE, gathered),
        lambda G: _slotted_grads(G, LATE), lambda G: _slotted_grads(G, FIRST[:3]))

    G["ev_w_in"] = jnp.transpose(G["ev_w_in"].reshape(D, NDEV, EVEN_IN // NDEV), (1, 0, 2))
    cw = jnp.transpose(G["ev_conv_w"].reshape(CK, NDEV, CW // NDEV), (1, 0, 2))
    slotted = [*_slotted_grads(G, FIRST[3:]), jnp.pad(cw, ((0, 0), (0, 1), (0, 0))), G["od_norm_g"].reshape(NDEV, 1, D // NDEV)]
    whole = {"s1024": _pad_rows(jnp.concatenate([G[n] for n in ("mem_norm_g", "ev_norm_g", "xa_norm_g", "final_norm_g")]), 8),
             "s512": _pad_rows(jnp.concatenate([G[n] for n in ("ev_s5_d", "ev_s5_glu_b", "ev_conv_b", "ev_conv_ln_g",
                                                                 "ev_conv_ln_b")]), 8),
             "s64": _pad_rows(jnp.concatenate([G[n] for n in ("ev_s5_c_re", "ev_s5_c_im", "ev_s5_b_re", "ev_s5_b_im",
                                                                "ev_s5_lambda_re", "ev_s5_lambda_im")]
                                              + [jnp.pad(G["ev_s5_log_dt"].reshape(1, NG), ((0, 0), (0, NS - NG)))]), 2120),
             "s257": G["od_rel_bias"]}
    recv_last = _exchange(slotted, list(whole.values()))
    recvs = dict(zip([g[0] for g in LATE], recv_early))
    recvs.update(zip([g[0] for g in FIRST[:3]], recv_mid))
    recvs.update(zip(["a320", "cw", "on", *whole], recv_last))

    sources = {}
    for buf, minor, members in FIRST + LATE:
        off = 0
        for n, l, _, rows in members:
            sources.setdefault(n, []).append((l or 0, recvs[buf], off, rows, minor))
            off += rows
    res = {}
    for n, src in sources.items():
        src = sorted(src, key=lambda t: t[0])
        minor = src[0][4]
        w, m, v = (a[k + n].reshape(-1, minor) for k in ("", "m_", "v_"))
        res[n] = _sum_adamw("adamw_" + n, [(r, off, rows) for _, r, off, rows, _ in src], w, m, v, ADAM_ROWS[minor])
    two_d = lambda n, t: _bt(t[0]) if n in S5_B else t.reshape(dict((s[0], s[1]) for s in SMALL)[n])
    res.update(_small_adamw({k: recvs[k] for k in ("s1024", "s512", "s64", "s257", "cw", "on")}, SMALL,
                            {n: tuple(two_d(n, a[k + n]) for k in ("", "m_", "v_")) for n, _, _, _ in SMALL}))
    shaped = lambda n, t: _unbt(t, a[n].shape) if n in S5_B else t.reshape(a[n].shape)
    loss = lax.psum(loss_part, AXES)
    return (loss, dx[None], *[shaped(n, res[n][k]) for k in range(4) for n in WEIGHTS])
```

```python
import functools
from typing import NamedTuple

import jax
import jax.numpy as jnp
import numpy as np
from jax import lax
from jax.experimental import pallas as pl
from jax.experimental.pallas import tpu as pltpu

f32, bf16 = jnp.float32, jnp.bfloat16
SDS = jax.ShapeDtypeStruct

D = 1024
SW = 512
NG, GS, NS = 32, 16, 64
NST = NG * NS
CW = 512
CK = 31
EVEN_IN = 2 * SW + 3 * CW
HEADS, DH = 16, 64
CHUNK, LEFT = 64, 8
PAD = LEFT * CHUNK
MAX_REL = 128
MEM_LEN = 256
XH, XD = 4, 256
EPS = 1e-6
NDEV = 8
AXES = ("x", "y", "c")

ADAM_LR, ADAM_B1, ADAM_B2, ADAM_EPS, ADAM_WD, ADAM_STEP = 0.001, 0.9, 0.999, 1e-08, 0.01, 10

VMEM_LIMIT = 56 << 20
S5_T = 512
S5_CH = 512
AQ = 256
AW = AQ + PAD

NN = ((1,), (0,))
NT = ((1,), (1,))
TN = ((0,), (0,))


def _dot(a, b, dims):
    return lax.dot_general(a.astype(bf16), b.astype(bf16), (dims, ((), ())), preferred_element_type=f32)


def _dotw(a, w):
    if w.ndim == 2:
        return _dot(a, w, NN)
    return jnp.concatenate([_dot(a, w[j], NN) for j in range(w.shape[0])], axis=1)


def _dotw_t(g, w):
    if w.ndim == 2:
        return _dot(g, w, NT)
    n = w.shape[2]
    out = _dot(g[:, :n], w[0], NT)
    for j in range(1, w.shape[0]):
        out = out + _dot(g[:, j * n:(j + 1) * n], w[j], NT)
    return out


def _wgrad(a, g, w):
    if w.ndim == 2:
        return _dot(a, g, TN)
    n = w.shape[2]
    return jnp.stack([_dot(a, g[:, j * n:(j + 1) * n], TN) for j in range(w.shape[0])])


@jax.custom_vjp
def _mm(a, w):
    return _dotw(a, w)


def _mm_f(a, w):
    return _dotw(a, w), w


def _mm_b(w, g):
    return _dotw_t(g, w), None


_mm.defvjp(_mm_f, _mm_b)


@jax.custom_vjp
def _mm_nt(a, b):
    return _dot(a, b, NT)


def _mm_nt_f(a, b):
    return _dot(a, b, NT), (a, b)


def _mm_nt_b(res, g):
    a, b = res
    return _dot(g, b, NN), _dot(g, a, TN)


_mm_nt.defvjp(_mm_nt_f, _mm_nt_b)


@jax.custom_vjp
def _mm_nn(a, b):
    return _dot(a, b, NN)


def _mm_nn_f(a, b):
    return _dot(a, b, NN), (a, b)


def _mm_nn_b(res, g):
    a, b = res
    return _dot(g, b, NT), _dot(a, g, TN)


_mm_nn.defvjp(_mm_nn_f, _mm_nn_b)


def _plain_mm(a, w):
    return _dotw(a, w)


def _plain_nt(a, b):
    return _dot(a, b, NT)


def _rms(x, g):
    return x * lax.rsqrt(jnp.mean(x * x, axis=-1, keepdims=True) + EPS) * g


def _cp(sem=("arbitrary",)):
    return pltpu.CompilerParams(dimension_semantics=sem, vmem_limit_bytes=VMEM_LIMIT)


def _row(tm, w, col=0):
    return pl.BlockSpec((tm, w), lambda i: (i, col))


def _full(shape):
    nd = len(shape)
    return pl.BlockSpec(tuple(shape), lambda i: (0,) * nd, pipeline_mode=pl.Buffered(1))


class _W(NamedTuple):
    arr: jax.Array
    rows: int
    idx: int
    kind: str

    @property
    def shape(self):
        c = self.arr.shape[2]
        return (NDEV * self.rows, c) if self.kind == "rows" else (NDEV, self.rows, c)


class _WRef:
    def __init__(self, ref, kind):
        self.ref, self.kind = ref, kind

    def __getitem__(self, _):
        v = self.ref[...]
        return v.reshape(v.shape[0] * v.shape[1], v.shape[2]) if self.kind == "rows" else v


def _wspec(w):
    if isinstance(w, _W):
        return pl.BlockSpec((NDEV, w.rows, w.arr.shape[2]), lambda i: (0, w.idx, 0), pipeline_mode=pl.Buffered(1))
    return _full(w.shape)


_ANY = pl.BlockSpec(memory_space=pl.ANY)
_NO_COMM = ((), ())


def _mesh_pos():
    return tuple(lax.axis_index(a) for a in AXES)


def _slot(ref, dev):
    return ref.at[4 * dev[0] + 2 * dev[1] + dev[2]]


def _comm_shapes(comm):
    slotted, whole = comm
    n = len(slotted) + len(whole)
    shapes = [SDS(a.shape, a.dtype) for a in slotted] + [SDS((NDEV, *a.shape), a.dtype) for a in whole]
    sems = [pltpu.SemaphoreType.DMA((7 * n,)), pltpu.SemaphoreType.DMA((7 * n,)), pltpu.SemaphoreType.DMA((n,))] if n else []
    return shapes, sems


def _comm_copies(ns, ins, outs, send_sems, recv_sems, local_sems):
    n = len(ins)
    x, y, c = _mesh_pos()
    me = (x, y, c)
    src = lambda p, dev: _slot(ins[p], dev) if p < ns else ins[p]
    copies = [pltpu.make_async_copy(src(p, me), _slot(outs[p], me), local_sems.at[p]) for p in range(n)]
    for k in range(1, NDEV):
        flip = lambda v, bit: 1 - v if (k >> bit) & 1 else v
        peer = (flip(x, 2), flip(y, 1), flip(c, 0))
        for p in range(n):
            copies.append(pltpu.make_async_remote_copy(
                src_ref=src(p, peer), dst_ref=_slot(outs[p], me), send_sem=send_sems.at[7 * p + k - 1],
                recv_sem=recv_sems.at[7 * p + k - 1], device_id=peer, device_id_type=pl.DeviceIdType.MESH))
    return copies


def _comm_hook(comm, cin, cout, sems, step, last):
    if not cin:
        return lambda: None

    @pl.when(step == 0)
    def _():
        for cp in _comm_copies(len(comm[0]), cin, cout, *sems):
            cp.start()

    def finish():
        @pl.when(step == last)
        def _():
            for cp in _comm_copies(len(comm[0]), cin, cout, *sems):
                cp.wait()

    return finish


def _tiled(name, body, L, tm, row_ins, full_ins, row_outs, acc_outs=(), comm=_NO_COMM):
    nr, nf, no, na = len(row_ins), len(full_ins), len(row_outs), len(acc_outs)
    cins = [*comm[0], *comm[1]]
    nc = len(cins)
    cshapes, csems = _comm_shapes(comm)

    def kern(*refs):
        rin = refs[:nr]
        fin = [_WRef(r, w.kind) if isinstance(w, _W) else r for r, w in zip(refs[nr:nr + nf], full_ins)]
        cin = refs[nr + nf:nr + nf + nc]
        outs = refs[nr + nf + nc:]
        rout, aout, cout, sems = outs[:no], outs[no:no + na], outs[no + na:no + na + nc], outs[no + na + nc:]
        finish = _comm_hook(comm, cin, cout, sems, pl.program_id(0), L // tm - 1)
        routs, accs = body([r[...] for r in rin], fin)
        for r, v in zip(rout, routs):
            r[...] = v.astype(r.dtype)
        if aout:
            @pl.when(pl.program_id(0) == 0)
            def _():
                for a in aout:
                    a[...] = jnp.zeros(a.shape, a.dtype)

            for a, v in zip(aout, accs):
                a[...] += v.astype(a.dtype)
        finish()

    return pl.pallas_call(
        kern, name=name, grid=(L // tm,),
        in_specs=[_row(tm, w, c) for (_, w, c) in row_ins] + [_wspec(a) for a in full_ins] + [_ANY] * nc,
        out_specs=[_row(tm, w) for (w, _) in row_outs] + [_full(s) for (s, _) in acc_outs] + [_ANY] * nc,
        out_shape=[SDS((L, w), dt) for (w, dt) in row_outs] + [SDS(tuple(s), dt) for (s, dt) in acc_outs] + cshapes,
        scratch_shapes=csems, compiler_params=_cp(),
    )(*[a for (a, _, _) in row_ins], *[a.arr if isinstance(a, _W) else a for a in full_ins], *cins)


def _norm_mm_fwd(name, x, g, w, tm):
    def body(rows, fulls):
        g_ref, w_ref = fulls
        return [_dotw(_rms(rows[0], g_ref[...]), w_ref[...])], []

    n_out = w.shape[1] if len(w.shape) == 2 else w.shape[0] * w.shape[2]
    return _tiled(name, body, x.shape[0], tm, [(x, D, 0)], [g, w], [(n_out, f32)])[0]


def _norm_mm_bwd(name, x, g, w, dz_parts, dres, tm):
    n = len(dz_parts)

    def body(rows, fulls):
        g_ref, w_ref = fulls
        dz = rows[1] if n == 1 else jnp.concatenate(rows[1:1 + n], axis=1)
        wv = w_ref[...]
        h, vjp = jax.vjp(_rms, rows[0], g_ref[...])
        dx, dg = vjp(_dotw_t(dz, wv))
        return [dx + rows[1 + n]], [dg, _wgrad(h, dz, wv)]

    return _tiled(name, body, x.shape[0], tm, [(x, D, 0), *dz_parts, (dres, D, 0)], [g, w],
                  [(D, f32)], [((1, D), f32), (tuple(w.shape), f32)])


S5_S = S5_T // 8


def _perm_load(refs):
    return jnp.concatenate([jnp.concatenate([r[pl.ds(i, 8, stride=S5_S), :] for i in range(S5_S)], axis=0) for r in refs],
                           axis=1)


def _perm_store(refs, v):
    for c, r in enumerate(refs):
        for i in range(S5_S):
            r[pl.ds(i, 8, stride=S5_S), :] = v[i * 8:(i + 1) * 8, c * 128:(c + 1) * 128]


def _cmul(pr, pi, qr, qi):
    return pr * qr - pi * qi, pr * qi + pi * qr


def _scan_tables(tab_ref, pw_re, pw_im, ar, ai, reverse):
    def powers(br, bi):
        pw = [(br, bi)]
        for _ in range(7):
            pw.append(_cmul(*pw[-1], br, bi))
        return pw

    row = lax.broadcasted_iota(jnp.int32, (8, NST), 0)

    def rows(pw, order):
        vr = jnp.zeros((8, NST), f32)
        vi = jnp.zeros((8, NST), f32)
        for t in range(8):
            vr = jnp.where(row == t, pw[order(t)][0], vr)
            vi = jnp.where(row == t, pw[order(t)][1], vi)
        return vr, vi

    pw = powers(ar, ai)
    base_r, base_i = rows(pw, lambda t: t)
    sr, si = jnp.ones((1, NST), f32), jnp.zeros((1, NST), f32)
    for m in range(S5_S // 8):
        pw_re[m * 8:(m + 1) * 8, :], pw_im[m * 8:(m + 1) * 8, :] = _cmul(base_r, base_i, sr, si)
        sr, si = _cmul(sr, si, *pw[7])
    big = powers(sr, si)
    for k, s in enumerate((1, 2, 4)):
        keep = (row <= 7 - s) if reverse else (row >= s)
        tab_ref[2 * k] = jnp.where(keep, big[s - 1][0], 0.0)
        tab_ref[2 * k + 1] = jnp.where(keep, big[s - 1][1], 0.0)
    tab_ref[6], tab_ref[7] = rows(big, (lambda t: 7 - t) if reverse else (lambda t: t))


def _scan(re_ref, im_ref, row0, a_re, a_im, tab_ref, pw_re, pw_im, car_re, car_im, reverse, fold=None):
    ng = S5_S // 8
    row = lax.broadcasted_iota(jnp.int32, (8, S5_CH), 0)
    accs = []
    for cc in range(NST // S5_CH):
        cols = slice(cc * S5_CH, (cc + 1) * S5_CH)
        ar = jnp.broadcast_to(a_re[:, cols], (8, S5_CH))
        ai = jnp.broadcast_to(a_im[:, cols], (8, S5_CH))

        def local(i, h, cols=cols, ar=ar, ai=ai):
            r0 = pl.multiple_of(row0 + ((S5_S - 1 - i) if reverse else i) * 8, 8)
            hr = ar * h[0] - ai * h[1] + re_ref[pl.ds(r0, 8), cols]
            hi = ar * h[1] + ai * h[0] + im_ref[pl.ds(r0, 8), cols]
            re_ref[pl.ds(r0, 8), cols] = hr
            im_ref[pl.ds(r0, 8), cols] = hi
            return hr, hi

        xr, xi = lax.fori_loop(0, S5_S, local, (jnp.zeros((8, S5_CH), f32), jnp.zeros((8, S5_CH), f32)))

        old_r, old_i = car_re[:, cols], car_im[:, cols]
        for k, s in enumerate((1, 2, 4)):
            yr = pltpu.roll(xr, (8 - s) if reverse else s, 0)
            yi = pltpu.roll(xi, (8 - s) if reverse else s, 0)
            dr, di = _cmul(tab_ref[2 * k, :, cols], tab_ref[2 * k + 1, :, cols], yr, yi)
            xr, xi = xr + dr, xi + di
        dr, di = _cmul(tab_ref[6, :, cols], tab_ref[7, :, cols], old_r, old_i)
        xr, xi = xr + dr, xi + di
        b = 0 if reverse else 7
        car_re[:, cols] = jnp.broadcast_to(xr[b:b + 1, :], (8, S5_CH))
        car_im[:, cols] = jnp.broadcast_to(xi[b:b + 1, :], (8, S5_CH))
        edge = 7 if reverse else 0
        cr = jnp.where(row == edge, old_r, pltpu.roll(xr, 7 if reverse else 1, 0))
        ci = jnp.where(row == edge, old_i, pltpu.roll(xi, 7 if reverse else 1, 0))

        def fix(g, acc, cols=cols, cr=cr, ci=ci):
            gg = (ng - 1 - g) if reverse else g
            pg = pl.multiple_of(g * 8, 8)
            ptr, pti = pw_re[pl.ds(pg, 8), cols], pw_im[pl.ds(pg, 8), cols]
            for u in range(8):
                uu = (7 - u) if reverse else u
                i = gg * 8 + uu
                r0 = pl.multiple_of(row0 + i * 8, 8)
                pr = jnp.broadcast_to(ptr[u:u + 1, :], (8, S5_CH))
                pi = jnp.broadcast_to(pti[u:u + 1, :], (8, S5_CH))
                dr, di = _cmul(pr, pi, cr, ci)
                hr = re_ref[pl.ds(r0, 8), cols] + dr
                hi = im_ref[pl.ds(r0, 8), cols] + di
                re_ref[pl.ds(r0, 8), cols] = hr
                im_ref[pl.ds(r0, 8), cols] = hi
                if fold is not None:
                    acc = fold(cols, i, hr, hi, acc)
            return acc

        acc0 = (jnp.zeros((8, S5_CH), f32), jnp.zeros((8, S5_CH), f32)) if fold is not None else 0
        accs.append(lax.fori_loop(0, ng, fix, acc0))
    return accs


def _s5_fwd(z, bmat, cmat, a_re, a_im, d_skip, comm=_NO_COMM):
    L = z.shape[0]
    nb = L // S5_T
    cins = [*comm[0], *comm[1]]
    nc = len(cins)
    cshapes, csems = _comm_shapes(comm)

    def kern(*refs):
        u_refs, (bre, bim, cre, cim, ar_ref, ai_ref, d_ref), rest = refs[:4], refs[4:11], refs[11:]
        cin, y_refs, (hbr_ref, hbi_ref), cout = rest[:nc], rest[nc:nc + 4], rest[nc + 4:nc + 6], rest[nc + 6:2 * nc + 6]
        h_re, h_im, tab, pw_re, pw_im, car_re, car_im, *sems = rest[2 * nc + 6:]
        finish = _comm_hook(comm, cin, cout, sems, pl.program_id(0), nb - 1)

        @pl.when(pl.program_id(0) == 0)
        def _():
            _scan_tables(tab, pw_re, pw_im, ar_ref[...], ai_ref[...], False)
            car_re[...] = jnp.zeros_like(car_re)
            car_im[...] = jnp.zeros_like(car_im)

        hbr_ref[...] = car_re[...]
        hbi_ref[...] = car_im[...]
        u = _perm_load(u_refs)
        for hf in range(2):
            uh = u[:, hf * 256:(hf + 1) * 256]
            h_re[:, hf * 1024:(hf + 1) * 1024] = _dot(uh, bre[hf], NN)
            h_im[:, hf * 1024:(hf + 1) * 1024] = _dot(uh, bim[hf], NN)
        _scan(h_re, h_im, 0, ar_ref[...], ai_ref[...], tab, pw_re, pw_im, car_re, car_im, False)
        ys = []
        for hf in range(2):
            cs = slice(hf * 1024, (hf + 1) * 1024)
            ys.append(_dot(h_re[:, cs], cre[hf], NT) - _dot(h_im[:, cs], cim[hf], NT))
        _perm_store(y_refs, jnp.concatenate(ys, axis=1) + d_ref[...] * u)
        finish()

    fulls = [*bmat, *cmat, a_re, a_im, d_skip]
    return pl.pallas_call(
        kern, name="s5_fwd", grid=(nb,),
        in_specs=[_row(S5_T, 128, c) for c in range(4)] + [_full(a.shape) for a in fulls] + [_ANY] * nc,
        out_specs=[_row(S5_T, 128)] * 4 + [_row(8, NST), _row(8, NST)] + [_ANY] * nc,
        out_shape=[SDS((L, 128), f32)] * 4 + [SDS((nb * 8, NST), f32), SDS((nb * 8, NST), f32)] + cshapes,
        scratch_shapes=[pltpu.VMEM((S5_T, NST), f32), pltpu.VMEM((S5_T, NST), f32), pltpu.VMEM((8, 8, NST), f32),
                        pltpu.VMEM((S5_S, NST), f32), pltpu.VMEM((S5_S, NST), f32),
                        pltpu.VMEM((8, NST), f32), pltpu.VMEM((8, NST), f32)] + csems,
        compiler_params=_cp(),
    )(z, z, z, z, *fulls, *cins)


def _s5_bwd(z, dy, hb_re, hb_im, bmat, cmat, a_re, a_im, d_skip, comm=_NO_COMM):
    L = z.shape[0]
    nb = L // S5_T
    T = S5_T
    cins = [*comm[0], *comm[1]]
    nc = len(cins)
    cshapes, csems = _comm_shapes(comm)

    def kern(*refs):
        u_refs, dy_refs, (hbr_ref, hbi_ref, bre, bim, cre, cim, ar_ref, ai_ref, d_ref), rest = \
            refs[:4], refs[4:8], refs[8:17], refs[17:]
        cin, du_refs, (dbre, dbim, dcre, dcim, dar, dai, dd), cout = \
            rest[:nc], rest[nc:nc + 4], rest[nc + 4:nc + 11], rest[nc + 11:2 * nc + 11]
        h_re, h_im, g_re, g_im, tabf, pwf_re, pwf_im, tabr, pwr_re, pwr_im, car_re, car_im, gcar_re, gcar_im, *sems = \
            rest[2 * nc + 11:]
        finish = _comm_hook(comm, cin, cout, sems, pl.program_id(0), nb - 1)

        @pl.when(pl.program_id(0) == 0)
        def _():
            _scan_tables(tabf, pwf_re, pwf_im, ar_ref[...], ai_ref[...], False)
            _scan_tables(tabr, pwr_re, pwr_im, ar_ref[...], -ai_ref[...], True)
            gcar_re[...] = jnp.zeros_like(gcar_re)
            gcar_im[...] = jnp.zeros_like(gcar_im)
            for r in (dbre, dbim, dcre, dcim, dar, dai, dd):
                r[...] = jnp.zeros_like(r)

        u = _perm_load(u_refs)
        dyv = _perm_load(dy_refs)
        car_re[...] = hbr_ref[...]
        car_im[...] = hbi_ref[...]
        for hf in range(2):
            cs = slice(hf * 1024, (hf + 1) * 1024)
            uh = u[:, hf * 256:(hf + 1) * 256]
            dyh = dyv[:, hf * 256:(hf + 1) * 256]
            h_re[8:, cs] = _dot(uh, bre[hf], NN)
            h_im[8:, cs] = _dot(uh, bim[hf], NN)
            g_re[:, cs] = _dot(dyh, cre[hf], NN)
            g_im[:, cs] = -_dot(dyh, cim[hf], NN)
        _scan(h_re, h_im, 8, ar_ref[...], ai_ref[...], tabf, pwf_re, pwf_im, car_re, car_im, False)
        row = lax.broadcasted_iota(jnp.int32, (8, NST), 0)
        h_re[0:8, :] = jnp.where(row == 0, hbr_ref[...], pltpu.roll(h_re[T:T + 8, :], 1, 0))
        h_im[0:8, :] = jnp.where(row == 0, hbi_ref[...], pltpu.roll(h_im[T:T + 8, :], 1, 0))

        def fold_da(cols, i, gr, gi, acc):
            r0 = pl.multiple_of(i * 8, 8)
            hr, hi = h_re[pl.ds(r0, 8), cols], h_im[pl.ds(r0, 8), cols]
            return acc[0] + gr * hr + gi * hi, acc[1] + gi * hr - gr * hi

        accs = _scan(g_re, g_im, 0, ar_ref[...], -ai_ref[...], tabr, pwr_re, pwr_im, gcar_re, gcar_im, True, fold=fold_da)
        for cc, (acc_r, acc_i) in enumerate(accs):
            cols = slice(cc * S5_CH, (cc + 1) * S5_CH)
            dar[:, cols] += jnp.sum(acc_r, axis=0, keepdims=True)
            dai[:, cols] += jnp.sum(acc_i, axis=0, keepdims=True)

        dus = []
        for hf in range(2):
            cs = slice(hf * 1024, (hf + 1) * 1024)
            uh = u[:, hf * 256:(hf + 1) * 256]
            dyh = dyv[:, hf * 256:(hf + 1) * 256]
            gr = g_re[:, cs]
            gi = g_im[:, cs]
            dus.append(_dot(gr, bre[hf], NT) + _dot(gi, bim[hf], NT))
            dbre[hf] += _dot(uh, gr, TN)
            dbim[hf] += _dot(uh, gi, TN)
            dcre[hf] += _dot(dyh, h_re[8:, cs], TN)
            dcim[hf] -= _dot(dyh, h_im[8:, cs], TN)
        _perm_store(du_refs, jnp.concatenate(dus, axis=1) + d_ref[...] * dyv)
        dd[...] += jnp.sum(dyv * u, axis=0, keepdims=True)
        finish()

    rev = lambda c: pl.BlockSpec((T, 128), lambda k: (nb - 1 - k, c))
    rev8 = pl.BlockSpec((8, NST), lambda k: (nb - 1 - k, 0))
    fulls = [*bmat, *cmat, a_re, a_im, d_skip]
    mat = ((2, 256, 1024), f32)
    accs = [mat, mat, mat, mat, ((1, NST), f32), ((1, NST), f32), ((1, SW), f32)]
    tables = [pltpu.VMEM((8, 8, NST), f32), pltpu.VMEM((S5_S, NST), f32), pltpu.VMEM((S5_S, NST), f32)]
    return pl.pallas_call(
        kern, name="s5_bwd", grid=(nb,),
        in_specs=[rev(c) for c in range(4)] * 2 + [rev8, rev8] + [_full(a.shape) for a in fulls] + [_ANY] * nc,
        out_specs=[rev(0)] * 4 + [_full(s) for (s, _) in accs] + [_ANY] * nc,
        out_shape=[SDS((L, 128), f32)] * 4 + [SDS(s, dt) for (s, dt) in accs] + cshapes,
        scratch_shapes=[pltpu.VMEM((T + 8, NST), f32), pltpu.VMEM((T + 8, NST), f32),
                        pltpu.VMEM((T, NST), f32), pltpu.VMEM((T, NST), f32), *tables, *tables,
                        pltpu.VMEM((8, NST), f32), pltpu.VMEM((8, NST), f32),
                        pltpu.VMEM((8, NST), f32), pltpu.VMEM((8, NST), f32)] + csems,
        compiler_params=_cp(),
    )(z, z, z, z, dy, dy, dy, dy, hb_re, hb_im, *fulls, *cins)


def _s5_disc(lam_re, lam_im, log_dt, bt_re, bt_im):
    dt = jnp.exp(log_dt)
    mag = jnp.exp(lam_re * dt)
    ab_re = mag * jnp.cos(lam_im * dt)
    ab_im = mag * jnp.sin(lam_im * dt)
    den = lam_re * lam_re + lam_im * lam_im
    nr = ab_re - 1.0
    coef_re = (nr * lam_re + ab_im * lam_im) / den
    coef_im = (ab_im * lam_re - nr * lam_im) / den
    rep = lambda c: jnp.broadcast_to(c[:, None, :], (NG, GS, NS)).reshape(NG * GS, NS)
    cr, ci = rep(coef_re), rep(coef_im)
    return ab_re, ab_im, cr * bt_re - ci * bt_im, cr * bt_im + ci * bt_re


def _s5_disc_fwd(*params):
    def kern(*refs):
        outs = _s5_disc(*[r[...] for r in refs[:5]])
        for r, v in zip(refs[5:], outs):
            r[...] = v

    shapes = [SDS((NG, NS), f32), SDS((NG, NS), f32), SDS((NG * GS, NS), f32), SDS((NG * GS, NS), f32)]
    return pl.pallas_call(kern, name="s5_disc_fwd", out_shape=shapes)(*params)


def _s5_disc_bwd(params, cts):
    def kern(*refs):
        _, vjp = jax.vjp(_s5_disc, *[r[...] for r in refs[:5]])
        for r, v in zip(refs[9:], vjp(tuple(r[...] for r in refs[5:9]))):
            r[...] = v

    return pl.pallas_call(kern, name="s5_disc_bwd", out_shape=[SDS(p.shape, f32) for p in params])(*params, *cts)


def _diag_mask():
    r = np.arange(256)[:, None] // GS
    c = np.arange(1024)[None, :] // NS
    return jnp.asarray(r == c, f32)


def _to_blockdiag(m):
    return jnp.tile(m.reshape(2, 256, NS), (1, 1, 16)) * _diag_mask()


def _from_blockdiag(dm):
    return (dm * _diag_mask()).reshape(2, 256, 16, NS).sum(axis=2).reshape(NG * GS, NS)


def _conv_shifts(win, sh, tm):
    win[tm + 32:, :] = jnp.zeros((8, CW), f32)
    for s in range(8):
        sh[s] = win[s:s + tm + 32, :]


def _conv_taps(offset, w_ref, sh, r, accs):
    for k in range(CK):
        s, q = offset(k) % 8, offset(k) // 8
        wk = w_ref[k:k + 1, :]
        accs = [acc + wk * sh[s, pl.ds(pl.multiple_of(r + 8 * (a + q), 8), 8), :] for a, acc in enumerate(accs)]
    return accs


def _conv_fwd(z, w32, b, tm):
    L = z.shape[0]
    hb = tm // 32

    def kern(val, glu, hval, hglu, w_ref, b_ref, c_ref, win, sh):
        i = pl.program_id(0)
        win[0:32, :] = hval[...] * jax.nn.sigmoid(hglu[...]) * (i > 0).astype(f32)
        win[32:tm + 32, :] = val[...] * jax.nn.sigmoid(glu[...])
        _conv_shifts(win, sh, tm)

        def strip(j, carry):
            r = j * 16
            accs = _conv_taps(lambda k: k + 2, w_ref, sh, r, [jnp.zeros((8, CW), f32) + b_ref[...]] * 2)
            c_ref[pl.ds(pl.multiple_of(r, 16), 16), :] = jnp.concatenate(accs, axis=0)
            return carry

        lax.fori_loop(0, tm // 16, strip, 0)

    halo = lambda col: pl.BlockSpec((32, CW), lambda i: (jnp.maximum(i * hb - 1, 0), col))
    return pl.pallas_call(
        kern, name="conv_fwd", grid=(L // tm,),
        in_specs=[_row(tm, CW, 2), _row(tm, CW, 3), halo(2), halo(3), _full(w32.shape), _full(b.shape)],
        out_specs=_row(tm, CW), out_shape=SDS((L, CW), f32),
        scratch_shapes=[pltpu.VMEM((tm + 40, CW), f32), pltpu.VMEM((8, tm + 32, CW), f32)], compiler_params=_cp(),
    )(z, z, z, z, w32, b)


def _conv_bwd(z, dc, w32, tm):
    L = z.shape[0]
    hb = tm // 32
    nt = L // tm

    def kern(val, glu, hval, hglu, dc_ref, hdc, w_ref, dval_ref, dglu_ref, dw_ref, db_ref, win, sh, dwin, dsh):
        i = pl.program_id(0)

        @pl.when(i == 0)
        def _():
            dw_ref[...] = jnp.zeros_like(dw_ref)
            db_ref[...] = jnp.zeros_like(db_ref)

        win[0:32, :] = hval[...] * jax.nn.sigmoid(hglu[...]) * (i > 0).astype(f32)
        win[32:tm + 32, :] = val[...] * jax.nn.sigmoid(glu[...])
        _conv_shifts(win, sh, tm)
        dwin[0:tm, :] = dc_ref[...]
        dwin[tm:tm + 32, :] = hdc[...] * (i < nt - 1).astype(f32)
        _conv_shifts(dwin, dsh, tm)

        def strip(j, carry):
            r = j * 16
            rows = pl.ds(pl.multiple_of(r, 16), 16)
            dv = jnp.concatenate(_conv_taps(lambda k: 30 - k, w_ref, dsh, r, [jnp.zeros((8, CW), f32)] * 2), axis=0)
            sg = jax.nn.sigmoid(glu[rows, :])
            dval_ref[rows, :] = dv * sg
            dglu_ref[rows, :] = dv * val[rows, :] * sg * (1.0 - sg)
            return carry

        lax.fori_loop(0, tm // 16, strip, 0)

        for k0 in range(0, CK, 2):
            taps = [k for k in (k0, k0 + 1) if k < CK]

            def rows64(j, accs, taps=taps):
                for u in range(8):
                    r = pl.multiple_of(j * 64 + u * 8, 8)
                    dcv = dc_ref[pl.ds(r, 8), :]
                    accs = [acc + dcv * sh[(k + 2) % 8, pl.ds(pl.multiple_of(r + 8 * ((k + 2) // 8), 8), 8), :]
                            for k, acc in zip(taps, accs)]
                return accs

            accs = lax.fori_loop(0, tm // 64, rows64, [jnp.zeros((8, CW), f32)] * len(taps))
            for k, acc in zip(taps, accs):
                dw_ref[k:k + 1, :] += jnp.sum(acc, axis=0, keepdims=True)
        db_ref[...] += jnp.sum(dc_ref[...], axis=0, keepdims=True)

    halo = lambda col: pl.BlockSpec((32, CW), lambda i: (jnp.maximum(i * hb - 1, 0), col))
    nxt = pl.BlockSpec((32, CW), lambda i: (jnp.minimum((i + 1) * hb, L // 32 - 1), 0))
    return pl.pallas_call(
        kern, name="conv_bwd", grid=(nt,),
        in_specs=[_row(tm, CW, 2), _row(tm, CW, 3), halo(2), halo(3), _row(tm, CW), nxt, _full(w32.shape)],
        out_specs=[_row(tm, CW), _row(tm, CW), _full((32, CW)), _full((1, CW))],
        out_shape=[SDS((L, CW), f32), SDS((L, CW), f32), SDS((32, CW), f32), SDS((1, CW), f32)],
        scratch_shapes=[pltpu.VMEM((tm + 40, CW), f32), pltpu.VMEM((8, tm + 32, CW), f32),
                        pltpu.VMEM((tm + 40, CW), f32), pltpu.VMEM((8, tm + 32, CW), f32)], compiler_params=_cp(),
    )(z, z, z, z, dc, dc, w32)


def _ev_out_f(y, c, ga, gb, tap, glu_b, ln_g, ln_b, glu_w, w_out, mm):
    z1 = jax.nn.gelu(y)
    ya = z1 * jax.nn.sigmoid(mm(z1, glu_w) + glu_b + tap) * jax.nn.silu(ga)
    mu = jnp.mean(c, axis=-1, keepdims=True)
    var = jnp.mean(jnp.square(c - mu), axis=-1, keepdims=True)
    cn = (c - mu) * lax.rsqrt(var + EPS) * ln_g + ln_b
    cat = jnp.concatenate([ya, jax.nn.silu(cn) * jax.nn.silu(gb)], axis=1)
    return mm(cat, w_out), (z1, cat)


def _ev_out_fwd(x, y, c, z, glu_b, ln_g, ln_b, glu_w, w_out, tm):
    def body(rows, fulls):
        xv, cv, ga, gb = rows[:4]
        yv = jnp.concatenate(rows[4:], axis=1)
        gb_ref, lg_ref, lb_ref, gw_ref, wo_ref = fulls
        out, _ = _ev_out_f(yv, cv, ga, gb, 0.0, gb_ref[...], lg_ref[...], lb_ref[...], gw_ref[...], wo_ref[...], _plain_mm)
        return [xv + out], []

    return _tiled("ev_out_fwd", body, x.shape[0], tm, [(x, D, 0), (c, CW, 0), (z, SW, 1), (z, CW, 4), *[(t, 128, 0) for t in y]],
                  [glu_b, ln_g, ln_b, glu_w, w_out], [(D, f32)])[0]


def _ev_out_bwd(dx1, y, c, z, glu_b, ln_g, ln_b, glu_w, w_out, tm):
    def body(rows, fulls):
        dxv, cv, ga, gb = rows[:4]
        yv = jnp.concatenate(rows[4:], axis=1)
        gb_ref, lg_ref, lb_ref, gw_ref, wo_ref = fulls
        gw, wo = gw_ref[...], wo_ref[...]
        f = lambda yv, cv, ga, gb, tap, b, lg, lb: _ev_out_f(yv, cv, ga, gb, tap, b, lg, lb, gw, wo, _mm)
        _, vjp, (z1, cat) = jax.vjp(f, yv, cv, ga, gb, jnp.zeros((tm, SW), f32), gb_ref[...], lg_ref[...], lb_ref[...],
                                    has_aux=True)
        dy, dc, dga, dgb, dtap, db, dlg, dlb = vjp(dxv)
        return [dy, dc, dga, dgb], [db, dlg, dlb, _wgrad(z1, dtap, gw), _wgrad(cat, dxv, wo)]

    return _tiled("ev_out_bwd", body, dx1.shape[0], tm, [(dx1, D, 0), (c, CW, 0), (z, SW, 1), (z, CW, 4), *[(t, 128, 0) for t in y]],
                  [glu_b, ln_g, ln_b, glu_w, w_out], [(SW, f32), (CW, f32), (SW, f32), (CW, f32)],
                  [((1, SW), f32), ((1, CW), f32), ((1, CW), f32), (glu_w.shape, f32), (w_out.shape, f32)])


def _kv_f(mem, g, tap0, tap1, w0, w1, mm):
    mn = _rms(mem, g)
    return (mm(mn, w0) + tap0, mm(mn, w1) + tap1), mn


def _kv_fwd(mem, g, w0, w1):
    def body(rows, fulls):
        g_ref, w0_ref, w1_ref = fulls
        (kv0, kv1), _ = _kv_f(rows[0], g_ref[...], 0.0, 0.0, w0_ref[...], w1_ref[...], _plain_mm)
        return [kv0, kv1], []

    return _tiled("kv_fwd", body, MEM_LEN, MEM_LEN, [(mem, D, 0)], [g, w0, w1], [(2 * D, f32), (2 * D, f32)])


def _kv_bwd(mem, g, w0, w1, dkv0, dkv1):
    def body(rows, fulls):
        g_ref, w0_ref, w1_ref = fulls
        w0v, w1v = w0_ref[...], w1_ref[...]
        tap = jnp.zeros((MEM_LEN, 2 * D), f32)
        _, vjp, mn = jax.vjp(lambda g_, t0, t1: _kv_f(rows[0], g_, t0, t1, w0v, w1v, _mm), g_ref[...], tap, tap, has_aux=True)
        dg, d0, d1 = vjp((rows[1], rows[2]))
        return [], [dg, _wgrad(mn, d0, w0v), _wgrad(mn, d1, w1v)]

    return _tiled("kv_bwd", body, MEM_LEN, MEM_LEN, [(mem, D, 0), (dkv0, 2 * D, 0), (dkv1, 2 * D, 0)], [g, w0, w1], [],
                  [((1, D), f32), (w0.shape, f32), (w1.shape, f32)])


def _xa_f(x, g, kv, tap, w_qg, w_o, mm, nt, nn):
    h = _rms(x, g)
    qg = mm(h, w_qg) + tap
    outs = []
    for hd in range(XH):
        q = qg[:, hd * XD:(hd + 1) * XD]
        k = kv[:, hd * XD:(hd + 1) * XD]
        v = kv[:, D + hd * XD:D + (hd + 1) * XD]
        s = nt(q, k) * (XD ** -0.5)
        e = jnp.exp(s - jnp.max(s, axis=-1, keepdims=True))
        outs.append(nn(e / jnp.sum(e, axis=-1, keepdims=True), v))
    u = jnp.concatenate(outs, axis=1) * jax.nn.silu(qg[:, D:])
    return mm(u, w_o), (h, u)


def _xa_fwd(name, x, g, kv, w_qg, w_o, tm):
    def body(rows, fulls):
        g_ref, kv_ref, wq_ref, wo_ref = fulls
        out, _ = _xa_f(rows[0], g_ref[...], kv_ref[...], 0.0, wq_ref[...], wo_ref[...], _plain_mm, _plain_nt, _plain_mm)
        return [rows[0] + out], []

    return _tiled(name, body, x.shape[0], tm, [(x, D, 0)], [g, kv, w_qg, w_o], [(D, f32)])[0]


def _xa_bwd(name, x, dxo, g, kv, w_qg, w_o, tm, comm=_NO_COMM):
    def body(rows, fulls):
        xv, dxv = rows
        g_ref, kv_ref, wq_ref, wo_ref = fulls
        wq, wo = wq_ref[...], wo_ref[...]
        f = lambda xv, gv, kvv, tap: _xa_f(xv, gv, kvv, tap, wq, wo, _mm, _mm_nt, _mm_nn)
        _, vjp, (h, u) = jax.vjp(f, xv, g_ref[...], kv_ref[...], jnp.zeros((tm, 2 * D), f32), has_aux=True)
        dx, dg, dkv, dtap = vjp(dxv)
        return [dx + dxv], [dg, dkv, _wgrad(h, dtap, wq), _wgrad(u, dxv, wo)]

    return _tiled(name, body, x.shape[0], tm, [(x, D, 0), (dxo, D, 0)], [g, kv, w_qg, w_o], [(D, f32)],
                  [((1, D), f32), ((MEM_LEN, 2 * D), f32), (w_qg.shape, f32), (w_o.shape, f32)], comm=comm)


def _attn_pad(k_ref, v_ref, kpad, vpad):
    kpad[0:PAD, :] = jnp.zeros((PAD, 128), bf16)
    vpad[0:PAD, :] = jnp.zeros((PAD, 128), bf16)
    kpad[PAD:, :] = k_ref[...].astype(bf16)
    vpad[PAD:, :] = v_ref[...].astype(bf16)


def _attn_exp(qm, kwin, bm, mask_row):
    x = _dot(qm, kwin, NT) + bm + mask_row
    e = jnp.exp(x - jnp.max(x, axis=-1, keepdims=True))
    return e, 1.0 / jnp.sum(e, axis=-1, keepdims=True)


def _attn_fwd(qkvg, bias):
    L = qkvg.shape[0]
    nq = L // AQ

    def kern(q_ref, k_ref, v_ref, b_ref, o_ref, kpad, vpad):
        i = pl.program_id(1)

        @pl.when(i == 0)
        def _():
            _attn_pad(k_ref, v_ref, kpad, vpad)

        r0 = pl.multiple_of(i * AQ, AQ)
        q = q_ref[...] * (DH ** -0.5)
        kwin = kpad[pl.ds(r0, AW), :]
        vwin = vpad[pl.ds(r0, AW), :]
        lane_hi = lax.broadcasted_iota(jnp.int32, (AQ, 128), 1) // DH
        mask_row = jnp.where(lax.broadcasted_iota(jnp.int32, (1, AW), 1) >= PAD - r0, 0.0, -1e30)
        outs = []
        for hh in range(2):
            e, inv = _attn_exp(jnp.where(lane_hi == hh, q, 0.0), kwin, b_ref[hh], mask_row)
            outs.append(_dot(e, vwin, NN) * inv)
        o_ref[...] = jnp.where(lane_hi == 0, outs[0], outs[1])

    return pl.pallas_call(
        kern, name="attn_fwd", grid=(HEADS // 2, nq),
        in_specs=[pl.BlockSpec((AQ, 128), lambda j, i: (i, j)),
                  pl.BlockSpec((L, 128), lambda j, i: (0, 8 + j)),
                  pl.BlockSpec((L, 128), lambda j, i: (0, 16 + j)),
                  pl.BlockSpec((2, AQ, AW), lambda j, i: (j, 0, 0))],
        out_specs=pl.BlockSpec((AQ, 128), lambda j, i: (i, j)),
        out_shape=SDS((L, D), f32),
        scratch_shapes=[pltpu.VMEM((L + PAD, 128), bf16), pltpu.VMEM((L + PAD, 128), bf16)],
        compiler_params=_cp(("arbitrary", "arbitrary")),
    )(qkvg, qkvg, qkvg, bias)


def _attn_bwd(qkvg, do, bias):
    L = qkvg.shape[0]
    nq = L // AQ

    def kern(q_ref, k_ref, v_ref, b_ref, do_ref, dq_ref, dk_ref, dv_ref, db_ref, kpad, vpad, dkp, dvp):
        i = pl.program_id(1)

        @pl.when(i == 0)
        def _():
            _attn_pad(k_ref, v_ref, kpad, vpad)
            dkp[...] = jnp.zeros_like(dkp)
            dvp[...] = jnp.zeros_like(dvp)
            db_ref[...] = jnp.zeros_like(db_ref)

        r0 = pl.multiple_of(i * AQ, AQ)
        q = q_ref[...] * (DH ** -0.5)
        dov = do_ref[...]
        kwin = kpad[pl.ds(r0, AW), :]
        vwin = vpad[pl.ds(r0, AW), :]
        lane_hi = lax.broadcasted_iota(jnp.int32, (AQ, 128), 1) // DH
        mask_row = jnp.where(lax.broadcasted_iota(jnp.int32, (1, AW), 1) >= PAD - r0, 0.0, -1e30)
        dqs = []
        dk = jnp.zeros((AW, 128), f32)
        dv = jnp.zeros((AW, 128), f32)
        for hh in range(2):
            qm = jnp.where(lane_hi == hh, q, 0.0).astype(bf16)
            dom = jnp.where(lane_hi == hh, dov, 0.0).astype(bf16)
            dp = _dot(dom, vwin, NT)
            e, inv = _attn_exp(qm, kwin, b_ref[hh], mask_row)
            p = e * inv
            ds = p * (dp - jnp.sum(p * dp, axis=-1, keepdims=True))
            db_ref[hh] += ds
            dsb = ds.astype(bf16)
            dqs.append(_dot(dsb, kwin, NN) * (DH ** -0.5))
            dk = dk + _dot(dsb, qm, TN)
            dv = dv + _dot(p, dom, TN)
        dq_ref[...] = jnp.where(lane_hi == 0, dqs[0], dqs[1])
        dkp[pl.ds(r0, AW), :] += dk
        dvp[pl.ds(r0, AW), :] += dv

        @pl.when(i == nq - 1)
        def _():
            dk_ref[...] = dkp[PAD:, :]
            dv_ref[...] = dvp[PAD:, :]

    return pl.pallas_call(
        kern, name="attn_bwd", grid=(HEADS // 2, nq),
        in_specs=[pl.BlockSpec((AQ, 128), lambda j, i: (i, j)),
                  pl.BlockSpec((L, 128), lambda j, i: (0, 8 + j)),
                  pl.BlockSpec((L, 128), lambda j, i: (0, 16 + j)),
                  pl.BlockSpec((2, AQ, AW), lambda j, i: (j, 0, 0)),
                  pl.BlockSpec((AQ, 128), lambda j, i: (i, j))],
        out_specs=[pl.BlockSpec((AQ, 128), lambda j, i: (i, j)),
                   pl.BlockSpec((L, 128), lambda j, i: (0, j)),
                   pl.BlockSpec((L, 128), lambda j, i: (0, j)),
                   pl.BlockSpec((2, AQ, AW), lambda j, i: (j, 0, 0))],
        out_shape=[SDS((L, D), f32), SDS((L, D), f32), SDS((L, D), f32), SDS((HEADS, AQ, AW), f32)],
        scratch_shapes=[pltpu.VMEM((L + PAD, 128), bf16), pltpu.VMEM((L + PAD, 128), bf16),
                        pltpu.VMEM((L + PAD, 128), f32), pltpu.VMEM((L + PAD, 128), f32)],
        compiler_params=_cp(("arbitrary", "arbitrary")),
    )(qkvg, qkvg, qkvg, bias, do)


_SKEW = AQ + AW - 1


def _rel_index():
    d = (AW - 1) - np.arange(_SKEW)
    return np.clip(d, -MAX_REL, MAX_REL) + MAX_REL


def _band_mask():
    qc = np.arange(AQ)[:, None] // CHUNK + LEFT
    kc = np.arange(AW)[None, :] // CHUNK
    return jnp.asarray(np.where((kc <= qc) & (kc >= qc - LEFT), 0.0, -1e30), f32)


def _bias_matrix(rel_bias):
    tv = jnp.take(rel_bias, jnp.asarray(_rel_index()), axis=1)
    flat = jnp.tile(jnp.pad(tv, ((0, 0), (0, 1))), (1, AQ))
    top = flat[:, AQ - 1:AQ - 1 + AQ * _SKEW].reshape(HEADS, AQ, _SKEW)[:, :, :AW]
    return top + _band_mask()


def _bias_grad(dbias, onehot):
    flat = jnp.pad(dbias, ((0, 0), (0, 0), (0, _SKEW - AW))).reshape(HEADS, AQ * _SKEW)
    z = jnp.pad(flat, ((0, 0), (AQ - 1, AQ * 1024 - AQ * _SKEW - (AQ - 1)))).reshape(HEADS, AQ, 1024)

    def kern(z_ref, oh_ref, out_ref):
        diag = jnp.sum(z_ref[...], axis=1)
        out_ref[...] = jnp.dot(diag, oh_ref[...], preferred_element_type=f32, precision=lax.Precision.HIGHEST)

    return pl.pallas_call(kern, name="bias_grad", out_shape=SDS((HEADS, 2 * MAX_REL + 1), f32),
                          compiler_params=_cp(()))(z, onehot)


def _rel_onehot():
    oh = np.zeros((1024, 2 * MAX_REL + 1), np.float32)
    oh[np.arange(_SKEW), _rel_index()] = 1.0
    return jnp.asarray(oh)


def _gated_out_fwd(name, x, o, gate_src, gate_col, w, tm):
    def body(rows, fulls):
        xv, ov, gv = rows
        return [xv + _dotw(ov * jax.nn.silu(gv), fulls[0][...])], []

    return _tiled(name, body, x.shape[0], tm, [(x, D, 0), (o, D, 0), (gate_src, D, gate_col)], [w], [(D, f32)])[0]


def _gated_out_bwd(name, dxo, o, gate_src, gate_col, w, tm):
    def body(rows, fulls):
        dxv, ov, gv = rows
        wv = fulls[0][...]
        u, vjp = jax.vjp(lambda ov, gv: ov * jax.nn.silu(gv), ov, gv)
        do, dg = vjp(_dotw_t(dxv, wv))
        return [do, dg], [_wgrad(u, dxv, wv)]

    return _tiled(name, body, dxo.shape[0], tm, [(dxo, D, 0), (o, D, 0), (gate_src, D, gate_col)], [w],
                  [(D, f32), (D, f32)], [(w.shape, f32)])


def _loss_head(x, g, target, tm):
    def body(rows, fulls):
        xv, tv = rows

        def f(xv, gv):
            e = jnp.square(_rms(xv, gv) - tv)
            return 0.5 * jnp.sum(jnp.mean(e, axis=-1, keepdims=True), axis=0, keepdims=True)

        loss, vjp = jax.vjp(f, xv, fulls[0][...])
        dx, dg = vjp(jnp.ones((1, 1), f32))
        return [dx], [jnp.broadcast_to(loss, (1, 128)), dg]

    return _tiled("loss_head", body, x.shape[0], tm, [(x, D, 0), (target, D, 0)], [g], [(D, f32)],
                  [((1, 128), f32), ((1, D), f32)])


def _local_step(x, mem, target, p, late_shards, late_views, pack_early, pack_mid):
    TM, TMF = 256, 512
    row = lambda v: v.reshape(1, -1)

    bt = lambda b: jnp.transpose(b, (0, 2, 1)).reshape(NG * GS, NS)
    disc_in = (p["ev_s5_lambda_re"], p["ev_s5_lambda_im"], p["ev_s5_log_dt"].reshape(NG, 1),
               bt(p["ev_s5_b_re"]), bt(p["ev_s5_b_im"]))
    ab_re, ab_im, bbt_re, bbt_im = _s5_disc_fwd(*disc_in)
    bmat = (_to_blockdiag(bbt_re).astype(bf16), _to_blockdiag(bbt_im).astype(bf16))
    cmat = (_to_blockdiag(p["ev_s5_c_re"].reshape(NG * GS, NS)).astype(bf16),
            _to_blockdiag(p["ev_s5_c_im"].reshape(NG * GS, NS)).astype(bf16))
    a_re, a_im = ab_re.reshape(1, NST), ab_im.reshape(1, NST)
    d_skip = row(p["ev_s5_d"])
    w32 = jnp.pad(p["ev_conv_w"], ((0, 1), (0, 0)))
    conv_b, ln_g, ln_b, glu_b = row(p["ev_conv_b"]), row(p["ev_conv_ln_g"]), row(p["ev_conv_ln_b"]), row(p["ev_s5_glu_b"])
    g_mem, g_ev, g_od, g_fin = row(p["mem_norm_g"]), row(p["ev_norm_g"]), row(p["od_norm_g"]), row(p["final_norm_g"])
    g_xa = [row(p["xa_norm_g"][l]) for l in range(2)]
    bias = _bias_matrix(p["od_rel_bias"])

    wkv = p["xa_w_kv"]
    kv0, kv1 = _kv_fwd(mem, g_mem, wkv[0], wkv[1])
    z = _norm_mm_fwd("ev_in_fwd", x, g_ev, p["ev_w_in"], TMF)
    s5_out = _s5_fwd(z, bmat, cmat, a_re, a_im, d_skip, comm=((), late_shards))
    y_s5, (hb_re, hb_im), late = s5_out[:4], s5_out[4:6], s5_out[6:]
    p = late_views(p, late)
    wqg, wo = p["xa_w_qg"], p["xa_w_o"]
    c = _conv_fwd(z, w32, conv_b, 512)
    x1 = _ev_out_fwd(x, y_s5, c, z, glu_b, ln_g, ln_b, p["ev_s5_glu_w"], p["ev_w_out"], TMF)
    x2 = _xa_fwd("xa0_fwd", x1, g_xa[0], kv0, wqg[0], wo[0], TMF)
    qkvg = _norm_mm_fwd("od_in_fwd", x2, g_od, p["od_w_in"], TMF)
    o = _attn_fwd(qkvg, bias)
    x3 = _gated_out_fwd("od_out_fwd", x2, o, qkvg, 3, p["od_w_out"], TMF)
    x4 = _xa_fwd("xa1_fwd", x3, g_xa[1], kv1, wqg[1], wo[1], TMF)

    G = {}
    dx4, loss_row, G["final_norm_g"] = _loss_head(x4, g_fin, target, TM)
    dx3, dg_xa1, dkv1, dwqg1, dwo1 = _xa_bwd("xa1_bwd", x3, dx4, g_xa[1], kv1, wqg[1], wo[1], TM)
    do, dgate, G["od_w_out"] = _gated_out_bwd("od_out_bwd", dx3, o, qkvg, 3, p["od_w_out"], TM)
    dq, dk, dv, dbias = _attn_bwd(qkvg, do, bias)
    G["od_rel_bias"] = _bias_grad(dbias, _rel_onehot())
    dx2, G["od_norm_g"], G["od_w_in"] = _norm_mm_bwd(
        "od_in_bwd", x2, g_od, p["od_w_in"], [(dq, D, 0), (dk, D, 0), (dv, D, 0), (dgate, D, 0)], dx3, TM)
    G["xa_w_qg"], G["xa_w_o"] = [None, dwqg1], [None, dwo1]
    dx1, dg_xa0, dkv0, G["xa_w_qg"][0], G["xa_w_o"][0], *recv_early = _xa_bwd(
        "xa0_bwd", x1, dx2, g_xa[0], kv0, wqg[0], wo[0], TM, comm=(pack_early(G), ()))
    G["xa_norm_g"] = jnp.concatenate([dg_xa0, dg_xa1], axis=0)
    G["mem_norm_g"], dwkv0, dwkv1 = _kv_bwd(mem, g_mem, wkv[0], wkv[1], dkv0, dkv1)
    G["xa_w_kv"] = [dwkv0, dwkv1]
    (dy_s5, dc, dga, dgb, G["ev_s5_glu_b"], G["ev_conv_ln_g"], G["ev_conv_ln_b"], G["ev_s5_glu_w"],
     G["ev_w_out"]) = _ev_out_bwd(dx1, y_s5, c, z, glu_b, ln_g, ln_b, p["ev_s5_glu_w"], p["ev_w_out"], TM)
    dval, dglu, dw32, G["ev_conv_b"] = _conv_bwd(z, dc, w32, 512)
    G["ev_conv_w"] = dw32[:CK]
    s5_out = _s5_bwd(z, dy_s5, hb_re, hb_im, bmat, cmat, a_re, a_im, d_skip, comm=(pack_mid(G), ()))
    du, (dbre, dbim, dcre, dcim, da_re, da_im, G["ev_s5_d"]), recv_mid = s5_out[:4], s5_out[4:11], s5_out[11:]
    G["ev_s5_c_re"], G["ev_s5_c_im"] = _from_blockdiag(dcre), _from_blockdiag(dcim)
    G["ev_s5_lambda_re"], G["ev_s5_lambda_im"], G["ev_s5_log_dt"], G["ev_s5_b_re"], G["ev_s5_b_im"] = _s5_disc_bwd(
        disc_in, (da_re.reshape(NG, NS), da_im.reshape(NG, NS), _from_blockdiag(dbre), _from_blockdiag(dbim)))
    dx, G["ev_norm_g"], G["ev_w_in"] = _norm_mm_bwd(
        "ev_in_bwd", x, g_ev, p["ev_w_in"], [*[(t, 128, 0) for t in du], (dga, SW, 0), (dval, CW, 0), (dglu, CW, 0), (dgb, CW, 0)], dx1, TM)
    return loss_row[0, 0], dx, G, recv_early, recv_mid


def _all_gather(payloads):
    n = len(payloads)

    def body(*refs):
        ins, outs = refs[:n], refs[n:2 * n]
        send_sems, recv_sems, local_sems = refs[2 * n:]
        x, y, c = _mesh_pos()
        me, sibling = (x, y, c), (x, y, 1 - c)
        chips = [(1 - x, y), (x, 1 - y), (1 - x, 1 - y)]

        def copy(p, k, block, to, src=None):
            return pltpu.make_async_remote_copy(
                src_ref=_slot(outs[p], block) if src is None else src, dst_ref=_slot(outs[p], block),
                send_sem=send_sems.at[7 * p + k], recv_sem=recv_sems.at[7 * p + k],
                device_id=to, device_id_type=pl.DeviceIdType.MESH)

        mine = [pltpu.make_async_copy(ins[p], _slot(outs[p], me), local_sems.at[p]) for p in range(n)]
        for cp in mine:
            cp.start()
        first = []
        for p in range(n):
            first.append(copy(p, 0, me, sibling, src=ins[p]))
            first += [copy(p, 1 + j, me, (*chip, c), src=ins[p]) for j, chip in enumerate(chips)]
        for cp in first:
            cp.start()
        passed = []
        for j, chip in enumerate(chips):
            for p in range(n):
                copy(p, 1 + j, (*chip, c), me).wait_recv()
                passed.append(copy(p, 4 + j, (*chip, c), sibling))
                passed[-1].start()
        for p in range(n):
            copy(p, 0, sibling, me).wait_recv()
            for j, chip in enumerate(chips):
                copy(p, 4 + j, (*chip, 1 - c), me).wait_recv()
        for cp in first + passed:
            cp.wait_send()
        for cp in mine:
            cp.wait()

    return pl.pallas_call(
        body, name="all_gather", in_specs=[_ANY] * n, out_specs=[_ANY] * n,
        out_shape=[SDS((NDEV, *a.shape), a.dtype) for a in payloads],
        scratch_shapes=[pltpu.SemaphoreType.DMA((7 * n,)), pltpu.SemaphoreType.DMA((7 * n,)), pltpu.SemaphoreType.DMA((n,))],
    )(*payloads)


def _exchange(slotted, whole):
    n = len(slotted) + len(whole)
    shapes, sems = _comm_shapes((slotted, whole))

    def body(*refs):
        copies = _comm_copies(len(slotted), refs[:n], refs[n:2 * n], *refs[2 * n:])
        for cp in copies:
            cp.start()
        for cp in copies:
            cp.wait()

    return pl.pallas_call(body, name="grad_exchange", in_specs=[_ANY] * n, out_specs=[_ANY] * n, out_shape=shapes,
                          scratch_shapes=sems)(*slotted, *whole)


def _adamw_math(g, w, m, v):
    m2 = ADAM_B1 * m + (1.0 - ADAM_B1) * g
    v2 = ADAM_B2 * v + (1.0 - ADAM_B2) * jnp.square(g)
    m_hat = m2 / (1.0 - ADAM_B1 ** ADAM_STEP)
    v_hat = v2 / (1.0 - ADAM_B2 ** ADAM_STEP)
    return -ADAM_LR * (m_hat / (jnp.sqrt(v_hat) + ADAM_EPS) + ADAM_WD * w), m2, v2


def _sum_slots(r_ref, rows=None):
    acc = r_ref[0].astype(f32) if rows is None else r_ref[0, rows[0]:rows[1], :].astype(f32)
    for k in range(1, NDEV):
        acc = acc + (r_ref[k] if rows is None else r_ref[k, rows[0]:rows[1], :]).astype(f32)
    return acc


def _sum_adamw(name, sources, w, m, v, tr):
    rows, C = w.shape
    starts = [int(t) for t in np.cumsum([0] + [r // tr for (_, _, r) in sources])]

    def kern(*refs):
        r_refs = refs[:len(sources)]
        w_ref, m_ref, v_ref, g_ref, d_ref, m2_ref, v2_ref = refs[len(sources):]
        g = _sum_slots(r_refs[0])
        for s in range(1, len(sources)):
            g = jnp.where(pl.program_id(0) >= starts[s], _sum_slots(r_refs[s]), g)
        g_ref[...] = g
        d_ref[...], m2_ref[...], v2_ref[...] = _adamw_math(g, w_ref[...], m_ref[...], v_ref[...])

    def src_spec(s):
        _, off, r = sources[s]
        return pl.BlockSpec((NDEV, tr, C), lambda i: (0, off // tr + jnp.clip(i - starts[s], 0, r // tr - 1), 0))

    blk = pl.BlockSpec((tr, C), lambda i: (i, 0))
    return pl.pallas_call(
        kern, name=name, grid=(rows // tr,),
        in_specs=[src_spec(s) for s in range(len(sources))] + [blk, blk, blk],
        out_specs=[blk] * 4, out_shape=[SDS((rows, C), f32)] * 4, compiler_params=_cp(),
    )(*[r for (r, _, _) in sources], w, m, v)


def _small_adamw(recvs, table, wmv):
    rnames = list(recvs)
    nr, nw = len(rnames), len(table)

    def kern(*refs):
        rr = dict(zip(rnames, refs[:nr]))
        ins, outs = refs[nr:nr + 3 * nw], refs[nr + 3 * nw:]
        for i, (_, shape, src, r0) in enumerate(table):
            g = _sum_slots(rr[src], (r0, r0 + shape[0]))[:, :shape[1]]
            outs[4 * i][...] = g
            res = _adamw_math(g, ins[3 * i][...], ins[3 * i + 1][...], ins[3 * i + 2][...])
            for o, val in zip(outs[4 * i + 1:4 * i + 4], res):
                o[...] = val

    flat = [t for (n, _, _, _) in table for t in wmv[n]]
    res = pl.pallas_call(
        kern, name="small_adamw", out_shape=[SDS(shape, f32) for (_, shape, _, _) in table for _ in range(4)],
        compiler_params=_cp(()),
    )(*[recvs[n] for n in rnames], *flat)
    return {n: tuple(res[4 * i:4 * i + 4]) for i, (n, _, _, _) in enumerate(table)}


WEIGHTS = ["mem_norm_g", "ev_norm_g", "ev_w_in", "ev_s5_lambda_re", "ev_s5_lambda_im", "ev_s5_log_dt", "ev_s5_b_re",
           "ev_s5_b_im", "ev_s5_c_re", "ev_s5_c_im", "ev_s5_d", "ev_s5_glu_w", "ev_s5_glu_b", "ev_conv_w", "ev_conv_b",
           "ev_conv_ln_g", "ev_conv_ln_b", "ev_w_out", "od_norm_g", "od_w_in", "od_rel_bias", "od_w_out", "xa_norm_g",
           "xa_w_qg", "xa_w_kv", "xa_w_o", "final_norm_g"]
FIRST = [("a1024", 1024, [("ev_w_out", None, "rows", 128), ("xa_w_o", 0, "rows", 128)]),
         ("a512", 512, [("ev_s5_glu_w", None, "rows", 64)]),
         ("a256", 256, [("xa_w_qg", 0, "cols", 1024), ("xa_w_kv", 0, "cols", 1024), ("xa_w_kv", 1, "cols", 1024)]),
         ("a320", 320, [("ev_w_in", None, "cols", 1024)])]
LATE = [("b1024", 1024, [("od_w_out", None, "rows", 128), ("xa_w_o", 1, "rows", 128)]),
        ("b512", 512, [("od_w_in", None, "cols", 1024)]),
        ("b256", 256, [("xa_w_qg", 1, "cols", 1024)])]
MATMUL_WEIGHTS = ["ev_w_out", "xa_w_o", "ev_s5_glu_w", "xa_w_qg", "xa_w_kv", "ev_w_in", "od_w_out", "od_w_in"]
ADAM_ROWS = {1024: 128, 512: 64, 256: 256, 320: 256}
SMALL = [("mem_norm_g", (1, D), "s1024", 0), ("ev_norm_g", (1, D), "s1024", 1), ("xa_norm_g", (2, D), "s1024", 2),
         ("final_norm_g", (1, D), "s1024", 4),
         ("ev_s5_d", (1, SW), "s512", 0), ("ev_s5_glu_b", (1, SW), "s512", 1), ("ev_conv_b", (1, CW), "s512", 2),
         ("ev_conv_ln_g", (1, CW), "s512", 3), ("ev_conv_ln_b", (1, CW), "s512", 4),
         ("ev_s5_c_re", (NG * GS, NS), "s64", 0), ("ev_s5_c_im", (NG * GS, NS), "s64", 512),
         ("ev_s5_b_re", (NG * GS, NS), "s64", 1024), ("ev_s5_b_im", (NG * GS, NS), "s64", 1536),
         ("ev_s5_lambda_re", (NG, NS), "s64", 2048), ("ev_s5_lambda_im", (NG, NS), "s64", 2080),
         ("ev_s5_log_dt", (1, NG), "s64", 2112),
         ("od_rel_bias", (HEADS, 2 * MAX_REL + 1), "s257", 0),
         ("ev_conv_w", (CK, CW // NDEV), "cw", 0), ("od_norm_g", (1, D // NDEV), "on", 0)]
S5_B = ("ev_s5_b_re", "ev_s5_b_im")


def _pad_rows(a, rows):
    return jnp.pad(a, ((0, rows - a.shape[0]), (0, 0)))


def _shard_payloads(a, groups):
    pick = lambda n, l, minor: (a[n] if l is None else a[n][l]).reshape(-1, minor)
    return [jnp.concatenate([pick(n, l, minor).astype(bf16) for n, l, _, _ in members], axis=0) for _, minor, members in groups]


def _weight_views(p, groups, gathered):
    p = dict(p)
    for (_, _, members), buf in zip(groups, gathered):
        off = 0
        for n, l, kind, rows in members:
            view = _W(buf, rows, off // rows, kind)
            if l is None:
                p[n] = view
            else:
                p[n] = list(p.get(n, [None, None]))
                p[n][l] = view
            off += rows
    return p


def _slotted_grads(G, groups):
    slot3 = lambda g, rows: g.reshape(NDEV, rows, g.shape[-1]) if g.ndim == 2 else g
    return [jnp.concatenate([slot3(G[n] if l is None else G[n][l], rows).astype(bf16) for n, l, _, rows in members], axis=1)
            for _, _, members in groups]


def _bt(b):
    return jnp.transpose(b.reshape(NG, NS, GS), (0, 2, 1)).reshape(NG * GS, NS)


def _unbt(b, shape):
    return jnp.transpose(b.reshape(NG, GS, NS), (0, 2, 1)).reshape(shape)


def kernel(*args):
    names = ["x", "mem", *WEIGHTS, "loss_target", *["m_" + n for n in WEIGHTS], *["v_" + n for n in WEIGHTS]]
    a = dict(zip(names, args, strict=True))

    other = jnp.concatenate([_pad_rows(a["ev_conv_w"][0], 32).reshape(16, 128), _pad_rows(a["od_norm_g"], 8)], axis=0)
    *gathered, gother = _all_gather([*_shard_payloads(a, FIRST), other])
    p = {n: (a[n] if n in ("xa_norm_g", "mem_norm_g", "final_norm_g") else a[n][0]) for n in WEIGHTS if n not in MATMUL_WEIGHTS}
    p = _weight_views(p, FIRST, gathered)
    p["ev_w_in"] = jnp.transpose(gathered[3], (1, 0, 2)).reshape(D, EVEN_IN)
    p["ev_conv_w"] = jnp.transpose(gother[:, :16].reshape(NDEV, 32, CW // NDEV)[:, :CK], (1, 0, 2)).reshape(CK, CW)
    p["od_norm_g"] = gother[:, 16].reshape(D)

    loss_part, dx, G, recv_early, recv_mid = _local_step(
        a["x"][0], a["mem"][0], a["loss_target"][0], p, _shard_payloads(a, LATE),
        lambda p, gathered: _weight_views(p, LATE, gathered),
        lambda G: _slotted_grads(G, LATE), lambda G: _slotted_grads(G, FIRST[:3]))

    G["ev_w_in"] = jnp.transpose(G["ev_w_in"].reshape(D, NDEV, EVEN_IN // NDEV), (1, 0, 2))
    cw = jnp.transpose(G["ev_conv_w"].reshape(CK, NDEV, CW // NDEV), (1, 0, 2))
    slotted = [*_slotted_grads(G, FIRST[3:]), jnp.pad(cw, ((0, 0), (0, 1), (0, 0))), G["od_norm_g"].reshape(NDEV, 1, D // NDEV)]
    whole = {"s1024": _pad_rows(jnp.concatenate([G[n] for n in ("mem_norm_g", "ev_norm_g", "xa_norm_g", "final_norm_g")]), 8),
             "s512": _pad_rows(jnp.concatenate([G[n] for n in ("ev_s5_d", "ev_s5_glu_b", "ev_conv_b", "ev_conv_ln_g",
                                                                 "ev_conv_ln_b")]), 8),
             "s64": _pad_rows(jnp.concatenate([G[n] for n in ("ev_s5_c_re", "ev_s5_c_im", "ev_s5_b_re", "ev_s5_b_im",
                                                                "ev_s5_lambda_re", "ev_s5_lambda_im")]
                                              + [jnp.pad(G["ev_s5_log_dt"].reshape(1, NG), ((0, 0), (0, NS - NG)))]), 2120),
             "s257": G["od_rel_bias"]}
    recv_last = _exchange(slotted, list(whole.values()))
    recvs = dict(zip([g[0] for g in LATE], recv_early))
    recvs.update(zip([g[0] for g in FIRST[:3]], recv_mid))
    recvs.update(zip(["a320", "cw", "on", *whole], recv_last))

    sources = {}
    for buf, minor, members in FIRST + LATE:
        off = 0
        for n, l, _, rows in members:
            sources.setdefault(n, []).append((l or 0, recvs[buf], off, rows, minor))
            off += rows
    res = {}
    for n, src in sources.items():
        src = sorted(src, key=lambda t: t[0])
        minor = src[0][4]
        w, m, v = (a[k + n].reshape(-1, minor) for k in ("", "m_", "v_"))
        res[n] = _sum_adamw("adamw_" + n, [(r, off, rows) for _, r, off, rows, _ in src], w, m, v, ADAM_ROWS[minor])
    two_d = lambda n, t: _bt(t[0]) if n in S5_B else t.reshape(dict((s[0], s[1]) for s in SMALL)[n])
    res.update(_small_adamw({k: recvs[k] for k in ("s1024", "s512", "s64", "s257", "cw", "on")}, SMALL,
                            {n: tuple(two_d(n, a[k + n]) for k in ("", "m_", "v_")) for n, _, _, _ in SMALL}))
    shaped = lambda n, t: _unbt(t, a[n].shape) if n in S5_B else t.reshape(a[n].shape)
    loss = lax.psum(loss_part, AXES)
    return (loss, dx[None], *[shaped(n, res[n][k]) for k in range(4) for n in WEIGHTS])
```

```python
import functools
from typing import NamedTuple

import jax
import jax.numpy as jnp
import numpy as np
from jax import lax
from jax.experimental import pallas as pl
from jax.experimental.pallas import tpu as pltpu

f32, bf16 = jnp.float32, jnp.bfloat16
SDS = jax.ShapeDtypeStruct

D = 1024
SW = 512
NG, GS, NS = 32, 16, 64
NST = NG * NS
CW = 512
CK = 31
EVEN_IN = 2 * SW + 3 * CW
HEADS, DH = 16, 64
CHUNK, LEFT = 64, 8
PAD = LEFT * CHUNK
MAX_REL = 128
MEM_LEN = 256
XH, XD = 4, 256
EPS = 1e-6
NDEV = 8
AXES = ("x", "y", "c")

ADAM_LR, ADAM_B1, ADAM_B2, ADAM_EPS, ADAM_WD, ADAM_STEP = 0.001, 0.9, 0.999, 1e-08, 0.01, 10

VMEM_LIMIT = 56 << 20
S5_T = 512
S5_CH = 512
AQ = 256
AW = AQ + PAD

NN = ((1,), (0,))
NT = ((1,), (1,))
TN = ((0,), (0,))


def _dot(a, b, dims):
    return lax.dot_general(a.astype(bf16), b.astype(bf16), (dims, ((), ())), preferred_element_type=f32)


def _dotw(a, w):
    if w.ndim == 2:
        return _dot(a, w, NN)
    return jnp.concatenate([_dot(a, w[j], NN) for j in range(w.shape[0])], axis=1)


def _dotw_t(g, w):
    if w.ndim == 2:
        return _dot(g, w, NT)
    n = w.shape[2]
    out = _dot(g[:, :n], w[0], NT)
    for j in range(1, w.shape[0]):
        out = out + _dot(g[:, j * n:(j + 1) * n], w[j], NT)
    return out


def _wgrad(a, g, w):
    if w.ndim == 2:
        return _dot(a, g, TN)
    n = w.shape[2]
    return jnp.stack([_dot(a, g[:, j * n:(j + 1) * n], TN) for j in range(w.shape[0])])


@jax.custom_vjp
def _mm(a, w):
    return _dotw(a, w)


def _mm_f(a, w):
    return _dotw(a, w), w


def _mm_b(w, g):
    return _dotw_t(g, w), None


_mm.defvjp(_mm_f, _mm_b)


@jax.custom_vjp
def _mm_nt(a, b):
    return _dot(a, b, NT)


def _mm_nt_f(a, b):
    return _dot(a, b, NT), (a, b)


def _mm_nt_b(res, g):
    a, b = res
    return _dot(g, b, NN), _dot(g, a, TN)


_mm_nt.defvjp(_mm_nt_f, _mm_nt_b)


@jax.custom_vjp
def _mm_nn(a, b):
    return _dot(a, b, NN)


def _mm_nn_f(a, b):
    return _dot(a, b, NN), (a, b)


def _mm_nn_b(res, g):
    a, b = res
    return _dot(g, b, NT), _dot(a, g, TN)


_mm_nn.defvjp(_mm_nn_f, _mm_nn_b)


def _plain_mm(a, w):
    return _dotw(a, w)


def _plain_nt(a, b):
    return _dot(a, b, NT)


def _rms(x, g):
    return x * lax.rsqrt(jnp.mean(x * x, axis=-1, keepdims=True) + EPS) * g


def _cp(sem=("arbitrary",)):
    return pltpu.CompilerParams(dimension_semantics=sem, vmem_limit_bytes=VMEM_LIMIT)


def _row(tm, w, col=0):
    return pl.BlockSpec((tm, w), lambda i: (i, col))


def _full(shape):
    nd = len(shape)
    return pl.BlockSpec(tuple(shape), lambda i: (0,) * nd, pipeline_mode=pl.Buffered(1))


class _W(NamedTuple):
    arr: jax.Array
    rows: int
    idx: int
    kind: str

    @property
    def shape(self):
        c = self.arr.shape[2]
        return (NDEV * self.rows, c) if self.kind == "rows" else (NDEV, self.rows, c)


class _WRef:
    def __init__(self, ref, kind):
        self.ref, self.kind = ref, kind

    def __getitem__(self, _):
        v = self.ref[...]
        return v.reshape(v.shape[0] * v.shape[1], v.shape[2]) if self.kind == "rows" else v


def _wspec(w):
    if isinstance(w, _W):
        return pl.BlockSpec((NDEV, w.rows, w.arr.shape[2]), lambda i: (0, w.idx, 0), pipeline_mode=pl.Buffered(1))
    return _full(w.shape)


_ANY = pl.BlockSpec(memory_space=pl.ANY)
_NO_COMM = ((), ())


def _mesh_pos():
    return tuple(lax.axis_index(a) for a in AXES)


def _slot(ref, dev):
    return ref.at[4 * dev[0] + 2 * dev[1] + dev[2]]


def _comm_shapes(comm):
    slotted, whole = comm
    n = len(slotted) + len(whole)
    shapes = [SDS(a.shape, a.dtype) for a in slotted] + [SDS((NDEV, *a.shape), a.dtype) for a in whole]
    sems = [pltpu.SemaphoreType.DMA((7 * n,)), pltpu.SemaphoreType.DMA((7 * n,)), pltpu.SemaphoreType.DMA((n,))] if n else []
    return shapes, sems


def _comm_copies(ns, ins, outs, send_sems, recv_sems, local_sems):
    n = len(ins)
    x, y, c = _mesh_pos()
    me = (x, y, c)
    src = lambda p, dev: _slot(ins[p], dev) if p < ns else ins[p]
    copies = [pltpu.make_async_copy(src(p, me), _slot(outs[p], me), local_sems.at[p]) for p in range(n)]
    for k in range(1, NDEV):
        flip = lambda v, bit: 1 - v if (k >> bit) & 1 else v
        peer = (flip(x, 2), flip(y, 1), flip(c, 0))
        for p in range(n):
            copies.append(pltpu.make_async_remote_copy(
                src_ref=src(p, peer), dst_ref=_slot(outs[p], me), send_sem=send_sems.at[7 * p + k - 1],
                recv_sem=recv_sems.at[7 * p + k - 1], device_id=peer, device_id_type=pl.DeviceIdType.MESH))
    return copies


def _comm_hook(comm, cin, cout, sems, step, last):
    if not cin:
        return lambda: None

    @pl.when(step == 0)
    def _():
        for cp in _comm_copies(len(comm[0]), cin, cout, *sems):
            cp.start()

    def finish():
        @pl.when(step == last)
        def _():
            for cp in _comm_copies(len(comm[0]), cin, cout, *sems):
                cp.wait()

    return finish


def _tiled(name, body, L, tm, row_ins, full_ins, row_outs, acc_outs=(), comm=_NO_COMM):
    nr, nf, no, na = len(row_ins), len(full_ins), len(row_outs), len(acc_outs)
    cins = [*comm[0], *comm[1]]
    nc = len(cins)
    cshapes, csems = _comm_shapes(comm)

    def kern(*refs):
        rin = refs[:nr]
        fin = [_WRef(r, w.kind) if isinstance(w, _W) else r for r, w in zip(refs[nr:nr + nf], full_ins)]
        cin = refs[nr + nf:nr + nf + nc]
        outs = refs[nr + nf + nc:]
        rout, aout, cout, sems = outs[:no], outs[no:no + na], outs[no + na:no + na + nc], outs[no + na + nc:]
        finish = _comm_hook(comm, cin, cout, sems, pl.program_id(0), L // tm - 1)
        routs, accs = body([r[...] for r in rin], fin)
        for r, v in zip(rout, routs):
            r[...] = v.astype(r.dtype)
        if aout:
            @pl.when(pl.program_id(0) == 0)
            def _():
                for a in aout:
                    a[...] = jnp.zeros(a.shape, a.dtype)

            for a, v in zip(aout, accs):
                a[...] += v.astype(a.dtype)
        finish()

    return pl.pallas_call(
        kern, name=name, grid=(L // tm,),
        in_specs=[_row(tm, w, c) for (_, w, c) in row_ins] + [_wspec(a) for a in full_ins] + [_ANY] * nc,
        out_specs=[_row(tm, w) for (w, _) in row_outs] + [_full(s) for (s, _) in acc_outs] + [_ANY] * nc,
        out_shape=[SDS((L, w), dt) for (w, dt) in row_outs] + [SDS(tuple(s), dt) for (s, dt) in acc_outs] + cshapes,
        scratch_shapes=csems, compiler_params=_cp(),
    )(*[a for (a, _, _) in row_ins], *[a.arr if isinstance(a, _W) else a for a in full_ins], *cins)


def _norm_mm_fwd(name, x, g, w, tm, comm=_NO_COMM):
    def body(rows, fulls):
        g_ref, w_ref = fulls
        return [_dotw(_rms(rows[0], g_ref[...]), w_ref[...])], []

    n_out = w.shape[1] if len(w.shape) == 2 else w.shape[0] * w.shape[2]
    res = _tiled(name, body, x.shape[0], tm, [(x, D, 0)], [g, w], [(n_out, f32)], comm=comm)
    return res if comm[1] else res[0]


def _norm_mm_bwd(name, x, g, w, dz_parts, dres, tm):
    n = len(dz_parts)

    def body(rows, fulls):
        g_ref, w_ref = fulls
        dz = rows[1] if n == 1 else jnp.concatenate(rows[1:1 + n], axis=1)
        wv = w_ref[...]
        h, vjp = jax.vjp(_rms, rows[0], g_ref[...])
        dx, dg = vjp(_dotw_t(dz, wv))
        return [dx + rows[1 + n]], [dg, _wgrad(h, dz, wv)]

    return _tiled(name, body, x.shape[0], tm, [(x, D, 0), *dz_parts, (dres, D, 0)], [g, w],
                  [(D, f32)], [((1, D), f32), (tuple(w.shape), f32)])


S5_S = S5_T // 8


def _perm_load(refs):
    return jnp.concatenate([jnp.concatenate([r[pl.ds(i, 8, stride=S5_S), :] for i in range(S5_S)], axis=0) for r in refs],
                           axis=1)


def _perm_store(refs, v):
    for c, r in enumerate(refs):
        for i in range(S5_S):
            r[pl.ds(i, 8, stride=S5_S), :] = v[i * 8:(i + 1) * 8, c * 128:(c + 1) * 128]


def _cmul(pr, pi, qr, qi):
    return pr * qr - pi * qi, pr * qi + pi * qr


def _scan_tables(tab_ref, pw_re, pw_im, ar, ai, reverse):
    def powers(br, bi):
        pw = [(br, bi)]
        for _ in range(7):
            pw.append(_cmul(*pw[-1], br, bi))
        return pw

    row = lax.broadcasted_iota(jnp.int32, (8, NST), 0)

    def rows(pw, order):
        vr = jnp.zeros((8, NST), f32)
        vi = jnp.zeros((8, NST), f32)
        for t in range(8):
            vr = jnp.where(row == t, pw[order(t)][0], vr)
            vi = jnp.where(row == t, pw[order(t)][1], vi)
        return vr, vi

    pw = powers(ar, ai)
    base_r, base_i = rows(pw, lambda t: t)
    sr, si = jnp.ones((1, NST), f32), jnp.zeros((1, NST), f32)
    for m in range(S5_S // 8):
        pw_re[m * 8:(m + 1) * 8, :], pw_im[m * 8:(m + 1) * 8, :] = _cmul(base_r, base_i, sr, si)
        sr, si = _cmul(sr, si, *pw[7])
    big = powers(sr, si)
    for k, s in enumerate((1, 2, 4)):
        keep = (row <= 7 - s) if reverse else (row >= s)
        tab_ref[2 * k] = jnp.where(keep, big[s - 1][0], 0.0)
        tab_ref[2 * k + 1] = jnp.where(keep, big[s - 1][1], 0.0)
    tab_ref[6], tab_ref[7] = rows(big, (lambda t: 7 - t) if reverse else (lambda t: t))


def _scan(re_ref, im_ref, row0, a_re, a_im, tab_ref, pw_re, pw_im, car_re, car_im, reverse, fold=None):
    ng = S5_S // 8
    row = lax.broadcasted_iota(jnp.int32, (8, S5_CH), 0)
    accs = []
    for cc in range(NST // S5_CH):
        cols = slice(cc * S5_CH, (cc + 1) * S5_CH)
        ar = jnp.broadcast_to(a_re[:, cols], (8, S5_CH))
        ai = jnp.broadcast_to(a_im[:, cols], (8, S5_CH))

        def local(i, h, cols=cols, ar=ar, ai=ai):
            r0 = pl.multiple_of(row0 + ((S5_S - 1 - i) if reverse else i) * 8, 8)
            hr = ar * h[0] - ai * h[1] + re_ref[pl.ds(r0, 8), cols]
            hi = ar * h[1] + ai * h[0] + im_ref[pl.ds(r0, 8), cols]
            re_ref[pl.ds(r0, 8), cols] = hr
            im_ref[pl.ds(r0, 8), cols] = hi
            return hr, hi

        xr, xi = lax.fori_loop(0, S5_S, local, (jnp.zeros((8, S5_CH), f32), jnp.zeros((8, S5_CH), f32)))

        old_r, old_i = car_re[:, cols], car_im[:, cols]
        for k, s in enumerate((1, 2, 4)):
            yr = pltpu.roll(xr, (8 - s) if reverse else s, 0)
            yi = pltpu.roll(xi, (8 - s) if reverse else s, 0)
            dr, di = _cmul(tab_ref[2 * k, :, cols], tab_ref[2 * k + 1, :, cols], yr, yi)
            xr, xi = xr + dr, xi + di
        dr, di = _cmul(tab_ref[6, :, cols], tab_ref[7, :, cols], old_r, old_i)
        xr, xi = xr + dr, xi + di
        b = 0 if reverse else 7
        car_re[:, cols] = jnp.broadcast_to(xr[b:b + 1, :], (8, S5_CH))
        car_im[:, cols] = jnp.broadcast_to(xi[b:b + 1, :], (8, S5_CH))
        edge = 7 if reverse else 0
        cr = jnp.where(row == edge, old_r, pltpu.roll(xr, 7 if reverse else 1, 0))
        ci = jnp.where(row == edge, old_i, pltpu.roll(xi, 7 if reverse else 1, 0))

        def fix(g, acc, cols=cols, cr=cr, ci=ci):
            gg = (ng - 1 - g) if reverse else g
            pg = pl.multiple_of(g * 8, 8)
            ptr, pti = pw_re[pl.ds(pg, 8), cols], pw_im[pl.ds(pg, 8), cols]
            for u in range(8):
                uu = (7 - u) if reverse else u
                i = gg * 8 + uu
                r0 = pl.multiple_of(row0 + i * 8, 8)
                pr = jnp.broadcast_to(ptr[u:u + 1, :], (8, S5_CH))
                pi = jnp.broadcast_to(pti[u:u + 1, :], (8, S5_CH))
                dr, di = _cmul(pr, pi, cr, ci)
                hr = re_ref[pl.ds(r0, 8), cols] + dr
                hi = im_ref[pl.ds(r0, 8), cols] + di
                re_ref[pl.ds(r0, 8), cols] = hr
                im_ref[pl.ds(r0, 8), cols] = hi
                if fold is not None:
                    acc = fold(cols, i, hr, hi, acc)
            return acc

        acc0 = (jnp.zeros((8, S5_CH), f32), jnp.zeros((8, S5_CH), f32)) if fold is not None else 0
        accs.append(lax.fori_loop(0, ng, fix, acc0))
    return accs


def _s5_fwd(z, bmat, cmat, a_re, a_im, d_skip, comm=_NO_COMM):
    L = z.shape[0]
    nb = L // S5_T
    cins = [*comm[0], *comm[1]]
    nc = len(cins)
    cshapes, csems = _comm_shapes(comm)

    def kern(*refs):
        u_refs, (bre, bim, cre, cim, ar_ref, ai_ref, d_ref), rest = refs[:4], refs[4:11], refs[11:]
        cin, y_refs, (hbr_ref, hbi_ref), cout = rest[:nc], rest[nc:nc + 4], rest[nc + 4:nc + 6], rest[nc + 6:2 * nc + 6]
        h_re, h_im, tab, pw_re, pw_im, car_re, car_im, *sems = rest[2 * nc + 6:]
        finish = _comm_hook(comm, cin, cout, sems, pl.program_id(0), nb - 1)

        @pl.when(pl.program_id(0) == 0)
        def _():
            _scan_tables(tab, pw_re, pw_im, ar_ref[...], ai_ref[...], False)
            car_re[...] = jnp.zeros_like(car_re)
            car_im[...] = jnp.zeros_like(car_im)

        hbr_ref[...] = car_re[...]
        hbi_ref[...] = car_im[...]
        u = _perm_load(u_refs)
        for hf in range(2):
            uh = u[:, hf * 256:(hf + 1) * 256]
            h_re[:, hf * 1024:(hf + 1) * 1024] = _dot(uh, bre[hf], NN)
            h_im[:, hf * 1024:(hf + 1) * 1024] = _dot(uh, bim[hf], NN)
        _scan(h_re, h_im, 0, ar_ref[...], ai_ref[...], tab, pw_re, pw_im, car_re, car_im, False)
        ys = []
        for hf in range(2):
            cs = slice(hf * 1024, (hf + 1) * 1024)
            ys.append(_dot(h_re[:, cs], cre[hf], NT) - _dot(h_im[:, cs], cim[hf], NT))
        _perm_store(y_refs, jnp.concatenate(ys, axis=1) + d_ref[...] * u)
        finish()

    fulls = [*bmat, *cmat, a_re, a_im, d_skip]
    return pl.pallas_call(
        kern, name="s5_fwd", grid=(nb,),
        in_specs=[_row(S5_T, 128, c) for c in range(4)] + [_full(a.shape) for a in fulls] + [_ANY] * nc,
        out_specs=[_row(S5_T, 128)] * 4 + [_row(8, NST), _row(8, NST)] + [_ANY] * nc,
        out_shape=[SDS((L, 128), f32)] * 4 + [SDS((nb * 8, NST), f32), SDS((nb * 8, NST), f32)] + cshapes,
        scratch_shapes=[pltpu.VMEM((S5_T, NST), f32), pltpu.VMEM((S5_T, NST), f32), pltpu.VMEM((8, 8, NST), f32),
                        pltpu.VMEM((S5_S, NST), f32), pltpu.VMEM((S5_S, NST), f32),
                        pltpu.VMEM((8, NST), f32), pltpu.VMEM((8, NST), f32)] + csems,
        compiler_params=_cp(),
    )(z, z, z, z, *fulls, *cins)


def _s5_bwd(z, dy, hb_re, hb_im, bmat, cmat, a_re, a_im, d_skip, comm=_NO_COMM):
    L = z.shape[0]
    nb = L // S5_T
    T = S5_T
    cins = [*comm[0], *comm[1]]
    nc = len(cins)
    cshapes, csems = _comm_shapes(comm)

    def kern(*refs):
        u_refs, dy_refs, (hbr_ref, hbi_ref, bre, bim, cre, cim, ar_ref, ai_ref, d_ref), rest = \
            refs[:4], refs[4:8], refs[8:17], refs[17:]
        cin, du_refs, (dbre, dbim, dcre, dcim, dar, dai, dd), cout = \
            rest[:nc], rest[nc:nc + 4], rest[nc + 4:nc + 11], rest[nc + 11:2 * nc + 11]
        h_re, h_im, g_re, g_im, tabf, pwf_re, pwf_im, tabr, pwr_re, pwr_im, car_re, car_im, gcar_re, gcar_im, *sems = \
            rest[2 * nc + 11:]
        finish = _comm_hook(comm, cin, cout, sems, pl.program_id(0), nb - 1)

        @pl.when(pl.program_id(0) == 0)
        def _():
            _scan_tables(tabf, pwf_re, pwf_im, ar_ref[...], ai_ref[...], False)
            _scan_tables(tabr, pwr_re, pwr_im, ar_ref[...], -ai_ref[...], True)
            gcar_re[...] = jnp.zeros_like(gcar_re)
            gcar_im[...] = jnp.zeros_like(gcar_im)
            for r in (dbre, dbim, dcre, dcim, dar, dai, dd):
                r[...] = jnp.zeros_like(r)

        u = _perm_load(u_refs)
        dyv = _perm_load(dy_refs)
        car_re[...] = hbr_ref[...]
        car_im[...] = hbi_ref[...]
        for hf in range(2):
            cs = slice(hf * 1024, (hf + 1) * 1024)
            uh = u[:, hf * 256:(hf + 1) * 256]
            dyh = dyv[:, hf * 256:(hf + 1) * 256]
            h_re[8:, cs] = _dot(uh, bre[hf], NN)
            h_im[8:, cs] = _dot(uh, bim[hf], NN)
            g_re[:, cs] = _dot(dyh, cre[hf], NN)
            g_im[:, cs] = -_dot(dyh, cim[hf], NN)
        _scan(h_re, h_im, 8, ar_ref[...], ai_ref[...], tabf, pwf_re, pwf_im, car_re, car_im, False)
        row = lax.broadcasted_iota(jnp.int32, (8, NST), 0)
        h_re[0:8, :] = jnp.where(row == 0, hbr_ref[...], pltpu.roll(h_re[T:T + 8, :], 1, 0))
        h_im[0:8, :] = jnp.where(row == 0, hbi_ref[...], pltpu.roll(h_im[T:T + 8, :], 1, 0))

        def fold_da(cols, i, gr, gi, acc):
            r0 = pl.multiple_of(i * 8, 8)
            hr, hi = h_re[pl.ds(r0, 8), cols], h_im[pl.ds(r0, 8), cols]
            return acc[0] + gr * hr + gi * hi, acc[1] + gi * hr - gr * hi

        accs = _scan(g_re, g_im, 0, ar_ref[...], -ai_ref[...], tabr, pwr_re, pwr_im, gcar_re, gcar_im, True, fold=fold_da)
        for cc, (acc_r, acc_i) in enumerate(accs):
            cols = slice(cc * S5_CH, (cc + 1) * S5_CH)
            dar[:, cols] += jnp.sum(acc_r, axis=0, keepdims=True)
            dai[:, cols] += jnp.sum(acc_i, axis=0, keepdims=True)

        dus = []
        for hf in range(2):
            cs = slice(hf * 1024, (hf + 1) * 1024)
            uh = u[:, hf * 256:(hf + 1) * 256]
            dyh = dyv[:, hf * 256:(hf + 1) * 256]
            gr = g_re[:, cs]
            gi = g_im[:, cs]
            dus.append(_dot(gr, bre[hf], NT) + _dot(gi, bim[hf], NT))
            dbre[hf] += _dot(uh, gr, TN)
            dbim[hf] += _dot(uh, gi, TN)
            dcre[hf] += _dot(dyh, h_re[8:, cs], TN)
            dcim[hf] -= _dot(dyh, h_im[8:, cs], TN)
        _perm_store(du_refs, jnp.concatenate(dus, axis=1) + d_ref[...] * dyv)
        dd[...] += jnp.sum(dyv * u, axis=0, keepdims=True)
        finish()

    rev = lambda c: pl.BlockSpec((T, 128), lambda k: (nb - 1 - k, c))
    rev8 = pl.BlockSpec((8, NST), lambda k: (nb - 1 - k, 0))
    fulls = [*bmat, *cmat, a_re, a_im, d_skip]
    mat = ((2, 256, 1024), f32)
    accs = [mat, mat, mat, mat, ((1, NST), f32), ((1, NST), f32), ((1, SW), f32)]
    tables = [pltpu.VMEM((8, 8, NST), f32), pltpu.VMEM((S5_S, NST), f32), pltpu.VMEM((S5_S, NST), f32)]
    return pl.pallas_call(
        kern, name="s5_bwd", grid=(nb,),
        in_specs=[rev(c) for c in range(4)] * 2 + [rev8, rev8] + [_full(a.shape) for a in fulls] + [_ANY] * nc,
        out_specs=[rev(0)] * 4 + [_full(s) for (s, _) in accs] + [_ANY] * nc,
        out_shape=[SDS((L, 128), f32)] * 4 + [SDS(s, dt) for (s, dt) in accs] + cshapes,
        scratch_shapes=[pltpu.VMEM((T + 8, NST), f32), pltpu.VMEM((T + 8, NST), f32),
                        pltpu.VMEM((T, NST), f32), pltpu.VMEM((T, NST), f32), *tables, *tables,
                        pltpu.VMEM((8, NST), f32), pltpu.VMEM((8, NST), f32),
                        pltpu.VMEM((8, NST), f32), pltpu.VMEM((8, NST), f32)] + csems,
        compiler_params=_cp(),
    )(z, z, z, z, dy, dy, dy, dy, hb_re, hb_im, *fulls, *cins)


def _s5_disc(lam_re, lam_im, log_dt, bt_re, bt_im):
    dt = jnp.exp(log_dt)
    mag = jnp.exp(lam_re * dt)
    ab_re = mag * jnp.cos(lam_im * dt)
    ab_im = mag * jnp.sin(lam_im * dt)
    den = lam_re * lam_re + lam_im * lam_im
    nr = ab_re - 1.0
    coef_re = (nr * lam_re + ab_im * lam_im) / den
    coef_im = (ab_im * lam_re - nr * lam_im) / den
    rep = lambda c: jnp.broadcast_to(c[:, None, :], (NG, GS, NS)).reshape(NG * GS, NS)
    cr, ci = rep(coef_re), rep(coef_im)
    return ab_re, ab_im, cr * bt_re - ci * bt_im, cr * bt_im + ci * bt_re


def _s5_disc_fwd(*params):
    def kern(*refs):
        outs = _s5_disc(*[r[...] for r in refs[:5]])
        for r, v in zip(refs[5:], outs):
            r[...] = v

    shapes = [SDS((NG, NS), f32), SDS((NG, NS), f32), SDS((NG * GS, NS), f32), SDS((NG * GS, NS), f32)]
    return pl.pallas_call(kern, name="s5_disc_fwd", out_shape=shapes)(*params)


def _s5_disc_bwd(params, cts):
    def kern(*refs):
        _, vjp = jax.vjp(_s5_disc, *[r[...] for r in refs[:5]])
        for r, v in zip(refs[9:], vjp(tuple(r[...] for r in refs[5:9]))):
            r[...] = v

    return pl.pallas_call(kern, name="s5_disc_bwd", out_shape=[SDS(p.shape, f32) for p in params])(*params, *cts)


def _diag_mask():
    r = np.arange(256)[:, None] // GS
    c = np.arange(1024)[None, :] // NS
    return jnp.asarray(r == c, f32)


def _to_blockdiag(m):
    return jnp.tile(m.reshape(2, 256, NS), (1, 1, 16)) * _diag_mask()


def _from_blockdiag(dm):
    return (dm * _diag_mask()).reshape(2, 256, 16, NS).sum(axis=2).reshape(NG * GS, NS)


def _conv_shifts(win, sh, tm):
    win[tm + 32:, :] = jnp.zeros((8, CW), f32)
    for s in range(8):
        sh[s] = win[s:s + tm + 32, :]


def _conv_taps(offset, w_ref, sh, r, accs):
    for k in range(CK):
        s, q = offset(k) % 8, offset(k) // 8
        wk = w_ref[k:k + 1, :]
        accs = [acc + wk * sh[s, pl.ds(pl.multiple_of(r + 8 * (a + q), 8), 8), :] for a, acc in enumerate(accs)]
    return accs


def _conv_fwd(z, w32, b, tm, comm=_NO_COMM):
    L = z.shape[0]
    hb = tm // 32
    cins = [*comm[0], *comm[1]]
    nc = len(cins)
    cshapes, csems = _comm_shapes(comm)

    def kern(val, glu, hval, hglu, w_ref, b_ref, *rest):
        cin, c_ref, cout, (win, sh, *sems) = rest[:nc], rest[nc], rest[nc + 1:2 * nc + 1], rest[2 * nc + 1:]
        i = pl.program_id(0)
        finish = _comm_hook(comm, cin, cout, sems, i, L // tm - 1)
        win[0:32, :] = hval[...] * jax.nn.sigmoid(hglu[...]) * (i > 0).astype(f32)
        win[32:tm + 32, :] = val[...] * jax.nn.sigmoid(glu[...])
        _conv_shifts(win, sh, tm)

        def strip(j, carry):
            r = j * 16
            accs = _conv_taps(lambda k: k + 2, w_ref, sh, r, [jnp.zeros((8, CW), f32) + b_ref[...]] * 2)
            c_ref[pl.ds(pl.multiple_of(r, 16), 16), :] = jnp.concatenate(accs, axis=0)
            return carry

        lax.fori_loop(0, tm // 16, strip, 0)
        finish()

    halo = lambda col: pl.BlockSpec((32, CW), lambda i: (jnp.maximum(i * hb - 1, 0), col))
    return pl.pallas_call(
        kern, name="conv_fwd", grid=(L // tm,),
        in_specs=[_row(tm, CW, 2), _row(tm, CW, 3), halo(2), halo(3), _full(w32.shape), _full(b.shape)] + [_ANY] * nc,
        out_specs=[_row(tm, CW)] + [_ANY] * nc, out_shape=[SDS((L, CW), f32)] + cshapes,
        scratch_shapes=[pltpu.VMEM((tm + 40, CW), f32), pltpu.VMEM((8, tm + 32, CW), f32)] + csems, compiler_params=_cp(),
    )(z, z, z, z, w32, b, *cins)


def _conv_bwd(z, dc, w32, tm):
    L = z.shape[0]
    hb = tm // 32
    nt = L // tm

    def kern(val, glu, hval, hglu, dc_ref, hdc, w_ref, dval_ref, dglu_ref, dw_ref, db_ref, win, sh, dwin, dsh):
        i = pl.program_id(0)

        @pl.when(i == 0)
        def _():
            dw_ref[...] = jnp.zeros_like(dw_ref)
            db_ref[...] = jnp.zeros_like(db_ref)

        win[0:32, :] = hval[...] * jax.nn.sigmoid(hglu[...]) * (i > 0).astype(f32)
        win[32:tm + 32, :] = val[...] * jax.nn.sigmoid(glu[...])
        _conv_shifts(win, sh, tm)
        dwin[0:tm, :] = dc_ref[...]
        dwin[tm:tm + 32, :] = hdc[...] * (i < nt - 1).astype(f32)
        _conv_shifts(dwin, dsh, tm)

        def strip(j, carry):
            r = j * 16
            rows = pl.ds(pl.multiple_of(r, 16), 16)
            dv = jnp.concatenate(_conv_taps(lambda k: 30 - k, w_ref, dsh, r, [jnp.zeros((8, CW), f32)] * 2), axis=0)
            sg = jax.nn.sigmoid(glu[rows, :])
            dval_ref[rows, :] = dv * sg
            dglu_ref[rows, :] = dv * val[rows, :] * sg * (1.0 - sg)
            return carry

        lax.fori_loop(0, tm // 16, strip, 0)

        for k0 in range(0, CK, 2):
            taps = [k for k in (k0, k0 + 1) if k < CK]

            def rows64(j, accs, taps=taps):
                for u in range(8):
                    r = pl.multiple_of(j * 64 + u * 8, 8)
                    dcv = dc_ref[pl.ds(r, 8), :]
                    accs = [acc + dcv * sh[(k + 2) % 8, pl.ds(pl.multiple_of(r + 8 * ((k + 2) // 8), 8), 8), :]
                            for k, acc in zip(taps, accs)]
                return accs

            accs = lax.fori_loop(0, tm // 64, rows64, [jnp.zeros((8, CW), f32)] * len(taps))
            for k, acc in zip(taps, accs):
                dw_ref[k:k + 1, :] += jnp.sum(acc, axis=0, keepdims=True)
        db_ref[...] += jnp.sum(dc_ref[...], axis=0, keepdims=True)

    halo = lambda col: pl.BlockSpec((32, CW), lambda i: (jnp.maximum(i * hb - 1, 0), col))
    nxt = pl.BlockSpec((32, CW), lambda i: (jnp.minimum((i + 1) * hb, L // 32 - 1), 0))
    return pl.pallas_call(
        kern, name="conv_bwd", grid=(nt,),
        in_specs=[_row(tm, CW, 2), _row(tm, CW, 3), halo(2), halo(3), _row(tm, CW), nxt, _full(w32.shape)],
        out_specs=[_row(tm, CW), _row(tm, CW), _full((32, CW)), _full((1, CW))],
        out_shape=[SDS((L, CW), f32), SDS((L, CW), f32), SDS((32, CW), f32), SDS((1, CW), f32)],
        scratch_shapes=[pltpu.VMEM((tm + 40, CW), f32), pltpu.VMEM((8, tm + 32, CW), f32),
                        pltpu.VMEM((tm + 40, CW), f32), pltpu.VMEM((8, tm + 32, CW), f32)], compiler_params=_cp(),
    )(z, z, z, z, dc, dc, w32)


def _ev_out_f(y, c, ga, gb, tap, glu_b, ln_g, ln_b, glu_w, w_out, mm):
    z1 = jax.nn.gelu(y)
    ya = z1 * jax.nn.sigmoid(mm(z1, glu_w) + glu_b + tap) * jax.nn.silu(ga)
    mu = jnp.mean(c, axis=-1, keepdims=True)
    var = jnp.mean(jnp.square(c - mu), axis=-1, keepdims=True)
    cn = (c - mu) * lax.rsqrt(var + EPS) * ln_g + ln_b
    cat = jnp.concatenate([ya, jax.nn.silu(cn) * jax.nn.silu(gb)], axis=1)
    return mm(cat, w_out), (z1, cat)


def _ev_out_fwd(x, y, c, z, glu_b, ln_g, ln_b, glu_w, w_out, tm, comm=_NO_COMM):
    def body(rows, fulls):
        xv, cv, ga, gb = rows[:4]
        yv = jnp.concatenate(rows[4:], axis=1)
        gb_ref, lg_ref, lb_ref, gw_ref, wo_ref = fulls
        out, _ = _ev_out_f(yv, cv, ga, gb, 0.0, gb_ref[...], lg_ref[...], lb_ref[...], gw_ref[...], wo_ref[...], _plain_mm)
        return [xv + out], []

    return _tiled("ev_out_fwd", body, x.shape[0], tm, [(x, D, 0), (c, CW, 0), (z, SW, 1), (z, CW, 4), *[(t, 128, 0) for t in y]],
                  [glu_b, ln_g, ln_b, glu_w, w_out], [(D, f32)], comm=comm)


def _ev_out_bwd(dx1, y, c, z, glu_b, ln_g, ln_b, glu_w, w_out, tm):
    def body(rows, fulls):
        dxv, cv, ga, gb = rows[:4]
        yv = jnp.concatenate(rows[4:], axis=1)
        gb_ref, lg_ref, lb_ref, gw_ref, wo_ref = fulls
        gw, wo = gw_ref[...], wo_ref[...]
        f = lambda yv, cv, ga, gb, tap, b, lg, lb: _ev_out_f(yv, cv, ga, gb, tap, b, lg, lb, gw, wo, _mm)
        _, vjp, (z1, cat) = jax.vjp(f, yv, cv, ga, gb, jnp.zeros((tm, SW), f32), gb_ref[...], lg_ref[...], lb_ref[...],
                                    has_aux=True)
        dy, dc, dga, dgb, dtap, db, dlg, dlb = vjp(dxv)
        return [dy, dc, dga, dgb], [db, dlg, dlb, _wgrad(z1, dtap, gw), _wgrad(cat, dxv, wo)]

    return _tiled("ev_out_bwd", body, dx1.shape[0], tm, [(dx1, D, 0), (c, CW, 0), (z, SW, 1), (z, CW, 4), *[(t, 128, 0) for t in y]],
                  [glu_b, ln_g, ln_b, glu_w, w_out], [(SW, f32), (CW, f32), (SW, f32), (CW, f32)],
                  [((1, SW), f32), ((1, CW), f32), ((1, CW), f32), (glu_w.shape, f32), (w_out.shape, f32)])


def _kv_f(mem, g, tap0, tap1, w0, w1, mm):
    mn = _rms(mem, g)
    return (mm(mn, w0) + tap0, mm(mn, w1) + tap1), mn


def _kv_bwd(mem, g, w0, w1, dkv0, dkv1):
    def body(rows, fulls):
        g_ref, w0_ref, w1_ref = fulls
        w0v, w1v = w0_ref[...], w1_ref[...]
        tap = jnp.zeros((MEM_LEN, 2 * D), f32)
        _, vjp, mn = jax.vjp(lambda g_, t0, t1: _kv_f(rows[0], g_, t0, t1, w0v, w1v, _mm), g_ref[...], tap, tap, has_aux=True)
        dg, d0, d1 = vjp((rows[1], rows[2]))
        return [], [dg, _wgrad(mn, d0, w0v), _wgrad(mn, d1, w1v)]

    return _tiled("kv_bwd", body, MEM_LEN, MEM_LEN, [(mem, D, 0), (dkv0, 2 * D, 0), (dkv1, 2 * D, 0)], [g, w0, w1], [],
                  [((1, D), f32), (w0.shape, f32), (w1.shape, f32)])


def _xa_f(x, g, kv, tap, w_qg, w_o, mm, nt, nn):
    h = _rms(x, g)
    qg = mm(h, w_qg) + tap
    outs = []
    for hd in range(XH):
        q = qg[:, hd * XD:(hd + 1) * XD]
        k = kv[:, hd * XD:(hd + 1) * XD]
        v = kv[:, D + hd * XD:D + (hd + 1) * XD]
        s = nt(q, k) * (XD ** -0.5)
        e = jnp.exp(s - jnp.max(s, axis=-1, keepdims=True))
        outs.append(nn(e / jnp.sum(e, axis=-1, keepdims=True), v))
    u = jnp.concatenate(outs, axis=1) * jax.nn.silu(qg[:, D:])
    return mm(u, w_o), (h, u)


def _xa_fwd(name, x, g, kv, w_qg, w_o, tm):
    def body(rows, fulls):
        g_ref, kv_ref, wq_ref, wo_ref = fulls
        out, _ = _xa_f(rows[0], g_ref[...], kv_ref[...], 0.0, wq_ref[...], wo_ref[...], _plain_mm, _plain_nt, _plain_mm)
        return [rows[0] + out], []

    return _tiled(name, body, x.shape[0], tm, [(x, D, 0)], [g, kv, w_qg, w_o], [(D, f32)])[0]


def _xa_bwd(name, x, dxo, g, kv, w_qg, w_o, tm, comm=_NO_COMM):
    def body(rows, fulls):
        xv, dxv = rows
        g_ref, kv_ref, wq_ref, wo_ref = fulls
        wq, wo = wq_ref[...], wo_ref[...]
        f = lambda xv, gv, kvv, tap: _xa_f(xv, gv, kvv, tap, wq, wo, _mm, _mm_nt, _mm_nn)
        _, vjp, (h, u) = jax.vjp(f, xv, g_ref[...], kv_ref[...], jnp.zeros((tm, 2 * D), f32), has_aux=True)
        dx, dg, dkv, dtap = vjp(dxv)
        return [dx + dxv], [dg, dkv, _wgrad(h, dtap, wq), _wgrad(u, dxv, wo)]

    return _tiled(name, body, x.shape[0], tm, [(x, D, 0), (dxo, D, 0)], [g, kv, w_qg, w_o], [(D, f32)],
                  [((1, D), f32), ((MEM_LEN, 2 * D), f32), (w_qg.shape, f32), (w_o.shape, f32)], comm=comm)


def _attn_pad(k_ref, v_ref, kpad, vpad):
    kpad[0:PAD, :] = jnp.zeros((PAD, 128), bf16)
    vpad[0:PAD, :] = jnp.zeros((PAD, 128), bf16)
    kpad[PAD:, :] = k_ref[...].astype(bf16)
    vpad[PAD:, :] = v_ref[...].astype(bf16)


def _attn_exp(qm, kwin, bm, mask_row):
    x = _dot(qm, kwin, NT) + bm + mask_row
    e = jnp.exp(x - jnp.max(x, axis=-1, keepdims=True))
    return e, 1.0 / jnp.sum(e, axis=-1, keepdims=True)


def _attn_fwd(qkvg, bias, comm=_NO_COMM):
    L = qkvg.shape[0]
    nq = L // AQ
    cins = [*comm[0], *comm[1]]
    nc = len(cins)
    cshapes, csems = _comm_shapes(comm)

    def kern(q_ref, k_ref, v_ref, b_ref, *rest):
        cin, o_ref, cout, (kpad, vpad, *sems) = rest[:nc], rest[nc], rest[nc + 1:2 * nc + 1], rest[2 * nc + 1:]
        i = pl.program_id(1)
        finish = _comm_hook(comm, cin, cout, sems, pl.program_id(0) * nq + i, HEADS // 2 * nq - 1)

        @pl.when(i == 0)
        def _():
            _attn_pad(k_ref, v_ref, kpad, vpad)

        r0 = pl.multiple_of(i * AQ, AQ)
        q = q_ref[...] * (DH ** -0.5)
        kwin = kpad[pl.ds(r0, AW), :]
        vwin = vpad[pl.ds(r0, AW), :]
        lane_hi = lax.broadcasted_iota(jnp.int32, (AQ, 128), 1) // DH
        mask_row = jnp.where(lax.broadcasted_iota(jnp.int32, (1, AW), 1) >= PAD - r0, 0.0, -1e30)
        outs = []
        for hh in range(2):
            e, inv = _attn_exp(jnp.where(lane_hi == hh, q, 0.0), kwin, b_ref[hh], mask_row)
            outs.append(_dot(e, vwin, NN) * inv)
        o_ref[...] = jnp.where(lane_hi == 0, outs[0], outs[1])
        finish()

    return pl.pallas_call(
        kern, name="attn_fwd", grid=(HEADS // 2, nq),
        in_specs=[pl.BlockSpec((AQ, 128), lambda j, i: (i, j)),
                  pl.BlockSpec((L, 128), lambda j, i: (0, 8 + j)),
                  pl.BlockSpec((L, 128), lambda j, i: (0, 16 + j)),
                  pl.BlockSpec((2, AQ, AW), lambda j, i: (j, 0, 0))] + [_ANY] * nc,
        out_specs=[pl.BlockSpec((AQ, 128), lambda j, i: (i, j))] + [_ANY] * nc,
        out_shape=[SDS((L, D), f32)] + cshapes,
        scratch_shapes=[pltpu.VMEM((L + PAD, 128), bf16), pltpu.VMEM((L + PAD, 128), bf16)] + csems,
        compiler_params=_cp(("arbitrary", "arbitrary")),
    )(qkvg, qkvg, qkvg, bias, *cins)


def _attn_bwd(qkvg, do, bias):
    L = qkvg.shape[0]
    nq = L // AQ

    def kern(q_ref, k_ref, v_ref, b_ref, do_ref, dq_ref, dk_ref, dv_ref, db_ref, kpad, vpad, dkp, dvp):
        i = pl.program_id(1)

        @pl.when(i == 0)
        def _():
            _attn_pad(k_ref, v_ref, kpad, vpad)
            dkp[...] = jnp.zeros_like(dkp)
            dvp[...] = jnp.zeros_like(dvp)
            db_ref[...] = jnp.zeros_like(db_ref)

        r0 = pl.multiple_of(i * AQ, AQ)
        q = q_ref[...] * (DH ** -0.5)
        dov = do_ref[...]
        kwin = kpad[pl.ds(r0, AW), :]
        vwin = vpad[pl.ds(r0, AW), :]
        lane_hi = lax.broadcasted_iota(jnp.int32, (AQ, 128), 1) // DH
        mask_row = jnp.where(lax.broadcasted_iota(jnp.int32, (1, AW), 1) >= PAD - r0, 0.0, -1e30)
        dqs = []
        dk = jnp.zeros((AW, 128), f32)
        dv = jnp.zeros((AW, 128), f32)
        for hh in range(2):
            qm = jnp.where(lane_hi == hh, q, 0.0).astype(bf16)
            dom = jnp.where(lane_hi == hh, dov, 0.0).astype(bf16)
            dp = _dot(dom, vwin, NT)
            e, inv = _attn_exp(qm, kwin, b_ref[hh], mask_row)
            p = e * inv
            ds = p * (dp - jnp.sum(p * dp, axis=-1, keepdims=True))
            db_ref[hh] += ds
            dsb = ds.astype(bf16)
            dqs.append(_dot(dsb, kwin, NN) * (DH ** -0.5))
            dk = dk + _dot(dsb, qm, TN)
            dv = dv + _dot(p, dom, TN)
        dq_ref[...] = jnp.where(lane_hi == 0, dqs[0], dqs[1])
        dkp[pl.ds(r0, AW), :] += dk
        dvp[pl.ds(r0, AW), :] += dv

        @pl.when(i == nq - 1)
        def _():
            dk_ref[...] = dkp[PAD:, :]
            dv_ref[...] = dvp[PAD:, :]

    return pl.pallas_call(
        kern, name="attn_bwd", grid=(HEADS // 2, nq),
        in_specs=[pl.BlockSpec((AQ, 128), lambda j, i: (i, j)),
                  pl.BlockSpec((L, 128), lambda j, i: (0, 8 + j)),
                  pl.BlockSpec((L, 128), lambda j, i: (0, 16 + j)),
                  pl.BlockSpec((2, AQ, AW), lambda j, i: (j, 0, 0)),
                  pl.BlockSpec((AQ, 128), lambda j, i: (i, j))],
        out_specs=[pl.BlockSpec((AQ, 128), lambda j, i: (i, j)),
                   pl.BlockSpec((L, 128), lambda j, i: (0, j)),
                   pl.BlockSpec((L, 128), lambda j, i: (0, j)),
                   pl.BlockSpec((2, AQ, AW), lambda j, i: (j, 0, 0))],
        out_shape=[SDS((L, D), f32), SDS((L, D), f32), SDS((L, D), f32), SDS((HEADS, AQ, AW), f32)],
        scratch_shapes=[pltpu.VMEM((L + PAD, 128), bf16), pltpu.VMEM((L + PAD, 128), bf16),
                        pltpu.VMEM((L + PAD, 128), f32), pltpu.VMEM((L + PAD, 128), f32)],
        compiler_params=_cp(("arbitrary", "arbitrary")),
    )(qkvg, qkvg, qkvg, bias, do)


_SKEW = AQ + AW - 1


def _rel_index():
    d = (AW - 1) - np.arange(_SKEW)
    return np.clip(d, -MAX_REL, MAX_REL) + MAX_REL


def _band_mask():
    qc = np.arange(AQ)[:, None] // CHUNK + LEFT
    kc = np.arange(AW)[None, :] // CHUNK
    return jnp.asarray(np.where((kc <= qc) & (kc >= qc - LEFT), 0.0, -1e30), f32)


def _bias_matrix(rel_bias):
    tv = jnp.take(rel_bias, jnp.asarray(_rel_index()), axis=1)
    flat = jnp.tile(jnp.pad(tv, ((0, 0), (0, 1))), (1, AQ))
    top = flat[:, AQ - 1:AQ - 1 + AQ * _SKEW].reshape(HEADS, AQ, _SKEW)[:, :, :AW]
    return top + _band_mask()


def _bias_grad(dbias, onehot):
    flat = jnp.pad(dbias, ((0, 0), (0, 0), (0, _SKEW - AW))).reshape(HEADS, AQ * _SKEW)
    z = jnp.pad(flat, ((0, 0), (AQ - 1, AQ * 1024 - AQ * _SKEW - (AQ - 1)))).reshape(HEADS, AQ, 1024)

    def kern(z_ref, oh_ref, out_ref):
        diag = jnp.sum(z_ref[...], axis=1)
        out_ref[...] = jnp.dot(diag, oh_ref[...], preferred_element_type=f32, precision=lax.Precision.HIGHEST)

    return pl.pallas_call(kern, name="bias_grad", out_shape=SDS((HEADS, 2 * MAX_REL + 1), f32),
                          compiler_params=_cp(()))(z, onehot)


def _rel_onehot():
    oh = np.zeros((1024, 2 * MAX_REL + 1), np.float32)
    oh[np.arange(_SKEW), _rel_index()] = 1.0
    return jnp.asarray(oh)


def _gated_out_fwd(name, x, o, gate_src, gate_col, w, tm):
    def body(rows, fulls):
        xv, ov, gv = rows
        return [xv + _dotw(ov * jax.nn.silu(gv), fulls[0][...])], []

    return _tiled(name, body, x.shape[0], tm, [(x, D, 0), (o, D, 0), (gate_src, D, gate_col)], [w], [(D, f32)])[0]


def _gated_out_bwd(name, dxo, o, gate_src, gate_col, w, tm):
    def body(rows, fulls):
        dxv, ov, gv = rows
        wv = fulls[0][...]
        u, vjp = jax.vjp(lambda ov, gv: ov * jax.nn.silu(gv), ov, gv)
        do, dg = vjp(_dotw_t(dxv, wv))
        return [do, dg], [_wgrad(u, dxv, wv)]

    return _tiled(name, body, dxo.shape[0], tm, [(dxo, D, 0), (o, D, 0), (gate_src, D, gate_col)], [w],
                  [(D, f32), (D, f32)], [(w.shape, f32)])


def _loss_head(x, g, target, tm):
    def body(rows, fulls):
        xv, tv = rows

        def f(xv, gv):
            e = jnp.square(_rms(xv, gv) - tv)
            return 0.5 * jnp.sum(jnp.mean(e, axis=-1, keepdims=True), axis=0, keepdims=True)

        loss, vjp = jax.vjp(f, xv, fulls[0][...])
        dx, dg = vjp(jnp.ones((1, 1), f32))
        return [dx], [jnp.broadcast_to(loss, (1, 128)), dg]

    return _tiled("loss_head", body, x.shape[0], tm, [(x, D, 0), (target, D, 0)], [g], [(D, f32)],
                  [((1, 128), f32), ((1, D), f32)])


def _local_step(x, mem, target, p, shards, views, pack_early, pack_mid):
    TM, TMF = 256, 512
    row = lambda v: v.reshape(1, -1)

    bt = lambda b: jnp.transpose(b, (0, 2, 1)).reshape(NG * GS, NS)
    disc_in = (p["ev_s5_lambda_re"], p["ev_s5_lambda_im"], p["ev_s5_log_dt"].reshape(NG, 1),
               bt(p["ev_s5_b_re"]), bt(p["ev_s5_b_im"]))
    ab_re, ab_im, bbt_re, bbt_im = _s5_disc_fwd(*disc_in)
    bmat = (_to_blockdiag(bbt_re).astype(bf16), _to_blockdiag(bbt_im).astype(bf16))
    cmat = (_to_blockdiag(p["ev_s5_c_re"].reshape(NG * GS, NS)).astype(bf16),
            _to_blockdiag(p["ev_s5_c_im"].reshape(NG * GS, NS)).astype(bf16))
    a_re, a_im = ab_re.reshape(1, NST), ab_im.reshape(1, NST)
    d_skip = row(p["ev_s5_d"])
    w32 = jnp.pad(p["ev_conv_w"], ((0, 1), (0, 0)))
    conv_b, ln_g, ln_b, glu_b = row(p["ev_conv_b"]), row(p["ev_conv_ln_g"]), row(p["ev_conv_ln_b"]), row(p["ev_s5_glu_b"])
    g_mem, g_ev, g_od, g_fin = row(p["mem_norm_g"]), row(p["ev_norm_g"]), row(p["od_norm_g"]), row(p["final_norm_g"])
    g_xa = [row(p["xa_norm_g"][l]) for l in range(2)]
    bias = _bias_matrix(p["od_rel_bias"])

    z, *got = _norm_mm_fwd("ev_in_fwd", x, g_ev, p["ev_w_in"], TMF, comm=((), shards["ev_in"]))
    p = views(p, "ev_in", got)
    s5_out = _s5_fwd(z, bmat, cmat, a_re, a_im, d_skip, comm=((), shards["s5"]))
    y_s5, (hb_re, hb_im), p = s5_out[:4], s5_out[4:6], views(p, "s5", s5_out[6:])
    c, *got = _conv_fwd(z, w32, conv_b, 512, comm=((), shards["conv"]))
    p = views(p, "conv", got)
    x1, *got = _ev_out_fwd(x, y_s5, c, z, glu_b, ln_g, ln_b, p["ev_s5_glu_w"], p["ev_w_out"], TMF, comm=((), shards["ev_out"]))
    p = views(p, "ev_out", got)
    kv0 = _norm_mm_fwd("kv0_fwd", mem, g_mem, p["xa_w_kv"][0], MEM_LEN)
    x2 = _xa_fwd("xa0_fwd", x1, g_xa[0], kv0, p["xa_w_qg"][0], p["xa_w_o"][0], TMF)
    qkvg = _norm_mm_fwd("od_in_fwd", x2, g_od, p["od_w_in"], TMF)
    o, *got = _attn_fwd(qkvg, bias, comm=((), shards["attn"]))
    p = views(p, "attn", got)
    wqg, wkv, wo = p["xa_w_qg"], p["xa_w_kv"], p["xa_w_o"]
    x3 = _gated_out_fwd("od_out_fwd", x2, o, qkvg, 3, p["od_w_out"], TMF)
    kv1 = _norm_mm_fwd("kv1_fwd", mem, g_mem, wkv[1], MEM_LEN)
    x4 = _xa_fwd("xa1_fwd", x3, g_xa[1], kv1, wqg[1], wo[1], TMF)

    G = {}
    dx4, loss_row, G["final_norm_g"] = _loss_head(x4, g_fin, target, TM)
    dx3, dg_xa1, dkv1, dwqg1, dwo1 = _xa_bwd("xa1_bwd", x3, dx4, g_xa[1], kv1, wqg[1], wo[1], TM)
    do, dgate, G["od_w_out"] = _gated_out_bwd("od_out_bwd", dx3, o, qkvg, 3, p["od_w_out"], TM)
    dq, dk, dv, dbias = _attn_bwd(qkvg, do, bias)
    G["od_rel_bias"] = _bias_grad(dbias, _rel_onehot())
    dx2, G["od_norm_g"], G["od_w_in"] = _norm_mm_bwd(
        "od_in_bwd", x2, g_od, p["od_w_in"], [(dq, D, 0), (dk, D, 0), (dv, D, 0), (dgate, D, 0)], dx3, TM)
    G["xa_w_qg"], G["xa_w_o"] = [None, dwqg1], [None, dwo1]
    dx1, dg_xa0, dkv0, G["xa_w_qg"][0], G["xa_w_o"][0], *recv_early = _xa_bwd(
        "xa0_bwd", x1, dx2, g_xa[0], kv0, wqg[0], wo[0], TM, comm=(pack_early(G), ()))
    G["xa_norm_g"] = jnp.concatenate([dg_xa0, dg_xa1], axis=0)
    G["mem_norm_g"], dwkv0, dwkv1 = _kv_bwd(mem, g_mem, wkv[0], wkv[1], dkv0, dkv1)
    G["xa_w_kv"] = [dwkv0, dwkv1]
    (dy_s5, dc, dga, dgb, G["ev_s5_glu_b"], G["ev_conv_ln_g"], G["ev_conv_ln_b"], G["ev_s5_glu_w"],
     G["ev_w_out"]) = _ev_out_bwd(dx1, y_s5, c, z, glu_b, ln_g, ln_b, p["ev_s5_glu_w"], p["ev_w_out"], TM)
    dval, dglu, dw32, G["ev_conv_b"] = _conv_bwd(z, dc, w32, 512)
    G["ev_conv_w"] = dw32[:CK]
    s5_out = _s5_bwd(z, dy_s5, hb_re, hb_im, bmat, cmat, a_re, a_im, d_skip, comm=(pack_mid(G), ()))
    du, (dbre, dbim, dcre, dcim, da_re, da_im, G["ev_s5_d"]), recv_mid = s5_out[:4], s5_out[4:11], s5_out[11:]
    G["ev_s5_c_re"], G["ev_s5_c_im"] = _from_blockdiag(dcre), _from_blockdiag(dcim)
    G["ev_s5_lambda_re"], G["ev_s5_lambda_im"], G["ev_s5_log_dt"], G["ev_s5_b_re"], G["ev_s5_b_im"] = _s5_disc_bwd(
        disc_in, (da_re.reshape(NG, NS), da_im.reshape(NG, NS), _from_blockdiag(dbre), _from_blockdiag(dbim)))
    dx, G["ev_norm_g"], G["ev_w_in"] = _norm_mm_bwd(
        "ev_in_bwd", x, g_ev, p["ev_w_in"], [*[(t, 128, 0) for t in du], (dga, SW, 0), (dval, CW, 0), (dglu, CW, 0), (dgb, CW, 0)], dx1, TM)
    return loss_row[0, 0], dx, G, recv_early, recv_mid


def _all_gather(payloads):
    n = len(payloads)

    def body(*refs):
        ins, outs = refs[:n], refs[n:2 * n]
        send_sems, recv_sems, local_sems = refs[2 * n:]
        x, y, c = _mesh_pos()
        me, sibling = (x, y, c), (x, y, 1 - c)
        chips = [(1 - x, y), (x, 1 - y), (1 - x, 1 - y)]

        def copy(p, k, block, to, src=None):
            return pltpu.make_async_remote_copy(
                src_ref=_slot(outs[p], block) if src is None else src, dst_ref=_slot(outs[p], block),
                send_sem=send_sems.at[7 * p + k], recv_sem=recv_sems.at[7 * p + k],
                device_id=to, device_id_type=pl.DeviceIdType.MESH)

        mine = [pltpu.make_async_copy(ins[p], _slot(outs[p], me), local_sems.at[p]) for p in range(n)]
        for cp in mine:
            cp.start()
        first = []
        for p in range(n):
            first.append(copy(p, 0, me, sibling, src=ins[p]))
            first += [copy(p, 1 + j, me, (*chip, c), src=ins[p]) for j, chip in enumerate(chips)]
        for cp in first:
            cp.start()
        passed = []
        for j, chip in enumerate(chips):
            for p in range(n):
                copy(p, 1 + j, (*chip, c), me).wait_recv()
                passed.append(copy(p, 4 + j, (*chip, c), sibling))
                passed[-1].start()
        for p in range(n):
            copy(p, 0, sibling, me).wait_recv()
            for j, chip in enumerate(chips):
                copy(p, 4 + j, (*chip, 1 - c), me).wait_recv()
        for cp in first + passed:
            cp.wait_send()
        for cp in mine:
            cp.wait()

    return pl.pallas_call(
        body, name="all_gather", in_specs=[_ANY] * n, out_specs=[_ANY] * n,
        out_shape=[SDS((NDEV, *a.shape), a.dtype) for a in payloads],
        scratch_shapes=[pltpu.SemaphoreType.DMA((7 * n,)), pltpu.SemaphoreType.DMA((7 * n,)), pltpu.SemaphoreType.DMA((n,))],
    )(*payloads)


def _exchange(slotted, whole):
    n = len(slotted) + len(whole)
    shapes, sems = _comm_shapes((slotted, whole))

    def body(*refs):
        copies = _comm_copies(len(slotted), refs[:n], refs[n:2 * n], *refs[2 * n:])
        for cp in copies:
            cp.start()
        for cp in copies:
            cp.wait()

    return pl.pallas_call(body, name="grad_exchange", in_specs=[_ANY] * n, out_specs=[_ANY] * n, out_shape=shapes,
                          scratch_shapes=sems)(*slotted, *whole)


def _adamw_math(g, w, m, v):
    m2 = ADAM_B1 * m + (1.0 - ADAM_B1) * g
    v2 = ADAM_B2 * v + (1.0 - ADAM_B2) * jnp.square(g)
    m_hat = m2 / (1.0 - ADAM_B1 ** ADAM_STEP)
    v_hat = v2 / (1.0 - ADAM_B2 ** ADAM_STEP)
    return -ADAM_LR * (m_hat / (jnp.sqrt(v_hat) + ADAM_EPS) + ADAM_WD * w), m2, v2


def _sum_slots(r_ref, rows=None):
    acc = r_ref[0].astype(f32) if rows is None else r_ref[0, rows[0]:rows[1], :].astype(f32)
    for k in range(1, NDEV):
        acc = acc + (r_ref[k] if rows is None else r_ref[k, rows[0]:rows[1], :]).astype(f32)
    return acc


def _sum_adamw(name, sources, w, m, v, tr):
    rows, C = w.shape
    starts = [int(t) for t in np.cumsum([0] + [r // tr for (_, _, r) in sources])]

    def kern(*refs):
        r_refs = refs[:len(sources)]
        w_ref, m_ref, v_ref, g_ref, d_ref, m2_ref, v2_ref = refs[len(sources):]
        g = _sum_slots(r_refs[0])
        for s in range(1, len(sources)):
            g = jnp.where(pl.program_id(0) >= starts[s], _sum_slots(r_refs[s]), g)
        g_ref[...] = g
        d_ref[...], m2_ref[...], v2_ref[...] = _adamw_math(g, w_ref[...], m_ref[...], v_ref[...])

    def src_spec(s):
        _, off, r = sources[s]
        return pl.BlockSpec((NDEV, tr, C), lambda i: (0, off // tr + jnp.clip(i - starts[s], 0, r // tr - 1), 0))

    blk = pl.BlockSpec((tr, C), lambda i: (i, 0))
    return pl.pallas_call(
        kern, name=name, grid=(rows // tr,),
        in_specs=[src_spec(s) for s in range(len(sources))] + [blk, blk, blk],
        out_specs=[blk] * 4, out_shape=[SDS((rows, C), f32)] * 4, compiler_params=_cp(),
    )(*[r for (r, _, _) in sources], w, m, v)


def _small_adamw(recvs, table, wmv):
    rnames = list(recvs)
    nr, nw = len(rnames), len(table)

    def kern(*refs):
        rr = dict(zip(rnames, refs[:nr]))
        ins, outs = refs[nr:nr + 3 * nw], refs[nr + 3 * nw:]
        for i, (_, shape, src, r0) in enumerate(table):
            g = _sum_slots(rr[src], (r0, r0 + shape[0]))[:, :shape[1]]
            outs[4 * i][...] = g
            res = _adamw_math(g, ins[3 * i][...], ins[3 * i + 1][...], ins[3 * i + 2][...])
            for o, val in zip(outs[4 * i + 1:4 * i + 4], res):
                o[...] = val

    flat = [t for (n, _, _, _) in table for t in wmv[n]]
    res = pl.pallas_call(
        kern, name="small_adamw", out_shape=[SDS(shape, f32) for (_, shape, _, _) in table for _ in range(4)],
        compiler_params=_cp(()),
    )(*[recvs[n] for n in rnames], *flat)
    return {n: tuple(res[4 * i:4 * i + 4]) for i, (n, _, _, _) in enumerate(table)}


WEIGHTS = ["mem_norm_g", "ev_norm_g", "ev_w_in", "ev_s5_lambda_re", "ev_s5_lambda_im", "ev_s5_log_dt", "ev_s5_b_re",
           "ev_s5_b_im", "ev_s5_c_re", "ev_s5_c_im", "ev_s5_d", "ev_s5_glu_w", "ev_s5_glu_b", "ev_conv_w", "ev_conv_b",
           "ev_conv_ln_g", "ev_conv_ln_b", "ev_w_out", "od_norm_g", "od_w_in", "od_rel_bias", "od_w_out", "xa_norm_g",
           "xa_w_qg", "xa_w_kv", "xa_w_o", "final_norm_g"]
GATHER = {"first": [("g320", 320, [("ev_w_in", None, "cols", 1024)])],
          "ev_in": [("h1024", 1024, [("ev_w_out", None, "rows", 128)]), ("h512", 512, [("ev_s5_glu_w", None, "rows", 64)])],
          "s5": [("i512", 512, [("od_w_in", None, "cols", 1024)]), ("i1024", 1024, [("xa_w_o", 0, "rows", 128)])],
          "conv": [("j256", 256, [("xa_w_qg", 0, "cols", 1024)])],
          "ev_out": [("k256", 256, [("xa_w_kv", 0, "cols", 1024)])],
          "attn": [("l1024", 1024, [("od_w_out", None, "rows", 128), ("xa_w_o", 1, "rows", 128)]),
                   ("l256", 256, [("xa_w_qg", 1, "cols", 1024), ("xa_w_kv", 1, "cols", 1024)])]}
FIRST = [("a1024", 1024, [("ev_w_out", None, "rows", 128), ("xa_w_o", 0, "rows", 128)]),
         ("a512", 512, [("ev_s5_glu_w", None, "rows", 64)]),
         ("a256", 256, [("xa_w_qg", 0, "cols", 1024), ("xa_w_kv", 0, "cols", 1024), ("xa_w_kv", 1, "cols", 1024)]),
         ("a320", 320, [("ev_w_in", None, "cols", 1024)])]
LATE = [("b1024", 1024, [("od_w_out", None, "rows", 128), ("xa_w_o", 1, "rows", 128)]),
        ("b512", 512, [("od_w_in", None, "cols", 1024)]),
        ("b256", 256, [("xa_w_qg", 1, "cols", 1024)])]
MATMUL_WEIGHTS = ["ev_w_out", "xa_w_o", "ev_s5_glu_w", "xa_w_qg", "xa_w_kv", "ev_w_in", "od_w_out", "od_w_in"]
ADAM_ROWS = {1024: 128, 512: 64, 256: 256, 320: 256}
SMALL = [("mem_norm_g", (1, D), "s1024", 0), ("ev_norm_g", (1, D), "s1024", 1), ("xa_norm_g", (2, D), "s1024", 2),
         ("final_norm_g", (1, D), "s1024", 4),
         ("ev_s5_d", (1, SW), "s512", 0), ("ev_s5_glu_b", (1, SW), "s512", 1), ("ev_conv_b", (1, CW), "s512", 2),
         ("ev_conv_ln_g", (1, CW), "s512", 3), ("ev_conv_ln_b", (1, CW), "s512", 4),
         ("ev_s5_c_re", (NG * GS, NS), "s64", 0), ("ev_s5_c_im", (NG * GS, NS), "s64", 512),
         ("ev_s5_b_re", (NG * GS, NS), "s64", 1024), ("ev_s5_b_im", (NG * GS, NS), "s64", 1536),
         ("ev_s5_lambda_re", (NG, NS), "s64", 2048), ("ev_s5_lambda_im", (NG, NS), "s64", 2080),
         ("ev_s5_log_dt", (1, NG), "s64", 2112),
         ("od_rel_bias", (HEADS, 2 * MAX_REL + 1), "s257", 0),
         ("ev_conv_w", (CK, CW // NDEV), "cw", 0), ("od_norm_g", (1, D // NDEV), "on", 0)]
S5_B = ("ev_s5_b_re", "ev_s5_b_im")


def _pad_rows(a, rows):
    return jnp.pad(a, ((0, rows - a.shape[0]), (0, 0)))


def _shard_payloads(a, groups):
    pick = lambda n, l, minor: (a[n] if l is None else a[n][l]).reshape(-1, minor)
    return [jnp.concatenate([pick(n, l, minor).astype(bf16) for n, l, _, _ in members], axis=0) for _, minor, members in groups]


def _weight_views(p, groups, gathered):
    p = dict(p)
    for (_, _, members), buf in zip(groups, gathered):
        off = 0
        for n, l, kind, rows in members:
            view = _W(buf, rows, off // rows, kind)
            if l is None:
                p[n] = view
            else:
                p[n] = list(p.get(n, [None, None]))
                p[n][l] = view
            off += rows
    return p


def _slotted_grads(G, groups):
    slot3 = lambda g, rows: g.reshape(NDEV, rows, g.shape[-1]) if g.ndim == 2 else g
    return [jnp.concatenate([slot3(G[n] if l is None else G[n][l], rows).astype(bf16) for n, l, _, rows in members], axis=1)
            for _, _, members in groups]


def _bt(b):
    return jnp.transpose(b.reshape(NG, NS, GS), (0, 2, 1)).reshape(NG * GS, NS)


def _unbt(b, shape):
    return jnp.transpose(b.reshape(NG, GS, NS), (0, 2, 1)).reshape(shape)


def kernel(*args):
    names = ["x", "mem", *WEIGHTS, "loss_target", *["m_" + n for n in WEIGHTS], *["v_" + n for n in WEIGHTS]]
    a = dict(zip(names, args, strict=True))

    other = jnp.concatenate([_pad_rows(a["ev_conv_w"][0], 32).reshape(16, 128), _pad_rows(a["od_norm_g"], 8)], axis=0)
    g_in, gother = _all_gather([*_shard_payloads(a, GATHER["first"]), other])
    p = {n: (a[n] if n in ("xa_norm_g", "mem_norm_g", "final_norm_g") else a[n][0]) for n in WEIGHTS if n not in MATMUL_WEIGHTS}
    p["ev_w_in"] = jnp.transpose(g_in, (1, 0, 2)).reshape(D, EVEN_IN)
    p["ev_conv_w"] = jnp.transpose(gother[:, :16].reshape(NDEV, 32, CW // NDEV)[:, :CK], (1, 0, 2)).reshape(CK, CW)
    p["od_norm_g"] = gother[:, 16].reshape(D)

    loss_part, dx, G, recv_early, recv_mid = _local_step(
        a["x"][0], a["mem"][0], a["loss_target"][0], p, {s: _shard_payloads(a, g) for s, g in GATHER.items()},
        lambda p, stage, gathered: _weight_views(p, GATHER[stage], gathered),
        lambda G: _slotted_grads(G, LATE), lambda G: _slotted_grads(G, FIRST[:3]))

    G["ev_w_in"] = jnp.transpose(G["ev_w_in"].reshape(D, NDEV, EVEN_IN // NDEV), (1, 0, 2))
    cw = jnp.transpose(G["ev_conv_w"].reshape(CK, NDEV, CW // NDEV), (1, 0, 2))
    slotted = [*_slotted_grads(G, FIRST[3:]), jnp.pad(cw, ((0, 0), (0, 1), (0, 0))), G["od_norm_g"].reshape(NDEV, 1, D // NDEV)]
    whole = {"s1024": _pad_rows(jnp.concatenate([G[n] for n in ("mem_norm_g", "ev_norm_g", "xa_norm_g", "final_norm_g")]), 8),
             "s512": _pad_rows(jnp.concatenate([G[n] for n in ("ev_s5_d", "ev_s5_glu_b", "ev_conv_b", "ev_conv_ln_g",
                                                                 "ev_conv_ln_b")]), 8),
             "s64": _pad_rows(jnp.concatenate([G[n] for n in ("ev_s5_c_re", "ev_s5_c_im", "ev_s5_b_re", "ev_s5_b_im",
                                                                "ev_s5_lambda_re", "ev_s5_lambda_im")]
                                              + [jnp.pad(G["ev_s5_log_dt"].reshape(1, NG), ((0, 0), (0, NS - NG)))]), 2120),
             "s257": G["od_rel_bias"]}
    recv_last = _exchange(slotted, list(whole.values()))
    recvs = dict(zip([g[0] for g in LATE], recv_early))
    recvs.update(zip([g[0] for g in FIRST[:3]], recv_mid))
    recvs.update(zip(["a320", "cw", "on", *whole], recv_last))

    sources = {}
    for buf, minor, members in FIRST + LATE:
        off = 0
        for n, l, _, rows in members:
            sources.setdefault(n, []).append((l or 0, recvs[buf], off, rows, minor))
            off += rows
    res = {}
    for n, src in sources.items():
        src = sorted(src, key=lambda t: t[0])
        minor = src[0][4]
        w, m, v = (a[k + n].reshape(-1, minor) for k in ("", "m_", "v_"))
        res[n] = _sum_adamw("adamw_" + n, [(r, off, rows) for _, r, off, rows, _ in src], w, m, v, ADAM_ROWS[minor])
    two_d = lambda n, t: _bt(t[0]) if n in S5_B else t.reshape(dict((s[0], s[1]) for s in SMALL)[n])
    res.update(_small_adamw({k: recvs[k] for k in ("s1024", "s512", "s64", "s257", "cw", "on")}, SMALL,
                            {n: tuple(two_d(n, a[k + n]) for k in ("", "m_", "v_")) for n, _, _, _ in SMALL}))
    shaped = lambda n, t: _unbt(t, a[n].shape) if n in S5_B else t.reshape(a[n].shape)
    loss = lax.psum(loss_part, AXES)
    return (loss, dx[None], *[shaped(n, res[n][k]) for k in range(4) for n in WEIGHTS])
```

```python
import functools
from typing import NamedTuple

import jax
import jax.numpy as jnp
import numpy as np
from jax import lax
from jax.experimental import pallas as pl
from jax.experimental.pallas import tpu as pltpu

f32, bf16 = jnp.float32, jnp.bfloat16
SDS = jax.ShapeDtypeStruct

D = 1024
SW = 512
NG, GS, NS = 32, 16, 64
NST = NG * NS
CW = 512
CK = 31
EVEN_IN = 2 * SW + 3 * CW
HEADS, DH = 16, 64
CHUNK, LEFT = 64, 8
PAD = LEFT * CHUNK
MAX_REL = 128
MEM_LEN = 256
XH, XD = 4, 256
EPS = 1e-6
NDEV = 8
AXES = ("x", "y", "c")

ADAM_LR, ADAM_B1, ADAM_B2, ADAM_EPS, ADAM_WD, ADAM_STEP = 0.001, 0.9, 0.999, 1e-08, 0.01, 10

VMEM_LIMIT = 56 << 20
S5_T = 512
S5_CH = 512
AQ = 256
AW = AQ + PAD

NN = ((1,), (0,))
NT = ((1,), (1,))
TN = ((0,), (0,))


def _dot(a, b, dims):
    return lax.dot_general(a.astype(bf16), b.astype(bf16), (dims, ((), ())), preferred_element_type=f32)


def _dotw(a, w):
    if w.ndim == 2:
        return _dot(a, w, NN)
    return jnp.concatenate([_dot(a, w[j], NN) for j in range(w.shape[0])], axis=1)


def _dotw_t(g, w):
    if w.ndim == 2:
        return _dot(g, w, NT)
    n = w.shape[2]
    out = _dot(g[:, :n], w[0], NT)
    for j in range(1, w.shape[0]):
        out = out + _dot(g[:, j * n:(j + 1) * n], w[j], NT)
    return out


def _wgrad(a, g, w):
    if w.ndim == 2:
        return _dot(a, g, TN)
    n = w.shape[2]
    return jnp.stack([_dot(a, g[:, j * n:(j + 1) * n], TN) for j in range(w.shape[0])])


@jax.custom_vjp
def _mm(a, w):
    return _dotw(a, w)


def _mm_f(a, w):
    return _dotw(a, w), w


def _mm_b(w, g):
    return _dotw_t(g, w), None


_mm.defvjp(_mm_f, _mm_b)


@jax.custom_vjp
def _mm_nt(a, b):
    return _dot(a, b, NT)


def _mm_nt_f(a, b):
    return _dot(a, b, NT), (a, b)


def _mm_nt_b(res, g):
    a, b = res
    return _dot(g, b, NN), _dot(g, a, TN)


_mm_nt.defvjp(_mm_nt_f, _mm_nt_b)


@jax.custom_vjp
def _mm_nn(a, b):
    return _dot(a, b, NN)


def _mm_nn_f(a, b):
    return _dot(a, b, NN), (a, b)


def _mm_nn_b(res, g):
    a, b = res
    return _dot(g, b, NT), _dot(a, g, TN)


_mm_nn.defvjp(_mm_nn_f, _mm_nn_b)


def _plain_mm(a, w):
    return _dotw(a, w)


def _plain_nt(a, b):
    return _dot(a, b, NT)


def _rms(x, g):
    return x * lax.rsqrt(jnp.mean(x * x, axis=-1, keepdims=True) + EPS) * g


def _cp(sem=("arbitrary",)):
    return pltpu.CompilerParams(dimension_semantics=sem, vmem_limit_bytes=VMEM_LIMIT)


def _row(tm, w, col=0):
    return pl.BlockSpec((tm, w), lambda i: (i, col))


def _full(shape):
    nd = len(shape)
    return pl.BlockSpec(tuple(shape), lambda i: (0,) * nd, pipeline_mode=pl.Buffered(1))


class _W(NamedTuple):
    arr: jax.Array
    rows: int
    idx: int
    kind: str

    @property
    def shape(self):
        c = self.arr.shape[2]
        return (NDEV * self.rows, c) if self.kind == "rows" else (NDEV, self.rows, c)


class _WRef:
    def __init__(self, ref, kind):
        self.ref, self.kind = ref, kind

    def __getitem__(self, _):
        v = self.ref[...]
        return v.reshape(v.shape[0] * v.shape[1], v.shape[2]) if self.kind == "rows" else v


def _wspec(w):
    if isinstance(w, _W):
        return pl.BlockSpec((NDEV, w.rows, w.arr.shape[2]), lambda i: (0, w.idx, 0), pipeline_mode=pl.Buffered(1))
    return _full(w.shape)


_ANY = pl.BlockSpec(memory_space=pl.ANY)
_NO_COMM = ((), ())


def _mesh_pos():
    return tuple(lax.axis_index(a) for a in AXES)


def _slot(ref, dev):
    return ref.at[4 * dev[0] + 2 * dev[1] + dev[2]]


def _comm_shapes(comm):
    slotted, whole = comm
    n = len(slotted) + len(whole)
    shapes = [SDS(a.shape, a.dtype) for a in slotted] + [SDS((NDEV, *a.shape), a.dtype) for a in whole]
    sems = [pltpu.SemaphoreType.DMA((7 * n,)), pltpu.SemaphoreType.DMA((7 * n,)), pltpu.SemaphoreType.DMA((n,))] if n else []
    return shapes, sems


def _comm_copies(ns, ins, outs, send_sems, recv_sems, local_sems):
    n = len(ins)
    x, y, c = _mesh_pos()
    me = (x, y, c)
    src = lambda p, dev: _slot(ins[p], dev) if p < ns else ins[p]
    copies = [pltpu.make_async_copy(src(p, me), _slot(outs[p], me), local_sems.at[p]) for p in range(n)]
    for k in range(1, NDEV):
        flip = lambda v, bit: 1 - v if (k >> bit) & 1 else v
        peer = (flip(x, 2), flip(y, 1), flip(c, 0))
        for p in range(n):
            copies.append(pltpu.make_async_remote_copy(
                src_ref=src(p, peer), dst_ref=_slot(outs[p], me), send_sem=send_sems.at[7 * p + k - 1],
                recv_sem=recv_sems.at[7 * p + k - 1], device_id=peer, device_id_type=pl.DeviceIdType.MESH))
    return copies


def _comm_hook(comm, cin, cout, sems, step, last):
    if not cin:
        return lambda: None

    @pl.when(step == 0)
    def _():
        for cp in _comm_copies(len(comm[0]), cin, cout, *sems):
            cp.start()

    def finish():
        @pl.when(step == last)
        def _():
            for cp in _comm_copies(len(comm[0]), cin, cout, *sems):
                cp.wait()

    return finish


def _tiled(name, body, L, tm, row_ins, full_ins, row_outs, acc_outs=(), comm=_NO_COMM):
    nr, nf, no, na = len(row_ins), len(full_ins), len(row_outs), len(acc_outs)
    cins = [*comm[0], *comm[1]]
    nc = len(cins)
    cshapes, csems = _comm_shapes(comm)

    def kern(*refs):
        rin = refs[:nr]
        fin = [_WRef(r, w.kind) if isinstance(w, _W) else r for r, w in zip(refs[nr:nr + nf], full_ins)]
        cin = refs[nr + nf:nr + nf + nc]
        outs = refs[nr + nf + nc:]
        rout, aout, cout, sems = outs[:no], outs[no:no + na], outs[no + na:no + na + nc], outs[no + na + nc:]
        finish = _comm_hook(comm, cin, cout, sems, pl.program_id(0), L // tm - 1)
        routs, accs = body([r[...] for r in rin], fin)
        for r, v in zip(rout, routs):
            r[...] = v.astype(r.dtype)
        if aout:
            @pl.when(pl.program_id(0) == 0)
            def _():
                for a in aout:
                    a[...] = jnp.zeros(a.shape, a.dtype)

            for a, v in zip(aout, accs):
                a[...] += v.astype(a.dtype)
        finish()

    return pl.pallas_call(
        kern, name=name, grid=(L // tm,),
        in_specs=[_row(tm, w, c) for (_, w, c) in row_ins] + [_wspec(a) for a in full_ins] + [_ANY] * nc,
        out_specs=[_row(tm, w) for (w, _) in row_outs] + [_full(s) for (s, _) in acc_outs] + [_ANY] * nc,
        out_shape=[SDS((L, w), dt) for (w, dt) in row_outs] + [SDS(tuple(s), dt) for (s, dt) in acc_outs] + cshapes,
        scratch_shapes=csems, compiler_params=_cp(),
    )(*[a for (a, _, _) in row_ins], *[a.arr if isinstance(a, _W) else a for a in full_ins], *cins)


def _norm_mm_fwd(name, x, g, w, tm, comm=_NO_COMM):
    def body(rows, fulls):
        g_ref, w_ref = fulls
        return [_dotw(_rms(rows[0], g_ref[...]), w_ref[...])], []

    n_out = w.shape[1] if len(w.shape) == 2 else w.shape[0] * w.shape[2]
    res = _tiled(name, body, x.shape[0], tm, [(x, D, 0)], [g, w], [(n_out, f32)], comm=comm)
    return res if comm[1] else res[0]


def _norm_mm_bwd(name, x, g, w, dz_parts, dres, tm):
    n = len(dz_parts)

    def body(rows, fulls):
        g_ref, w_ref = fulls
        dz = rows[1] if n == 1 else jnp.concatenate(rows[1:1 + n], axis=1)
        wv = w_ref[...]
        h, vjp = jax.vjp(_rms, rows[0], g_ref[...])
        dx, dg = vjp(_dotw_t(dz, wv))
        return [dx + rows[1 + n]], [dg, _wgrad(h, dz, wv)]

    return _tiled(name, body, x.shape[0], tm, [(x, D, 0), *dz_parts, (dres, D, 0)], [g, w],
                  [(D, f32)], [((1, D), f32), (tuple(w.shape), f32)])


S5_S = S5_T // 8


def _perm_load(refs):
    return jnp.concatenate([jnp.concatenate([r[pl.ds(i, 8, stride=S5_S), :] for i in range(S5_S)], axis=0) for r in refs],
                           axis=1)


def _perm_store(refs, v):
    for c, r in enumerate(refs):
        for i in range(S5_S):
            r[pl.ds(i, 8, stride=S5_S), :] = v[i * 8:(i + 1) * 8, c * 128:(c + 1) * 128]


def _cmul(pr, pi, qr, qi):
    return pr * qr - pi * qi, pr * qi + pi * qr


def _scan_tables(tab_ref, pw_re, pw_im, ar, ai, reverse):
    def powers(br, bi):
        pw = [(br, bi)]
        for _ in range(7):
            pw.append(_cmul(*pw[-1], br, bi))
        return pw

    row = lax.broadcasted_iota(jnp.int32, (8, NST), 0)

    def rows(pw, order):
        vr = jnp.zeros((8, NST), f32)
        vi = jnp.zeros((8, NST), f32)
        for t in range(8):
            vr = jnp.where(row == t, pw[order(t)][0], vr)
            vi = jnp.where(row == t, pw[order(t)][1], vi)
        return vr, vi

    pw = powers(ar, ai)
    base_r, base_i = rows(pw, lambda t: t)
    sr, si = jnp.ones((1, NST), f32), jnp.zeros((1, NST), f32)
    for m in range(S5_S // 8):
        pw_re[m * 8:(m + 1) * 8, :], pw_im[m * 8:(m + 1) * 8, :] = _cmul(base_r, base_i, sr, si)
        sr, si = _cmul(sr, si, *pw[7])
    big = powers(sr, si)
    for k, s in enumerate((1, 2, 4)):
        keep = (row <= 7 - s) if reverse else (row >= s)
        tab_ref[2 * k] = jnp.where(keep, big[s - 1][0], 0.0)
        tab_ref[2 * k + 1] = jnp.where(keep, big[s - 1][1], 0.0)
    tab_ref[6], tab_ref[7] = rows(big, (lambda t: 7 - t) if reverse else (lambda t: t))


def _scan(re_ref, im_ref, row0, a_re, a_im, tab_ref, pw_re, pw_im, car_re, car_im, reverse, fold=None):
    ng = S5_S // 8
    row = lax.broadcasted_iota(jnp.int32, (8, S5_CH), 0)
    accs = []
    for cc in range(NST // S5_CH):
        cols = slice(cc * S5_CH, (cc + 1) * S5_CH)
        ar = jnp.broadcast_to(a_re[:, cols], (8, S5_CH))
        ai = jnp.broadcast_to(a_im[:, cols], (8, S5_CH))

        def local(i, h, cols=cols, ar=ar, ai=ai):
            r0 = pl.multiple_of(row0 + ((S5_S - 1 - i) if reverse else i) * 8, 8)
            hr = ar * h[0] - ai * h[1] + re_ref[pl.ds(r0, 8), cols]
            hi = ar * h[1] + ai * h[0] + im_ref[pl.ds(r0, 8), cols]
            re_ref[pl.ds(r0, 8), cols] = hr
            im_ref[pl.ds(r0, 8), cols] = hi
            return hr, hi

        xr, xi = lax.fori_loop(0, S5_S, local, (jnp.zeros((8, S5_CH), f32), jnp.zeros((8, S5_CH), f32)))

        old_r, old_i = car_re[:, cols], car_im[:, cols]
        for k, s in enumerate((1, 2, 4)):
            yr = pltpu.roll(xr, (8 - s) if reverse else s, 0)
            yi = pltpu.roll(xi, (8 - s) if reverse else s, 0)
            dr, di = _cmul(tab_ref[2 * k, :, cols], tab_ref[2 * k + 1, :, cols], yr, yi)
            xr, xi = xr + dr, xi + di
        dr, di = _cmul(tab_ref[6, :, cols], tab_ref[7, :, cols], old_r, old_i)
        xr, xi = xr + dr, xi + di
        b = 0 if reverse else 7
        car_re[:, cols] = jnp.broadcast_to(xr[b:b + 1, :], (8, S5_CH))
        car_im[:, cols] = jnp.broadcast_to(xi[b:b + 1, :], (8, S5_CH))
        edge = 7 if reverse else 0
        cr = jnp.where(row == edge, old_r, pltpu.roll(xr, 7 if reverse else 1, 0))
        ci = jnp.where(row == edge, old_i, pltpu.roll(xi, 7 if reverse else 1, 0))

        def fix(g, acc, cols=cols, cr=cr, ci=ci):
            gg = (ng - 1 - g) if reverse else g
            pg = pl.multiple_of(g * 8, 8)
            ptr, pti = pw_re[pl.ds(pg, 8), cols], pw_im[pl.ds(pg, 8), cols]
            for u in range(8):
                uu = (7 - u) if reverse else u
                i = gg * 8 + uu
                r0 = pl.multiple_of(row0 + i * 8, 8)
                pr = jnp.broadcast_to(ptr[u:u + 1, :], (8, S5_CH))
                pi = jnp.broadcast_to(pti[u:u + 1, :], (8, S5_CH))
                dr, di = _cmul(pr, pi, cr, ci)
                hr = re_ref[pl.ds(r0, 8), cols] + dr
                hi = im_ref[pl.ds(r0, 8), cols] + di
                re_ref[pl.ds(r0, 8), cols] = hr
                im_ref[pl.ds(r0, 8), cols] = hi
                if fold is not None:
                    acc = fold(cols, i, hr, hi, acc)
            return acc

        acc0 = (jnp.zeros((8, S5_CH), f32), jnp.zeros((8, S5_CH), f32)) if fold is not None else 0
        accs.append(lax.fori_loop(0, ng, fix, acc0))
    return accs


def _s5_fwd(z, bmat, cmat, a_re, a_im, d_skip, comm=_NO_COMM):
    L = z.shape[0]
    nb = L // S5_T
    cins = [*comm[0], *comm[1]]
    nc = len(cins)
    cshapes, csems = _comm_shapes(comm)

    def kern(*refs):
        u_refs, (bre, bim, cre, cim, ar_ref, ai_ref, d_ref), rest = refs[:4], refs[4:11], refs[11:]
        cin, y_refs, (hbr_ref, hbi_ref), cout = rest[:nc], rest[nc:nc + 4], rest[nc + 4:nc + 6], rest[nc + 6:2 * nc + 6]
        h_re, h_im, tab, pw_re, pw_im, car_re, car_im, *sems = rest[2 * nc + 6:]
        finish = _comm_hook(comm, cin, cout, sems, pl.program_id(0), nb - 1)

        @pl.when(pl.program_id(0) == 0)
        def _():
            _scan_tables(tab, pw_re, pw_im, ar_ref[...], ai_ref[...], False)
            car_re[...] = jnp.zeros_like(car_re)
            car_im[...] = jnp.zeros_like(car_im)

        hbr_ref[...] = car_re[...]
        hbi_ref[...] = car_im[...]
        u = _perm_load(u_refs)
        for hf in range(2):
            uh = u[:, hf * 256:(hf + 1) * 256]
            h_re[:, hf * 1024:(hf + 1) * 1024] = _dot(uh, bre[hf], NN)
            h_im[:, hf * 1024:(hf + 1) * 1024] = _dot(uh, bim[hf], NN)
        _scan(h_re, h_im, 0, ar_ref[...], ai_ref[...], tab, pw_re, pw_im, car_re, car_im, False)
        ys = []
        for hf in range(2):
            cs = slice(hf * 1024, (hf + 1) * 1024)
            ys.append(_dot(h_re[:, cs], cre[hf], NT) - _dot(h_im[:, cs], cim[hf], NT))
        _perm_store(y_refs, jnp.concatenate(ys, axis=1) + d_ref[...] * u)
        finish()

    fulls = [*bmat, *cmat, a_re, a_im, d_skip]
    return pl.pallas_call(
        kern, name="s5_fwd", grid=(nb,),
        in_specs=[_row(S5_T, 128, c) for c in range(4)] + [_full(a.shape) for a in fulls] + [_ANY] * nc,
        out_specs=[_row(S5_T, 128)] * 4 + [_row(8, NST), _row(8, NST)] + [_ANY] * nc,
        out_shape=[SDS((L, 128), f32)] * 4 + [SDS((nb * 8, NST), f32), SDS((nb * 8, NST), f32)] + cshapes,
        scratch_shapes=[pltpu.VMEM((S5_T, NST), f32), pltpu.VMEM((S5_T, NST), f32), pltpu.VMEM((8, 8, NST), f32),
                        pltpu.VMEM((S5_S, NST), f32), pltpu.VMEM((S5_S, NST), f32),
                        pltpu.VMEM((8, NST), f32), pltpu.VMEM((8, NST), f32)] + csems,
        compiler_params=_cp(),
    )(z, z, z, z, *fulls, *cins)


def _s5_bwd(z, dy, hb_re, hb_im, bmat, cmat, a_re, a_im, d_skip, comm=_NO_COMM):
    L = z.shape[0]
    nb = L // S5_T
    T = S5_T
    cins = [*comm[0], *comm[1]]
    nc = len(cins)
    cshapes, csems = _comm_shapes(comm)

    def kern(*refs):
        u_refs, dy_refs, (hbr_ref, hbi_ref, bre, bim, cre, cim, ar_ref, ai_ref, d_ref), rest = \
            refs[:4], refs[4:8], refs[8:17], refs[17:]
        cin, du_refs, (dbre, dbim, dcre, dcim, dar, dai, dd), cout = \
            rest[:nc], rest[nc:nc + 4], rest[nc + 4:nc + 11], rest[nc + 11:2 * nc + 11]
        h_re, h_im, g_re, g_im, tabf, pwf_re, pwf_im, tabr, pwr_re, pwr_im, car_re, car_im, gcar_re, gcar_im, *sems = \
            rest[2 * nc + 11:]
        finish = _comm_hook(comm, cin, cout, sems, pl.program_id(0), nb - 1)

        @pl.when(pl.program_id(0) == 0)
        def _():
            _scan_tables(tabf, pwf_re, pwf_im, ar_ref[...], ai_ref[...], False)
            _scan_tables(tabr, pwr_re, pwr_im, ar_ref[...], -ai_ref[...], True)
            gcar_re[...] = jnp.zeros_like(gcar_re)
            gcar_im[...] = jnp.zeros_like(gcar_im)
            for r in (dbre, dbim, dcre, dcim, dar, dai, dd):
                r[...] = jnp.zeros_like(r)

        u = _perm_load(u_refs)
        dyv = _perm_load(dy_refs)
        car_re[...] = hbr_ref[...]
        car_im[...] = hbi_ref[...]
        for hf in range(2):
            cs = slice(hf * 1024, (hf + 1) * 1024)
            uh = u[:, hf * 256:(hf + 1) * 256]
            dyh = dyv[:, hf * 256:(hf + 1) * 256]
            h_re[8:, cs] = _dot(uh, bre[hf], NN)
            h_im[8:, cs] = _dot(uh, bim[hf], NN)
            g_re[:, cs] = _dot(dyh, cre[hf], NN)
            g_im[:, cs] = -_dot(dyh, cim[hf], NN)
        _scan(h_re, h_im, 8, ar_ref[...], ai_ref[...], tabf, pwf_re, pwf_im, car_re, car_im, False)
        row = lax.broadcasted_iota(jnp.int32, (8, NST), 0)
        h_re[0:8, :] = jnp.where(row == 0, hbr_ref[...], pltpu.roll(h_re[T:T + 8, :], 1, 0))
        h_im[0:8, :] = jnp.where(row == 0, hbi_ref[...], pltpu.roll(h_im[T:T + 8, :], 1, 0))

        def fold_da(cols, i, gr, gi, acc):
            r0 = pl.multiple_of(i * 8, 8)
            hr, hi = h_re[pl.ds(r0, 8), cols], h_im[pl.ds(r0, 8), cols]
            return acc[0] + gr * hr + gi * hi, acc[1] + gi * hr - gr * hi

        accs = _scan(g_re, g_im, 0, ar_ref[...], -ai_ref[...], tabr, pwr_re, pwr_im, gcar_re, gcar_im, True, fold=fold_da)
        for cc, (acc_r, acc_i) in enumerate(accs):
            cols = slice(cc * S5_CH, (cc + 1) * S5_CH)
            dar[:, cols] += jnp.sum(acc_r, axis=0, keepdims=True)
            dai[:, cols] += jnp.sum(acc_i, axis=0, keepdims=True)

        dus = []
        for hf in range(2):
            cs = slice(hf * 1024, (hf + 1) * 1024)
            uh = u[:, hf * 256:(hf + 1) * 256]
            dyh = dyv[:, hf * 256:(hf + 1) * 256]
            gr = g_re[:, cs]
            gi = g_im[:, cs]
            dus.append(_dot(gr, bre[hf], NT) + _dot(gi, bim[hf], NT))
            dbre[hf] += _dot(uh, gr, TN)
            dbim[hf] += _dot(uh, gi, TN)
            dcre[hf] += _dot(dyh, h_re[8:, cs], TN)
            dcim[hf] -= _dot(dyh, h_im[8:, cs], TN)
        _perm_store(du_refs, jnp.concatenate(dus, axis=1) + d_ref[...] * dyv)
        dd[...] += jnp.sum(dyv * u, axis=0, keepdims=True)
        finish()

    rev = lambda c: pl.BlockSpec((T, 128), lambda k: (nb - 1 - k, c))
    rev8 = pl.BlockSpec((8, NST), lambda k: (nb - 1 - k, 0))
    fulls = [*bmat, *cmat, a_re, a_im, d_skip]
    mat = ((2, 256, 1024), f32)
    accs = [mat, mat, mat, mat, ((1, NST), f32), ((1, NST), f32), ((1, SW), f32)]
    tables = [pltpu.VMEM((8, 8, NST), f32), pltpu.VMEM((S5_S, NST), f32), pltpu.VMEM((S5_S, NST), f32)]
    return pl.pallas_call(
        kern, name="s5_bwd", grid=(nb,),
        in_specs=[rev(c) for c in range(4)] * 2 + [rev8, rev8] + [_full(a.shape) for a in fulls] + [_ANY] * nc,
        out_specs=[rev(0)] * 4 + [_full(s) for (s, _) in accs] + [_ANY] * nc,
        out_shape=[SDS((L, 128), f32)] * 4 + [SDS(s, dt) for (s, dt) in accs] + cshapes,
        scratch_shapes=[pltpu.VMEM((T + 8, NST), f32), pltpu.VMEM((T + 8, NST), f32),
                        pltpu.VMEM((T, NST), f32), pltpu.VMEM((T, NST), f32), *tables, *tables,
                        pltpu.VMEM((8, NST), f32), pltpu.VMEM((8, NST), f32),
                        pltpu.VMEM((8, NST), f32), pltpu.VMEM((8, NST), f32)] + csems,
        compiler_params=_cp(),
    )(z, z, z, z, dy, dy, dy, dy, hb_re, hb_im, *fulls, *cins)


def _s5_disc(lam_re, lam_im, log_dt, bt_re, bt_im):
    dt = jnp.exp(log_dt)
    mag = jnp.exp(lam_re * dt)
    ab_re = mag * jnp.cos(lam_im * dt)
    ab_im = mag * jnp.sin(lam_im * dt)
    den = lam_re * lam_re + lam_im * lam_im
    nr = ab_re - 1.0
    coef_re = (nr * lam_re + ab_im * lam_im) / den
    coef_im = (ab_im * lam_re - nr * lam_im) / den
    rep = lambda c: jnp.broadcast_to(c[:, None, :], (NG, GS, NS)).reshape(NG * GS, NS)
    cr, ci = rep(coef_re), rep(coef_im)
    return ab_re, ab_im, cr * bt_re - ci * bt_im, cr * bt_im + ci * bt_re


def _s5_disc_fwd(*params):
    def kern(*refs):
        outs = _s5_disc(*[r[...] for r in refs[:5]])
        for r, v in zip(refs[5:], outs):
            r[...] = v

    shapes = [SDS((NG, NS), f32), SDS((NG, NS), f32), SDS((NG * GS, NS), f32), SDS((NG * GS, NS), f32)]
    return pl.pallas_call(kern, name="s5_disc_fwd", out_shape=shapes)(*params)


def _s5_disc_bwd(params, cts):
    def kern(*refs):
        _, vjp = jax.vjp(_s5_disc, *[r[...] for r in refs[:5]])
        for r, v in zip(refs[9:], vjp(tuple(r[...] for r in refs[5:9]))):
            r[...] = v

    return pl.pallas_call(kern, name="s5_disc_bwd", out_shape=[SDS(p.shape, f32) for p in params])(*params, *cts)


def _diag_mask():
    r = np.arange(256)[:, None] // GS
    c = np.arange(1024)[None, :] // NS
    return jnp.asarray(r == c, f32)


def _to_blockdiag(m):
    return jnp.tile(m.reshape(2, 256, NS), (1, 1, 16)) * _diag_mask()


def _from_blockdiag(dm):
    return (dm * _diag_mask()).reshape(2, 256, 16, NS).sum(axis=2).reshape(NG * GS, NS)


def _conv_shifts(win, sh, tm):
    win[tm + 32:, :] = jnp.zeros((8, CW), f32)
    for s in range(8):
        sh[s] = win[s:s + tm + 32, :]


def _conv_taps(offset, w_ref, sh, r, accs):
    for k in range(CK):
        s, q = offset(k) % 8, offset(k) // 8
        wk = w_ref[k:k + 1, :]
        accs = [acc + wk * sh[s, pl.ds(pl.multiple_of(r + 8 * (a + q), 8), 8), :] for a, acc in enumerate(accs)]
    return accs


def _conv_fwd(z, w32, b, tm, comm=_NO_COMM):
    L = z.shape[0]
    hb = tm // 32
    cins = [*comm[0], *comm[1]]
    nc = len(cins)
    cshapes, csems = _comm_shapes(comm)

    def kern(val, glu, hval, hglu, w_ref, b_ref, *rest):
        cin, c_ref, cout, (win, sh, *sems) = rest[:nc], rest[nc], rest[nc + 1:2 * nc + 1], rest[2 * nc + 1:]
        i = pl.program_id(0)
        finish = _comm_hook(comm, cin, cout, sems, i, L // tm - 1)
        win[0:32, :] = hval[...] * jax.nn.sigmoid(hglu[...]) * (i > 0).astype(f32)
        win[32:tm + 32, :] = val[...] * jax.nn.sigmoid(glu[...])
        _conv_shifts(win, sh, tm)

        def strip(j, carry):
            r = j * 16
            accs = _conv_taps(lambda k: k + 2, w_ref, sh, r, [jnp.zeros((8, CW), f32) + b_ref[...]] * 2)
            c_ref[pl.ds(pl.multiple_of(r, 16), 16), :] = jnp.concatenate(accs, axis=0)
            return carry

        lax.fori_loop(0, tm // 16, strip, 0)
        finish()

    halo = lambda col: pl.BlockSpec((32, CW), lambda i: (jnp.maximum(i * hb - 1, 0), col))
    return pl.pallas_call(
        kern, name="conv_fwd", grid=(L // tm,),
        in_specs=[_row(tm, CW, 2), _row(tm, CW, 3), halo(2), halo(3), _full(w32.shape), _full(b.shape)] + [_ANY] * nc,
        out_specs=[_row(tm, CW)] + [_ANY] * nc, out_shape=[SDS((L, CW), f32)] + cshapes,
        scratch_shapes=[pltpu.VMEM((tm + 40, CW), f32), pltpu.VMEM((8, tm + 32, CW), f32)] + csems, compiler_params=_cp(),
    )(z, z, z, z, w32, b, *cins)


def _conv_bwd(z, dc, w32, tm):
    L = z.shape[0]
    hb = tm // 32
    nt = L // tm

    def kern(val, glu, hval, hglu, dc_ref, hdc, w_ref, dval_ref, dglu_ref, dw_ref, db_ref, win, sh, dwin, dsh):
        i = pl.program_id(0)

        @pl.when(i == 0)
        def _():
            dw_ref[...] = jnp.zeros_like(dw_ref)
            db_ref[...] = jnp.zeros_like(db_ref)

        win[0:32, :] = hval[...] * jax.nn.sigmoid(hglu[...]) * (i > 0).astype(f32)
        win[32:tm + 32, :] = val[...] * jax.nn.sigmoid(glu[...])
        _conv_shifts(win, sh, tm)
        dwin[0:tm, :] = dc_ref[...]
        dwin[tm:tm + 32, :] = hdc[...] * (i < nt - 1).astype(f32)
        _conv_shifts(dwin, dsh, tm)

        def strip(j, carry):
            r = j * 16
            rows = pl.ds(pl.multiple_of(r, 16), 16)
            dv = jnp.concatenate(_conv_taps(lambda k: 30 - k, w_ref, dsh, r, [jnp.zeros((8, CW), f32)] * 2), axis=0)
            sg = jax.nn.sigmoid(glu[rows, :])
            dval_ref[rows, :] = dv * sg
            dglu_ref[rows, :] = dv * val[rows, :] * sg * (1.0 - sg)
            return carry

        lax.fori_loop(0, tm // 16, strip, 0)

        for k0 in range(0, CK, 2):
            taps = [k for k in (k0, k0 + 1) if k < CK]

            def rows64(j, accs, taps=taps):
                for u in range(8):
                    r = pl.multiple_of(j * 64 + u * 8, 8)
                    dcv = dc_ref[pl.ds(r, 8), :]
                    accs = [acc + dcv * sh[(k + 2) % 8, pl.ds(pl.multiple_of(r + 8 * ((k + 2) // 8), 8), 8), :]
                            for k, acc in zip(taps, accs)]
                return accs

            accs = lax.fori_loop(0, tm // 64, rows64, [jnp.zeros((8, CW), f32)] * len(taps))
            for k, acc in zip(taps, accs):
                dw_ref[k:k + 1, :] += jnp.sum(acc, axis=0, keepdims=True)
        db_ref[...] += jnp.sum(dc_ref[...], axis=0, keepdims=True)

    halo = lambda col: pl.BlockSpec((32, CW), lambda i: (jnp.maximum(i * hb - 1, 0), col))
    nxt = pl.BlockSpec((32, CW), lambda i: (jnp.minimum((i + 1) * hb, L // 32 - 1), 0))
    return pl.pallas_call(
        kern, name="conv_bwd", grid=(nt,),
        in_specs=[_row(tm, CW, 2), _row(tm, CW, 3), halo(2), halo(3), _row(tm, CW), nxt, _full(w32.shape)],
        out_specs=[_row(tm, CW), _row(tm, CW), _full((32, CW)), _full((1, CW))],
        out_shape=[SDS((L, CW), f32), SDS((L, CW), f32), SDS((32, CW), f32), SDS((1, CW), f32)],
        scratch_shapes=[pltpu.VMEM((tm + 40, CW), f32), pltpu.VMEM((8, tm + 32, CW), f32),
                        pltpu.VMEM((tm + 40, CW), f32), pltpu.VMEM((8, tm + 32, CW), f32)], compiler_params=_cp(),
    )(z, z, z, z, dc, dc, w32)


def _ev_out_f(y, c, ga, gb, tap, glu_b, ln_g, ln_b, glu_w, w_out, mm):
    z1 = jax.nn.gelu(y)
    ya = z1 * jax.nn.sigmoid(mm(z1, glu_w) + glu_b + tap) * jax.nn.silu(ga)
    mu = jnp.mean(c, axis=-1, keepdims=True)
    var = jnp.mean(jnp.square(c - mu), axis=-1, keepdims=True)
    cn = (c - mu) * lax.rsqrt(var + EPS) * ln_g + ln_b
    cat = jnp.concatenate([ya, jax.nn.silu(cn) * jax.nn.silu(gb)], axis=1)
    return mm(cat, w_out), (z1, cat)


def _ev_out_fwd(x, y, c, z, glu_b, ln_g, ln_b, glu_w, w_out, tm, comm=_NO_COMM):
    def body(rows, fulls):
        xv, cv, ga, gb = rows[:4]
        yv = jnp.concatenate(rows[4:], axis=1)
        gb_ref, lg_ref, lb_ref, gw_ref, wo_ref = fulls
        out, _ = _ev_out_f(yv, cv, ga, gb, 0.0, gb_ref[...], lg_ref[...], lb_ref[...], gw_ref[...], wo_ref[...], _plain_mm)
        return [xv + out], []

    return _tiled("ev_out_fwd", body, x.shape[0], tm, [(x, D, 0), (c, CW, 0), (z, SW, 1), (z, CW, 4), *[(t, 128, 0) for t in y]],
                  [glu_b, ln_g, ln_b, glu_w, w_out], [(D, f32)], comm=comm)


def _ev_out_bwd(dx1, y, c, z, glu_b, ln_g, ln_b, glu_w, w_out, tm):
    def body(rows, fulls):
        dxv, cv, ga, gb = rows[:4]
        yv = jnp.concatenate(rows[4:], axis=1)
        gb_ref, lg_ref, lb_ref, gw_ref, wo_ref = fulls
        gw, wo = gw_ref[...], wo_ref[...]
        f = lambda yv, cv, ga, gb, tap, b, lg, lb: _ev_out_f(yv, cv, ga, gb, tap, b, lg, lb, gw, wo, _mm)
        _, vjp, (z1, cat) = jax.vjp(f, yv, cv, ga, gb, jnp.zeros((tm, SW), f32), gb_ref[...], lg_ref[...], lb_ref[...],
                                    has_aux=True)
        dy, dc, dga, dgb, dtap, db, dlg, dlb = vjp(dxv)
        return [dy, dc, dga, dgb], [db, dlg, dlb, _wgrad(z1, dtap, gw), _wgrad(cat, dxv, wo)]

    return _tiled("ev_out_bwd", body, dx1.shape[0], tm, [(dx1, D, 0), (c, CW, 0), (z, SW, 1), (z, CW, 4), *[(t, 128, 0) for t in y]],
                  [glu_b, ln_g, ln_b, glu_w, w_out], [(SW, f32), (CW, f32), (SW, f32), (CW, f32)],
                  [((1, SW), f32), ((1, CW), f32), ((1, CW), f32), (glu_w.shape, f32), (w_out.shape, f32)])


def _kv_f(mem, g, tap0, tap1, w0, w1, mm):
    mn = _rms(mem, g)
    return (mm(mn, w0) + tap0, mm(mn, w1) + tap1), mn


def _kv_bwd(mem, g, w0, w1, dkv0, dkv1):
    def body(rows, fulls):
        g_ref, w0_ref, w1_ref = fulls
        w0v, w1v = w0_ref[...], w1_ref[...]
        tap = jnp.zeros((MEM_LEN, 2 * D), f32)
        _, vjp, mn = jax.vjp(lambda g_, t0, t1: _kv_f(rows[0], g_, t0, t1, w0v, w1v, _mm), g_ref[...], tap, tap, has_aux=True)
        dg, d0, d1 = vjp((rows[1], rows[2]))
        return [], [dg, _wgrad(mn, d0, w0v), _wgrad(mn, d1, w1v)]

    return _tiled("kv_bwd", body, MEM_LEN, MEM_LEN, [(mem, D, 0), (dkv0, 2 * D, 0), (dkv1, 2 * D, 0)], [g, w0, w1], [],
                  [((1, D), f32), (w0.shape, f32), (w1.shape, f32)])


def _xa_f(x, g, kv, tap, w_qg, w_o, mm, nt, nn):
    h = _rms(x, g)
    qg = mm(h, w_qg) + tap
    outs = []
    for hd in range(XH):
        q = qg[:, hd * XD:(hd + 1) * XD]
        k = kv[:, hd * XD:(hd + 1) * XD]
        v = kv[:, D + hd * XD:D + (hd + 1) * XD]
        s = nt(q, k) * (XD ** -0.5)
        e = jnp.exp(s - jnp.max(s, axis=-1, keepdims=True))
        outs.append(nn(e / jnp.sum(e, axis=-1, keepdims=True), v))
    u = jnp.concatenate(outs, axis=1) * jax.nn.silu(qg[:, D:])
    return mm(u, w_o), (h, u)


def _xa_fwd(name, x, g, kv, w_qg, w_o, tm):
    def body(rows, fulls):
        g_ref, kv_ref, wq_ref, wo_ref = fulls
        out, _ = _xa_f(rows[0], g_ref[...], kv_ref[...], 0.0, wq_ref[...], wo_ref[...], _plain_mm, _plain_nt, _plain_mm)
        return [rows[0] + out], []

    return _tiled(name, body, x.shape[0], tm, [(x, D, 0)], [g, kv, w_qg, w_o], [(D, f32)])[0]


def _xa_bwd(name, x, dxo, g, kv, w_qg, w_o, tm, comm=_NO_COMM):
    def body(rows, fulls):
        xv, dxv = rows
        g_ref, kv_ref, wq_ref, wo_ref = fulls
        wq, wo = wq_ref[...], wo_ref[...]
        f = lambda xv, gv, kvv, tap: _xa_f(xv, gv, kvv, tap, wq, wo, _mm, _mm_nt, _mm_nn)
        _, vjp, (h, u) = jax.vjp(f, xv, g_ref[...], kv_ref[...], jnp.zeros((tm, 2 * D), f32), has_aux=True)
        dx, dg, dkv, dtap = vjp(dxv)
        return [dx + dxv], [dg, dkv, _wgrad(h, dtap, wq), _wgrad(u, dxv, wo)]

    return _tiled(name, body, x.shape[0], tm, [(x, D, 0), (dxo, D, 0)], [g, kv, w_qg, w_o], [(D, f32)],
                  [((1, D), f32), ((MEM_LEN, 2 * D), f32), (w_qg.shape, f32), (w_o.shape, f32)], comm=comm)


def _attn_pad(k_ref, v_ref, kpad, vpad):
    kpad[0:PAD, :] = jnp.zeros((PAD, 128), bf16)
    vpad[0:PAD, :] = jnp.zeros((PAD, 128), bf16)
    kpad[PAD:, :] = k_ref[...].astype(bf16)
    vpad[PAD:, :] = v_ref[...].astype(bf16)


def _attn_exp(qm, kwin, bm, mask_row):
    x = _dot(qm, kwin, NT) + bm + mask_row
    e = jnp.exp(x - jnp.max(x, axis=-1, keepdims=True))
    return e, 1.0 / jnp.sum(e, axis=-1, keepdims=True)


def _attn_fwd(qkvg, bias, comm=_NO_COMM):
    L = qkvg.shape[0]
    nq = L // AQ
    cins = [*comm[0], *comm[1]]
    nc = len(cins)
    cshapes, csems = _comm_shapes(comm)

    def kern(q_ref, k_ref, v_ref, b_ref, *rest):
        cin, o_ref, cout, (kpad, vpad, *sems) = rest[:nc], rest[nc], rest[nc + 1:2 * nc + 1], rest[2 * nc + 1:]
        i = pl.program_id(1)
        finish = _comm_hook(comm, cin, cout, sems, pl.program_id(0) * nq + i, HEADS // 2 * nq - 1)

        @pl.when(i == 0)
        def _():
            _attn_pad(k_ref, v_ref, kpad, vpad)

        r0 = pl.multiple_of(i * AQ, AQ)
        q = q_ref[...] * (DH ** -0.5)
        kwin = kpad[pl.ds(r0, AW), :]
        vwin = vpad[pl.ds(r0, AW), :]
        lane_hi = lax.broadcasted_iota(jnp.int32, (AQ, 128), 1) // DH
        mask_row = jnp.where(lax.broadcasted_iota(jnp.int32, (1, AW), 1) >= PAD - r0, 0.0, -1e30)
        outs = []
        for hh in range(2):
            e, inv = _attn_exp(jnp.where(lane_hi == hh, q, 0.0), kwin, b_ref[hh], mask_row)
            outs.append(_dot(e, vwin, NN) * inv)
        o_ref[...] = jnp.where(lane_hi == 0, outs[0], outs[1])
        finish()

    return pl.pallas_call(
        kern, name="attn_fwd", grid=(HEADS // 2, nq),
        in_specs=[pl.BlockSpec((AQ, 128), lambda j, i: (i, j)),
                  pl.BlockSpec((L, 128), lambda j, i: (0, 8 + j)),
                  pl.BlockSpec((L, 128), lambda j, i: (0, 16 + j)),
                  pl.BlockSpec((2, AQ, AW), lambda j, i: (j, 0, 0))] + [_ANY] * nc,
        out_specs=[pl.BlockSpec((AQ, 128), lambda j, i: (i, j))] + [_ANY] * nc,
        out_shape=[SDS((L, D), f32)] + cshapes,
        scratch_shapes=[pltpu.VMEM((L + PAD, 128), bf16), pltpu.VMEM((L + PAD, 128), bf16)] + csems,
        compiler_params=_cp(("arbitrary", "arbitrary")),
    )(qkvg, qkvg, qkvg, bias, *cins)


def _attn_bwd(qkvg, do, bias):
    L = qkvg.shape[0]
    nq = L // AQ

    def kern(q_ref, k_ref, v_ref, b_ref, do_ref, dq_ref, dk_ref, dv_ref, db_ref, kpad, vpad, dkp, dvp):
        i = pl.program_id(1)

        @pl.when(i == 0)
        def _():
            _attn_pad(k_ref, v_ref, kpad, vpad)
            dkp[...] = jnp.zeros_like(dkp)
            dvp[...] = jnp.zeros_like(dvp)
            db_ref[...] = jnp.zeros_like(db_ref)

        r0 = pl.multiple_of(i * AQ, AQ)
        q = q_ref[...] * (DH ** -0.5)
        dov = do_ref[...]
        kwin = kpad[pl.ds(r0, AW), :]
        vwin = vpad[pl.ds(r0, AW), :]
        lane_hi = lax.broadcasted_iota(jnp.int32, (AQ, 128), 1) // DH
        mask_row = jnp.where(lax.broadcasted_iota(jnp.int32, (1, AW), 1) >= PAD - r0, 0.0, -1e30)
        dqs = []
        dk = jnp.zeros((AW, 128), f32)
        dv = jnp.zeros((AW, 128), f32)
        for hh in range(2):
            qm = jnp.where(lane_hi == hh, q, 0.0).astype(bf16)
            dom = jnp.where(lane_hi == hh, dov, 0.0).astype(bf16)
            dp = _dot(dom, vwin, NT)
            e, inv = _attn_exp(qm, kwin, b_ref[hh], mask_row)
            p = e * inv
            ds = p * (dp - jnp.sum(p * dp, axis=-1, keepdims=True))
            db_ref[hh] += ds
            dsb = ds.astype(bf16)
            dqs.append(_dot(dsb, kwin, NN) * (DH ** -0.5))
            dk = dk + _dot(dsb, qm, TN)
            dv = dv + _dot(p, dom, TN)
        dq_ref[...] = jnp.where(lane_hi == 0, dqs[0], dqs[1])
        dkp[pl.ds(r0, AW), :] += dk
        dvp[pl.ds(r0, AW), :] += dv

        @pl.when(i == nq - 1)
        def _():
            dk_ref[...] = dkp[PAD:, :]
            dv_ref[...] = dvp[PAD:, :]

    return pl.pallas_call(
        kern, name="attn_bwd", grid=(HEADS // 2, nq),
        in_specs=[pl.BlockSpec((AQ, 128), lambda j, i: (i, j)),
                  pl.BlockSpec((L, 128), lambda j, i: (0, 8 + j)),
                  pl.BlockSpec((L, 128), lambda j, i: (0, 16 + j)),
                  pl.BlockSpec((2, AQ, AW), lambda j, i: (j, 0, 0)),
                  pl.BlockSpec((AQ, 128), lambda j, i: (i, j))],
        out_specs=[pl.BlockSpec((AQ, 128), lambda j, i: (i, j)),
                   pl.BlockSpec((L, 128), lambda j, i: (0, j)),
                   pl.BlockSpec((L, 128), lambda j, i: (0, j)),
                   pl.BlockSpec((2, AQ, AW), lambda j, i: (j, 0, 0))],
        out_shape=[SDS((L, D), f32), SDS((L, D), f32), SDS((L, D), f32), SDS((HEADS, AQ, AW), f32)],
        scratch_shapes=[pltpu.VMEM((L + PAD, 128), bf16), pltpu.VMEM((L + PAD, 128), bf16),
                        pltpu.VMEM((L + PAD, 128), f32), pltpu.VMEM((L + PAD, 128), f32)],
        compiler_params=_cp(("arbitrary", "arbitrary")),
    )(qkvg, qkvg, qkvg, bias, do)


_SKEW = AQ + AW - 1


def _rel_index():
    d = (AW - 1) - np.arange(_SKEW)
    return np.clip(d, -MAX_REL, MAX_REL) + MAX_REL


def _band_mask():
    qc = np.arange(AQ)[:, None] // CHUNK + LEFT
    kc = np.arange(AW)[None, :] // CHUNK
    return jnp.asarray(np.where((kc <= qc) & (kc >= qc - LEFT), 0.0, -1e30), f32)


def _bias_matrix(rel_bias):
    tv = jnp.take(rel_bias, jnp.asarray(_rel_index()), axis=1)
    flat = jnp.tile(jnp.pad(tv, ((0, 0), (0, 1))), (1, AQ))
    top = flat[:, AQ - 1:AQ - 1 + AQ * _SKEW].reshape(HEADS, AQ, _SKEW)[:, :, :AW]
    return top + _band_mask()


def _bias_grad(dbias, onehot):
    flat = jnp.pad(dbias, ((0, 0), (0, 0), (0, _SKEW - AW))).reshape(HEADS, AQ * _SKEW)
    z = jnp.pad(flat, ((0, 0), (AQ - 1, AQ * 1024 - AQ * _SKEW - (AQ - 1)))).reshape(HEADS, AQ, 1024)

    def kern(z_ref, oh_ref, out_ref):
        diag = jnp.sum(z_ref[...], axis=1)
        out_ref[...] = jnp.dot(diag, oh_ref[...], preferred_element_type=f32, precision=lax.Precision.HIGHEST)

    return pl.pallas_call(kern, name="bias_grad", out_shape=SDS((HEADS, 2 * MAX_REL + 1), f32),
                          compiler_params=_cp(()))(z, onehot)


def _rel_onehot():
    oh = np.zeros((1024, 2 * MAX_REL + 1), np.float32)
    oh[np.arange(_SKEW), _rel_index()] = 1.0
    return jnp.asarray(oh)


def _gated_out_fwd(name, x, o, gate_src, gate_col, w, tm):
    def body(rows, fulls):
        xv, ov, gv = rows
        return [xv + _dotw(ov * jax.nn.silu(gv), fulls[0][...])], []

    return _tiled(name, body, x.shape[0], tm, [(x, D, 0), (o, D, 0), (gate_src, D, gate_col)], [w], [(D, f32)])[0]


def _gated_out_bwd(name, dxo, o, gate_src, gate_col, w, tm):
    def body(rows, fulls):
        dxv, ov, gv = rows
        wv = fulls[0][...]
        u, vjp = jax.vjp(lambda ov, gv: ov * jax.nn.silu(gv), ov, gv)
        do, dg = vjp(_dotw_t(dxv, wv))
        return [do, dg], [_wgrad(u, dxv, wv)]

    return _tiled(name, body, dxo.shape[0], tm, [(dxo, D, 0), (o, D, 0), (gate_src, D, gate_col)], [w],
                  [(D, f32), (D, f32)], [(w.shape, f32)])


def _loss_head(x, g, target, tm):
    def body(rows, fulls):
        xv, tv = rows

        def f(xv, gv):
            e = jnp.square(_rms(xv, gv) - tv)
            return 0.5 * jnp.sum(jnp.mean(e, axis=-1, keepdims=True), axis=0, keepdims=True)

        loss, vjp = jax.vjp(f, xv, fulls[0][...])
        dx, dg = vjp(jnp.ones((1, 1), f32))
        return [dx], [jnp.broadcast_to(loss, (1, 128)), dg]

    return _tiled("loss_head", body, x.shape[0], tm, [(x, D, 0), (target, D, 0)], [g], [(D, f32)],
                  [((1, 128), f32), ((1, D), f32)])


def _local_step(x, mem, target, p, shards, views, pack_early, pack_mid):
    TM, TMF = 256, 512
    row = lambda v: v.reshape(1, -1)

    bt = lambda b: jnp.transpose(b, (0, 2, 1)).reshape(NG * GS, NS)
    disc_in = (p["ev_s5_lambda_re"], p["ev_s5_lambda_im"], p["ev_s5_log_dt"].reshape(NG, 1),
               bt(p["ev_s5_b_re"]), bt(p["ev_s5_b_im"]))
    ab_re, ab_im, bbt_re, bbt_im = _s5_disc_fwd(*disc_in)
    bmat = (_to_blockdiag(bbt_re).astype(bf16), _to_blockdiag(bbt_im).astype(bf16))
    cmat = (_to_blockdiag(p["ev_s5_c_re"].reshape(NG * GS, NS)).astype(bf16),
            _to_blockdiag(p["ev_s5_c_im"].reshape(NG * GS, NS)).astype(bf16))
    a_re, a_im = ab_re.reshape(1, NST), ab_im.reshape(1, NST)
    d_skip = row(p["ev_s5_d"])
    w32 = jnp.pad(p["ev_conv_w"], ((0, 1), (0, 0)))
    conv_b, ln_g, ln_b, glu_b = row(p["ev_conv_b"]), row(p["ev_conv_ln_g"]), row(p["ev_conv_ln_b"]), row(p["ev_s5_glu_b"])
    g_mem, g_ev, g_od, g_fin = row(p["mem_norm_g"]), row(p["ev_norm_g"]), row(p["od_norm_g"]), row(p["final_norm_g"])
    g_xa = [row(p["xa_norm_g"][l]) for l in range(2)]
    bias = _bias_matrix(p["od_rel_bias"])

    z, *got = _norm_mm_fwd("ev_in_fwd", x, g_ev, p["ev_w_in"], TMF, comm=((), shards["ev_in"]))
    p = views(p, "ev_in", got)
    s5_out = _s5_fwd(z, bmat, cmat, a_re, a_im, d_skip, comm=((), shards["s5"]))
    y_s5, (hb_re, hb_im), p = s5_out[:4], s5_out[4:6], views(p, "s5", s5_out[6:])
    c, *got = _conv_fwd(z, w32, conv_b, 512, comm=((), shards["conv"]))
    p = views(p, "conv", got)
    x1, *got = _ev_out_fwd(x, y_s5, c, z, glu_b, ln_g, ln_b, p["ev_s5_glu_w"], p["ev_w_out"], TMF, comm=((), shards["ev_out"]))
    p = views(p, "ev_out", got)
    kv0 = _norm_mm_fwd("kv0_fwd", mem, g_mem, p["xa_w_kv"][0], MEM_LEN)
    x2 = _xa_fwd("xa0_fwd", x1, g_xa[0], kv0, p["xa_w_qg"][0], p["xa_w_o"][0], TMF)
    qkvg = _norm_mm_fwd("od_in_fwd", x2, g_od, p["od_w_in"], TMF)
    o, *got = _attn_fwd(qkvg, bias, comm=((), shards["attn"]))
    p = views(p, "attn", got)
    wqg, wkv, wo = p["xa_w_qg"], p["xa_w_kv"], p["xa_w_o"]
    x3 = _gated_out_fwd("od_out_fwd", x2, o, qkvg, 3, p["od_w_out"], TMF)
    kv1 = _norm_mm_fwd("kv1_fwd", mem, g_mem, wkv[1], MEM_LEN)
    x4 = _xa_fwd("xa1_fwd", x3, g_xa[1], kv1, wqg[1], wo[1], TMF)

    G = {}
    dx4, loss_row, G["final_norm_g"] = _loss_head(x4, g_fin, target, TMF)
    dx3, dg_xa1, dkv1, dwqg1, dwo1 = _xa_bwd("xa1_bwd", x3, dx4, g_xa[1], kv1, wqg[1], wo[1], TM)
    do, dgate, G["od_w_out"] = _gated_out_bwd("od_out_bwd", dx3, o, qkvg, 3, p["od_w_out"], TMF)
    dq, dk, dv, dbias = _attn_bwd(qkvg, do, bias)
    G["od_rel_bias"] = _bias_grad(dbias, _rel_onehot())
    dx2, G["od_norm_g"], G["od_w_in"] = _norm_mm_bwd(
        "od_in_bwd", x2, g_od, p["od_w_in"], [(dq, D, 0), (dk, D, 0), (dv, D, 0), (dgate, D, 0)], dx3, TM)
    G["xa_w_qg"], G["xa_w_o"] = [None, dwqg1], [None, dwo1]
    dx1, dg_xa0, dkv0, G["xa_w_qg"][0], G["xa_w_o"][0], *recv_early = _xa_bwd(
        "xa0_bwd", x1, dx2, g_xa[0], kv0, wqg[0], wo[0], TM, comm=(pack_early(G), ()))
    G["xa_norm_g"] = jnp.concatenate([dg_xa0, dg_xa1], axis=0)
    G["mem_norm_g"], dwkv0, dwkv1 = _kv_bwd(mem, g_mem, wkv[0], wkv[1], dkv0, dkv1)
    G["xa_w_kv"] = [dwkv0, dwkv1]
    (dy_s5, dc, dga, dgb, G["ev_s5_glu_b"], G["ev_conv_ln_g"], G["ev_conv_ln_b"], G["ev_s5_glu_w"],
     G["ev_w_out"]) = _ev_out_bwd(dx1, y_s5, c, z, glu_b, ln_g, ln_b, p["ev_s5_glu_w"], p["ev_w_out"], TMF)
    dval, dglu, dw32, G["ev_conv_b"] = _conv_bwd(z, dc, w32, 512)
    G["ev_conv_w"] = dw32[:CK]
    s5_out = _s5_bwd(z, dy_s5, hb_re, hb_im, bmat, cmat, a_re, a_im, d_skip, comm=(pack_mid(G), ()))
    du, (dbre, dbim, dcre, dcim, da_re, da_im, G["ev_s5_d"]), recv_mid = s5_out[:4], s5_out[4:11], s5_out[11:]
    G["ev_s5_c_re"], G["ev_s5_c_im"] = _from_blockdiag(dcre), _from_blockdiag(dcim)
    G["ev_s5_lambda_re"], G["ev_s5_lambda_im"], G["ev_s5_log_dt"], G["ev_s5_b_re"], G["ev_s5_b_im"] = _s5_disc_bwd(
        disc_in, (da_re.reshape(NG, NS), da_im.reshape(NG, NS), _from_blockdiag(dbre), _from_blockdiag(dbim)))
    dx, G["ev_norm_g"], G["ev_w_in"] = _norm_mm_bwd(
        "ev_in_bwd", x, g_ev, p["ev_w_in"], [*[(t, 128, 0) for t in du], (dga, SW, 0), (dval, CW, 0), (dglu, CW, 0), (dgb, CW, 0)], dx1, TM)
    return loss_row[0, 0], dx, G, recv_early, recv_mid


def _all_gather(payloads):
    n = len(payloads)

    def body(*refs):
        ins, outs = refs[:n], refs[n:2 * n]
        send_sems, recv_sems, local_sems = refs[2 * n:]
        x, y, c = _mesh_pos()
        me, sibling = (x, y, c), (x, y, 1 - c)
        chips = [(1 - x, y), (x, 1 - y), (1 - x, 1 - y)]

        def copy(p, k, block, to, src=None):
            return pltpu.make_async_remote_copy(
                src_ref=_slot(outs[p], block) if src is None else src, dst_ref=_slot(outs[p], block),
                send_sem=send_sems.at[7 * p + k], recv_sem=recv_sems.at[7 * p + k],
                device_id=to, device_id_type=pl.DeviceIdType.MESH)

        mine = [pltpu.make_async_copy(ins[p], _slot(outs[p], me), local_sems.at[p]) for p in range(n)]
        for cp in mine:
            cp.start()
        first = []
        for p in range(n):
            first.append(copy(p, 0, me, sibling, src=ins[p]))
            first += [copy(p, 1 + j, me, (*chip, c), src=ins[p]) for j, chip in enumerate(chips)]
        for cp in first:
            cp.start()
        passed = []
        for j, chip in enumerate(chips):
            for p in range(n):
                copy(p, 1 + j, (*chip, c), me).wait_recv()
                passed.append(copy(p, 4 + j, (*chip, c), sibling))
                passed[-1].start()
        for p in range(n):
            copy(p, 0, sibling, me).wait_recv()
            for j, chip in enumerate(chips):
                copy(p, 4 + j, (*chip, 1 - c), me).wait_recv()
        for cp in first + passed:
            cp.wait_send()
        for cp in mine:
            cp.wait()

    return pl.pallas_call(
        body, name="all_gather", in_specs=[_ANY] * n, out_specs=[_ANY] * n,
        out_shape=[SDS((NDEV, *a.shape), a.dtype) for a in payloads],
        scratch_shapes=[pltpu.SemaphoreType.DMA((7 * n,)), pltpu.SemaphoreType.DMA((7 * n,)), pltpu.SemaphoreType.DMA((n,))],
    )(*payloads)


def _exchange(slotted, whole):
    n = len(slotted) + len(whole)
    shapes, sems = _comm_shapes((slotted, whole))

    def body(*refs):
        copies = _comm_copies(len(slotted), refs[:n], refs[n:2 * n], *refs[2 * n:])
        for cp in copies:
            cp.start()
        for cp in copies:
            cp.wait()

    return pl.pallas_call(body, name="grad_exchange", in_specs=[_ANY] * n, out_specs=[_ANY] * n, out_shape=shapes,
                          scratch_shapes=sems)(*slotted, *whole)


def _adamw_math(g, w, m, v):
    m2 = ADAM_B1 * m + (1.0 - ADAM_B1) * g
    v2 = ADAM_B2 * v + (1.0 - ADAM_B2) * jnp.square(g)
    m_hat = m2 / (1.0 - ADAM_B1 ** ADAM_STEP)
    v_hat = v2 / (1.0 - ADAM_B2 ** ADAM_STEP)
    return -ADAM_LR * (m_hat / (jnp.sqrt(v_hat) + ADAM_EPS) + ADAM_WD * w), m2, v2


def _sum_slots(r_ref, rows=None):
    acc = r_ref[0].astype(f32) if rows is None else r_ref[0, rows[0]:rows[1], :].astype(f32)
    for k in range(1, NDEV):
        acc = acc + (r_ref[k] if rows is None else r_ref[k, rows[0]:rows[1], :]).astype(f32)
    return acc


def _sum_adamw(name, sources, w, m, v, tr):
    rows, C = w.shape
    starts = [int(t) for t in np.cumsum([0] + [r // tr for (_, _, r) in sources])]

    def kern(*refs):
        r_refs = refs[:len(sources)]
        w_ref, m_ref, v_ref, g_ref, d_ref, m2_ref, v2_ref = refs[len(sources):]
        g = _sum_slots(r_refs[0])
        for s in range(1, len(sources)):
            g = jnp.where(pl.program_id(0) >= starts[s], _sum_slots(r_refs[s]), g)
        g_ref[...] = g
        d_ref[...], m2_ref[...], v2_ref[...] = _adamw_math(g, w_ref[...], m_ref[...], v_ref[...])

    def src_spec(s):
        _, off, r = sources[s]
        return pl.BlockSpec((NDEV, tr, C), lambda i: (0, off // tr + jnp.clip(i - starts[s], 0, r // tr - 1), 0))

    blk = pl.BlockSpec((tr, C), lambda i: (i, 0))
    return pl.pallas_call(
        kern, name=name, grid=(rows // tr,),
        in_specs=[src_spec(s) for s in range(len(sources))] + [blk, blk, blk],
        out_specs=[blk] * 4, out_shape=[SDS((rows, C), f32)] * 4, compiler_params=_cp(),
    )(*[r for (r, _, _) in sources], w, m, v)


def _small_adamw(recvs, table, wmv):
    rnames = list(recvs)
    nr, nw = len(rnames), len(table)

    def kern(*refs):
        rr = dict(zip(rnames, refs[:nr]))
        ins, outs = refs[nr:nr + 3 * nw], refs[nr + 3 * nw:]
        for i, (_, shape, src, r0) in enumerate(table):
            g = _sum_slots(rr[src], (r0, r0 + shape[0]))[:, :shape[1]]
            outs[4 * i][...] = g
            res = _adamw_math(g, ins[3 * i][...], ins[3 * i + 1][...], ins[3 * i + 2][...])
            for o, val in zip(outs[4 * i + 1:4 * i + 4], res):
                o[...] = val

    flat = [t for (n, _, _, _) in table for t in wmv[n]]
    res = pl.pallas_call(
        kern, name="small_adamw", out_shape=[SDS(shape, f32) for (_, shape, _, _) in table for _ in range(4)],
        compiler_params=_cp(()),
    )(*[recvs[n] for n in rnames], *flat)
    return {n: tuple(res[4 * i:4 * i + 4]) for i, (n, _, _, _) in enumerate(table)}


WEIGHTS = ["mem_norm_g", "ev_norm_g", "ev_w_in", "ev_s5_lambda_re", "ev_s5_lambda_im", "ev_s5_log_dt", "ev_s5_b_re",
           "ev_s5_b_im", "ev_s5_c_re", "ev_s5_c_im", "ev_s5_d", "ev_s5_glu_w", "ev_s5_glu_b", "ev_conv_w", "ev_conv_b",
           "ev_conv_ln_g", "ev_conv_ln_b", "ev_w_out", "od_norm_g", "od_w_in", "od_rel_bias", "od_w_out", "xa_norm_g",
           "xa_w_qg", "xa_w_kv", "xa_w_o", "final_norm_g"]
GATHER = {"first": [("g320", 320, [("ev_w_in", None, "cols", 1024)])],
          "ev_in": [("h1024", 1024, [("ev_w_out", None, "rows", 128)]), ("h512", 512, [("ev_s5_glu_w", None, "rows", 64)])],
          "s5": [("i512", 512, [("od_w_in", None, "cols", 1024)]), ("i1024", 1024, [("xa_w_o", 0, "rows", 128)])],
          "conv": [("j256", 256, [("xa_w_qg", 0, "cols", 1024)])],
          "ev_out": [("k256", 256, [("xa_w_kv", 0, "cols", 1024)])],
          "attn": [("l1024", 1024, [("od_w_out", None, "rows", 128), ("xa_w_o", 1, "rows", 128)]),
                   ("l256", 256, [("xa_w_qg", 1, "cols", 1024), ("xa_w_kv", 1, "cols", 1024)])]}
FIRST = [("a1024", 1024, [("ev_w_out", None, "rows", 128), ("xa_w_o", 0, "rows", 128)]),
         ("a512", 512, [("ev_s5_glu_w", None, "rows", 64)]),
         ("a256", 256, [("xa_w_qg", 0, "cols", 1024), ("xa_w_kv", 0, "cols", 1024), ("xa_w_kv", 1, "cols", 1024)]),
         ("a320", 320, [("ev_w_in", None, "cols", 1024)])]
LATE = [("b1024", 1024, [("od_w_out", None, "rows", 128), ("xa_w_o", 1, "rows", 128)]),
        ("b512", 512, [("od_w_in", None, "cols", 1024)]),
        ("b256", 256, [("xa_w_qg", 1, "cols", 1024)])]
MATMUL_WEIGHTS = ["ev_w_out", "xa_w_o", "ev_s5_glu_w", "xa_w_qg", "xa_w_kv", "ev_w_in", "od_w_out", "od_w_in"]
ADAM_ROWS = {1024: 128, 512: 64, 256: 256, 320: 256}
SMALL = [("mem_norm_g", (1, D), "s1024", 0), ("ev_norm_g", (1, D), "s1024", 1), ("xa_norm_g", (2, D), "s1024", 2),
         ("final_norm_g", (1, D), "s1024", 4),
         ("ev_s5_d", (1, SW), "s512", 0), ("ev_s5_glu_b", (1, SW), "s512", 1), ("ev_conv_b", (1, CW), "s512", 2),
         ("ev_conv_ln_g", (1, CW), "s512", 3), ("ev_conv_ln_b", (1, CW), "s512", 4),
         ("ev_s5_c_re", (NG * GS, NS), "s64", 0), ("ev_s5_c_im", (NG * GS, NS), "s64", 512),
         ("ev_s5_b_re", (NG * GS, NS), "s64", 1024), ("ev_s5_b_im", (NG * GS, NS), "s64", 1536),
         ("ev_s5_lambda_re", (NG, NS), "s64", 2048), ("ev_s5_lambda_im", (NG, NS), "s64", 2080),
         ("ev_s5_log_dt", (1, NG), "s64", 2112),
         ("od_rel_bias", (HEADS, 2 * MAX_REL + 1), "s257", 0),
         ("ev_conv_w", (CK, CW // NDEV), "cw", 0), ("od_norm_g", (1, D // NDEV), "on", 0)]
S5_B = ("ev_s5_b_re", "ev_s5_b_im")


def _pad_rows(a, rows):
    return jnp.pad(a, ((0, rows - a.shape[0]), (0, 0)))


def _shard_payloads(a, groups):
    pick = lambda n, l, minor: (a[n] if l is None else a[n][l]).reshape(-1, minor)
    return [jnp.concatenate([pick(n, l, minor).astype(bf16) for n, l, _, _ in members], axis=0) for _, minor, members in groups]


def _weight_views(p, groups, gathered):
    p = dict(p)
    for (_, _, members), buf in zip(groups, gathered):
        off = 0
        for n, l, kind, rows in members:
            view = _W(buf, rows, off // rows, kind)
            if l is None:
                p[n] = view
            else:
                p[n] = list(p.get(n, [None, None]))
                p[n][l] = view
            off += rows
    return p


def _slotted_grads(G, groups):
    slot3 = lambda g, rows: g.reshape(NDEV, rows, g.shape[-1]) if g.ndim == 2 else g
    return [jnp.concatenate([slot3(G[n] if l is None else G[n][l], rows).astype(bf16) for n, l, _, rows in members], axis=1)
            for _, _, members in groups]


def _bt(b):
    return jnp.transpose(b.reshape(NG, NS, GS), (0, 2, 1)).reshape(NG * GS, NS)


def _unbt(b, shape):
    return jnp.transpose(b.reshape(NG, GS, NS), (0, 2, 1)).reshape(shape)


def kernel(*args):
    names = ["x", "mem", *WEIGHTS, "loss_target", *["m_" + n for n in WEIGHTS], *["v_" + n for n in WEIGHTS]]
    a = dict(zip(names, args, strict=True))

    other = jnp.concatenate([_pad_rows(a["ev_conv_w"][0], 32).reshape(16, 128), _pad_rows(a["od_norm_g"], 8)], axis=0)
    g_in, gother = _all_gather([*_shard_payloads(a, GATHER["first"]), other])
    p = {n: (a[n] if n in ("xa_norm_g", "mem_norm_g", "final_norm_g") else a[n][0]) for n in WEIGHTS if n not in MATMUL_WEIGHTS}
    p["ev_w_in"] = jnp.transpose(g_in, (1, 0, 2)).reshape(D, EVEN_IN)
    p["ev_conv_w"] = jnp.transpose(gother[:, :16].reshape(NDEV, 32, CW // NDEV)[:, :CK], (1, 0, 2)).reshape(CK, CW)
    p["od_norm_g"] = gother[:, 16].reshape(D)

    loss_part, dx, G, recv_early, recv_mid = _local_step(
        a["x"][0], a["mem"][0], a["loss_target"][0], p, {s: _shard_payloads(a, g) for s, g in GATHER.items()},
        lambda p, stage, gathered: _weight_views(p, GATHER[stage], gathered),
        lambda G: _slotted_grads(G, LATE), lambda G: _slotted_grads(G, FIRST[:3]))

    G["ev_w_in"] = jnp.transpose(G["ev_w_in"].reshape(D, NDEV, EVEN_IN // NDEV), (1, 0, 2))
    cw = jnp.transpose(G["ev_conv_w"].reshape(CK, NDEV, CW // NDEV), (1, 0, 2))
    slotted = [*_slotted_grads(G, FIRST[3:]), jnp.pad(cw, ((0, 0), (0, 1), (0, 0))), G["od_norm_g"].reshape(NDEV, 1, D // NDEV)]
    whole = {"s1024": _pad_rows(jnp.concatenate([G[n] for n in ("mem_norm_g", "ev_norm_g", "xa_norm_g", "final_norm_g")]), 8),
             "s512": _pad_rows(jnp.concatenate([G[n] for n in ("ev_s5_d", "ev_s5_glu_b", "ev_conv_b", "ev_conv_ln_g",
                                                                 "ev_conv_ln_b")]), 8),
             "s64": _pad_rows(jnp.concatenate([G[n] for n in ("ev_s5_c_re", "ev_s5_c_im", "ev_s5_b_re", "ev_s5_b_im",
                                                                "ev_s5_lambda_re", "ev_s5_lambda_im")]
                                              + [jnp.pad(G["ev_s5_log_dt"].reshape(1, NG), ((0, 0), (0, NS - NG)))]), 2128).astype(bf16),
             "s257": G["od_rel_bias"]}
    recv_last = _exchange(slotted, list(whole.values()))
    recvs = dict(zip([g[0] for g in LATE], recv_early))
    recvs.update(zip([g[0] for g in FIRST[:3]], recv_mid))
    recvs.update(zip(["a320", "cw", "on", *whole], recv_last))

    sources = {}
    for buf, minor, members in FIRST + LATE:
        off = 0
        for n, l, _, rows in members:
            sources.setdefault(n, []).append((l or 0, recvs[buf], off, rows, minor))
            off += rows
    res = {}
    for n, src in sources.items():
        src = sorted(src, key=lambda t: t[0])
        minor = src[0][4]
        w, m, v = (a[k + n].reshape(-1, minor) for k in ("", "m_", "v_"))
        res[n] = _sum_adamw("adamw_" + n, [(r, off, rows) for _, r, off, rows, _ in src], w, m, v, ADAM_ROWS[minor])
    two_d = lambda n, t: _bt(t[0]) if n in S5_B else t.reshape(dict((s[0], s[1]) for s in SMALL)[n])
    res.update(_small_adamw({k: recvs[k] for k in ("s1024", "s512", "s64", "s257", "cw", "on")}, SMALL,
                            {n: tuple(two_d(n, a[k + n]) for k in ("", "m_", "v_")) for n, _, _, _ in SMALL}))
    shaped = lambda n, t: _unbt(t, a[n].shape) if n in S5_B else t.reshape(a[n].shape)
    loss = lax.psum(loss_part, AXES)
    return (loss, dx[None], *[shaped(n, res[n][k]) for k in range(4) for n in WEIGHTS])
```

```python
import functools
from typing import NamedTuple

import jax
import jax.numpy as jnp
import numpy as np
from jax import lax
from jax.experimental import pallas as pl
from jax.experimental.pallas import tpu as pltpu

f32, bf16 = jnp.float32, jnp.bfloat16
SDS = jax.ShapeDtypeStruct

D = 1024
SW = 512
NG, GS, NS = 32, 16, 64
NST = NG * NS
CW = 512
CK = 31
EVEN_IN = 2 * SW + 3 * CW
HEADS, DH = 16, 64
CHUNK, LEFT = 64, 8
PAD = LEFT * CHUNK
MAX_REL = 128
MEM_LEN = 256
XH, XD = 4, 256
EPS = 1e-6
NDEV = 8
AXES = ("x", "y", "c")

ADAM_LR, ADAM_B1, ADAM_B2, ADAM_EPS, ADAM_WD, ADAM_STEP = 0.001, 0.9, 0.999, 1e-08, 0.01, 10

VMEM_LIMIT = 56 << 20
S5_T = 512
S5_CH = 512
AQ = 256
AW = AQ + PAD

NN = ((1,), (0,))
NT = ((1,), (1,))
TN = ((0,), (0,))


def _dot(a, b, dims):
    return lax.dot_general(a.astype(bf16), b.astype(bf16), (dims, ((), ())), preferred_element_type=f32)


def _dotw(a, w):
    if w.ndim == 2:
        return _dot(a, w, NN)
    return jnp.concatenate([_dot(a, w[j], NN) for j in range(w.shape[0])], axis=1)


def _dotw_t(g, w):
    if w.ndim == 2:
        return _dot(g, w, NT)
    n = w.shape[2]
    out = _dot(g[:, :n], w[0], NT)
    for j in range(1, w.shape[0]):
        out = out + _dot(g[:, j * n:(j + 1) * n], w[j], NT)
    return out


def _wgrad(a, g, w):
    if w.ndim == 2:
        return _dot(a, g, TN)
    n = w.shape[2]
    return jnp.stack([_dot(a, g[:, j * n:(j + 1) * n], TN) for j in range(w.shape[0])])


@jax.custom_vjp
def _mm(a, w):
    return _dotw(a, w)


def _mm_f(a, w):
    return _dotw(a, w), w


def _mm_b(w, g):
    return _dotw_t(g, w), None


_mm.defvjp(_mm_f, _mm_b)


@jax.custom_vjp
def _mm_nt(a, b):
    return _dot(a, b, NT)


def _mm_nt_f(a, b):
    return _dot(a, b, NT), (a, b)


def _mm_nt_b(res, g):
    a, b = res
    return _dot(g, b, NN), _dot(g, a, TN)


_mm_nt.defvjp(_mm_nt_f, _mm_nt_b)


@jax.custom_vjp
def _mm_nn(a, b):
    return _dot(a, b, NN)


def _mm_nn_f(a, b):
    return _dot(a, b, NN), (a, b)


def _mm_nn_b(res, g):
    a, b = res
    return _dot(g, b, NT), _dot(a, g, TN)


_mm_nn.defvjp(_mm_nn_f, _mm_nn_b)


def _plain_mm(a, w):
    return _dotw(a, w)


def _plain_nt(a, b):
    return _dot(a, b, NT)


def _rms(x, g):
    return x * lax.rsqrt(jnp.mean(x * x, axis=-1, keepdims=True) + EPS) * g


def _cp(sem=("arbitrary",)):
    return pltpu.CompilerParams(dimension_semantics=sem, vmem_limit_bytes=VMEM_LIMIT)


def _row(tm, w, col=0):
    return pl.BlockSpec((tm, w), lambda i: (i, col))


def _full(shape):
    nd = len(shape)
    return pl.BlockSpec(tuple(shape), lambda i: (0,) * nd, pipeline_mode=pl.Buffered(1))


class _W(NamedTuple):
    arr: jax.Array
    rows: int
    idx: int
    kind: str

    @property
    def shape(self):
        c = self.arr.shape[2]
        return (NDEV * self.rows, c) if self.kind == "rows" else (NDEV, self.rows, c)


class _WRef:
    def __init__(self, ref, kind):
        self.ref, self.kind = ref, kind

    def __getitem__(self, _):
        v = self.ref[...]
        return v.reshape(v.shape[0] * v.shape[1], v.shape[2]) if self.kind == "rows" else v


def _wspec(w):
    if isinstance(w, _W):
        return pl.BlockSpec((NDEV, w.rows, w.arr.shape[2]), lambda i: (0, w.idx, 0), pipeline_mode=pl.Buffered(1))
    return _full(w.shape)


_ANY = pl.BlockSpec(memory_space=pl.ANY)
_NO_COMM = ((), ())


def _mesh_pos():
    return tuple(lax.axis_index(a) for a in AXES)


def _slot(ref, dev):
    return ref.at[4 * dev[0] + 2 * dev[1] + dev[2]]


def _comm_shapes(comm):
    slotted, whole = comm
    n = len(slotted) + len(whole)
    shapes = [SDS(a.shape, a.dtype) for a in slotted] + [SDS((NDEV, *a.shape), a.dtype) for a in whole]
    sems = [pltpu.SemaphoreType.DMA((7 * n,)), pltpu.SemaphoreType.DMA((7 * n,)), pltpu.SemaphoreType.DMA((n,))] if n else []
    return shapes, sems


def _comm_copies(ns, ins, outs, send_sems, recv_sems, local_sems):
    n = len(ins)
    x, y, c = _mesh_pos()
    me = (x, y, c)
    src = lambda p, dev: _slot(ins[p], dev) if p < ns else ins[p]
    copies = [pltpu.make_async_copy(src(p, me), _slot(outs[p], me), local_sems.at[p]) for p in range(n)]
    for k in range(1, NDEV):
        flip = lambda v, bit: 1 - v if (k >> bit) & 1 else v
        peer = (flip(x, 2), flip(y, 1), flip(c, 0))
        for p in range(n):
            copies.append(pltpu.make_async_remote_copy(
                src_ref=src(p, peer), dst_ref=_slot(outs[p], me), send_sem=send_sems.at[7 * p + k - 1],
                recv_sem=recv_sems.at[7 * p + k - 1], device_id=peer, device_id_type=pl.DeviceIdType.MESH))
    return copies


def _comm_hook(comm, cin, cout, sems, step, last):
    if not cin:
        return lambda: None

    @pl.when(step == 0)
    def _():
        for cp in _comm_copies(len(comm[0]), cin, cout, *sems):
            cp.start()

    def finish():
        @pl.when(step == last)
        def _():
            for cp in _comm_copies(len(comm[0]), cin, cout, *sems):
                cp.wait()

    return finish


def _tiled(name, body, L, tm, row_ins, full_ins, row_outs, acc_outs=(), comm=_NO_COMM):
    nr, nf, no, na = len(row_ins), len(full_ins), len(row_outs), len(acc_outs)
    cins = [*comm[0], *comm[1]]
    nc = len(cins)
    cshapes, csems = _comm_shapes(comm)

    def kern(*refs):
        rin = refs[:nr]
        fin = [_WRef(r, w.kind) if isinstance(w, _W) else r for r, w in zip(refs[nr:nr + nf], full_ins)]
        cin = refs[nr + nf:nr + nf + nc]
        outs = refs[nr + nf + nc:]
        rout, aout, cout, sems = outs[:no], outs[no:no + na], outs[no + na:no + na + nc], outs[no + na + nc:]
        finish = _comm_hook(comm, cin, cout, sems, pl.program_id(0), L // tm - 1)
        routs, accs = body([r[...] for r in rin], fin)
        for r, v in zip(rout, routs):
            r[...] = v.astype(r.dtype)
        if aout:
            @pl.when(pl.program_id(0) == 0)
            def _():
                for a in aout:
                    a[...] = jnp.zeros(a.shape, a.dtype)

            for a, v in zip(aout, accs):
                a[...] += v.astype(a.dtype)
        finish()

    return pl.pallas_call(
        kern, name=name, grid=(L // tm,),
        in_specs=[_row(tm, w, c) for (_, w, c) in row_ins] + [_wspec(a) for a in full_ins] + [_ANY] * nc,
        out_specs=[_row(tm, w) for (w, _) in row_outs] + [_full(s) for (s, _) in acc_outs] + [_ANY] * nc,
        out_shape=[SDS((L, w), dt) for (w, dt) in row_outs] + [SDS(tuple(s), dt) for (s, dt) in acc_outs] + cshapes,
        scratch_shapes=csems, compiler_params=_cp(),
    )(*[a for (a, _, _) in row_ins], *[a.arr if isinstance(a, _W) else a for a in full_ins], *cins)


def _norm_mm_fwd(name, x, g, w, tm, comm=_NO_COMM):
    def body(rows, fulls):
        g_ref, w_ref = fulls
        return [_dotw(_rms(rows[0], g_ref[...]), w_ref[...])], []

    n_out = w.shape[1] if len(w.shape) == 2 else w.shape[0] * w.shape[2]
    res = _tiled(name, body, x.shape[0], tm, [(x, D, 0)], [g, w], [(n_out, f32)], comm=comm)
    return res if comm[1] else res[0]


def _norm_mm_bwd(name, x, g, w, dz_parts, dres, tm, comm=_NO_COMM):
    n = len(dz_parts)

    def body(rows, fulls):
        g_ref, w_ref = fulls
        dz = rows[1] if n == 1 else jnp.concatenate(rows[1:1 + n], axis=1)
        wv = w_ref[...]
        h, vjp = jax.vjp(_rms, rows[0], g_ref[...])
        dx, dg = vjp(_dotw_t(dz, wv))
        return [dx + rows[1 + n]], [dg, _wgrad(h, dz, wv)]

    return _tiled(name, body, x.shape[0], tm, [(x, D, 0), *dz_parts, (dres, D, 0)], [g, w],
                  [(D, f32)], [((1, D), f32), (tuple(w.shape), f32)], comm=comm)


S5_S = S5_T // 8


def _perm_load(refs):
    return jnp.concatenate([jnp.concatenate([r[pl.ds(i, 8, stride=S5_S), :] for i in range(S5_S)], axis=0) for r in refs],
                           axis=1)


def _perm_store(refs, v):
    for c, r in enumerate(refs):
        for i in range(S5_S):
            r[pl.ds(i, 8, stride=S5_S), :] = v[i * 8:(i + 1) * 8, c * 128:(c + 1) * 128]


def _cmul(pr, pi, qr, qi):
    return pr * qr - pi * qi, pr * qi + pi * qr


def _scan_tables(tab_ref, pw_re, pw_im, ar, ai, reverse):
    def powers(br, bi):
        pw = [(br, bi)]
        for _ in range(7):
            pw.append(_cmul(*pw[-1], br, bi))
        return pw

    row = lax.broadcasted_iota(jnp.int32, (8, NST), 0)

    def rows(pw, order):
        vr = jnp.zeros((8, NST), f32)
        vi = jnp.zeros((8, NST), f32)
        for t in range(8):
            vr = jnp.where(row == t, pw[order(t)][0], vr)
            vi = jnp.where(row == t, pw[order(t)][1], vi)
        return vr, vi

    pw = powers(ar, ai)
    base_r, base_i = rows(pw, lambda t: t)
    sr, si = jnp.ones((1, NST), f32), jnp.zeros((1, NST), f32)
    for m in range(S5_S // 8):
        pw_re[m * 8:(m + 1) * 8, :], pw_im[m * 8:(m + 1) * 8, :] = _cmul(base_r, base_i, sr, si)
        sr, si = _cmul(sr, si, *pw[7])
    big = powers(sr, si)
    for k, s in enumerate((1, 2, 4)):
        keep = (row <= 7 - s) if reverse else (row >= s)
        tab_ref[2 * k] = jnp.where(keep, big[s - 1][0], 0.0)
        tab_ref[2 * k + 1] = jnp.where(keep, big[s - 1][1], 0.0)
    tab_ref[6], tab_ref[7] = rows(big, (lambda t: 7 - t) if reverse else (lambda t: t))


def _scan(re_ref, im_ref, row0, a_re, a_im, tab_ref, pw_re, pw_im, car_re, car_im, reverse, fold=None):
    ng = S5_S // 8
    row = lax.broadcasted_iota(jnp.int32, (8, S5_CH), 0)
    accs = []
    for cc in range(NST // S5_CH):
        cols = slice(cc * S5_CH, (cc + 1) * S5_CH)
        ar = jnp.broadcast_to(a_re[:, cols], (8, S5_CH))
        ai = jnp.broadcast_to(a_im[:, cols], (8, S5_CH))

        def local(i, h, cols=cols, ar=ar, ai=ai):
            r0 = pl.multiple_of(row0 + ((S5_S - 1 - i) if reverse else i) * 8, 8)
            hr = ar * h[0] - ai * h[1] + re_ref[pl.ds(r0, 8), cols]
            hi = ar * h[1] + ai * h[0] + im_ref[pl.ds(r0, 8), cols]
            re_ref[pl.ds(r0, 8), cols] = hr
            im_ref[pl.ds(r0, 8), cols] = hi
            return hr, hi

        xr, xi = lax.fori_loop(0, S5_S, local, (jnp.zeros((8, S5_CH), f32), jnp.zeros((8, S5_CH), f32)))

        old_r, old_i = car_re[:, cols], car_im[:, cols]
        for k, s in enumerate((1, 2, 4)):
            yr = pltpu.roll(xr, (8 - s) if reverse else s, 0)
            yi = pltpu.roll(xi, (8 - s) if reverse else s, 0)
            dr, di = _cmul(tab_ref[2 * k, :, cols], tab_ref[2 * k + 1, :, cols], yr, yi)
            xr, xi = xr + dr, xi + di
        dr, di = _cmul(tab_ref[6, :, cols], tab_ref[7, :, cols], old_r, old_i)
        xr, xi = xr + dr, xi + di
        b = 0 if reverse else 7
        car_re[:, cols] = jnp.broadcast_to(xr[b:b + 1, :], (8, S5_CH))
        car_im[:, cols] = jnp.broadcast_to(xi[b:b + 1, :], (8, S5_CH))
        edge = 7 if reverse else 0
        cr = jnp.where(row == edge, old_r, pltpu.roll(xr, 7 if reverse else 1, 0))
        ci = jnp.where(row == edge, old_i, pltpu.roll(xi, 7 if reverse else 1, 0))

        def fix(g, acc, cols=cols, cr=cr, ci=ci):
            gg = (ng - 1 - g) if reverse else g
            pg = pl.multiple_of(g * 8, 8)
            ptr, pti = pw_re[pl.ds(pg, 8), cols], pw_im[pl.ds(pg, 8), cols]
            for u in range(8):
                uu = (7 - u) if reverse else u
                i = gg * 8 + uu
                r0 = pl.multiple_of(row0 + i * 8, 8)
                pr = jnp.broadcast_to(ptr[u:u + 1, :], (8, S5_CH))
                pi = jnp.broadcast_to(pti[u:u + 1, :], (8, S5_CH))
                dr, di = _cmul(pr, pi, cr, ci)
                hr = re_ref[pl.ds(r0, 8), cols] + dr
                hi = im_ref[pl.ds(r0, 8), cols] + di
                re_ref[pl.ds(r0, 8), cols] = hr
                im_ref[pl.ds(r0, 8), cols] = hi
                if fold is not None:
                    acc = fold(cols, i, hr, hi, acc)
            return acc

        acc0 = (jnp.zeros((8, S5_CH), f32), jnp.zeros((8, S5_CH), f32)) if fold is not None else 0
        accs.append(lax.fori_loop(0, ng, fix, acc0))
    return accs


def _s5_fwd(z, bmat, cmat, a_re, a_im, d_skip, comm=_NO_COMM):
    L = z.shape[0]
    nb = L // S5_T
    cins = [*comm[0], *comm[1]]
    nc = len(cins)
    cshapes, csems = _comm_shapes(comm)

    def kern(*refs):
        u_refs, (bre, bim, cre, cim, ar_ref, ai_ref, d_ref), rest = refs[:4], refs[4:11], refs[11:]
        cin, y_refs, (hbr_ref, hbi_ref), cout = rest[:nc], rest[nc:nc + 4], rest[nc + 4:nc + 6], rest[nc + 6:2 * nc + 6]
        h_re, h_im, tab, pw_re, pw_im, car_re, car_im, *sems = rest[2 * nc + 6:]
        finish = _comm_hook(comm, cin, cout, sems, pl.program_id(0), nb - 1)

        @pl.when(pl.program_id(0) == 0)
        def _():
            _scan_tables(tab, pw_re, pw_im, ar_ref[...], ai_ref[...], False)
            car_re[...] = jnp.zeros_like(car_re)
            car_im[...] = jnp.zeros_like(car_im)

        hbr_ref[...] = car_re[...]
        hbi_ref[...] = car_im[...]
        u = _perm_load(u_refs)
        for hf in range(2):
            uh = u[:, hf * 256:(hf + 1) * 256]
            h_re[:, hf * 1024:(hf + 1) * 1024] = _dot(uh, bre[hf], NN)
            h_im[:, hf * 1024:(hf + 1) * 1024] = _dot(uh, bim[hf], NN)
        _scan(h_re, h_im, 0, ar_ref[...], ai_ref[...], tab, pw_re, pw_im, car_re, car_im, False)
        ys = []
        for hf in range(2):
            cs = slice(hf * 1024, (hf + 1) * 1024)
            ys.append(_dot(h_re[:, cs], cre[hf], NT) - _dot(h_im[:, cs], cim[hf], NT))
        _perm_store(y_refs, jnp.concatenate(ys, axis=1) + d_ref[...] * u)
        finish()

    fulls = [*bmat, *cmat, a_re, a_im, d_skip]
    return pl.pallas_call(
        kern, name="s5_fwd", grid=(nb,),
        in_specs=[_row(S5_T, 128, c) for c in range(4)] + [_full(a.shape) for a in fulls] + [_ANY] * nc,
        out_specs=[_row(S5_T, 128)] * 4 + [_row(8, NST), _row(8, NST)] + [_ANY] * nc,
        out_shape=[SDS((L, 128), f32)] * 4 + [SDS((nb * 8, NST), f32), SDS((nb * 8, NST), f32)] + cshapes,
        scratch_shapes=[pltpu.VMEM((S5_T, NST), f32), pltpu.VMEM((S5_T, NST), f32), pltpu.VMEM((8, 8, NST), f32),
                        pltpu.VMEM((S5_S, NST), f32), pltpu.VMEM((S5_S, NST), f32),
                        pltpu.VMEM((8, NST), f32), pltpu.VMEM((8, NST), f32)] + csems,
        compiler_params=_cp(),
    )(z, z, z, z, *fulls, *cins)


def _s5_bwd(z, dy, hb_re, hb_im, bmat, cmat, a_re, a_im, d_skip, comm=_NO_COMM):
    L = z.shape[0]
    nb = L // S5_T
    T = S5_T
    cins = [*comm[0], *comm[1]]
    nc = len(cins)
    cshapes, csems = _comm_shapes(comm)

    def kern(*refs):
        u_refs, dy_refs, (hbr_ref, hbi_ref, bre, bim, cre, cim, ar_ref, ai_ref, d_ref), rest = \
            refs[:4], refs[4:8], refs[8:17], refs[17:]
        cin, du_refs, (dbre, dbim, dcre, dcim, dar, dai, dd), cout = \
            rest[:nc], rest[nc:nc + 4], rest[nc + 4:nc + 11], rest[nc + 11:2 * nc + 11]
        h_re, h_im, g_re, g_im, tabf, pwf_re, pwf_im, tabr, pwr_re, pwr_im, car_re, car_im, gcar_re, gcar_im, *sems = \
            rest[2 * nc + 11:]
        finish = _comm_hook(comm, cin, cout, sems, pl.program_id(0), nb - 1)

        @pl.when(pl.program_id(0) == 0)
        def _():
            _scan_tables(tabf, pwf_re, pwf_im, ar_ref[...], ai_ref[...], False)
            _scan_tables(tabr, pwr_re, pwr_im, ar_ref[...], -ai_ref[...], True)
            gcar_re[...] = jnp.zeros_like(gcar_re)
            gcar_im[...] = jnp.zeros_like(gcar_im)
            for r in (dbre, dbim, dcre, dcim, dar, dai, dd):
                r[...] = jnp.zeros_like(r)

        u = _perm_load(u_refs)
        dyv = _perm_load(dy_refs)
        car_re[...] = hbr_ref[...]
        car_im[...] = hbi_ref[...]
        for hf in range(2):
            cs = slice(hf * 1024, (hf + 1) * 1024)
            uh = u[:, hf * 256:(hf + 1) * 256]
            dyh = dyv[:, hf * 256:(hf + 1) * 256]
            h_re[8:, cs] = _dot(uh, bre[hf], NN)
            h_im[8:, cs] = _dot(uh, bim[hf], NN)
            g_re[:, cs] = _dot(dyh, cre[hf], NN)
            g_im[:, cs] = -_dot(dyh, cim[hf], NN)
        _scan(h_re, h_im, 8, ar_ref[...], ai_ref[...], tabf, pwf_re, pwf_im, car_re, car_im, False)
        row = lax.broadcasted_iota(jnp.int32, (8, NST), 0)
        h_re[0:8, :] = jnp.where(row == 0, hbr_ref[...], pltpu.roll(h_re[T:T + 8, :], 1, 0))
        h_im[0:8, :] = jnp.where(row == 0, hbi_ref[...], pltpu.roll(h_im[T:T + 8, :], 1, 0))

        def fold_da(cols, i, gr, gi, acc):
            r0 = pl.multiple_of(i * 8, 8)
            hr, hi = h_re[pl.ds(r0, 8), cols], h_im[pl.ds(r0, 8), cols]
            return acc[0] + gr * hr + gi * hi, acc[1] + gi * hr - gr * hi

        accs = _scan(g_re, g_im, 0, ar_ref[...], -ai_ref[...], tabr, pwr_re, pwr_im, gcar_re, gcar_im, True, fold=fold_da)
        for cc, (acc_r, acc_i) in enumerate(accs):
            cols = slice(cc * S5_CH, (cc + 1) * S5_CH)
            dar[:, cols] += jnp.sum(acc_r, axis=0, keepdims=True)
            dai[:, cols] += jnp.sum(acc_i, axis=0, keepdims=True)

        dus = []
        for hf in range(2):
            cs = slice(hf * 1024, (hf + 1) * 1024)
            uh = u[:, hf * 256:(hf + 1) * 256]
            dyh = dyv[:, hf * 256:(hf + 1) * 256]
            gr = g_re[:, cs]
            gi = g_im[:, cs]
            dus.append(_dot(gr, bre[hf], NT) + _dot(gi, bim[hf], NT))
            dbre[hf] += _dot(uh, gr, TN)
            dbim[hf] += _dot(uh, gi, TN)
            dcre[hf] += _dot(dyh, h_re[8:, cs], TN)
            dcim[hf] -= _dot(dyh, h_im[8:, cs], TN)
        _perm_store(du_refs, jnp.concatenate(dus, axis=1) + d_ref[...] * dyv)
        dd[...] += jnp.sum(dyv * u, axis=0, keepdims=True)
        finish()

    rev = lambda c: pl.BlockSpec((T, 128), lambda k: (nb - 1 - k, c))
    rev8 = pl.BlockSpec((8, NST), lambda k: (nb - 1 - k, 0))
    fulls = [*bmat, *cmat, a_re, a_im, d_skip]
    mat = ((2, 256, 1024), f32)
    accs = [mat, mat, mat, mat, ((1, NST), f32), ((1, NST), f32), ((1, SW), f32)]
    tables = [pltpu.VMEM((8, 8, NST), f32), pltpu.VMEM((S5_S, NST), f32), pltpu.VMEM((S5_S, NST), f32)]
    return pl.pallas_call(
        kern, name="s5_bwd", grid=(nb,),
        in_specs=[rev(c) for c in range(4)] * 2 + [rev8, rev8] + [_full(a.shape) for a in fulls] + [_ANY] * nc,
        out_specs=[rev(0)] * 4 + [_full(s) for (s, _) in accs] + [_ANY] * nc,
        out_shape=[SDS((L, 128), f32)] * 4 + [SDS(s, dt) for (s, dt) in accs] + cshapes,
        scratch_shapes=[pltpu.VMEM((T + 8, NST), f32), pltpu.VMEM((T + 8, NST), f32),
                        pltpu.VMEM((T, NST), f32), pltpu.VMEM((T, NST), f32), *tables, *tables,
                        pltpu.VMEM((8, NST), f32), pltpu.VMEM((8, NST), f32),
                        pltpu.VMEM((8, NST), f32), pltpu.VMEM((8, NST), f32)] + csems,
        compiler_params=_cp(),
    )(z, z, z, z, dy, dy, dy, dy, hb_re, hb_im, *fulls, *cins)


def _s5_disc(lam_re, lam_im, log_dt, bt_re, bt_im):
    dt = jnp.exp(log_dt)
    mag = jnp.exp(lam_re * dt)
    ab_re = mag * jnp.cos(lam_im * dt)
    ab_im = mag * jnp.sin(lam_im * dt)
    den = lam_re * lam_re + lam_im * lam_im
    nr = ab_re - 1.0
    coef_re = (nr * lam_re + ab_im * lam_im) / den
    coef_im = (ab_im * lam_re - nr * lam_im) / den
    rep = lambda c: jnp.broadcast_to(c[:, None, :], (NG, GS, NS)).reshape(NG * GS, NS)
    cr, ci = rep(coef_re), rep(coef_im)
    return ab_re, ab_im, cr * bt_re - ci * bt_im, cr * bt_im + ci * bt_re


def _s5_disc_fwd(*params):
    def kern(*refs):
        outs = _s5_disc(*[r[...] for r in refs[:5]])
        for r, v in zip(refs[5:], outs):
            r[...] = v

    shapes = [SDS((NG, NS), f32), SDS((NG, NS), f32), SDS((NG * GS, NS), f32), SDS((NG * GS, NS), f32)]
    return pl.pallas_call(kern, name="s5_disc_fwd", out_shape=shapes)(*params)


def _s5_disc_bwd(params, cts):
    def kern(*refs):
        _, vjp = jax.vjp(_s5_disc, *[r[...] for r in refs[:5]])
        for r, v in zip(refs[9:], vjp(tuple(r[...] for r in refs[5:9]))):
            r[...] = v

    return pl.pallas_call(kern, name="s5_disc_bwd", out_shape=[SDS(p.shape, f32) for p in params])(*params, *cts)


def _diag_mask():
    r = np.arange(256)[:, None] // GS
    c = np.arange(1024)[None, :] // NS
    return jnp.asarray(r == c, f32)


def _to_blockdiag(m):
    return jnp.tile(m.reshape(2, 256, NS), (1, 1, 16)) * _diag_mask()


def _from_blockdiag(dm):
    return (dm * _diag_mask()).reshape(2, 256, 16, NS).sum(axis=2).reshape(NG * GS, NS)


def _conv_shifts(win, sh, tm):
    win[tm + 32:, :] = jnp.zeros((8, CW), f32)
    for s in range(8):
        sh[s] = win[s:s + tm + 32, :]


def _conv_taps(offset, w_ref, sh, r, accs):
    for k in range(CK):
        s, q = offset(k) % 8, offset(k) // 8
        wk = w_ref[k:k + 1, :]
        accs = [acc + wk * sh[s, pl.ds(pl.multiple_of(r + 8 * (a + q), 8), 8), :] for a, acc in enumerate(accs)]
    return accs


def _conv_fwd(z, w32, b, tm, comm=_NO_COMM):
    L = z.shape[0]
    hb = tm // 32
    cins = [*comm[0], *comm[1]]
    nc = len(cins)
    cshapes, csems = _comm_shapes(comm)

    def kern(val, glu, hval, hglu, w_ref, b_ref, *rest):
        cin, c_ref, cout, (win, sh, *sems) = rest[:nc], rest[nc], rest[nc + 1:2 * nc + 1], rest[2 * nc + 1:]
        i = pl.program_id(0)
        finish = _comm_hook(comm, cin, cout, sems, i, L // tm - 1)
        win[0:32, :] = hval[...] * jax.nn.sigmoid(hglu[...]) * (i > 0).astype(f32)
        win[32:tm + 32, :] = val[...] * jax.nn.sigmoid(glu[...])
        _conv_shifts(win, sh, tm)

        def strip(j, carry):
            r = j * 16
            accs = _conv_taps(lambda k: k + 2, w_ref, sh, r, [jnp.zeros((8, CW), f32) + b_ref[...]] * 2)
            c_ref[pl.ds(pl.multiple_of(r, 16), 16), :] = jnp.concatenate(accs, axis=0)
            return carry

        lax.fori_loop(0, tm // 16, strip, 0)
        finish()

    halo = lambda col: pl.BlockSpec((32, CW), lambda i: (jnp.maximum(i * hb - 1, 0), col))
    return pl.pallas_call(
        kern, name="conv_fwd", grid=(L // tm,),
        in_specs=[_row(tm, CW, 2), _row(tm, CW, 3), halo(2), halo(3), _full(w32.shape), _full(b.shape)] + [_ANY] * nc,
        out_specs=[_row(tm, CW)] + [_ANY] * nc, out_shape=[SDS((L, CW), f32)] + cshapes,
        scratch_shapes=[pltpu.VMEM((tm + 40, CW), f32), pltpu.VMEM((8, tm + 32, CW), f32)] + csems, compiler_params=_cp(),
    )(z, z, z, z, w32, b, *cins)


def _conv_bwd(z, dc, w32, tm):
    L = z.shape[0]
    hb = tm // 32
    nt = L // tm

    def kern(val, glu, hval, hglu, dc_ref, hdc, w_ref, dval_ref, dglu_ref, dw_ref, db_ref, win, sh, dwin, dsh):
        i = pl.program_id(0)

        @pl.when(i == 0)
        def _():
            dw_ref[...] = jnp.zeros_like(dw_ref)
            db_ref[...] = jnp.zeros_like(db_ref)

        win[0:32, :] = hval[...] * jax.nn.sigmoid(hglu[...]) * (i > 0).astype(f32)
        win[32:tm + 32, :] = val[...] * jax.nn.sigmoid(glu[...])
        _conv_shifts(win, sh, tm)
        dwin[0:tm, :] = dc_ref[...]
        dwin[tm:tm + 32, :] = hdc[...] * (i < nt - 1).astype(f32)
        _conv_shifts(dwin, dsh, tm)

        def strip(j, carry):
            r = j * 16
            rows = pl.ds(pl.multiple_of(r, 16), 16)
            dv = jnp.concatenate(_conv_taps(lambda k: 30 - k, w_ref, dsh, r, [jnp.zeros((8, CW), f32)] * 2), axis=0)
            sg = jax.nn.sigmoid(glu[rows, :])
            dval_ref[rows, :] = dv * sg
            dglu_ref[rows, :] = dv * val[rows, :] * sg * (1.0 - sg)
            return carry

        lax.fori_loop(0, tm // 16, strip, 0)

        for k0 in range(0, CK, 2):
            taps = [k for k in (k0, k0 + 1) if k < CK]

            def rows64(j, accs, taps=taps):
                for u in range(8):
                    r = pl.multiple_of(j * 64 + u * 8, 8)
                    dcv = dc_ref[pl.ds(r, 8), :]
                    accs = [acc + dcv * sh[(k + 2) % 8, pl.ds(pl.multiple_of(r + 8 * ((k + 2) // 8), 8), 8), :]
                            for k, acc in zip(taps, accs)]
                return accs

            accs = lax.fori_loop(0, tm // 64, rows64, [jnp.zeros((8, CW), f32)] * len(taps))
            for k, acc in zip(taps, accs):
                dw_ref[k:k + 1, :] += jnp.sum(acc, axis=0, keepdims=True)
        db_ref[...] += jnp.sum(dc_ref[...], axis=0, keepdims=True)

    halo = lambda col: pl.BlockSpec((32, CW), lambda i: (jnp.maximum(i * hb - 1, 0), col))
    nxt = pl.BlockSpec((32, CW), lambda i: (jnp.minimum((i + 1) * hb, L // 32 - 1), 0))
    return pl.pallas_call(
        kern, name="conv_bwd", grid=(nt,),
        in_specs=[_row(tm, CW, 2), _row(tm, CW, 3), halo(2), halo(3), _row(tm, CW), nxt, _full(w32.shape)],
        out_specs=[_row(tm, CW), _row(tm, CW), _full((32, CW)), _full((1, CW))],
        out_shape=[SDS((L, CW), f32), SDS((L, CW), f32), SDS((32, CW), f32), SDS((1, CW), f32)],
        scratch_shapes=[pltpu.VMEM((tm + 40, CW), f32), pltpu.VMEM((8, tm + 32, CW), f32),
                        pltpu.VMEM((tm + 40, CW), f32), pltpu.VMEM((8, tm + 32, CW), f32)], compiler_params=_cp(),
    )(z, z, z, z, dc, dc, w32)


def _ev_out_f(y, c, ga, gb, tap, glu_b, ln_g, ln_b, glu_w, w_out, mm):
    z1 = jax.nn.gelu(y)
    ya = z1 * jax.nn.sigmoid(mm(z1, glu_w) + glu_b + tap) * jax.nn.silu(ga)
    mu = jnp.mean(c, axis=-1, keepdims=True)
    var = jnp.mean(jnp.square(c - mu), axis=-1, keepdims=True)
    cn = (c - mu) * lax.rsqrt(var + EPS) * ln_g + ln_b
    cat = jnp.concatenate([ya, jax.nn.silu(cn) * jax.nn.silu(gb)], axis=1)
    return mm(cat, w_out), (z1, cat)


def _ev_out_fwd(x, y, c, z, glu_b, ln_g, ln_b, glu_w, w_out, tm, comm=_NO_COMM):
    def body(rows, fulls):
        xv, cv, ga, gb = rows[:4]
        yv = jnp.concatenate(rows[4:], axis=1)
        gb_ref, lg_ref, lb_ref, gw_ref, wo_ref = fulls
        out, _ = _ev_out_f(yv, cv, ga, gb, 0.0, gb_ref[...], lg_ref[...], lb_ref[...], gw_ref[...], wo_ref[...], _plain_mm)
        return [xv + out], []

    return _tiled("ev_out_fwd", body, x.shape[0], tm, [(x, D, 0), (c, CW, 0), (z, SW, 1), (z, CW, 4), *[(t, 128, 0) for t in y]],
                  [glu_b, ln_g, ln_b, glu_w, w_out], [(D, f32)], comm=comm)


def _ev_out_bwd(dx1, y, c, z, glu_b, ln_g, ln_b, glu_w, w_out, tm):
    def body(rows, fulls):
        dxv, cv, ga, gb = rows[:4]
        yv = jnp.concatenate(rows[4:], axis=1)
        gb_ref, lg_ref, lb_ref, gw_ref, wo_ref = fulls
        gw, wo = gw_ref[...], wo_ref[...]
        f = lambda yv, cv, ga, gb, tap, b, lg, lb: _ev_out_f(yv, cv, ga, gb, tap, b, lg, lb, gw, wo, _mm)
        _, vjp, (z1, cat) = jax.vjp(f, yv, cv, ga, gb, jnp.zeros((tm, SW), f32), gb_ref[...], lg_ref[...], lb_ref[...],
                                    has_aux=True)
        dy, dc, dga, dgb, dtap, db, dlg, dlb = vjp(dxv)
        return [dy, dc, dga, dgb], [db, dlg, dlb, _wgrad(z1, dtap, gw), _wgrad(cat, dxv, wo)]

    return _tiled("ev_out_bwd", body, dx1.shape[0], tm, [(dx1, D, 0), (c, CW, 0), (z, SW, 1), (z, CW, 4), *[(t, 128, 0) for t in y]],
                  [glu_b, ln_g, ln_b, glu_w, w_out], [(SW, f32), (CW, f32), (SW, f32), (CW, f32)],
                  [((1, SW), f32), ((1, CW), f32), ((1, CW), f32), (glu_w.shape, f32), (w_out.shape, f32)])


def _kv_f(mem, g, tap0, tap1, w0, w1, mm):
    mn = _rms(mem, g)
    return (mm(mn, w0) + tap0, mm(mn, w1) + tap1), mn


def _kv_bwd(mem, g, w0, w1, dkv0, dkv1):
    def body(rows, fulls):
        g_ref, w0_ref, w1_ref = fulls
        w0v, w1v = w0_ref[...], w1_ref[...]
        tap = jnp.zeros((MEM_LEN, 2 * D), f32)
        _, vjp, mn = jax.vjp(lambda g_, t0, t1: _kv_f(rows[0], g_, t0, t1, w0v, w1v, _mm), g_ref[...], tap, tap, has_aux=True)
        dg, d0, d1 = vjp((rows[1], rows[2]))
        return [], [dg, _wgrad(mn, d0, w0v), _wgrad(mn, d1, w1v)]

    return _tiled("kv_bwd", body, MEM_LEN, MEM_LEN, [(mem, D, 0), (dkv0, 2 * D, 0), (dkv1, 2 * D, 0)], [g, w0, w1], [],
                  [((1, D), f32), (w0.shape, f32), (w1.shape, f32)])


def _xa_f(x, g, kv, tap, w_qg, w_o, mm, nt, nn):
    h = _rms(x, g)
    qg = mm(h, w_qg) + tap
    outs = []
    for hd in range(XH):
        q = qg[:, hd * XD:(hd + 1) * XD]
        k = kv[:, hd * XD:(hd + 1) * XD]
        v = kv[:, D + hd * XD:D + (hd + 1) * XD]
        s = nt(q, k) * (XD ** -0.5)
        e = jnp.exp(s - jnp.max(s, axis=-1, keepdims=True))
        outs.append(nn(e / jnp.sum(e, axis=-1, keepdims=True), v))
    u = jnp.concatenate(outs, axis=1) * jax.nn.silu(qg[:, D:])
    return mm(u, w_o), (h, u)


def _xa_fwd(name, x, g, kv, w_qg, w_o, tm):
    def body(rows, fulls):
        g_ref, kv_ref, wq_ref, wo_ref = fulls
        out, _ = _xa_f(rows[0], g_ref[...], kv_ref[...], 0.0, wq_ref[...], wo_ref[...], _plain_mm, _plain_nt, _plain_mm)
        return [rows[0] + out], []

    return _tiled(name, body, x.shape[0], tm, [(x, D, 0)], [g, kv, w_qg, w_o], [(D, f32)])[0]


def _xa_bwd(name, x, dxo, g, kv, w_qg, w_o, tm, comm=_NO_COMM):
    def body(rows, fulls):
        xv, dxv = rows
        g_ref, kv_ref, wq_ref, wo_ref = fulls
        wq, wo = wq_ref[...], wo_ref[...]
        f = lambda xv, gv, kvv, tap: _xa_f(xv, gv, kvv, tap, wq, wo, _mm, _mm_nt, _mm_nn)
        _, vjp, (h, u) = jax.vjp(f, xv, g_ref[...], kv_ref[...], jnp.zeros((tm, 2 * D), f32), has_aux=True)
        dx, dg, dkv, dtap = vjp(dxv)
        return [dx + dxv], [dg, dkv, _wgrad(h, dtap, wq), _wgrad(u, dxv, wo)]

    return _tiled(name, body, x.shape[0], tm, [(x, D, 0), (dxo, D, 0)], [g, kv, w_qg, w_o], [(D, f32)],
                  [((1, D), f32), ((MEM_LEN, 2 * D), f32), (w_qg.shape, f32), (w_o.shape, f32)], comm=comm)


def _attn_pad(k_ref, v_ref, kpad, vpad):
    kpad[0:PAD, :] = jnp.zeros((PAD, 128), bf16)
    vpad[0:PAD, :] = jnp.zeros((PAD, 128), bf16)
    kpad[PAD:, :] = k_ref[...].astype(bf16)
    vpad[PAD:, :] = v_ref[...].astype(bf16)


def _attn_exp(qm, kwin, bm, mask_row):
    x = _dot(qm, kwin, NT) + bm + mask_row
    e = jnp.exp(x - jnp.max(x, axis=-1, keepdims=True))
    return e, 1.0 / jnp.sum(e, axis=-1, keepdims=True)


def _attn_fwd(qkvg, bias, comm=_NO_COMM):
    L = qkvg.shape[0]
    nq = L // AQ
    cins = [*comm[0], *comm[1]]
    nc = len(cins)
    cshapes, csems = _comm_shapes(comm)

    def kern(q_ref, k_ref, v_ref, b_ref, *rest):
        cin, o_ref, cout, (kpad, vpad, *sems) = rest[:nc], rest[nc], rest[nc + 1:2 * nc + 1], rest[2 * nc + 1:]
        i = pl.program_id(1)
        finish = _comm_hook(comm, cin, cout, sems, pl.program_id(0) * nq + i, HEADS // 2 * nq - 1)

        @pl.when(i == 0)
        def _():
            _attn_pad(k_ref, v_ref, kpad, vpad)

        r0 = pl.multiple_of(i * AQ, AQ)
        q = q_ref[...] * (DH ** -0.5)
        kwin = kpad[pl.ds(r0, AW), :]
        vwin = vpad[pl.ds(r0, AW), :]
        lane_hi = lax.broadcasted_iota(jnp.int32, (AQ, 128), 1) // DH
        mask_row = jnp.where(lax.broadcasted_iota(jnp.int32, (1, AW), 1) >= PAD - r0, 0.0, -1e30)
        outs = []
        for hh in range(2):
            e, inv = _attn_exp(jnp.where(lane_hi == hh, q, 0.0), kwin, b_ref[hh], mask_row)
            outs.append(_dot(e, vwin, NN) * inv)
        o_ref[...] = jnp.where(lane_hi == 0, outs[0], outs[1])
        finish()

    return pl.pallas_call(
        kern, name="attn_fwd", grid=(HEADS // 2, nq),
        in_specs=[pl.BlockSpec((AQ, 128), lambda j, i: (i, j)),
                  pl.BlockSpec((L, 128), lambda j, i: (0, 8 + j)),
                  pl.BlockSpec((L, 128), lambda j, i: (0, 16 + j)),
                  pl.BlockSpec((2, AQ, AW), lambda j, i: (j, 0, 0))] + [_ANY] * nc,
        out_specs=[pl.BlockSpec((AQ, 128), lambda j, i: (i, j))] + [_ANY] * nc,
        out_shape=[SDS((L, D), f32)] + cshapes,
        scratch_shapes=[pltpu.VMEM((L + PAD, 128), bf16), pltpu.VMEM((L + PAD, 128), bf16)] + csems,
        compiler_params=_cp(("arbitrary", "arbitrary")),
    )(qkvg, qkvg, qkvg, bias, *cins)


def _attn_bwd(qkvg, do, bias):
    L = qkvg.shape[0]
    nq = L // AQ

    def kern(q_ref, k_ref, v_ref, b_ref, do_ref, dq_ref, dk_ref, dv_ref, db_ref, kpad, vpad, dkp, dvp):
        i = pl.program_id(1)

        @pl.when(i == 0)
        def _():
            _attn_pad(k_ref, v_ref, kpad, vpad)
            dkp[...] = jnp.zeros_like(dkp)
            dvp[...] = jnp.zeros_like(dvp)
            db_ref[...] = jnp.zeros_like(db_ref)

        r0 = pl.multiple_of(i * AQ, AQ)
        q = q_ref[...] * (DH ** -0.5)
        dov = do_ref[...]
        kwin = kpad[pl.ds(r0, AW), :]
        vwin = vpad[pl.ds(r0, AW), :]
        lane_hi = lax.broadcasted_iota(jnp.int32, (AQ, 128), 1) // DH
        mask_row = jnp.where(lax.broadcasted_iota(jnp.int32, (1, AW), 1) >= PAD - r0, 0.0, -1e30)
        dqs = []
        dk = jnp.zeros((AW, 128), f32)
        dv = jnp.zeros((AW, 128), f32)
        for hh in range(2):
            qm = jnp.where(lane_hi == hh, q, 0.0).astype(bf16)
            dom = jnp.where(lane_hi == hh, dov, 0.0).astype(bf16)
            dp = _dot(dom, vwin, NT)
            e, inv = _attn_exp(qm, kwin, b_ref[hh], mask_row)
            p = e * inv
            ds = p * (dp - jnp.sum(p * dp, axis=-1, keepdims=True))
            db_ref[hh] += ds
            dsb = ds.astype(bf16)
            dqs.append(_dot(dsb, kwin, NN) * (DH ** -0.5))
            dk = dk + _dot(dsb, qm, TN)
            dv = dv + _dot(p, dom, TN)
        dq_ref[...] = jnp.where(lane_hi == 0, dqs[0], dqs[1])
        dkp[pl.ds(r0, AW), :] += dk
        dvp[pl.ds(r0, AW), :] += dv

        @pl.when(i == nq - 1)
        def _():
            dk_ref[...] = dkp[PAD:, :]
            dv_ref[...] = dvp[PAD:, :]

    return pl.pallas_call(
        kern, name="attn_bwd", grid=(HEADS // 2, nq),
        in_specs=[pl.BlockSpec((AQ, 128), lambda j, i: (i, j)),
                  pl.BlockSpec((L, 128), lambda j, i: (0, 8 + j)),
                  pl.BlockSpec((L, 128), lambda j, i: (0, 16 + j)),
                  pl.BlockSpec((2, AQ, AW), lambda j, i: (j, 0, 0)),
                  pl.BlockSpec((AQ, 128), lambda j, i: (i, j))],
        out_specs=[pl.BlockSpec((AQ, 128), lambda j, i: (i, j)),
                   pl.BlockSpec((L, 128), lambda j, i: (0, j)),
                   pl.BlockSpec((L, 128), lambda j, i: (0, j)),
                   pl.BlockSpec((2, AQ, AW), lambda j, i: (j, 0, 0))],
        out_shape=[SDS((L, D), f32), SDS((L, D), f32), SDS((L, D), f32), SDS((HEADS, AQ, AW), f32)],
        scratch_shapes=[pltpu.VMEM((L + PAD, 128), bf16), pltpu.VMEM((L + PAD, 128), bf16),
                        pltpu.VMEM((L + PAD, 128), f32), pltpu.VMEM((L + PAD, 128), f32)],
        compiler_params=_cp(("arbitrary", "arbitrary")),
    )(qkvg, qkvg, qkvg, bias, do)


_SKEW = AQ + AW - 1


def _rel_index():
    d = (AW - 1) - np.arange(_SKEW)
    return np.clip(d, -MAX_REL, MAX_REL) + MAX_REL


def _band_mask():
    qc = np.arange(AQ)[:, None] // CHUNK + LEFT
    kc = np.arange(AW)[None, :] // CHUNK
    return jnp.asarray(np.where((kc <= qc) & (kc >= qc - LEFT), 0.0, -1e30), f32)


def _bias_matrix(rel_bias):
    tv = jnp.take(rel_bias, jnp.asarray(_rel_index()), axis=1)
    flat = jnp.tile(jnp.pad(tv, ((0, 0), (0, 1))), (1, AQ))
    top = flat[:, AQ - 1:AQ - 1 + AQ * _SKEW].reshape(HEADS, AQ, _SKEW)[:, :, :AW]
    return top + _band_mask()


def _bias_grad(dbias, onehot):
    flat = jnp.pad(dbias, ((0, 0), (0, 0), (0, _SKEW - AW))).reshape(HEADS, AQ * _SKEW)
    z = jnp.pad(flat, ((0, 0), (AQ - 1, AQ * 1024 - AQ * _SKEW - (AQ - 1)))).reshape(HEADS, AQ, 1024)

    def kern(z_ref, oh_ref, out_ref):
        diag = jnp.sum(z_ref[...], axis=1)
        out_ref[...] = jnp.dot(diag, oh_ref[...], preferred_element_type=f32, precision=lax.Precision.HIGHEST)

    return pl.pallas_call(kern, name="bias_grad", out_shape=SDS((HEADS, 2 * MAX_REL + 1), f32),
                          compiler_params=_cp(()))(z, onehot)


def _rel_onehot():
    oh = np.zeros((1024, 2 * MAX_REL + 1), np.float32)
    oh[np.arange(_SKEW), _rel_index()] = 1.0
    return jnp.asarray(oh)


def _gated_out_fwd(name, x, o, gate_src, gate_col, w, tm):
    def body(rows, fulls):
        xv, ov, gv = rows
        return [xv + _dotw(ov * jax.nn.silu(gv), fulls[0][...])], []

    return _tiled(name, body, x.shape[0], tm, [(x, D, 0), (o, D, 0), (gate_src, D, gate_col)], [w], [(D, f32)])[0]


def _gated_out_bwd(name, dxo, o, gate_src, gate_col, w, tm):
    def body(rows, fulls):
        dxv, ov, gv = rows
        wv = fulls[0][...]
        u, vjp = jax.vjp(lambda ov, gv: ov * jax.nn.silu(gv), ov, gv)
        do, dg = vjp(_dotw_t(dxv, wv))
        return [do, dg], [_wgrad(u, dxv, wv)]

    return _tiled(name, body, dxo.shape[0], tm, [(dxo, D, 0), (o, D, 0), (gate_src, D, gate_col)], [w],
                  [(D, f32), (D, f32)], [(w.shape, f32)])


def _loss_head(x, g, target, tm):
    def body(rows, fulls):
        xv, tv = rows

        def f(xv, gv):
            e = jnp.square(_rms(xv, gv) - tv)
            return 0.5 * jnp.sum(jnp.mean(e, axis=-1, keepdims=True), axis=0, keepdims=True)

        loss, vjp = jax.vjp(f, xv, fulls[0][...])
        dx, dg = vjp(jnp.ones((1, 1), f32))
        return [dx], [jnp.broadcast_to(loss, (1, 128)), dg]

    return _tiled("loss_head", body, x.shape[0], tm, [(x, D, 0), (target, D, 0)], [g], [(D, f32)],
                  [((1, 128), f32), ((1, D), f32)])


def _local_step(x, mem, target, p, shards, views, pack_early, pack_mid, pack_small):
    TM, TMF = 256, 512
    row = lambda v: v.reshape(1, -1)

    bt = lambda b: jnp.transpose(b, (0, 2, 1)).reshape(NG * GS, NS)
    disc_in = (p["ev_s5_lambda_re"], p["ev_s5_lambda_im"], p["ev_s5_log_dt"].reshape(NG, 1),
               bt(p["ev_s5_b_re"]), bt(p["ev_s5_b_im"]))
    ab_re, ab_im, bbt_re, bbt_im = _s5_disc_fwd(*disc_in)
    bmat = (_to_blockdiag(bbt_re).astype(bf16), _to_blockdiag(bbt_im).astype(bf16))
    cmat = (_to_blockdiag(p["ev_s5_c_re"].reshape(NG * GS, NS)).astype(bf16),
            _to_blockdiag(p["ev_s5_c_im"].reshape(NG * GS, NS)).astype(bf16))
    a_re, a_im = ab_re.reshape(1, NST), ab_im.reshape(1, NST)
    d_skip = row(p["ev_s5_d"])
    w32 = jnp.pad(p["ev_conv_w"], ((0, 1), (0, 0)))
    conv_b, ln_g, ln_b, glu_b = row(p["ev_conv_b"]), row(p["ev_conv_ln_g"]), row(p["ev_conv_ln_b"]), row(p["ev_s5_glu_b"])
    g_mem, g_ev, g_od, g_fin = row(p["mem_norm_g"]), row(p["ev_norm_g"]), row(p["od_norm_g"]), row(p["final_norm_g"])
    g_xa = [row(p["xa_norm_g"][l]) for l in range(2)]
    bias = _bias_matrix(p["od_rel_bias"])

    z, *got = _norm_mm_fwd("ev_in_fwd", x, g_ev, p["ev_w_in"], TMF, comm=((), shards["ev_in"]))
    p = views(p, "ev_in", got)
    s5_out = _s5_fwd(z, bmat, cmat, a_re, a_im, d_skip, comm=((), shards["s5"]))
    y_s5, (hb_re, hb_im), p = s5_out[:4], s5_out[4:6], views(p, "s5", s5_out[6:])
    c, *got = _conv_fwd(z, w32, conv_b, 512, comm=((), shards["conv"]))
    p = views(p, "conv", got)
    x1, *got = _ev_out_fwd(x, y_s5, c, z, glu_b, ln_g, ln_b, p["ev_s5_glu_w"], p["ev_w_out"], TMF, comm=((), shards["ev_out"]))
    p = views(p, "ev_out", got)
    kv0 = _norm_mm_fwd("kv0_fwd", mem, g_mem, p["xa_w_kv"][0], MEM_LEN)
    x2 = _xa_fwd("xa0_fwd", x1, g_xa[0], kv0, p["xa_w_qg"][0], p["xa_w_o"][0], TMF)
    qkvg = _norm_mm_fwd("od_in_fwd", x2, g_od, p["od_w_in"], TMF)
    o, *got = _attn_fwd(qkvg, bias, comm=((), shards["attn"]))
    p = views(p, "attn", got)
    wqg, wkv, wo = p["xa_w_qg"], p["xa_w_kv"], p["xa_w_o"]
    x3 = _gated_out_fwd("od_out_fwd", x2, o, qkvg, 3, p["od_w_out"], TMF)
    kv1 = _norm_mm_fwd("kv1_fwd", mem, g_mem, wkv[1], MEM_LEN)
    x4 = _xa_fwd("xa1_fwd", x3, g_xa[1], kv1, wqg[1], wo[1], TMF)

    G = {}
    dx4, loss_row, G["final_norm_g"] = _loss_head(x4, g_fin, target, TMF)
    dx3, dg_xa1, dkv1, dwqg1, dwo1 = _xa_bwd("xa1_bwd", x3, dx4, g_xa[1], kv1, wqg[1], wo[1], TM)
    do, dgate, G["od_w_out"] = _gated_out_bwd("od_out_bwd", dx3, o, qkvg, 3, p["od_w_out"], TMF)
    dq, dk, dv, dbias = _attn_bwd(qkvg, do, bias)
    G["od_rel_bias"] = _bias_grad(dbias, _rel_onehot())
    dx2, G["od_norm_g"], G["od_w_in"] = _norm_mm_bwd(
        "od_in_bwd", x2, g_od, p["od_w_in"], [(dq, D, 0), (dk, D, 0), (dv, D, 0), (dgate, D, 0)], dx3, TM)
    G["xa_w_qg"], G["xa_w_o"] = [None, dwqg1], [None, dwo1]
    dx1, dg_xa0, dkv0, G["xa_w_qg"][0], G["xa_w_o"][0], *recv_early = _xa_bwd(
        "xa0_bwd", x1, dx2, g_xa[0], kv0, wqg[0], wo[0], TM, comm=(pack_early(G), ()))
    G["xa_norm_g"] = jnp.concatenate([dg_xa0, dg_xa1], axis=0)
    G["mem_norm_g"], dwkv0, dwkv1 = _kv_bwd(mem, g_mem, wkv[0], wkv[1], dkv0, dkv1)
    G["xa_w_kv"] = [dwkv0, dwkv1]
    (dy_s5, dc, dga, dgb, G["ev_s5_glu_b"], G["ev_conv_ln_g"], G["ev_conv_ln_b"], G["ev_s5_glu_w"],
     G["ev_w_out"]) = _ev_out_bwd(dx1, y_s5, c, z, glu_b, ln_g, ln_b, p["ev_s5_glu_w"], p["ev_w_out"], TMF)
    dval, dglu, dw32, G["ev_conv_b"] = _conv_bwd(z, dc, w32, 512)
    G["ev_conv_w"] = dw32[:CK]
    s5_out = _s5_bwd(z, dy_s5, hb_re, hb_im, bmat, cmat, a_re, a_im, d_skip, comm=(pack_mid(G), ()))
    du, (dbre, dbim, dcre, dcim, da_re, da_im, G["ev_s5_d"]), recv_mid = s5_out[:4], s5_out[4:11], s5_out[11:]
    G["ev_s5_c_re"], G["ev_s5_c_im"] = _from_blockdiag(dcre), _from_blockdiag(dcim)
    G["ev_s5_lambda_re"], G["ev_s5_lambda_im"], G["ev_s5_log_dt"], G["ev_s5_b_re"], G["ev_s5_b_im"] = _s5_disc_bwd(
        disc_in, (da_re.reshape(NG, NS), da_im.reshape(NG, NS), _from_blockdiag(dbre), _from_blockdiag(dbim)))
    dx, G["ev_norm_g"], G["ev_w_in"], *recv_small = _norm_mm_bwd(
        "ev_in_bwd", x, g_ev, p["ev_w_in"], [*[(t, 128, 0) for t in du], (dga, SW, 0), (dval, CW, 0), (dglu, CW, 0), (dgb, CW, 0)], dx1, TM,
        comm=pack_small(G))
    return loss_row[0, 0], dx, G, recv_early, recv_mid, recv_small


def _all_gather(payloads):
    n = len(payloads)

    def body(*refs):
        ins, outs = refs[:n], refs[n:2 * n]
        send_sems, recv_sems, local_sems = refs[2 * n:]
        x, y, c = _mesh_pos()
        me, sibling = (x, y, c), (x, y, 1 - c)
        chips = [(1 - x, y), (x, 1 - y), (1 - x, 1 - y)]

        def copy(p, k, block, to, src=None):
            return pltpu.make_async_remote_copy(
                src_ref=_slot(outs[p], block) if src is None else src, dst_ref=_slot(outs[p], block),
                send_sem=send_sems.at[7 * p + k], recv_sem=recv_sems.at[7 * p + k],
                device_id=to, device_id_type=pl.DeviceIdType.MESH)

        mine = [pltpu.make_async_copy(ins[p], _slot(outs[p], me), local_sems.at[p]) for p in range(n)]
        for cp in mine:
            cp.start()
        first = []
        for p in range(n):
            first.append(copy(p, 0, me, sibling, src=ins[p]))
            first += [copy(p, 1 + j, me, (*chip, c), src=ins[p]) for j, chip in enumerate(chips)]
        for cp in first:
            cp.start()
        passed = []
        for j, chip in enumerate(chips):
            for p in range(n):
                copy(p, 1 + j, (*chip, c), me).wait_recv()
                passed.append(copy(p, 4 + j, (*chip, c), sibling))
                passed[-1].start()
        for p in range(n):
            copy(p, 0, sibling, me).wait_recv()
            for j, chip in enumerate(chips):
                copy(p, 4 + j, (*chip, 1 - c), me).wait_recv()
        for cp in first + passed:
            cp.wait_send()
        for cp in mine:
            cp.wait()

    return pl.pallas_call(
        body, name="all_gather", in_specs=[_ANY] * n, out_specs=[_ANY] * n,
        out_shape=[SDS((NDEV, *a.shape), a.dtype) for a in payloads],
        scratch_shapes=[pltpu.SemaphoreType.DMA((7 * n,)), pltpu.SemaphoreType.DMA((7 * n,)), pltpu.SemaphoreType.DMA((n,))],
    )(*payloads)


def _exchange(slotted, whole):
    n = len(slotted) + len(whole)
    shapes, sems = _comm_shapes((slotted, whole))

    def body(*refs):
        copies = _comm_copies(len(slotted), refs[:n], refs[n:2 * n], *refs[2 * n:])
        for cp in copies:
            cp.start()
        for cp in copies:
            cp.wait()

    return pl.pallas_call(body, name="grad_exchange", in_specs=[_ANY] * n, out_specs=[_ANY] * n, out_shape=shapes,
                          scratch_shapes=sems)(*slotted, *whole)


def _adamw_math(g, w, m, v):
    m2 = ADAM_B1 * m + (1.0 - ADAM_B1) * g
    v2 = ADAM_B2 * v + (1.0 - ADAM_B2) * jnp.square(g)
    m_hat = m2 / (1.0 - ADAM_B1 ** ADAM_STEP)
    v_hat = v2 / (1.0 - ADAM_B2 ** ADAM_STEP)
    return -ADAM_LR * (m_hat / (jnp.sqrt(v_hat) + ADAM_EPS) + ADAM_WD * w), m2, v2


def _sum_slots(r_ref, rows=None):
    acc = r_ref[0].astype(f32) if rows is None else r_ref[0, rows[0]:rows[1], :].astype(f32)
    for k in range(1, NDEV):
        acc = acc + (r_ref[k] if rows is None else r_ref[k, rows[0]:rows[1], :]).astype(f32)
    return acc


def _sum_adamw(name, sources, w, m, v, tr):
    rows, C = w.shape
    starts = [int(t) for t in np.cumsum([0] + [r // tr for (_, _, r) in sources])]

    def kern(*refs):
        r_refs = refs[:len(sources)]
        w_ref, m_ref, v_ref, g_ref, d_ref, m2_ref, v2_ref = refs[len(sources):]
        g = _sum_slots(r_refs[0])
        for s in range(1, len(sources)):
            g = jnp.where(pl.program_id(0) >= starts[s], _sum_slots(r_refs[s]), g)
        g_ref[...] = g
        d_ref[...], m2_ref[...], v2_ref[...] = _adamw_math(g, w_ref[...], m_ref[...], v_ref[...])

    def src_spec(s):
        _, off, r = sources[s]
        return pl.BlockSpec((NDEV, tr, C), lambda i: (0, off // tr + jnp.clip(i - starts[s], 0, r // tr - 1), 0))

    blk = pl.BlockSpec((tr, C), lambda i: (i, 0))
    return pl.pallas_call(
        kern, name=name, grid=(rows // tr,),
        in_specs=[src_spec(s) for s in range(len(sources))] + [blk, blk, blk],
        out_specs=[blk] * 4, out_shape=[SDS((rows, C), f32)] * 4, compiler_params=_cp(),
    )(*[r for (r, _, _) in sources], w, m, v)


def _small_adamw(recvs, table, wmv):
    rnames = list(recvs)
    nr, nw = len(rnames), len(table)

    def kern(*refs):
        rr = dict(zip(rnames, refs[:nr]))
        ins, outs = refs[nr:nr + 3 * nw], refs[nr + 3 * nw:]
        for i, (_, shape, src, r0) in enumerate(table):
            g = _sum_slots(rr[src], (r0, r0 + shape[0]))[:, :shape[1]]
            outs[4 * i][...] = g
            res = _adamw_math(g, ins[3 * i][...], ins[3 * i + 1][...], ins[3 * i + 2][...])
            for o, val in zip(outs[4 * i + 1:4 * i + 4], res):
                o[...] = val

    flat = [t for (n, _, _, _) in table for t in wmv[n]]
    res = pl.pallas_call(
        kern, name="small_adamw", out_shape=[SDS(shape, f32) for (_, shape, _, _) in table for _ in range(4)],
        compiler_params=_cp(()),
    )(*[recvs[n] for n in rnames], *flat)
    return {n: tuple(res[4 * i:4 * i + 4]) for i, (n, _, _, _) in enumerate(table)}


WEIGHTS = ["mem_norm_g", "ev_norm_g", "ev_w_in", "ev_s5_lambda_re", "ev_s5_lambda_im", "ev_s5_log_dt", "ev_s5_b_re",
           "ev_s5_b_im", "ev_s5_c_re", "ev_s5_c_im", "ev_s5_d", "ev_s5_glu_w", "ev_s5_glu_b", "ev_conv_w", "ev_conv_b",
           "ev_conv_ln_g", "ev_conv_ln_b", "ev_w_out", "od_norm_g", "od_w_in", "od_rel_bias", "od_w_out", "xa_norm_g",
           "xa_w_qg", "xa_w_kv", "xa_w_o", "final_norm_g"]
GATHER = {"first": [("g320", 320, [("ev_w_in", None, "cols", 1024)])],
          "ev_in": [("h1024", 1024, [("ev_w_out", None, "rows", 128)]), ("h512", 512, [("ev_s5_glu_w", None, "rows", 64)])],
          "s5": [("i512", 512, [("od_w_in", None, "cols", 1024)]), ("i1024", 1024, [("xa_w_o", 0, "rows", 128)])],
          "conv": [("j256", 256, [("xa_w_qg", 0, "cols", 1024)])],
          "ev_out": [("k256", 256, [("xa_w_kv", 0, "cols", 1024)])],
          "attn": [("l1024", 1024, [("od_w_out", None, "rows", 128), ("xa_w_o", 1, "rows", 128)]),
                   ("l256", 256, [("xa_w_qg", 1, "cols", 1024), ("xa_w_kv", 1, "cols", 1024)])]}
FIRST = [("a1024", 1024, [("ev_w_out", None, "rows", 128), ("xa_w_o", 0, "rows", 128)]),
         ("a512", 512, [("ev_s5_glu_w", None, "rows", 64)]),
         ("a256", 256, [("xa_w_qg", 0, "cols", 1024), ("xa_w_kv", 0, "cols", 1024), ("xa_w_kv", 1, "cols", 1024)]),
         ("a320", 320, [("ev_w_in", None, "cols", 1024)])]
LATE = [("b1024", 1024, [("od_w_out", None, "rows", 128), ("xa_w_o", 1, "rows", 128)]),
        ("b512", 512, [("od_w_in", None, "cols", 1024)]),
        ("b256", 256, [("xa_w_qg", 1, "cols", 1024)])]
MATMUL_WEIGHTS = ["ev_w_out", "xa_w_o", "ev_s5_glu_w", "xa_w_qg", "xa_w_kv", "ev_w_in", "od_w_out", "od_w_in"]
ADAM_ROWS = {1024: 128, 512: 64, 256: 256, 320: 256}
SMALL = [("mem_norm_g", (1, D), "s1024", 0), ("ev_norm_g", (1, D), "s1024", 1), ("xa_norm_g", (2, D), "s1024", 2),
         ("final_norm_g", (1, D), "s1024", 4),
         ("ev_s5_d", (1, SW), "s512", 0), ("ev_s5_glu_b", (1, SW), "s512", 1), ("ev_conv_b", (1, CW), "s512", 2),
         ("ev_conv_ln_g", (1, CW), "s512", 3), ("ev_conv_ln_b", (1, CW), "s512", 4),
         ("ev_s5_c_re", (NG * GS, NS), "s64", 0), ("ev_s5_c_im", (NG * GS, NS), "s64", 512),
         ("ev_s5_b_re", (NG * GS, NS), "s64", 1024), ("ev_s5_b_im", (NG * GS, NS), "s64", 1536),
         ("ev_s5_lambda_re", (NG, NS), "s64", 2048), ("ev_s5_lambda_im", (NG, NS), "s64", 2080),
         ("ev_s5_log_dt", (1, NG), "s64", 2112),
         ("od_rel_bias", (HEADS, 2 * MAX_REL + 1), "s257", 0),
         ("ev_conv_w", (CK, CW // NDEV), "cw", 0), ("od_norm_g", (1, D // NDEV), "on", 0)]
S5_B = ("ev_s5_b_re", "ev_s5_b_im")


def _pad_rows(a, rows):
    return jnp.pad(a, ((0, rows - a.shape[0]), (0, 0)))


def _shard_payloads(a, groups):
    pick = lambda n, l, minor: (a[n] if l is None else a[n][l]).reshape(-1, minor)
    return [jnp.concatenate([pick(n, l, minor).astype(bf16) for n, l, _, _ in members], axis=0) for _, minor, members in groups]


def _weight_views(p, groups, gathered):
    p = dict(p)
    for (_, _, members), buf in zip(groups, gathered):
        off = 0
        for n, l, kind, rows in members:
            view = _W(buf, rows, off // rows, kind)
            if l is None:
                p[n] = view
            else:
                p[n] = list(p.get(n, [None, None]))
                p[n][l] = view
            off += rows
    return p


def _slotted_grads(G, groups):
    slot3 = lambda g, rows: g.reshape(NDEV, rows, g.shape[-1]) if g.ndim == 2 else g
    return [jnp.concatenate([slot3(G[n] if l is None else G[n][l], rows).astype(bf16) for n, l, _, rows in members], axis=1)
            for _, _, members in groups]


def _bt(b):
    return jnp.transpose(b.reshape(NG, NS, GS), (0, 2, 1)).reshape(NG * GS, NS)


def _unbt(b, shape):
    return jnp.transpose(b.reshape(NG, GS, NS), (0, 2, 1)).reshape(shape)


def kernel(*args):
    names = ["x", "mem", *WEIGHTS, "loss_target", *["m_" + n for n in WEIGHTS], *["v_" + n for n in WEIGHTS]]
    a = dict(zip(names, args, strict=True))

    other = jnp.concatenate([_pad_rows(a["ev_conv_w"][0], 32).reshape(16, 128), _pad_rows(a["od_norm_g"], 8)], axis=0)
    g_in, gother = _all_gather([*_shard_payloads(a, GATHER["first"]), other])
    p = {n: (a[n] if n in ("xa_norm_g", "mem_norm_g", "final_norm_g") else a[n][0]) for n in WEIGHTS if n not in MATMUL_WEIGHTS}
    p["ev_w_in"] = jnp.transpose(g_in, (1, 0, 2)).reshape(D, EVEN_IN)
    p["ev_conv_w"] = jnp.transpose(gother[:, :16].reshape(NDEV, 32, CW // NDEV)[:, :CK], (1, 0, 2)).reshape(CK, CW)
    p["od_norm_g"] = gother[:, 16].reshape(D)

    def pack_small(G):
        cw = jnp.transpose(G["ev_conv_w"].reshape(CK, NDEV, CW // NDEV), (1, 0, 2))
        whole = [_pad_rows(jnp.concatenate([G[n] for n in ("ev_s5_d", "ev_s5_glu_b", "ev_conv_b", "ev_conv_ln_g", "ev_conv_ln_b")]), 8),
                 _pad_rows(jnp.concatenate([G[n] for n in ("ev_s5_c_re", "ev_s5_c_im", "ev_s5_b_re", "ev_s5_b_im",
                                                           "ev_s5_lambda_re", "ev_s5_lambda_im")]
                                           + [jnp.pad(G["ev_s5_log_dt"].reshape(1, NG), ((0, 0), (0, NS - NG)))]), 2128).astype(bf16),
                 G["od_rel_bias"]]
        return [jnp.pad(cw, ((0, 0), (0, 1), (0, 0))), G["od_norm_g"].reshape(NDEV, 1, D // NDEV)], whole

    loss_part, dx, G, recv_early, recv_mid, recv_small = _local_step(
        a["x"][0], a["mem"][0], a["loss_target"][0], p, {s: _shard_payloads(a, g) for s, g in GATHER.items()},
        lambda p, stage, gathered: _weight_views(p, GATHER[stage], gathered),
        lambda G: _slotted_grads(G, LATE), lambda G: _slotted_grads(G, FIRST[:3]), pack_small)

    G["ev_w_in"] = jnp.transpose(G["ev_w_in"].reshape(D, NDEV, EVEN_IN // NDEV), (1, 0, 2))
    gains = _pad_rows(jnp.concatenate([G[n] for n in ("mem_norm_g", "ev_norm_g", "xa_norm_g", "final_norm_g")]), 8)
    recv_last = _exchange(_slotted_grads(G, FIRST[3:]), [gains])
    recvs = dict(zip([g[0] for g in LATE], recv_early))
    recvs.update(zip([g[0] for g in FIRST[:3]], recv_mid))
    recvs.update(zip(["cw", "on", "s512", "s64", "s257"], recv_small))
    recvs.update(zip(["a320", "s1024"], recv_last))

    sources = {}
    for buf, minor, members in FIRST + LATE:
        off = 0
        for n, l, _, rows in members:
            sources.setdefault(n, []).append((l or 0, recvs[buf], off, rows, minor))
            off += rows
    res = {}
    for n, src in sources.items():
        src = sorted(src, key=lambda t: t[0])
        minor = src[0][4]
        w, m, v = (a[k + n].reshape(-1, minor) for k in ("", "m_", "v_"))
        res[n] = _sum_adamw("adamw_" + n, [(r, off, rows) for _, r, off, rows, _ in src], w, m, v, ADAM_ROWS[minor])
    two_d = lambda n, t: _bt(t[0]) if n in S5_B else t.reshape(dict((s[0], s[1]) for s in SMALL)[n])
    res.update(_small_adamw({k: recvs[k] for k in ("s1024", "s512", "s64", "s257", "cw", "on")}, SMALL,
                            {n: tuple(two_d(n, a[k + n]) for k in ("", "m_", "v_")) for n, _, _, _ in SMALL}))
    shaped = lambda n, t: _unbt(t, a[n].shape) if n in S5_B else t.reshape(a[n].shape)
    loss = lax.psum(loss_part, AXES)
    return (loss, dx[None], *[shaped(n, res[n][k]) for k in range(4) for n in WEIGHTS])
```

```python
import functools
from typing import NamedTuple

import jax
import jax.numpy as jnp
import numpy as np
from jax import lax
from jax.experimental import pallas as pl
from jax.experimental.pallas import tpu as pltpu

f32, bf16 = jnp.float32, jnp.bfloat16
SDS = jax.ShapeDtypeStruct

D = 1024
SW = 512
NG, GS, NS = 32, 16, 64
NST = NG * NS
CW = 512
CK = 31
EVEN_IN = 2 * SW + 3 * CW
HEADS, DH = 16, 64
CHUNK, LEFT = 64, 8
PAD = LEFT * CHUNK
MAX_REL = 128
MEM_LEN = 256
XH, XD = 4, 256
EPS = 1e-6
NDEV = 8
AXES = ("x", "y", "c")

ADAM_LR, ADAM_B1, ADAM_B2, ADAM_EPS, ADAM_WD, ADAM_STEP = 0.001, 0.9, 0.999, 1e-08, 0.01, 10

VMEM_LIMIT = 56 << 20
S5_T = 512
S5_CH = 512
AQ = 256
AW = AQ + PAD

NN = ((1,), (0,))
NT = ((1,), (1,))
TN = ((0,), (0,))


def _dot(a, b, dims):
    return lax.dot_general(a.astype(bf16), b.astype(bf16), (dims, ((), ())), preferred_element_type=f32)


def _dotw(a, w):
    if w.ndim == 2:
        return _dot(a, w, NN)
    return jnp.concatenate([_dot(a, w[j], NN) for j in range(w.shape[0])], axis=1)


def _dotw_t(g, w):
    if w.ndim == 2:
        return _dot(g, w, NT)
    n = w.shape[2]
    out = _dot(g[:, :n], w[0], NT)
    for j in range(1, w.shape[0]):
        out = out + _dot(g[:, j * n:(j + 1) * n], w[j], NT)
    return out


def _wgrad(a, g, w):
    if w.ndim == 2:
        return _dot(a, g, TN)
    n = w.shape[2]
    return jnp.stack([_dot(a, g[:, j * n:(j + 1) * n], TN) for j in range(w.shape[0])])


@jax.custom_vjp
def _mm(a, w):
    return _dotw(a, w)


def _mm_f(a, w):
    return _dotw(a, w), w


def _mm_b(w, g):
    return _dotw_t(g, w), None


_mm.defvjp(_mm_f, _mm_b)


@jax.custom_vjp
def _mm_nt(a, b):
    return _dot(a, b, NT)


def _mm_nt_f(a, b):
    return _dot(a, b, NT), (a, b)


def _mm_nt_b(res, g):
    a, b = res
    return _dot(g, b, NN), _dot(g, a, TN)


_mm_nt.defvjp(_mm_nt_f, _mm_nt_b)


@jax.custom_vjp
def _mm_nn(a, b):
    return _dot(a, b, NN)


def _mm_nn_f(a, b):
    return _dot(a, b, NN), (a, b)


def _mm_nn_b(res, g):
    a, b = res
    return _dot(g, b, NT), _dot(a, g, TN)


_mm_nn.defvjp(_mm_nn_f, _mm_nn_b)


def _plain_mm(a, w):
    return _dotw(a, w)


def _plain_nt(a, b):
    return _dot(a, b, NT)


def _rms(x, g):
    return x * lax.rsqrt(jnp.mean(x * x, axis=-1, keepdims=True) + EPS) * g


def _cp(sem=("arbitrary",)):
    return pltpu.CompilerParams(dimension_semantics=sem, vmem_limit_bytes=VMEM_LIMIT)


def _row(tm, w, col=0):
    return pl.BlockSpec((tm, w), lambda i: (i, col))


def _full(shape):
    nd = len(shape)
    return pl.BlockSpec(tuple(shape), lambda i: (0,) * nd, pipeline_mode=pl.Buffered(1))


class _W(NamedTuple):
    arr: jax.Array
    rows: int
    idx: int
    kind: str

    @property
    def shape(self):
        c = self.arr.shape[2]
        return (NDEV * self.rows, c) if self.kind == "rows" else (NDEV, self.rows, c)


class _WRef:
    def __init__(self, ref, kind):
        self.ref, self.kind = ref, kind

    def __getitem__(self, _):
        v = self.ref[...]
        return v.reshape(v.shape[0] * v.shape[1], v.shape[2]) if self.kind == "rows" else v


def _wspec(w):
    if isinstance(w, _W):
        return pl.BlockSpec((NDEV, w.rows, w.arr.shape[2]), lambda i: (0, w.idx, 0), pipeline_mode=pl.Buffered(1))
    return _full(w.shape)


_ANY = pl.BlockSpec(memory_space=pl.ANY)
_NO_COMM = ((), ())


def _mesh_pos():
    return tuple(lax.axis_index(a) for a in AXES)


def _slot(ref, dev):
    return ref.at[4 * dev[0] + 2 * dev[1] + dev[2]]


def _comm_shapes(comm):
    slotted, whole = comm
    n = len(slotted) + len(whole)
    shapes = [SDS(a.shape, a.dtype) for a in slotted] + [SDS((NDEV, *a.shape), a.dtype) for a in whole]
    sems = [pltpu.SemaphoreType.DMA((7 * n,)), pltpu.SemaphoreType.DMA((7 * n,)), pltpu.SemaphoreType.DMA((n,))] if n else []
    return shapes, sems


def _comm_copies(ns, ins, outs, send_sems, recv_sems, local_sems):
    n = len(ins)
    x, y, c = _mesh_pos()
    me = (x, y, c)
    src = lambda p, dev: _slot(ins[p], dev) if p < ns else ins[p]
    copies = [pltpu.make_async_copy(src(p, me), _slot(outs[p], me), local_sems.at[p]) for p in range(n)]
    for k in range(1, NDEV):
        flip = lambda v, bit: 1 - v if (k >> bit) & 1 else v
        peer = (flip(x, 2), flip(y, 1), flip(c, 0))
        for p in range(n):
            copies.append(pltpu.make_async_remote_copy(
                src_ref=src(p, peer), dst_ref=_slot(outs[p], me), send_sem=send_sems.at[7 * p + k - 1],
                recv_sem=recv_sems.at[7 * p + k - 1], device_id=peer, device_id_type=pl.DeviceIdType.MESH))
    return copies


def _comm_hook(comm, cin, cout, sems, step, last):
    if not cin:
        return lambda: None

    @pl.when(step == 0)
    def _():
        for cp in _comm_copies(len(comm[0]), cin, cout, *sems):
            cp.start()

    def finish():
        @pl.when(step == last)
        def _():
            for cp in _comm_copies(len(comm[0]), cin, cout, *sems):
                cp.wait()

    return finish


def _tiled(name, body, L, tm, row_ins, full_ins, row_outs, acc_outs=(), comm=_NO_COMM):
    nr, nf, no, na = len(row_ins), len(full_ins), len(row_outs), len(acc_outs)
    cins = [*comm[0], *comm[1]]
    nc = len(cins)
    cshapes, csems = _comm_shapes(comm)

    def kern(*refs):
        rin = refs[:nr]
        fin = [_WRef(r, w.kind) if isinstance(w, _W) else r for r, w in zip(refs[nr:nr + nf], full_ins)]
        cin = refs[nr + nf:nr + nf + nc]
        outs = refs[nr + nf + nc:]
        rout, aout, cout, sems = outs[:no], outs[no:no + na], outs[no + na:no + na + nc], outs[no + na + nc:]
        finish = _comm_hook(comm, cin, cout, sems, pl.program_id(0), L // tm - 1)
        routs, accs = body([r[...] for r in rin], fin)
        for r, v in zip(rout, routs):
            r[...] = v.astype(r.dtype)
        if aout:
            @pl.when(pl.program_id(0) == 0)
            def _():
                for a in aout:
                    a[...] = jnp.zeros(a.shape, a.dtype)

            for a, v in zip(aout, accs):
                a[...] += v.astype(a.dtype)
        finish()

    return pl.pallas_call(
        kern, name=name, grid=(L // tm,),
        in_specs=[_row(tm, w, c) for (_, w, c) in row_ins] + [_wspec(a) for a in full_ins] + [_ANY] * nc,
        out_specs=[_row(tm, w) for (w, _) in row_outs] + [_full(s) for (s, _) in acc_outs] + [_ANY] * nc,
        out_shape=[SDS((L, w), dt) for (w, dt) in row_outs] + [SDS(tuple(s), dt) for (s, dt) in acc_outs] + cshapes,
        scratch_shapes=csems, compiler_params=_cp(),
    )(*[a for (a, _, _) in row_ins], *[a.arr if isinstance(a, _W) else a for a in full_ins], *cins)


def _norm_mm_fwd(name, x, g, w, tm, comm=_NO_COMM):
    def body(rows, fulls):
        g_ref, w_ref = fulls
        return [_dotw(_rms(rows[0], g_ref[...]), w_ref[...])], []

    n_out = w.shape[1] if len(w.shape) == 2 else w.shape[0] * w.shape[2]
    res = _tiled(name, body, x.shape[0], tm, [(x, D, 0)], [g, w], [(n_out, f32)], comm=comm)
    return res if comm[1] else res[0]


def _norm_mm_bwd(name, x, g, w, dz_parts, dres, tm, comm=_NO_COMM):
    n = len(dz_parts)

    def body(rows, fulls):
        g_ref, w_ref = fulls
        dz = rows[1] if n == 1 else jnp.concatenate(rows[1:1 + n], axis=1)
        wv = w_ref[...]
        h, vjp = jax.vjp(_rms, rows[0], g_ref[...])
        dx, dg = vjp(_dotw_t(dz, wv))
        return [dx + rows[1 + n]], [dg, _wgrad(h, dz, wv)]

    return _tiled(name, body, x.shape[0], tm, [(x, D, 0), *dz_parts, (dres, D, 0)], [g, w],
                  [(D, f32)], [((1, D), f32), (tuple(w.shape), f32)], comm=comm)


S5_S = S5_T // 8


def _perm_load(refs):
    return jnp.concatenate([jnp.concatenate([r[pl.ds(i, 8, stride=S5_S), :] for i in range(S5_S)], axis=0) for r in refs],
                           axis=1)


def _perm_store(refs, v):
    for c, r in enumerate(refs):
        for i in range(S5_S):
            r[pl.ds(i, 8, stride=S5_S), :] = v[i * 8:(i + 1) * 8, c * 128:(c + 1) * 128]


def _cmul(pr, pi, qr, qi):
    return pr * qr - pi * qi, pr * qi + pi * qr


def _scan_tables(tab_ref, pw_re, pw_im, ar, ai, reverse):
    def powers(br, bi):
        pw = [(br, bi)]
        for _ in range(7):
            pw.append(_cmul(*pw[-1], br, bi))
        return pw

    row = lax.broadcasted_iota(jnp.int32, (8, NST), 0)

    def rows(pw, order):
        vr = jnp.zeros((8, NST), f32)
        vi = jnp.zeros((8, NST), f32)
        for t in range(8):
            vr = jnp.where(row == t, pw[order(t)][0], vr)
            vi = jnp.where(row == t, pw[order(t)][1], vi)
        return vr, vi

    pw = powers(ar, ai)
    base_r, base_i = rows(pw, lambda t: t)
    sr, si = jnp.ones((1, NST), f32), jnp.zeros((1, NST), f32)
    for m in range(S5_S // 8):
        pw_re[m * 8:(m + 1) * 8, :], pw_im[m * 8:(m + 1) * 8, :] = _cmul(base_r, base_i, sr, si)
        sr, si = _cmul(sr, si, *pw[7])
    big = powers(sr, si)
    for k, s in enumerate((1, 2, 4)):
        keep = (row <= 7 - s) if reverse else (row >= s)
        tab_ref[2 * k] = jnp.where(keep, big[s - 1][0], 0.0)
        tab_ref[2 * k + 1] = jnp.where(keep, big[s - 1][1], 0.0)
    tab_ref[6], tab_ref[7] = rows(big, (lambda t: 7 - t) if reverse else (lambda t: t))


def _scan(re_ref, im_ref, row0, a_re, a_im, tab_ref, pw_re, pw_im, car_re, car_im, reverse, fold=None):
    ng = S5_S // 8
    row = lax.broadcasted_iota(jnp.int32, (8, S5_CH), 0)
    accs = []
    for cc in range(NST // S5_CH):
        cols = slice(cc * S5_CH, (cc + 1) * S5_CH)
        ar = jnp.broadcast_to(a_re[:, cols], (8, S5_CH))
        ai = jnp.broadcast_to(a_im[:, cols], (8, S5_CH))

        def local(i, h, cols=cols, ar=ar, ai=ai):
            r0 = pl.multiple_of(row0 + ((S5_S - 1 - i) if reverse else i) * 8, 8)
            hr = ar * h[0] - ai * h[1] + re_ref[pl.ds(r0, 8), cols]
            hi = ar * h[1] + ai * h[0] + im_ref[pl.ds(r0, 8), cols]
            re_ref[pl.ds(r0, 8), cols] = hr
            im_ref[pl.ds(r0, 8), cols] = hi
            return hr, hi

        xr, xi = lax.fori_loop(0, S5_S, local, (jnp.zeros((8, S5_CH), f32), jnp.zeros((8, S5_CH), f32)))

        old_r, old_i = car_re[:, cols], car_im[:, cols]
        for k, s in enumerate((1, 2, 4)):
            yr = pltpu.roll(xr, (8 - s) if reverse else s, 0)
            yi = pltpu.roll(xi, (8 - s) if reverse else s, 0)
            dr, di = _cmul(tab_ref[2 * k, :, cols], tab_ref[2 * k + 1, :, cols], yr, yi)
            xr, xi = xr + dr, xi + di
        dr, di = _cmul(tab_ref[6, :, cols], tab_ref[7, :, cols], old_r, old_i)
        xr, xi = xr + dr, xi + di
        b = 0 if reverse else 7
        car_re[:, cols] = jnp.broadcast_to(xr[b:b + 1, :], (8, S5_CH))
        car_im[:, cols] = jnp.broadcast_to(xi[b:b + 1, :], (8, S5_CH))
        edge = 7 if reverse else 0
        cr = jnp.where(row == edge, old_r, pltpu.roll(xr, 7 if reverse else 1, 0))
        ci = jnp.where(row == edge, old_i, pltpu.roll(xi, 7 if reverse else 1, 0))

        def fix(g, acc, cols=cols, cr=cr, ci=ci):
            gg = (ng - 1 - g) if reverse else g
            pg = pl.multiple_of(g * 8, 8)
            ptr, pti = pw_re[pl.ds(pg, 8), cols], pw_im[pl.ds(pg, 8), cols]
            for u in range(8):
                uu = (7 - u) if reverse else u
                i = gg * 8 + uu
                r0 = pl.multiple_of(row0 + i * 8, 8)
                pr = jnp.broadcast_to(ptr[u:u + 1, :], (8, S5_CH))
                pi = jnp.broadcast_to(pti[u:u + 1, :], (8, S5_CH))
                dr, di = _cmul(pr, pi, cr, ci)
                hr = re_ref[pl.ds(r0, 8), cols] + dr
                hi = im_ref[pl.ds(r0, 8), cols] + di
                re_ref[pl.ds(r0, 8), cols] = hr
                im_ref[pl.ds(r0, 8), cols] = hi
                if fold is not None:
                    acc = fold(cols, i, hr, hi, acc)
            return acc

        acc0 = (jnp.zeros((8, S5_CH), f32), jnp.zeros((8, S5_CH), f32)) if fold is not None else 0
        accs.append(lax.fori_loop(0, ng, fix, acc0))
    return accs


def _s5_fwd(z, bmat, cmat, a_re, a_im, d_skip, comm=_NO_COMM):
    L = z.shape[0]
    nb = L // S5_T
    cins = [*comm[0], *comm[1]]
    nc = len(cins)
    cshapes, csems = _comm_shapes(comm)

    def kern(*refs):
        u_refs, (bre, bim, cre, cim, ar_ref, ai_ref, d_ref), rest = refs[:4], refs[4:11], refs[11:]
        cin, y_refs, (hbr_ref, hbi_ref), cout = rest[:nc], rest[nc:nc + 4], rest[nc + 4:nc + 6], rest[nc + 6:2 * nc + 6]
        h_re, h_im, tab, pw_re, pw_im, car_re, car_im, *sems = rest[2 * nc + 6:]
        finish = _comm_hook(comm, cin, cout, sems, pl.program_id(0), nb - 1)

        @pl.when(pl.program_id(0) == 0)
        def _():
            _scan_tables(tab, pw_re, pw_im, ar_ref[...], ai_ref[...], False)
            car_re[...] = jnp.zeros_like(car_re)
            car_im[...] = jnp.zeros_like(car_im)

        hbr_ref[...] = car_re[...]
        hbi_ref[...] = car_im[...]
        u = _perm_load(u_refs)
        for hf in range(2):
            uh = u[:, hf * 256:(hf + 1) * 256]
            h_re[:, hf * 1024:(hf + 1) * 1024] = _dot(uh, bre[hf], NN)
            h_im[:, hf * 1024:(hf + 1) * 1024] = _dot(uh, bim[hf], NN)
        _scan(h_re, h_im, 0, ar_ref[...], ai_ref[...], tab, pw_re, pw_im, car_re, car_im, False)
        ys = []
        for hf in range(2):
            cs = slice(hf * 1024, (hf + 1) * 1024)
            ys.append(_dot(h_re[:, cs], cre[hf], NT) - _dot(h_im[:, cs], cim[hf], NT))
        _perm_store(y_refs, jnp.concatenate(ys, axis=1) + d_ref[...] * u)
        finish()

    fulls = [*bmat, *cmat, a_re, a_im, d_skip]
    return pl.pallas_call(
        kern, name="s5_fwd", grid=(nb,),
        in_specs=[_row(S5_T, 128, c) for c in range(4)] + [_full(a.shape) for a in fulls] + [_ANY] * nc,
        out_specs=[_row(S5_T, 128)] * 4 + [_row(8, NST), _row(8, NST)] + [_ANY] * nc,
        out_shape=[SDS((L, 128), f32)] * 4 + [SDS((nb * 8, NST), f32), SDS((nb * 8, NST), f32)] + cshapes,
        scratch_shapes=[pltpu.VMEM((S5_T, NST), f32), pltpu.VMEM((S5_T, NST), f32), pltpu.VMEM((8, 8, NST), f32),
                        pltpu.VMEM((S5_S, NST), f32), pltpu.VMEM((S5_S, NST), f32),
                        pltpu.VMEM((8, NST), f32), pltpu.VMEM((8, NST), f32)] + csems,
        compiler_params=_cp(),
    )(z, z, z, z, *fulls, *cins)


def _s5_bwd(z, dy, hb_re, hb_im, bmat, cmat, a_re, a_im, d_skip, comm=_NO_COMM):
    L = z.shape[0]
    nb = L // S5_T
    T = S5_T
    cins = [*comm[0], *comm[1]]
    nc = len(cins)
    cshapes, csems = _comm_shapes(comm)

    def kern(*refs):
        u_refs, dy_refs, (hbr_ref, hbi_ref, bre, bim, cre, cim, ar_ref, ai_ref, d_ref), rest = \
            refs[:4], refs[4:8], refs[8:17], refs[17:]
        cin, du_refs, (dbre, dbim, dcre, dcim, dar, dai, dd), cout = \
            rest[:nc], rest[nc:nc + 4], rest[nc + 4:nc + 11], rest[nc + 11:2 * nc + 11]
        h_re, h_im, g_re, g_im, tabf, pwf_re, pwf_im, tabr, pwr_re, pwr_im, car_re, car_im, gcar_re, gcar_im, *sems = \
            rest[2 * nc + 11:]
        finish = _comm_hook(comm, cin, cout, sems, pl.program_id(0), nb - 1)

        @pl.when(pl.program_id(0) == 0)
        def _():
            _scan_tables(tabf, pwf_re, pwf_im, ar_ref[...], ai_ref[...], False)
            _scan_tables(tabr, pwr_re, pwr_im, ar_ref[...], -ai_ref[...], True)
            gcar_re[...] = jnp.zeros_like(gcar_re)
            gcar_im[...] = jnp.zeros_like(gcar_im)
            for r in (dbre, dbim, dcre, dcim, dar, dai, dd):
                r[...] = jnp.zeros_like(r)

        u = _perm_load(u_refs)
        dyv = _perm_load(dy_refs)
        car_re[...] = hbr_ref[...]
        car_im[...] = hbi_ref[...]
        for hf in range(2):
            cs = slice(hf * 1024, (hf + 1) * 1024)
            uh = u[:, hf * 256:(hf + 1) * 256]
            dyh = dyv[:, hf * 256:(hf + 1) * 256]
            h_re[8:, cs] = _dot(uh, bre[hf], NN)
            h_im[8:, cs] = _dot(uh, bim[hf], NN)
            g_re[:, cs] = _dot(dyh, cre[hf], NN)
            g_im[:, cs] = -_dot(dyh, cim[hf], NN)
        _scan(h_re, h_im, 8, ar_ref[...], ai_ref[...], tabf, pwf_re, pwf_im, car_re, car_im, False)
        row = lax.broadcasted_iota(jnp.int32, (8, NST), 0)
        h_re[0:8, :] = jnp.where(row == 0, hbr_ref[...], pltpu.roll(h_re[T:T + 8, :], 1, 0))
        h_im[0:8, :] = jnp.where(row == 0, hbi_ref[...], pltpu.roll(h_im[T:T + 8, :], 1, 0))

        def fold_da(cols, i, gr, gi, acc):
            r0 = pl.multiple_of(i * 8, 8)
            hr, hi = h_re[pl.ds(r0, 8), cols], h_im[pl.ds(r0, 8), cols]
            return acc[0] + gr * hr + gi * hi, acc[1] + gi * hr - gr * hi

        accs = _scan(g_re, g_im, 0, ar_ref[...], -ai_ref[...], tabr, pwr_re, pwr_im, gcar_re, gcar_im, True, fold=fold_da)
        for cc, (acc_r, acc_i) in enumerate(accs):
            cols = slice(cc * S5_CH, (cc + 1) * S5_CH)
            dar[:, cols] += jnp.sum(acc_r, axis=0, keepdims=True)
            dai[:, cols] += jnp.sum(acc_i, axis=0, keepdims=True)

        dus = []
        for hf in range(2):
            cs = slice(hf * 1024, (hf + 1) * 1024)
            uh = u[:, hf * 256:(hf + 1) * 256]
            dyh = dyv[:, hf * 256:(hf + 1) * 256]
            gr = g_re[:, cs]
            gi = g_im[:, cs]
            dus.append(_dot(gr, bre[hf], NT) + _dot(gi, bim[hf], NT))
            dbre[hf] += _dot(uh, gr, TN)
            dbim[hf] += _dot(uh, gi, TN)
            dcre[hf] += _dot(dyh, h_re[8:, cs], TN)
            dcim[hf] -= _dot(dyh, h_im[8:, cs], TN)
        _perm_store(du_refs, jnp.concatenate(dus, axis=1) + d_ref[...] * dyv)
        dd[...] += jnp.sum(dyv * u, axis=0, keepdims=True)
        finish()

    rev = lambda c: pl.BlockSpec((T, 128), lambda k: (nb - 1 - k, c))
    rev8 = pl.BlockSpec((8, NST), lambda k: (nb - 1 - k, 0))
    fulls = [*bmat, *cmat, a_re, a_im, d_skip]
    mat = ((2, 256, 1024), f32)
    accs = [mat, mat, mat, mat, ((1, NST), f32), ((1, NST), f32), ((1, SW), f32)]
    tables = [pltpu.VMEM((8, 8, NST), f32), pltpu.VMEM((S5_S, NST), f32), pltpu.VMEM((S5_S, NST), f32)]
    return pl.pallas_call(
        kern, name="s5_bwd", grid=(nb,),
        in_specs=[rev(c) for c in range(4)] * 2 + [rev8, rev8] + [_full(a.shape) for a in fulls] + [_ANY] * nc,
        out_specs=[rev(0)] * 4 + [_full(s) for (s, _) in accs] + [_ANY] * nc,
        out_shape=[SDS((L, 128), f32)] * 4 + [SDS(s, dt) for (s, dt) in accs] + cshapes,
        scratch_shapes=[pltpu.VMEM((T + 8, NST), f32), pltpu.VMEM((T + 8, NST), f32),
                        pltpu.VMEM((T, NST), f32), pltpu.VMEM((T, NST), f32), *tables, *tables,
                        pltpu.VMEM((8, NST), f32), pltpu.VMEM((8, NST), f32),
                        pltpu.VMEM((8, NST), f32), pltpu.VMEM((8, NST), f32)] + csems,
        compiler_params=_cp(),
    )(z, z, z, z, dy, dy, dy, dy, hb_re, hb_im, *fulls, *cins)


def _s5_disc(lam_re, lam_im, log_dt, bt_re, bt_im):
    dt = jnp.exp(log_dt)
    mag = jnp.exp(lam_re * dt)
    ab_re = mag * jnp.cos(lam_im * dt)
    ab_im = mag * jnp.sin(lam_im * dt)
    den = lam_re * lam_re + lam_im * lam_im
    nr = ab_re - 1.0
    coef_re = (nr * lam_re + ab_im * lam_im) / den
    coef_im = (ab_im * lam_re - nr * lam_im) / den
    rep = lambda c: jnp.broadcast_to(c[:, None, :], (NG, GS, NS)).reshape(NG * GS, NS)
    cr, ci = rep(coef_re), rep(coef_im)
    return ab_re, ab_im, cr * bt_re - ci * bt_im, cr * bt_im + ci * bt_re


def _s5_disc_fwd(*params):
    def kern(*refs):
        outs = _s5_disc(*[r[...] for r in refs[:5]])
        for r, v in zip(refs[5:], outs):
            r[...] = v

    shapes = [SDS((NG, NS), f32), SDS((NG, NS), f32), SDS((NG * GS, NS), f32), SDS((NG * GS, NS), f32)]
    return pl.pallas_call(kern, name="s5_disc_fwd", out_shape=shapes)(*params)


def _s5_disc_bwd(params, cts):
    def kern(*refs):
        _, vjp = jax.vjp(_s5_disc, *[r[...] for r in refs[:5]])
        for r, v in zip(refs[9:], vjp(tuple(r[...] for r in refs[5:9]))):
            r[...] = v

    return pl.pallas_call(kern, name="s5_disc_bwd", out_shape=[SDS(p.shape, f32) for p in params])(*params, *cts)


def _diag_mask():
    r = np.arange(256)[:, None] // GS
    c = np.arange(1024)[None, :] // NS
    return jnp.asarray(r == c, f32)


def _to_blockdiag(m):
    return jnp.tile(m.reshape(2, 256, NS), (1, 1, 16)) * _diag_mask()


def _from_blockdiag(dm):
    return (dm * _diag_mask()).reshape(2, 256, 16, NS).sum(axis=2).reshape(NG * GS, NS)


def _conv_shifts(win, sh, tm):
    win[tm + 32:, :] = jnp.zeros((8, CW), f32)
    for s in range(8):
        sh[s] = win[s:s + tm + 32, :]


def _conv_taps(offset, w_ref, sh, r, accs):
    for k in range(CK):
        s, q = offset(k) % 8, offset(k) // 8
        wk = w_ref[k:k + 1, :]
        accs = [acc + wk * sh[s, pl.ds(pl.multiple_of(r + 8 * (a + q), 8), 8), :] for a, acc in enumerate(accs)]
    return accs


def _conv_fwd(z, w32, b, tm, comm=_NO_COMM):
    L = z.shape[0]
    hb = tm // 32
    cins = [*comm[0], *comm[1]]
    nc = len(cins)
    cshapes, csems = _comm_shapes(comm)

    def kern(val, glu, hval, hglu, w_ref, b_ref, *rest):
        cin, c_ref, cout, (win, sh, *sems) = rest[:nc], rest[nc], rest[nc + 1:2 * nc + 1], rest[2 * nc + 1:]
        i = pl.program_id(0)
        finish = _comm_hook(comm, cin, cout, sems, i, L // tm - 1)
        win[0:32, :] = hval[...] * jax.nn.sigmoid(hglu[...]) * (i > 0).astype(f32)
        win[32:tm + 32, :] = val[...] * jax.nn.sigmoid(glu[...])
        _conv_shifts(win, sh, tm)

        def strip(j, carry):
            r = j * 16
            accs = _conv_taps(lambda k: k + 2, w_ref, sh, r, [jnp.zeros((8, CW), f32) + b_ref[...]] * 2)
            c_ref[pl.ds(pl.multiple_of(r, 16), 16), :] = jnp.concatenate(accs, axis=0)
            return carry

        lax.fori_loop(0, tm // 16, strip, 0)
        finish()

    halo = lambda col: pl.BlockSpec((32, CW), lambda i: (jnp.maximum(i * hb - 1, 0), col))
    return pl.pallas_call(
        kern, name="conv_fwd", grid=(L // tm,),
        in_specs=[_row(tm, CW, 2), _row(tm, CW, 3), halo(2), halo(3), _full(w32.shape), _full(b.shape)] + [_ANY] * nc,
        out_specs=[_row(tm, CW)] + [_ANY] * nc, out_shape=[SDS((L, CW), f32)] + cshapes,
        scratch_shapes=[pltpu.VMEM((tm + 40, CW), f32), pltpu.VMEM((8, tm + 32, CW), f32)] + csems, compiler_params=_cp(),
    )(z, z, z, z, w32, b, *cins)


def _conv_bwd(z, dc, w32, tm, comm=_NO_COMM):
    L = z.shape[0]
    hb = tm // 32
    nt = L // tm
    cins = [*comm[0], *comm[1]]
    nc = len(cins)
    cshapes, csems = _comm_shapes(comm)

    def kern(val, glu, hval, hglu, dc_ref, hdc, w_ref, *rest):
        cin, (dval_ref, dglu_ref, dw_ref, db_ref), cout = rest[:nc], rest[nc:nc + 4], rest[nc + 4:2 * nc + 4]
        win, sh, dwin, dsh, *sems = rest[2 * nc + 4:]
        i = pl.program_id(0)
        finish = _comm_hook(comm, cin, cout, sems, i, nt - 1)

        @pl.when(i == 0)
        def _():
            dw_ref[...] = jnp.zeros_like(dw_ref)
            db_ref[...] = jnp.zeros_like(db_ref)

        win[0:32, :] = hval[...] * jax.nn.sigmoid(hglu[...]) * (i > 0).astype(f32)
        win[32:tm + 32, :] = val[...] * jax.nn.sigmoid(glu[...])
        _conv_shifts(win, sh, tm)
        dwin[0:tm, :] = dc_ref[...]
        dwin[tm:tm + 32, :] = hdc[...] * (i < nt - 1).astype(f32)
        _conv_shifts(dwin, dsh, tm)

        def strip(j, carry):
            r = j * 16
            rows = pl.ds(pl.multiple_of(r, 16), 16)
            dv = jnp.concatenate(_conv_taps(lambda k: 30 - k, w_ref, dsh, r, [jnp.zeros((8, CW), f32)] * 2), axis=0)
            sg = jax.nn.sigmoid(glu[rows, :])
            dval_ref[rows, :] = dv * sg
            dglu_ref[rows, :] = dv * val[rows, :] * sg * (1.0 - sg)
            return carry

        lax.fori_loop(0, tm // 16, strip, 0)

        for k0 in range(0, CK, 2):
            taps = [k for k in (k0, k0 + 1) if k < CK]

            def rows64(j, accs, taps=taps):
                for u in range(8):
                    r = pl.multiple_of(j * 64 + u * 8, 8)
                    dcv = dc_ref[pl.ds(r, 8), :]
                    accs = [acc + dcv * sh[(k + 2) % 8, pl.ds(pl.multiple_of(r + 8 * ((k + 2) // 8), 8), 8), :]
                            for k, acc in zip(taps, accs)]
                return accs

            accs = lax.fori_loop(0, tm // 64, rows64, [jnp.zeros((8, CW), f32)] * len(taps))
            for k, acc in zip(taps, accs):
                dw_ref[k:k + 1, :] += jnp.sum(acc, axis=0, keepdims=True)
        db_ref[...] += jnp.sum(dc_ref[...], axis=0, keepdims=True)
        finish()

    halo = lambda col: pl.BlockSpec((32, CW), lambda i: (jnp.maximum(i * hb - 1, 0), col))
    nxt = pl.BlockSpec((32, CW), lambda i: (jnp.minimum((i + 1) * hb, L // 32 - 1), 0))
    return pl.pallas_call(
        kern, name="conv_bwd", grid=(nt,),
        in_specs=[_row(tm, CW, 2), _row(tm, CW, 3), halo(2), halo(3), _row(tm, CW), nxt, _full(w32.shape)] + [_ANY] * nc,
        out_specs=[_row(tm, CW), _row(tm, CW), _full((32, CW)), _full((1, CW))] + [_ANY] * nc,
        out_shape=[SDS((L, CW), f32), SDS((L, CW), f32), SDS((32, CW), f32), SDS((1, CW), f32)] + cshapes,
        scratch_shapes=[pltpu.VMEM((tm + 40, CW), f32), pltpu.VMEM((8, tm + 32, CW), f32),
                        pltpu.VMEM((tm + 40, CW), f32), pltpu.VMEM((8, tm + 32, CW), f32)] + csems, compiler_params=_cp(),
    )(z, z, z, z, dc, dc, w32, *cins)


def _ev_out_f(y, c, ga, gb, tap, glu_b, ln_g, ln_b, glu_w, w_out, mm):
    z1 = jax.nn.gelu(y)
    ya = z1 * jax.nn.sigmoid(mm(z1, glu_w) + glu_b + tap) * jax.nn.silu(ga)
    mu = jnp.mean(c, axis=-1, keepdims=True)
    var = jnp.mean(jnp.square(c - mu), axis=-1, keepdims=True)
    cn = (c - mu) * lax.rsqrt(var + EPS) * ln_g + ln_b
    cat = jnp.concatenate([ya, jax.nn.silu(cn) * jax.nn.silu(gb)], axis=1)
    return mm(cat, w_out), (z1, cat)


def _ev_out_fwd(x, y, c, z, glu_b, ln_g, ln_b, glu_w, w_out, tm, comm=_NO_COMM):
    def body(rows, fulls):
        xv, cv, ga, gb = rows[:4]
        yv = jnp.concatenate(rows[4:], axis=1)
        gb_ref, lg_ref, lb_ref, gw_ref, wo_ref = fulls
        out, _ = _ev_out_f(yv, cv, ga, gb, 0.0, gb_ref[...], lg_ref[...], lb_ref[...], gw_ref[...], wo_ref[...], _plain_mm)
        return [xv + out], []

    return _tiled("ev_out_fwd", body, x.shape[0], tm, [(x, D, 0), (c, CW, 0), (z, SW, 1), (z, CW, 4), *[(t, 128, 0) for t in y]],
                  [glu_b, ln_g, ln_b, glu_w, w_out], [(D, f32)], comm=comm)


def _ev_out_bwd(dx1, y, c, z, glu_b, ln_g, ln_b, glu_w, w_out, tm):
    def body(rows, fulls):
        dxv, cv, ga, gb = rows[:4]
        yv = jnp.concatenate(rows[4:], axis=1)
        gb_ref, lg_ref, lb_ref, gw_ref, wo_ref = fulls
        gw, wo = gw_ref[...], wo_ref[...]
        f = lambda yv, cv, ga, gb, tap, b, lg, lb: _ev_out_f(yv, cv, ga, gb, tap, b, lg, lb, gw, wo, _mm)
        _, vjp, (z1, cat) = jax.vjp(f, yv, cv, ga, gb, jnp.zeros((tm, SW), f32), gb_ref[...], lg_ref[...], lb_ref[...],
                                    has_aux=True)
        dy, dc, dga, dgb, dtap, db, dlg, dlb = vjp(dxv)
        return [dy, dc, dga, dgb], [db, dlg, dlb, _wgrad(z1, dtap, gw), _wgrad(cat, dxv, wo)]

    return _tiled("ev_out_bwd", body, dx1.shape[0], tm, [(dx1, D, 0), (c, CW, 0), (z, SW, 1), (z, CW, 4), *[(t, 128, 0) for t in y]],
                  [glu_b, ln_g, ln_b, glu_w, w_out], [(SW, f32), (CW, f32), (SW, f32), (CW, f32)],
                  [((1, SW), f32), ((1, CW), f32), ((1, CW), f32), (glu_w.shape, f32), (w_out.shape, f32)])


def _kv_f(mem, g, tap0, tap1, w0, w1, mm):
    mn = _rms(mem, g)
    return (mm(mn, w0) + tap0, mm(mn, w1) + tap1), mn


def _kv_bwd(mem, g, w0, w1, dkv0, dkv1):
    def body(rows, fulls):
        g_ref, w0_ref, w1_ref = fulls
        w0v, w1v = w0_ref[...], w1_ref[...]
        tap = jnp.zeros((MEM_LEN, 2 * D), f32)
        _, vjp, mn = jax.vjp(lambda g_, t0, t1: _kv_f(rows[0], g_, t0, t1, w0v, w1v, _mm), g_ref[...], tap, tap, has_aux=True)
        dg, d0, d1 = vjp((rows[1], rows[2]))
        return [], [dg, _wgrad(mn, d0, w0v), _wgrad(mn, d1, w1v)]

    return _tiled("kv_bwd", body, MEM_LEN, MEM_LEN, [(mem, D, 0), (dkv0, 2 * D, 0), (dkv1, 2 * D, 0)], [g, w0, w1], [],
                  [((1, D), f32), (w0.shape, f32), (w1.shape, f32)])


def _xa_f(x, g, kv, tap, w_qg, w_o, mm, nt, nn):
    h = _rms(x, g)
    qg = mm(h, w_qg) + tap
    outs = []
    for hd in range(XH):
        q = qg[:, hd * XD:(hd + 1) * XD]
        k = kv[:, hd * XD:(hd + 1) * XD]
        v = kv[:, D + hd * XD:D + (hd + 1) * XD]
        s = nt(q, k) * (XD ** -0.5)
        e = jnp.exp(s - jnp.max(s, axis=-1, keepdims=True))
        outs.append(nn(e / jnp.sum(e, axis=-1, keepdims=True), v))
    u = jnp.concatenate(outs, axis=1) * jax.nn.silu(qg[:, D:])
    return mm(u, w_o), (h, u)


def _xa_fwd(name, x, g, kv, w_qg, w_o, tm):
    def body(rows, fulls):
        g_ref, kv_ref, wq_ref, wo_ref = fulls
        out, _ = _xa_f(rows[0], g_ref[...], kv_ref[...], 0.0, wq_ref[...], wo_ref[...], _plain_mm, _plain_nt, _plain_mm)
        return [rows[0] + out], []

    return _tiled(name, body, x.shape[0], tm, [(x, D, 0)], [g, kv, w_qg, w_o], [(D, f32)])[0]


def _xa_bwd(name, x, dxo, g, kv, w_qg, w_o, tm, comm=_NO_COMM):
    def body(rows, fulls):
        xv, dxv = rows
        g_ref, kv_ref, wq_ref, wo_ref = fulls
        wq, wo = wq_ref[...], wo_ref[...]
        f = lambda xv, gv, kvv, tap: _xa_f(xv, gv, kvv, tap, wq, wo, _mm, _mm_nt, _mm_nn)
        _, vjp, (h, u) = jax.vjp(f, xv, g_ref[...], kv_ref[...], jnp.zeros((tm, 2 * D), f32), has_aux=True)
        dx, dg, dkv, dtap = vjp(dxv)
        return [dx + dxv], [dg, dkv, _wgrad(h, dtap, wq), _wgrad(u, dxv, wo)]

    return _tiled(name, body, x.shape[0], tm, [(x, D, 0), (dxo, D, 0)], [g, kv, w_qg, w_o], [(D, f32)],
                  [((1, D), f32), ((MEM_LEN, 2 * D), f32), (w_qg.shape, f32), (w_o.shape, f32)], comm=comm)


def _attn_pad(k_ref, v_ref, kpad, vpad):
    kpad[0:PAD, :] = jnp.zeros((PAD, 128), bf16)
    vpad[0:PAD, :] = jnp.zeros((PAD, 128), bf16)
    kpad[PAD:, :] = k_ref[...].astype(bf16)
    vpad[PAD:, :] = v_ref[...].astype(bf16)


def _attn_exp(qm, kwin, bm, mask_row):
    x = _dot(qm, kwin, NT) + bm + mask_row
    e = jnp.exp(x - jnp.max(x, axis=-1, keepdims=True))
    return e, 1.0 / jnp.sum(e, axis=-1, keepdims=True)


def _attn_fwd(qkvg, bias, comm=_NO_COMM):
    L = qkvg.shape[0]
    nq = L // AQ
    cins = [*comm[0], *comm[1]]
    nc = len(cins)
    cshapes, csems = _comm_shapes(comm)

    def kern(q_ref, k_ref, v_ref, b_ref, *rest):
        cin, o_ref, cout, (kpad, vpad, *sems) = rest[:nc], rest[nc], rest[nc + 1:2 * nc + 1], rest[2 * nc + 1:]
        i = pl.program_id(1)
        finish = _comm_hook(comm, cin, cout, sems, pl.program_id(0) * nq + i, HEADS // 2 * nq - 1)

        @pl.when(i == 0)
        def _():
            _attn_pad(k_ref, v_ref, kpad, vpad)

        r0 = pl.multiple_of(i * AQ, AQ)
        q = q_ref[...] * (DH ** -0.5)
        kwin = kpad[pl.ds(r0, AW), :]
        vwin = vpad[pl.ds(r0, AW), :]
        lane_hi = lax.broadcasted_iota(jnp.int32, (AQ, 128), 1) // DH
        mask_row = jnp.where(lax.broadcasted_iota(jnp.int32, (1, AW), 1) >= PAD - r0, 0.0, -1e30)
        outs = []
        for hh in range(2):
            e, inv = _attn_exp(jnp.where(lane_hi == hh, q, 0.0), kwin, b_ref[hh], mask_row)
            outs.append(_dot(e, vwin, NN) * inv)
        o_ref[...] = jnp.where(lane_hi == 0, outs[0], outs[1])
        finish()

    return pl.pallas_call(
        kern, name="attn_fwd", grid=(HEADS // 2, nq),
        in_specs=[pl.BlockSpec((AQ, 128), lambda j, i: (i, j)),
                  pl.BlockSpec((L, 128), lambda j, i: (0, 8 + j)),
                  pl.BlockSpec((L, 128), lambda j, i: (0, 16 + j)),
                  pl.BlockSpec((2, AQ, AW), lambda j, i: (j, 0, 0))] + [_ANY] * nc,
        out_specs=[pl.BlockSpec((AQ, 128), lambda j, i: (i, j))] + [_ANY] * nc,
        out_shape=[SDS((L, D), f32)] + cshapes,
        scratch_shapes=[pltpu.VMEM((L + PAD, 128), bf16), pltpu.VMEM((L + PAD, 128), bf16)] + csems,
        compiler_params=_cp(("arbitrary", "arbitrary")),
    )(qkvg, qkvg, qkvg, bias, *cins)


def _attn_bwd(qkvg, do, bias):
    L = qkvg.shape[0]
    nq = L // AQ

    def kern(q_ref, k_ref, v_ref, b_ref, do_ref, dq_ref, dk_ref, dv_ref, db_ref, kpad, vpad, dkp, dvp):
        i = pl.program_id(1)

        @pl.when(i == 0)
        def _():
            _attn_pad(k_ref, v_ref, kpad, vpad)
            dkp[...] = jnp.zeros_like(dkp)
            dvp[...] = jnp.zeros_like(dvp)
            db_ref[...] = jnp.zeros_like(db_ref)

        r0 = pl.multiple_of(i * AQ, AQ)
        q = q_ref[...] * (DH ** -0.5)
        dov = do_ref[...]
        kwin = kpad[pl.ds(r0, AW), :]
        vwin = vpad[pl.ds(r0, AW), :]
        lane_hi = lax.broadcasted_iota(jnp.int32, (AQ, 128), 1) // DH
        mask_row = jnp.where(lax.broadcasted_iota(jnp.int32, (1, AW), 1) >= PAD - r0, 0.0, -1e30)
        dqs = []
        dk = jnp.zeros((AW, 128), f32)
        dv = jnp.zeros((AW, 128), f32)
        for hh in range(2):
            qm = jnp.where(lane_hi == hh, q, 0.0).astype(bf16)
            dom = jnp.where(lane_hi == hh, dov, 0.0).astype(bf16)
            dp = _dot(dom, vwin, NT)
            e, inv = _attn_exp(qm, kwin, b_ref[hh], mask_row)
            p = e * inv
            ds = p * (dp - jnp.sum(p * dp, axis=-1, keepdims=True))
            db_ref[hh] += ds
            dsb = ds.astype(bf16)
            dqs.append(_dot(dsb, kwin, NN) * (DH ** -0.5))
            dk = dk + _dot(dsb, qm, TN)
            dv = dv + _dot(p, dom, TN)
        dq_ref[...] = jnp.where(lane_hi == 0, dqs[0], dqs[1])
        dkp[pl.ds(r0, AW), :] += dk
        dvp[pl.ds(r0, AW), :] += dv

        @pl.when(i == nq - 1)
        def _():
            dk_ref[...] = dkp[PAD:, :]
            dv_ref[...] = dvp[PAD:, :]

    return pl.pallas_call(
        kern, name="attn_bwd", grid=(HEADS // 2, nq),
        in_specs=[pl.BlockSpec((AQ, 128), lambda j, i: (i, j)),
                  pl.BlockSpec((L, 128), lambda j, i: (0, 8 + j)),
                  pl.BlockSpec((L, 128), lambda j, i: (0, 16 + j)),
                  pl.BlockSpec((2, AQ, AW), lambda j, i: (j, 0, 0)),
                  pl.BlockSpec((AQ, 128), lambda j, i: (i, j))],
        out_specs=[pl.BlockSpec((AQ, 128), lambda j, i: (i, j)),
                   pl.BlockSpec((L, 128), lambda j, i: (0, j)),
                   pl.BlockSpec((L, 128), lambda j, i: (0, j)),
                   pl.BlockSpec((2, AQ, AW), lambda j, i: (j, 0, 0))],
        out_shape=[SDS((L, D), f32), SDS((L, D), f32), SDS((L, D), f32), SDS((HEADS, AQ, AW), f32)],
        scratch_shapes=[pltpu.VMEM((L + PAD, 128), bf16), pltpu.VMEM((L + PAD, 128), bf16),
                        pltpu.VMEM((L + PAD, 128), f32), pltpu.VMEM((L + PAD, 128), f32)],
        compiler_params=_cp(("arbitrary", "arbitrary")),
    )(qkvg, qkvg, qkvg, bias, do)


_SKEW = AQ + AW - 1


def _rel_index():
    d = (AW - 1) - np.arange(_SKEW)
    return np.clip(d, -MAX_REL, MAX_REL) + MAX_REL


def _band_mask():
    qc = np.arange(AQ)[:, None] // CHUNK + LEFT
    kc = np.arange(AW)[None, :] // CHUNK
    return jnp.asarray(np.where((kc <= qc) & (kc >= qc - LEFT), 0.0, -1e30), f32)


def _bias_matrix(rel_bias):
    tv = jnp.take(rel_bias, jnp.asarray(_rel_index()), axis=1)
    flat = jnp.tile(jnp.pad(tv, ((0, 0), (0, 1))), (1, AQ))
    top = flat[:, AQ - 1:AQ - 1 + AQ * _SKEW].reshape(HEADS, AQ, _SKEW)[:, :, :AW]
    return top + _band_mask()


def _bias_grad(dbias, onehot):
    flat = jnp.pad(dbias, ((0, 0), (0, 0), (0, _SKEW - AW))).reshape(HEADS, AQ * _SKEW)
    z = jnp.pad(flat, ((0, 0), (AQ - 1, AQ * 1024 - AQ * _SKEW - (AQ - 1)))).reshape(HEADS, AQ, 1024)

    def kern(z_ref, oh_ref, out_ref):
        diag = jnp.sum(z_ref[...], axis=1)
        out_ref[...] = jnp.dot(diag, oh_ref[...], preferred_element_type=f32, precision=lax.Precision.HIGHEST)

    return pl.pallas_call(kern, name="bias_grad", out_shape=SDS((HEADS, 2 * MAX_REL + 1), f32),
                          compiler_params=_cp(()))(z, onehot)


def _rel_onehot():
    oh = np.zeros((1024, 2 * MAX_REL + 1), np.float32)
    oh[np.arange(_SKEW), _rel_index()] = 1.0
    return jnp.asarray(oh)


def _gated_out_fwd(name, x, o, gate_src, gate_col, w, tm):
    def body(rows, fulls):
        xv, ov, gv = rows
        return [xv + _dotw(ov * jax.nn.silu(gv), fulls[0][...])], []

    return _tiled(name, body, x.shape[0], tm, [(x, D, 0), (o, D, 0), (gate_src, D, gate_col)], [w], [(D, f32)])[0]


def _gated_out_bwd(name, dxo, o, gate_src, gate_col, w, tm):
    def body(rows, fulls):
        dxv, ov, gv = rows
        wv = fulls[0][...]
        u, vjp = jax.vjp(lambda ov, gv: ov * jax.nn.silu(gv), ov, gv)
        do, dg = vjp(_dotw_t(dxv, wv))
        return [do, dg], [_wgrad(u, dxv, wv)]

    return _tiled(name, body, dxo.shape[0], tm, [(dxo, D, 0), (o, D, 0), (gate_src, D, gate_col)], [w],
                  [(D, f32), (D, f32)], [(w.shape, f32)])


def _loss_head(x, g, target, tm):
    def body(rows, fulls):
        xv, tv = rows

        def f(xv, gv):
            e = jnp.square(_rms(xv, gv) - tv)
            return 0.5 * jnp.sum(jnp.mean(e, axis=-1, keepdims=True), axis=0, keepdims=True)

        loss, vjp = jax.vjp(f, xv, fulls[0][...])
        dx, dg = vjp(jnp.ones((1, 1), f32))
        return [dx], [jnp.broadcast_to(loss, (1, 128)), dg]

    return _tiled("loss_head", body, x.shape[0], tm, [(x, D, 0), (target, D, 0)], [g], [(D, f32)],
                  [((1, 128), f32), ((1, D), f32)])


def _local_step(x, mem, target, p, shards, views, pack_early, pack_mid, pack_small):
    TM, TMF = 256, 512
    row = lambda v: v.reshape(1, -1)

    bt = lambda b: jnp.transpose(b, (0, 2, 1)).reshape(NG * GS, NS)
    disc_in = (p["ev_s5_lambda_re"], p["ev_s5_lambda_im"], p["ev_s5_log_dt"].reshape(NG, 1),
               bt(p["ev_s5_b_re"]), bt(p["ev_s5_b_im"]))
    ab_re, ab_im, bbt_re, bbt_im = _s5_disc_fwd(*disc_in)
    bmat = (_to_blockdiag(bbt_re).astype(bf16), _to_blockdiag(bbt_im).astype(bf16))
    cmat = (_to_blockdiag(p["ev_s5_c_re"].reshape(NG * GS, NS)).astype(bf16),
            _to_blockdiag(p["ev_s5_c_im"].reshape(NG * GS, NS)).astype(bf16))
    a_re, a_im = ab_re.reshape(1, NST), ab_im.reshape(1, NST)
    d_skip = row(p["ev_s5_d"])
    w32 = jnp.pad(p["ev_conv_w"], ((0, 1), (0, 0)))
    conv_b, ln_g, ln_b, glu_b = row(p["ev_conv_b"]), row(p["ev_conv_ln_g"]), row(p["ev_conv_ln_b"]), row(p["ev_s5_glu_b"])
    g_mem, g_ev, g_od, g_fin = row(p["mem_norm_g"]), row(p["ev_norm_g"]), row(p["od_norm_g"]), row(p["final_norm_g"])
    g_xa = [row(p["xa_norm_g"][l]) for l in range(2)]
    bias = _bias_matrix(p["od_rel_bias"])

    z, *got = _norm_mm_fwd("ev_in_fwd", x, g_ev, p["ev_w_in"], TMF, comm=((), shards["ev_in"]))
    p = views(p, "ev_in", got)
    s5_out = _s5_fwd(z, bmat, cmat, a_re, a_im, d_skip, comm=((), shards["s5"]))
    y_s5, (hb_re, hb_im), p = s5_out[:4], s5_out[4:6], views(p, "s5", s5_out[6:])
    c, *got = _conv_fwd(z, w32, conv_b, 512, comm=((), shards["conv"]))
    p = views(p, "conv", got)
    x1, *got = _ev_out_fwd(x, y_s5, c, z, glu_b, ln_g, ln_b, p["ev_s5_glu_w"], p["ev_w_out"], TMF, comm=((), shards["ev_out"]))
    p = views(p, "ev_out", got)
    kv0 = _norm_mm_fwd("kv0_fwd", mem, g_mem, p["xa_w_kv"][0], MEM_LEN)
    x2 = _xa_fwd("xa0_fwd", x1, g_xa[0], kv0, p["xa_w_qg"][0], p["xa_w_o"][0], TMF)
    qkvg = _norm_mm_fwd("od_in_fwd", x2, g_od, p["od_w_in"], TMF)
    o, *got = _attn_fwd(qkvg, bias, comm=((), shards["attn"]))
    p = views(p, "attn", got)
    wqg, wkv, wo = p["xa_w_qg"], p["xa_w_kv"], p["xa_w_o"]
    x3 = _gated_out_fwd("od_out_fwd", x2, o, qkvg, 3, p["od_w_out"], TMF)
    kv1 = _norm_mm_fwd("kv1_fwd", mem, g_mem, wkv[1], MEM_LEN)
    x4 = _xa_fwd("xa1_fwd", x3, g_xa[1], kv1, wqg[1], wo[1], TMF)

    G = {}
    dx4, loss_row, G["final_norm_g"] = _loss_head(x4, g_fin, target, TMF)
    dx3, dg_xa1, dkv1, dwqg1, dwo1 = _xa_bwd("xa1_bwd", x3, dx4, g_xa[1], kv1, wqg[1], wo[1], TM)
    do, dgate, G["od_w_out"] = _gated_out_bwd("od_out_bwd", dx3, o, qkvg, 3, p["od_w_out"], TMF)
    dq, dk, dv, dbias = _attn_bwd(qkvg, do, bias)
    G["od_rel_bias"] = _bias_grad(dbias, _rel_onehot())
    dx2, G["od_norm_g"], G["od_w_in"] = _norm_mm_bwd(
        "od_in_bwd", x2, g_od, p["od_w_in"], [(dq, D, 0), (dk, D, 0), (dv, D, 0), (dgate, D, 0)], dx3, TM)
    G["xa_w_qg"], G["xa_w_o"] = [None, dwqg1], [None, dwo1]
    dx1, dg_xa0, dkv0, G["xa_w_qg"][0], G["xa_w_o"][0], *recv_early = _xa_bwd(
        "xa0_bwd", x1, dx2, g_xa[0], kv0, wqg[0], wo[0], TM, comm=(pack_early(G), ()))
    G["xa_norm_g"] = jnp.concatenate([dg_xa0, dg_xa1], axis=0)
    G["mem_norm_g"], dwkv0, dwkv1 = _kv_bwd(mem, g_mem, wkv[0], wkv[1], dkv0, dkv1)
    G["xa_w_kv"] = [dwkv0, dwkv1]
    (dy_s5, dc, dga, dgb, G["ev_s5_glu_b"], G["ev_conv_ln_g"], G["ev_conv_ln_b"], G["ev_s5_glu_w"],
     G["ev_w_out"]) = _ev_out_bwd(dx1, y_s5, c, z, glu_b, ln_g, ln_b, p["ev_s5_glu_w"], p["ev_w_out"], TMF)
    s5_out = _s5_bwd(z, dy_s5, hb_re, hb_im, bmat, cmat, a_re, a_im, d_skip, comm=(pack_mid(G), ()))
    du, (dbre, dbim, dcre, dcim, da_re, da_im, G["ev_s5_d"]), recv_mid = s5_out[:4], s5_out[4:11], s5_out[11:]
    G["ev_s5_c_re"], G["ev_s5_c_im"] = _from_blockdiag(dcre), _from_blockdiag(dcim)
    G["ev_s5_lambda_re"], G["ev_s5_lambda_im"], G["ev_s5_log_dt"], G["ev_s5_b_re"], G["ev_s5_b_im"] = _s5_disc_bwd(
        disc_in, (da_re.reshape(NG, NS), da_im.reshape(NG, NS), _from_blockdiag(dbre), _from_blockdiag(dbim)))
    dval, dglu, dw32, G["ev_conv_b"], *recv_small = _conv_bwd(z, dc, w32, 512, comm=pack_small(G))
    G["ev_conv_w"] = dw32[:CK]
    dx, G["ev_norm_g"], G["ev_w_in"] = _norm_mm_bwd(
        "ev_in_bwd", x, g_ev, p["ev_w_in"], [*[(t, 128, 0) for t in du], (dga, SW, 0), (dval, CW, 0), (dglu, CW, 0), (dgb, CW, 0)], dx1, TM)
    return loss_row[0, 0], dx, G, recv_early, recv_mid, recv_small


def _all_gather(payloads):
    n = len(payloads)

    def body(*refs):
        ins, outs = refs[:n], refs[n:2 * n]
        send_sems, recv_sems, local_sems = refs[2 * n:]
        x, y, c = _mesh_pos()
        me, sibling = (x, y, c), (x, y, 1 - c)
        chips = [(1 - x, y), (x, 1 - y), (1 - x, 1 - y)]

        def copy(p, k, block, to, src=None):
            return pltpu.make_async_remote_copy(
                src_ref=_slot(outs[p], block) if src is None else src, dst_ref=_slot(outs[p], block),
                send_sem=send_sems.at[7 * p + k], recv_sem=recv_sems.at[7 * p + k],
                device_id=to, device_id_type=pl.DeviceIdType.MESH)

        mine = [pltpu.make_async_copy(ins[p], _slot(outs[p], me), local_sems.at[p]) for p in range(n)]
        for cp in mine:
            cp.start()
        first = []
        for p in range(n):
            first.append(copy(p, 0, me, sibling, src=ins[p]))
            first += [copy(p, 1 + j, me, (*chip, c), src=ins[p]) for j, chip in enumerate(chips)]
        for cp in first:
            cp.start()
        passed = []
        for j, chip in enumerate(chips):
            for p in range(n):
                copy(p, 1 + j, (*chip, c), me).wait_recv()
                passed.append(copy(p, 4 + j, (*chip, c), sibling))
                passed[-1].start()
        for p in range(n):
            copy(p, 0, sibling, me).wait_recv()
            for j, chip in enumerate(chips):
                copy(p, 4 + j, (*chip, 1 - c), me).wait_recv()
        for cp in first + passed:
            cp.wait_send()
        for cp in mine:
            cp.wait()

    return pl.pallas_call(
        body, name="all_gather", in_specs=[_ANY] * n, out_specs=[_ANY] * n,
        out_shape=[SDS((NDEV, *a.shape), a.dtype) for a in payloads],
        scratch_shapes=[pltpu.SemaphoreType.DMA((7 * n,)), pltpu.SemaphoreType.DMA((7 * n,)), pltpu.SemaphoreType.DMA((n,))],
    )(*payloads)


def _exchange(slotted, whole):
    n = len(slotted) + len(whole)
    shapes, sems = _comm_shapes((slotted, whole))

    def body(*refs):
        copies = _comm_copies(len(slotted), refs[:n], refs[n:2 * n], *refs[2 * n:])
        for cp in copies:
            cp.start()
        for cp in copies:
            cp.wait()

    return pl.pallas_call(body, name="grad_exchange", in_specs=[_ANY] * n, out_specs=[_ANY] * n, out_shape=shapes,
                          scratch_shapes=sems)(*slotted, *whole)


def _adamw_math(g, w, m, v):
    m2 = ADAM_B1 * m + (1.0 - ADAM_B1) * g
    v2 = ADAM_B2 * v + (1.0 - ADAM_B2) * jnp.square(g)
    m_hat = m2 / (1.0 - ADAM_B1 ** ADAM_STEP)
    v_hat = v2 / (1.0 - ADAM_B2 ** ADAM_STEP)
    return -ADAM_LR * (m_hat / (jnp.sqrt(v_hat) + ADAM_EPS) + ADAM_WD * w), m2, v2


def _sum_slots(r_ref, rows=None):
    acc = r_ref[0].astype(f32) if rows is None else r_ref[0, rows[0]:rows[1], :].astype(f32)
    for k in range(1, NDEV):
        acc = acc + (r_ref[k] if rows is None else r_ref[k, rows[0]:rows[1], :]).astype(f32)
    return acc


def _sum_adamw(name, sources, w, m, v, tr):
    rows, C = w.shape
    starts = [int(t) for t in np.cumsum([0] + [r // tr for (_, _, r) in sources])]

    def kern(*refs):
        r_refs = refs[:len(sources)]
        w_ref, m_ref, v_ref, g_ref, d_ref, m2_ref, v2_ref = refs[len(sources):]
        g = _sum_slots(r_refs[0])
        for s in range(1, len(sources)):
            g = jnp.where(pl.program_id(0) >= starts[s], _sum_slots(r_refs[s]), g)
        g_ref[...] = g
        d_ref[...], m2_ref[...], v2_ref[...] = _adamw_math(g, w_ref[...], m_ref[...], v_ref[...])

    def src_spec(s):
        _, off, r = sources[s]
        return pl.BlockSpec((NDEV, tr, C), lambda i: (0, off // tr + jnp.clip(i - starts[s], 0, r // tr - 1), 0))

    blk = pl.BlockSpec((tr, C), lambda i: (i, 0))
    return pl.pallas_call(
        kern, name=name, grid=(rows // tr,),
        in_specs=[src_spec(s) for s in range(len(sources))] + [blk, blk, blk],
        out_specs=[blk] * 4, out_shape=[SDS((rows, C), f32)] * 4, compiler_params=_cp(),
    )(*[r for (r, _, _) in sources], w, m, v)


def _small_adamw(recvs, table, wmv):
    rnames = list(recvs)
    nr, nw = len(rnames), len(table)

    def kern(*refs):
        rr = dict(zip(rnames, refs[:nr]))
        ins, outs = refs[nr:nr + 3 * nw], refs[nr + 3 * nw:]
        for i, (_, shape, src, r0) in enumerate(table):
            g = _sum_slots(rr[src], (r0, r0 + shape[0]))[:, :shape[1]]
            outs[4 * i][...] = g
            res = _adamw_math(g, ins[3 * i][...], ins[3 * i + 1][...], ins[3 * i + 2][...])
            for o, val in zip(outs[4 * i + 1:4 * i + 4], res):
                o[...] = val

    flat = [t for (n, _, _, _) in table for t in wmv[n]]
    res = pl.pallas_call(
        kern, name="small_adamw", out_shape=[SDS(shape, f32) for (_, shape, _, _) in table for _ in range(4)],
        compiler_params=_cp(()),
    )(*[recvs[n] for n in rnames], *flat)
    return {n: tuple(res[4 * i:4 * i + 4]) for i, (n, _, _, _) in enumerate(table)}


WEIGHTS = ["mem_norm_g", "ev_norm_g", "ev_w_in", "ev_s5_lambda_re", "ev_s5_lambda_im", "ev_s5_log_dt", "ev_s5_b_re",
           "ev_s5_b_im", "ev_s5_c_re", "ev_s5_c_im", "ev_s5_d", "ev_s5_glu_w", "ev_s5_glu_b", "ev_conv_w", "ev_conv_b",
           "ev_conv_ln_g", "ev_conv_ln_b", "ev_w_out", "od_norm_g", "od_w_in", "od_rel_bias", "od_w_out", "xa_norm_g",
           "xa_w_qg", "xa_w_kv", "xa_w_o", "final_norm_g"]
GATHER = {"first": [("g320", 320, [("ev_w_in", None, "cols", 1024)])],
          "ev_in": [("h1024", 1024, [("ev_w_out", None, "rows", 128)]), ("h512", 512, [("ev_s5_glu_w", None, "rows", 64)])],
          "s5": [("i512", 512, [("od_w_in", None, "cols", 1024)]), ("i1024", 1024, [("xa_w_o", 0, "rows", 128)])],
          "conv": [("j256", 256, [("xa_w_qg", 0, "cols", 1024)])],
          "ev_out": [("k256", 256, [("xa_w_kv", 0, "cols", 1024)])],
          "attn": [("l1024", 1024, [("od_w_out", None, "rows", 128), ("xa_w_o", 1, "rows", 128)]),
                   ("l256", 256, [("xa_w_qg", 1, "cols", 1024), ("xa_w_kv", 1, "cols", 1024)])]}
FIRST = [("a1024", 1024, [("ev_w_out", None, "rows", 128), ("xa_w_o", 0, "rows", 128)]),
         ("a512", 512, [("ev_s5_glu_w", None, "rows", 64)]),
         ("a256", 256, [("xa_w_qg", 0, "cols", 1024), ("xa_w_kv", 0, "cols", 1024), ("xa_w_kv", 1, "cols", 1024)]),
         ("a320", 320, [("ev_w_in", None, "cols", 1024)])]
LATE = [("b1024", 1024, [("od_w_out", None, "rows", 128), ("xa_w_o", 1, "rows", 128)]),
        ("b512", 512, [("od_w_in", None, "cols", 1024)]),
        ("b256", 256, [("xa_w_qg", 1, "cols", 1024)])]
MATMUL_WEIGHTS = ["ev_w_out", "xa_w_o", "ev_s5_glu_w", "xa_w_qg", "xa_w_kv", "ev_w_in", "od_w_out", "od_w_in"]
ADAM_ROWS = {1024: 128, 512: 64, 256: 256, 320: 256}
SMALL = [("mem_norm_g", (1, D), "s1024", 0), ("ev_norm_g", (1, D), "s1024", 1), ("xa_norm_g", (2, D), "s1024", 2),
         ("final_norm_g", (1, D), "s1024", 4),
         ("ev_s5_d", (1, SW), "s512", 0), ("ev_s5_glu_b", (1, SW), "s512", 1), ("ev_conv_b", (1, CW), "s512", 2),
         ("ev_conv_ln_g", (1, CW), "s512", 3), ("ev_conv_ln_b", (1, CW), "s512", 4),
         ("ev_s5_c_re", (NG * GS, NS), "s64", 0), ("ev_s5_c_im", (NG * GS, NS), "s64", 512),
         ("ev_s5_b_re", (NG * GS, NS), "s64", 1024), ("ev_s5_b_im", (NG * GS, NS), "s64", 1536),
         ("ev_s5_lambda_re", (NG, NS), "s64", 2048), ("ev_s5_lambda_im", (NG, NS), "s64", 2080),
         ("ev_s5_log_dt", (1, NG), "s64", 2112),
         ("od_rel_bias", (HEADS, 2 * MAX_REL + 1), "s257", 0),
         ("ev_conv_w", (CK, CW // NDEV), "cw", 0), ("od_norm_g", (1, D // NDEV), "on", 0)]
S5_B = ("ev_s5_b_re", "ev_s5_b_im")


def _pad_rows(a, rows):
    return jnp.pad(a, ((0, rows - a.shape[0]), (0, 0)))


def _shard_payloads(a, groups):
    pick = lambda n, l, minor: (a[n] if l is None else a[n][l]).reshape(-1, minor)
    return [jnp.concatenate([pick(n, l, minor).astype(bf16) for n, l, _, _ in members], axis=0) for _, minor, members in groups]


def _weight_views(p, groups, gathered):
    p = dict(p)
    for (_, _, members), buf in zip(groups, gathered):
        off = 0
        for n, l, kind, rows in members:
            view = _W(buf, rows, off // rows, kind)
            if l is None:
                p[n] = view
            else:
                p[n] = list(p.get(n, [None, None]))
                p[n][l] = view
            off += rows
    return p


def _slotted_grads(G, groups):
    slot3 = lambda g, rows: g.reshape(NDEV, rows, g.shape[-1]) if g.ndim == 2 else g
    return [jnp.concatenate([slot3(G[n] if l is None else G[n][l], rows).astype(bf16) for n, l, _, rows in members], axis=1)
            for _, _, members in groups]


def _bt(b):
    return jnp.transpose(b.reshape(NG, NS, GS), (0, 2, 1)).reshape(NG * GS, NS)


def _unbt(b, shape):
    return jnp.transpose(b.reshape(NG, GS, NS), (0, 2, 1)).reshape(shape)


def kernel(*args):
    names = ["x", "mem", *WEIGHTS, "loss_target", *["m_" + n for n in WEIGHTS], *["v_" + n for n in WEIGHTS]]
    a = dict(zip(names, args, strict=True))

    other = jnp.concatenate([_pad_rows(a["ev_conv_w"][0], 32).reshape(16, 128), _pad_rows(a["od_norm_g"], 8)], axis=0)
    g_in, gother = _all_gather([*_shard_payloads(a, GATHER["first"]), other])
    p = {n: (a[n] if n in ("xa_norm_g", "mem_norm_g", "final_norm_g") else a[n][0]) for n in WEIGHTS if n not in MATMUL_WEIGHTS}
    p["ev_w_in"] = jnp.transpose(g_in, (1, 0, 2)).reshape(D, EVEN_IN)
    p["ev_conv_w"] = jnp.transpose(gother[:, :16].reshape(NDEV, 32, CW // NDEV)[:, :CK], (1, 0, 2)).reshape(CK, CW)
    p["od_norm_g"] = gother[:, 16].reshape(D)

    def pack_small(G):
        whole = [_pad_rows(jnp.concatenate([G[n] for n in ("ev_s5_c_re", "ev_s5_c_im", "ev_s5_b_re", "ev_s5_b_im",
                                                           "ev_s5_lambda_re", "ev_s5_lambda_im")]
                                           + [jnp.pad(G["ev_s5_log_dt"].reshape(1, NG), ((0, 0), (0, NS - NG)))]), 2128).astype(bf16),
                 G["od_rel_bias"]]
        return [G["od_norm_g"].reshape(NDEV, 1, D // NDEV)], whole

    loss_part, dx, G, recv_early, recv_mid, recv_small = _local_step(
        a["x"][0], a["mem"][0], a["loss_target"][0], p, {s: _shard_payloads(a, g) for s, g in GATHER.items()},
        lambda p, stage, gathered: _weight_views(p, GATHER[stage], gathered),
        lambda G: _slotted_grads(G, LATE), lambda G: _slotted_grads(G, FIRST[:3]), pack_small)

    G["ev_w_in"] = jnp.transpose(G["ev_w_in"].reshape(D, NDEV, EVEN_IN // NDEV), (1, 0, 2))
    gains = _pad_rows(jnp.concatenate([G[n] for n in ("mem_norm_g", "ev_norm_g", "xa_norm_g", "final_norm_g")]), 8)
    rows512 = _pad_rows(jnp.concatenate([G[n] for n in ("ev_s5_d", "ev_s5_glu_b", "ev_conv_b", "ev_conv_ln_g", "ev_conv_ln_b")]), 8)
    cw = jnp.pad(jnp.transpose(G["ev_conv_w"].reshape(CK, NDEV, CW // NDEV), (1, 0, 2)), ((0, 0), (0, 1), (0, 0)))
    recv_last = _exchange([*_slotted_grads(G, FIRST[3:]), cw], [gains, rows512])
    recvs = dict(zip([g[0] for g in LATE], recv_early))
    recvs.update(zip([g[0] for g in FIRST[:3]], recv_mid))
    recvs.update(zip(["on", "s64", "s257"], recv_small))
    recvs.update(zip(["a320", "cw", "s1024", "s512"], recv_last))

    sources = {}
    for buf, minor, members in FIRST + LATE:
        off = 0
        for n, l, _, rows in members:
            sources.setdefault(n, []).append((l or 0, recvs[buf], off, rows, minor))
            off += rows
    res = {}
    for n, src in sources.items():
        src = sorted(src, key=lambda t: t[0])
        minor = src[0][4]
        w, m, v = (a[k + n].reshape(-1, minor) for k in ("", "m_", "v_"))
        res[n] = _sum_adamw("adamw_" + n, [(r, off, rows) for _, r, off, rows, _ in src], w, m, v, ADAM_ROWS[minor])
    two_d = lambda n, t: _bt(t[0]) if n in S5_B else t.reshape(dict((s[0], s[1]) for s in SMALL)[n])
    res.update(_small_adamw({k: recvs[k] for k in ("s1024", "s512", "s64", "s257", "cw", "on")}, SMALL,
                            {n: tuple(two_d(n, a[k + n]) for k in ("", "m_", "v_")) for n, _, _, _ in SMALL}))
    shaped = lambda n, t: _unbt(t, a[n].shape) if n in S5_B else t.reshape(a[n].shape)
    loss = lax.psum(loss_part, AXES)
    return (loss, dx[None], *[shaped(n, res[n][k]) for k in range(4) for n in WEIGHTS])
```
